```python
import math
import jax
import jax.numpy as jnp
from jax import lax
import numpy as np

D_MODEL = 1024
BATCH = 4
SEQ = 4096
DEPTH = 1

N_META = 16
GLA_HEADS = 4
GLA_DK = 128
GLA_DV = 256
GLA_QK = GLA_HEADS * GLA_DK
GLA_VW = GLA_HEADS * GLA_DV
GLA_GATE_RANK = 16
GLA_GATE_TAU = 16.0
GLA_CHUNK = 64
MLA_HEADS = 16
MLA_Q_RANK = 384
MLA_KV_RANK = 256
MLA_NOPE = 64
MLA_ROPE = 32
MLA_DV = 64
MLA_QDIM = MLA_NOPE + MLA_ROPE
MLA_VW = MLA_HEADS * MLA_DV
ROPE_BASE = 10000.0
Q_BLOCK = 128
N_GROUPS = 4
EXPERTS_PER_GROUP = 8
EXPERT_FF = 256
TOP_K = 2
ALPHA = (2.0 * DEPTH) ** 0.25
BETA = (8.0 * DEPTH) ** -0.25
LN_EPS = 1e-5
RMS_EPS = 1e-6
IN_SIZES = (GLA_QK, GLA_QK, GLA_VW, GLA_VW, GLA_GATE_RANK, MLA_Q_RANK, MLA_KV_RANK, MLA_ROPE, D_MODEL, D_MODEL)
IN_WIDTH = sum(IN_SIZES)
IN_OFFSETS = tuple(int(o) for o in np.cumsum(IN_SIZES)[:-1])

kernel_name = "hybrid_gla_mla_hmoe_deepnorm_meta"


def layer_norm(x, g, b):
    xf = x.astype(jnp.float32)
    mu = jnp.mean(xf, axis=-1, keepdims=True)
    var = jnp.mean(jnp.square(xf - mu), axis=-1, keepdims=True)
    return ((xf - mu) * lax.rsqrt(var + LN_EPS) * g.astype(jnp.float32) + b.astype(jnp.float32)).astype(x.dtype)


def rms_norm(x, g):
    xf = x.astype(jnp.float32)
    ms = jnp.mean(jnp.square(xf), axis=-1, keepdims=True)
    return (xf * lax.rsqrt(ms + RMS_EPS) * g.astype(jnp.float32)).astype(x.dtype)


def apply_rope(x, cos, sin):
    half = x.shape[-1] // 2
    x1, x2 = x[..., :half], x[..., half:]
    cos = cos.astype(x.dtype)
    sin = sin.astype(x.dtype)
    return jnp.concatenate([x1 * cos - x2 * sin, x1 * sin + x2 * cos], axis=-1)


def gla_mixer(q, k, v, log_a, r, norm_g):
    B, L, _ = q.shape
    pad = GLA_CHUNK - N_META
    padw = ((0, 0), (pad, 0), (0, 0))
    q, k, v, log_a = (jnp.pad(t, padw) for t in (q, k, v, log_a))
    Lp = L + pad
    n = Lp // GLA_CHUNK

    def to_chunks(t, d):
        return t.reshape(B, n, GLA_CHUNK, GLA_HEADS, d).transpose(0, 3, 1, 2, 4)

    qc = to_chunks(q, GLA_DK).astype(jnp.float32) * (GLA_DK ** -0.5)
    kc = to_chunks(k, GLA_DK).astype(jnp.float32)
    vc = to_chunks(v, GLA_DV).astype(jnp.float32)
    bc = jnp.cumsum(to_chunks(log_a, GLA_DK).astype(jnp.float32), axis=3)
    b_last = bc[:, :, :, -1:, :]
    q_t = qc * jnp.exp(bc)
    k_t = kc * jnp.exp(-bc)
    k_end = kc * jnp.exp(b_last - bc)
    causal = jnp.tril(jnp.ones((GLA_CHUNK, GLA_CHUNK), dtype=bool))
    att = jnp.einsum('bhncd,bhnsd->bhncs', q_t, k_t)
    att = jnp.where(causal, att, 0.0)
    o_intra = jnp.einsum('bhncs,bhnsv->bhncv', att, vc)
    upd = jnp.einsum('bhncd,bhncv->nbhdv', k_end, vc)
    dec = jnp.exp(b_last[:, :, :, 0, :]).transpose(2, 0, 1, 3)

    def step(state, inp):
        u, d = inp
        return d[..., None] * state + u, state

    s0 = jnp.zeros((B, GLA_HEADS, GLA_DK, GLA_DV), dtype=upd.dtype)
    _, s_before = lax.scan(step, s0, (upd, dec))
    o_inter = jnp.einsum('bhncd,nbhdv->bhncv', q_t, s_before)
    o = (o_intra + o_inter).transpose(0, 2, 3, 1, 4).reshape(B, Lp, GLA_HEADS, GLA_DV)[:, pad:]
    o = rms_norm(o, norm_g) * jax.nn.silu(r.reshape(B, L, GLA_HEADS, GLA_DV).astype(jnp.float32))
    return o.reshape(B, L, GLA_VW).astype(r.dtype)


def mla_mixer(c_q, c_kv, k_r, q_norm_g, w_uq, kv_norm_g, w_uk, w_uv, cos, sin):
    B, L, _ = c_q.shape
    q = (rms_norm(c_q, q_norm_g) @ w_uq).reshape(B, L, MLA_HEADS, MLA_QDIM)
    q_nope = q[..., :MLA_NOPE]
    q_rope = apply_rope(q[..., MLA_NOPE:], cos[:, None, :], sin[:, None, :])
    ckv = rms_norm(c_kv, kv_norm_g)
    k_nope = (ckv @ w_uk).reshape(B, L, MLA_HEADS, MLA_NOPE)
    v = (ckv @ w_uv).reshape(B, L, MLA_HEADS, MLA_DV)
    k_rope = apply_rope(k_r, cos, sin)
    nb = -(-L // Q_BLOCK)
    Lq = nb * Q_BLOCK
    padw = ((0, 0), (0, Lq - L), (0, 0), (0, 0))

    def to_blocks(t):
        return jnp.pad(t, padw).reshape(B, nb, Q_BLOCK, MLA_HEADS, t.shape[-1]).transpose(1, 0, 2, 3, 4)

    qn_b = to_blocks(q_nope)
    qr_b = to_blocks(q_rope)
    qpos_b = jnp.arange(Lq, dtype=jnp.int32).reshape(nb, Q_BLOCK)
    kpos = jnp.arange(L, dtype=jnp.int32)
    scale = MLA_QDIM ** -0.5

    def attend(args):
        qn, qr, qpos = args
        s = jnp.einsum('bqhd,bkhd->bhqk', qn, k_nope) + jnp.einsum('bqhd,bkd->bhqk', qr, k_rope)
        s = s.astype(jnp.float32) * scale
        s = jnp.where(qpos[:, None] >= kpos[None, :], s, jnp.finfo(jnp.float32).min)
        p = jax.nn.softmax(s, axis=-1).astype(v.dtype)
        return jnp.einsum('bhqk,bkhd->bqhd', p, v)

    o = lax.map(attend, (qn_b, qr_b, qpos_b))
    o = o.transpose(1, 0, 2, 3, 4).reshape(B, Lq, MLA_HEADS, MLA_DV)[:, :L]
    return o.reshape(B, L, MLA_VW)


def moe_ffn(h, w_rg, b_rg, w_re, b_re, w_gate, w_up, w_down):
    B, L, D = h.shape
    t = h.reshape(B * L, D)
    p_group = jax.nn.softmax((t @ w_rg + b_rg).astype(jnp.float32), axis=-1)
    g_idx = jnp.argmax(p_group, axis=-1)
    p_g = jnp.max(p_group, axis=-1)
    e_logits = (t @ w_re + b_re).astype(jnp.float32).reshape(-1, N_GROUPS, EXPERTS_PER_GROUP)
    e_sel = jnp.take_along_axis(e_logits, g_idx[:, None, None], axis=1)[:, 0]
    top_v, top_i = lax.top_k(e_sel, TOP_K)
    top_p = jax.nn.softmax(top_v, axis=-1)
    w_e = jnp.einsum('nk,nke->ne', top_p, jax.nn.one_hot(top_i, EXPERTS_PER_GROUP, dtype=jnp.float32))
    combine = (p_g[:, None, None] * jax.nn.one_hot(g_idx, N_GROUPS, dtype=jnp.float32)[:, :, None]
               * w_e[:, None, :]).astype(h.dtype)
    y = jnp.zeros_like(t)
    for grp in range(N_GROUPS):
        a = jnp.einsum('nd,edf->nef', t, w_gate[grp])
        u = jnp.einsum('nd,edf->nef', t, w_up[grp])
        hid = jax.nn.silu(a) * u * combine[:, grp, :, None]
        y = y + jnp.einsum('nef,efd->nd', hid, w_down[grp])
    return y.reshape(B, L, D)


def setup_inputs(seed: int = 0) -> dict:
    key = jax.random.key(seed)
    ks = jax.random.split(key, 32)
    f32 = jnp.float32
    nrm = lambda k, shape, s: jax.random.normal(k, shape, dtype=f32) * s
    gain = lambda k, shape: 1.0 + 0.02 * jax.random.normal(k, shape, dtype=f32)
    G, E, F, D = N_GROUPS, EXPERTS_PER_GROUP, EXPERT_FF, D_MODEL
    col_scale = jnp.ones((IN_WIDTH,), f32).at[2 * GLA_QK:2 * GLA_QK + GLA_VW].set(BETA)
    return {
        "x": nrm(ks[0], (BATCH, SEQ, D), 1.0),
        "meta_tokens": nrm(ks[1], (N_META, D), 1.0),
        "ln_emb_g": gain(ks[2], (D,)),
        "ln_emb_b": nrm(ks[3], (D,), 0.02),
        "w_in": nrm(ks[4], (DEPTH, D, IN_WIDTH), D ** -0.5) * col_scale,
        "gla_gate_w2": nrm(ks[5], (DEPTH, GLA_GATE_RANK, GLA_QK), GLA_GATE_RANK ** -0.5),
        "gla_gate_b": nrm(ks[6], (DEPTH, GLA_QK), 0.02),
        "gla_norm_g": gain(ks[7], (DEPTH, GLA_DV)),
        "mla_q_norm_g": gain(ks[8], (DEPTH, MLA_Q_RANK)),
        "mla_w_uq": nrm(ks[9], (DEPTH, MLA_Q_RANK, MLA_HEADS * MLA_QDIM), MLA_Q_RANK ** -0.5),
        "mla_kv_norm_g": gain(ks[10], (DEPTH, MLA_KV_RANK)),
        "mla_w_uk": nrm(ks[11], (DEPTH, MLA_KV_RANK, MLA_HEADS * MLA_NOPE), MLA_KV_RANK ** -0.5),
        "mla_w_uv": nrm(ks[12], (DEPTH, MLA_KV_RANK, MLA_VW), MLA_KV_RANK ** -0.5 * BETA),
        "w_branch_gla": nrm(ks[13], (DEPTH, GLA_VW, D), GLA_VW ** -0.5 * BETA),
        "w_branch_mla": nrm(ks[14], (DEPTH, MLA_VW, D), MLA_VW ** -0.5 * BETA),
        "w_out": nrm(ks[15], (DEPTH, D, D), D ** -0.5 * BETA),
        "ln_mix_g": gain(ks[16], (DEPTH, D)),
        "ln_mix_b": nrm(ks[17], (DEPTH, D), 0.02),
        "router_group_w": nrm(ks[18], (DEPTH, D, G), D ** -0.5),
        "router_group_b": nrm(ks[19], (DEPTH, G), 0.01),
        "router_expert_w": nrm(ks[20], (DEPTH, D, G * E), D ** -0.5),
        "router_expert_b": nrm(ks[21], (DEPTH, G * E), 0.01),
        "expert_w_gate": nrm(ks[22], (DEPTH, G, E, D, F), D ** -0.5),
        "expert_w_up": nrm(ks[23], (DEPTH, G, E, D, F), D ** -0.5 * BETA),
        "expert_w_down": nrm(ks[24], (DEPTH, G, E, F, D), F ** -0.5 * BETA),
        "ln_ffn_g": gain(ks[25], (DEPTH, D)),
        "ln_ffn_b": nrm(ks[26], (DEPTH, D), 0.02),
    }


def reference(x, meta_tokens, ln_emb_g, ln_emb_b, w_in, gla_gate_w2, gla_gate_b, gla_norm_g,
              mla_q_norm_g, mla_w_uq, mla_kv_norm_g, mla_w_uk, mla_w_uv, w_branch_gla, w_branch_mla,
              w_out, ln_mix_g, ln_mix_b, router_group_w, router_group_b, router_expert_w,
              router_expert_b, expert_w_gate, expert_w_up, expert_w_down, ln_ffn_g, ln_ffn_b):
    B = x.shape[0]
    meta = jnp.broadcast_to(meta_tokens[None].astype(x.dtype), (B, N_META, x.shape[-1]))
    s = layer_norm(jnp.concatenate([meta, x], axis=1), ln_emb_g, ln_emb_b)
    L = s.shape[1]
    pos = jnp.arange(L, dtype=jnp.float32)
    inv_freq = ROPE_BASE ** (-jnp.arange(0, MLA_ROPE, 2, dtype=jnp.float32) / MLA_ROPE)
    ang = pos[:, None] * inv_freq[None, :]
    cos, sin = jnp.cos(ang), jnp.sin(ang)
    for l in range(DEPTH):
        proj = s @ w_in[l]
        (q_g, k_g, v_g, r_g, a_lr, c_q, c_kv, k_r, gate_a, gate_b) = jnp.split(proj, IN_OFFSETS, axis=-1)
        log_a = jax.nn.log_sigmoid((a_lr @ gla_gate_w2[l] + gla_gate_b[l]).astype(jnp.float32)) / GLA_GATE_TAU
        o_gla = gla_mixer(q_g, k_g, v_g, log_a, r_g, gla_norm_g[l])
        o_mla = mla_mixer(c_q, c_kv, k_r, mla_q_norm_g[l], mla_w_uq[l], mla_kv_norm_g[l],
                          mla_w_uk[l], mla_w_uv[l], cos, sin)
        merged = (jax.nn.sigmoid(gate_a) * (o_gla @ w_branch_gla[l])
                  + jax.nn.sigmoid(gate_b) * (o_mla @ w_branch_mla[l]))
        s = layer_norm(ALPHA * s + merged @ w_out[l], ln_mix_g[l], ln_mix_b[l])
        ffn = moe_ffn(s, router_group_w[l], router_group_b[l], router_expert_w[l], router_expert_b[l],
                      expert_w_gate[l], expert_w_up[l], expert_w_down[l])
        s = layer_norm(ALPHA * s + ffn, ln_ffn_g[l], ln_ffn_b[l])
    return s[:, N_META:]
```

```python
import functools

import jax
import jax.numpy as jnp
import numpy as np
from jax import lax
from jax.experimental import pallas as pl
from jax.experimental.pallas import tpu as pltpu

D_MODEL = 1024
N_META = 16
GLA_HEADS = 4
GLA_DK = 128
GLA_DV = 256
GLA_QK = GLA_HEADS * GLA_DK
GLA_VW = GLA_HEADS * GLA_DV
GLA_GATE_RANK = 16
GLA_GATE_TAU = 16.0
GLA_CHUNK = 64
MLA_HEADS = 16
MLA_Q_RANK = 384
MLA_KV_RANK = 256
MLA_NOPE = 64
MLA_ROPE = 32
MLA_DV = 64
MLA_QDIM = MLA_NOPE + MLA_ROPE
ROPE_BASE = 10000.0
N_GROUPS = 4
EXPERTS_PER_GROUP = 8
N_EXPERTS = N_GROUPS * EXPERTS_PER_GROUP
EXPERT_FF = 256
DEPTH = 1
ALPHA = (2.0 * DEPTH) ** 0.25
LN_EPS = 1e-5
RMS_EPS = 1e-6

LANES = 128
TILE = 256
PADL = TILE - N_META
HEAD_PAD = LANES
Q_TILE = 512
KV_TILE = 256
MERGE_TILE = 512
EXPERT_TILE = 256
COMBINE_TILE = 256
NEG = -1e30
VMEM_LIMIT = 56 * 1024 * 1024

_C_Q, _C_K, _C_V, _C_R = 0, 512, 1024, 2048
_C_A = 3072
_C_CQ = _C_A + LANES
_C_CKV = _C_CQ + MLA_Q_RANK
_C_KR = _C_CKV + MLA_KV_RANK
_C_GA = _C_KR + LANES
_C_GB = _C_GA + D_MODEL
_W_COLS = _C_GB + D_MODEL

_f32 = jnp.float32
_bf16 = jnp.bfloat16


def _dot(a, b):
    return jnp.dot(a, b, preferred_element_type=_f32)


def _dot_nt(a, b):
    return lax.dot_general(a, b, (((1,), (1,)), ((), ())), preferred_element_type=_f32)


def _dot_tn(a, b):
    return lax.dot_general(a, b, (((0,), (0,)), ((), ())), preferred_element_type=_f32)


def _layer_norm(x, g, b):
    mu = jnp.mean(x, axis=-1, keepdims=True)
    xc = x - mu
    var = jnp.mean(xc * xc, axis=-1, keepdims=True)
    return xc * lax.rsqrt(var + LN_EPS) * g + b


def _rms_norm(x, g):
    ms = jnp.mean(x * x, axis=-1, keepdims=True)
    return x * lax.rsqrt(ms + RMS_EPS) * g


def _sigmoid(x):
    return 1.0 / (1.0 + jnp.exp(-x))


def _rope(x, cos, sin_lo, sin_hi):
    half = MLA_ROPE // 2
    from_hi = pltpu.roll(x, LANES - half, 1)
    from_lo = pltpu.roll(x, half, 1)
    return x * cos + from_hi * sin_lo + from_lo * sin_hi


def _inproj_kernel(x_ref, lng_ref, lnb_ref, w_ref, w2_ref, gb_ref, qg_ref, wuq_ref, kvg_ref,
                   wuk_ref, wuv_ref, cos_ref, sl_ref, sh_ref, tril_ref,
                   s_ref, qt_ref, kt_ref, ke_ref, gv_ref, dec_ref, sr_ref, qm_ref, km_ref,
                   vm_ref, ga_ref, gbt_ref):
    t = pl.program_id(1)
    sn = _layer_norm(x_ref[0], lng_ref[...], lnb_ref[...])
    s_ref[0] = sn
    snb = sn.astype(_bf16)
    row = t * TILE + lax.broadcasted_iota(jnp.int32, (TILE, 1), 0)
    valid = row >= PADL

    a_lr = _dot(snb, w_ref[:, _C_A:_C_A + LANES])
    z = _dot(a_lr.astype(_bf16), w2_ref[...]) + gb_ref[...]
    la = (jnp.minimum(z, 0.0) - jnp.log1p(jnp.exp(-jnp.abs(z)))) * (1.0 / GLA_GATE_TAU)
    la = jnp.where(valid, la, 0.0)
    hi = la.astype(_bf16)
    r1 = la - hi.astype(_f32)
    mid = r1.astype(_bf16)
    lo = (r1 - mid.astype(_f32)).astype(_bf16)
    tril = tril_ref[...]
    bc = _dot(tril, hi) + _dot(tril, mid) + _dot(tril, lo)
    n_chunks = TILE // GLA_CHUNK
    lasts = [bc[c * GLA_CHUNK + GLA_CHUNK - 1:(c + 1) * GLA_CHUNK, :] for c in range(n_chunks)]
    for c in range(n_chunks):
        dec_ref[0, c:c + 1, :] = jnp.exp(lasts[c])
    b_last = jnp.concatenate(
        [jnp.broadcast_to(l, (GLA_CHUNK, GLA_QK)) for l in lasts], axis=0)
    gq = _dot(snb, w_ref[:, _C_Q:_C_Q + GLA_QK])
    gk = jnp.where(valid, _dot(snb, w_ref[:, _C_K:_C_K + GLA_QK]), 0.0)
    qt_ref[0] = (gq * (GLA_DK ** -0.5) * jnp.exp(bc)).astype(_bf16)
    kt_ref[0] = (gk * jnp.exp(-bc)).astype(_bf16)
    ke_ref[0] = (gk * jnp.exp(b_last - bc)).astype(_bf16)
    gv_ref[0] = jnp.where(valid, _dot(snb, w_ref[:, _C_V:_C_V + GLA_VW]), 0.0).astype(_bf16)
    r = _dot(snb, w_ref[:, _C_R:_C_R + GLA_VW])
    sr_ref[0] = (r * _sigmoid(r)).astype(_bf16)

    cos = cos_ref[...]
    sl = sl_ref[...]
    sh = sh_ref[...]
    cq = _dot(snb, w_ref[:, _C_CQ:_C_CQ + MLA_Q_RANK])
    cqn = _rms_norm(cq, qg_ref[...]).astype(_bf16)
    qf = _dot(cqn, wuq_ref[...])
    scale = MLA_QDIM ** -0.5
    for h in range(MLA_HEADS):
        blk = qf[:, h * HEAD_PAD:(h + 1) * HEAD_PAD]
        qm_ref[0, :, h * HEAD_PAD:(h + 1) * HEAD_PAD] = (_rope(blk, cos, sl, sh) * scale).astype(_bf16)
    ckv = _dot(snb, w_ref[:, _C_CKV:_C_CKV + MLA_KV_RANK])
    ckvn = _rms_norm(ckv, kvg_ref[...]).astype(_bf16)
    kf = _dot(ckvn, wuk_ref[...])
    kr = _rope(_dot(snb, w_ref[:, _C_KR:_C_KR + LANES]), cos, sl, sh)
    for h in range(MLA_HEADS):
        km_ref[0, :, h * HEAD_PAD:(h + 1) * HEAD_PAD] = (
            kf[:, h * HEAD_PAD:(h + 1) * HEAD_PAD] + kr).astype(_bf16)
    vm_ref[0] = _dot(ckvn, wuv_ref[...]).astype(_bf16)

    ga_ref[0] = _sigmoid(_dot(snb, w_ref[:, _C_GA:_C_GA + D_MODEL])).astype(_bf16)
    gbt_ref[0] = _sigmoid(_dot(snb, w_ref[:, _C_GB:_C_GB + D_MODEL])).astype(_bf16)


def _gla_kernel(qt_ref, kt_ref, ke_ref, gv_ref, dec_ref, sr_ref, ng_ref, o_ref, st_ref):
    t = pl.program_id(1)

    @pl.when(t == 0)
    def _():
        st_ref[...] = jnp.zeros_like(st_ref)

    ri = lax.broadcasted_iota(jnp.int32, (GLA_CHUNK, GLA_CHUNK), 0)
    ci = lax.broadcasted_iota(jnp.int32, (GLA_CHUNK, GLA_CHUNK), 1)
    causal = ri >= ci
    ng = ng_ref[...]
    for c in range(TILE // GLA_CHUNK):
        rows = slice(c * GLA_CHUNK, (c + 1) * GLA_CHUNK)
        for h in range(GLA_HEADS):
            kc = slice(h * GLA_DK, (h + 1) * GLA_DK)
            vc = slice(h * GLA_DV, (h + 1) * GLA_DV)
            q = qt_ref[0, rows, kc]
            k = kt_ref[0, rows, kc]
            e = ke_ref[0, rows, kc]
            v = gv_ref[0, rows, vc]
            st = st_ref[h]
            att = jnp.where(causal, _dot_nt(q, k), 0.0)
            o = _dot(att.astype(_bf16), v) + _dot_nt(q, st.astype(_bf16))
            st_ref[h] = st * dec_ref[0, c:c + 1, kc] + _dot_tn(v, e)
            o = _rms_norm(o, ng) * sr_ref[0, rows, vc].astype(_f32)
            o_ref[0, rows, vc] = o.astype(_bf16)


def _mla_kernel(q_ref, k_ref, v_ref, o_ref, acc_ref, m_ref, l_ref):
    i = pl.program_id(2)
    heads = 2
    q_pos = (TILE + i * Q_TILE) + lax.broadcasted_iota(jnp.int32, (Q_TILE, KV_TILE), 0)
    col = lax.broadcasted_iota(jnp.int32, (Q_TILE, KV_TILE), 1)
    n_full = 2 * i + 1

    for h in range(heads):
        q = q_ref[0, :, h * HEAD_PAD:(h + 1) * HEAD_PAD]
        m_ref[h] = jnp.full((Q_TILE, 1), NEG, _f32)
        l_ref[h] = jnp.zeros((Q_TILE, 1), _f32)
        acc_ref[h] = jnp.zeros((Q_TILE, LANES), _f32)

        def step(j, mask):
            start = pl.multiple_of(j * KV_TILE, KV_TILE)
            kb = k_ref[0, pl.ds(start, KV_TILE), h * HEAD_PAD:(h + 1) * HEAD_PAD]
            vb = v_ref[0, pl.ds(start, KV_TILE), :]
            s = _dot_nt(q, kb)
            if mask is not None:
                s = jnp.where(mask(j), s, NEG)
            m_prev = m_ref[h]
            m_new = jnp.maximum(m_prev, jnp.max(s, axis=-1, keepdims=True))
            alpha = jnp.exp(m_prev - m_new)
            p = jnp.exp(s - m_new)
            l_ref[h] = alpha * l_ref[h] + jnp.sum(p, axis=-1, keepdims=True)
            acc_ref[h] = alpha * acc_ref[h] + _dot(p.astype(_bf16), vb)
            m_ref[h] = m_new

        step(0, lambda j: col >= PADL)

        def body(j, carry):
            step(j, None)
            return carry

        lax.fori_loop(1, n_full, body, 0)
        causal = lambda j: (j * KV_TILE + col) <= q_pos
        step(n_full, causal)
        step(n_full + 1, causal)

    lane = lax.broadcasted_iota(jnp.int32, (Q_TILE, LANES), 1)
    o0 = acc_ref[0] / l_ref[0]
    o1 = acc_ref[1] / l_ref[1]
    o_ref[0] = jnp.where(lane < MLA_DV, o0, o1).astype(_bf16)


def _merge_kernel(og_ref, om_ref, ga_ref, gbt_ref, s_ref, wbg_ref, wbm_ref, wo_ref, lng_ref,
                  lnb_ref, rwh_ref, rwl_ref, rb_ref, tril_ref,
                  s2_ref, info_ref, cnt_ref, carry_ref):
    step = pl.program_id(0)

    @pl.when(step == 0)
    def _():
        carry_ref[...] = jnp.zeros_like(carry_ref)

    ba = _dot(og_ref[...], wbg_ref[...])
    bb = _dot(om_ref[...], wbm_ref[...])
    merged = ga_ref[...].astype(_f32) * ba + gbt_ref[...].astype(_f32) * bb
    y = ALPHA * s_ref[...] + _dot(merged.astype(_bf16), wo_ref[...])
    s2 = _layer_norm(y, lng_ref[...], lnb_ref[...])
    s2_ref[...] = s2

    xh = s2.astype(_bf16)
    xl = (s2 - xh.astype(_f32)).astype(_bf16)
    logits = _dot(xh, rwh_ref[...]) + _dot(xl, rwh_ref[...]) + _dot(xh, rwl_ref[...]) + rb_ref[...]
    lane = lax.broadcasted_iota(jnp.int32, (MERGE_TILE, LANES), 1)
    is_g = lane < N_GROUPS
    gl = jnp.where(is_g, logits, NEG)
    gmax = jnp.max(gl, axis=-1, keepdims=True)
    gidx = jnp.min(jnp.where(gl == gmax, lane, LANES), axis=-1, keepdims=True)
    p_g = 1.0 / jnp.sum(jnp.where(is_g, jnp.exp(gl - gmax), 0.0), axis=-1, keepdims=True)
    lo = N_GROUPS + EXPERTS_PER_GROUP * gidx
    el = jnp.where((lane >= lo) & (lane < lo + EXPERTS_PER_GROUP), logits, NEG)
    v1 = jnp.max(el, axis=-1, keepdims=True)
    i1 = jnp.min(jnp.where(el == v1, lane, LANES), axis=-1, keepdims=True)
    el2 = jnp.where(lane == i1, NEG, el)
    v2 = jnp.max(el2, axis=-1, keepdims=True)
    i2 = jnp.min(jnp.where(el2 == v2, lane, LANES), axis=-1, keepdims=True)
    tt = jnp.exp(v2 - v1)
    p1 = 1.0 / (1.0 + tt)
    p2 = tt / (1.0 + tt)
    e1 = i1 - N_GROUPS
    e2 = i2 - N_GROUPS
    hit1 = lane == e1
    hit2 = lane == e2
    onehot = jnp.where(hit1 | hit2, 1.0, 0.0)
    before = _dot(tril_ref[...], onehot.astype(_bf16)) + carry_ref[0:1, :]
    r1 = jnp.sum(jnp.where(hit1, before, 0.0), axis=-1, keepdims=True)
    r2 = jnp.sum(jnp.where(hit2, before, 0.0), axis=-1, keepdims=True)
    new_carry = carry_ref[0:1, :] + jnp.sum(onehot, axis=0, keepdims=True)
    carry_ref[...] = jnp.broadcast_to(new_carry, carry_ref.shape)
    cnt_ref[...] = jnp.broadcast_to(new_carry, cnt_ref.shape)
    info = jnp.where(lane == 0, e1.astype(_f32),
           jnp.where(lane == 1, e2.astype(_f32),
           jnp.where(lane == 2, p_g * p1,
           jnp.where(lane == 3, p_g * p2,
           jnp.where(lane == 4, r1,
           jnp.where(lane == 5, r2, 0.0))))))
    info_ref[...] = info


def _expert_kernel(te_ref, nused_ref, src_ref, s2_hbm, rw_ref, wg_ref, wu_ref, wd_ref, o_ref,
                   buf_ref, sem):
    u = pl.program_id(0)

    def row_copy(r, tok):
        return pltpu.make_async_copy(s2_hbm.at[pl.ds(tok, 1)], buf_ref.at[pl.ds(r, 1)], sem)

    @pl.when(u < nused_ref[0])
    def _():
        def issue(r, c):
            row_copy(r, src_ref[u * EXPERT_TILE + r]).start()
            return c

        lax.fori_loop(0, EXPERT_TILE, issue, 0)

        def wait(r, c):
            row_copy(r, 0).wait()
            return c

        lax.fori_loop(0, EXPERT_TILE, wait, 0)
        x = buf_ref[...].astype(_bf16)
        a = _dot(x, wg_ref[0])
        up = _dot(x, wu_ref[0])
        hid = a * _sigmoid(a) * up * rw_ref[...]
        o_ref[...] = _dot(hid.astype(_bf16), wd_ref[0])

    @pl.when(u >= nused_ref[0])
    def _():
        o_ref[...] = jnp.zeros_like(o_ref)


def _combine_kernel(pos_ref, s2_ref, ys_hbm, lng_ref, lnb_ref, o_ref, buf_ref, sem):
    step = pl.program_id(0)

    def row_copy(k, r, src_row):
        return pltpu.make_async_copy(ys_hbm.at[pl.ds(src_row, 1)], buf_ref.at[k, pl.ds(r, 1)], sem)

    def issue(r, c):
        tok = step * COMBINE_TILE + r
        row_copy(0, r, pos_ref[2 * tok]).start()
        row_copy(1, r, pos_ref[2 * tok + 1]).start()
        return c

    lax.fori_loop(0, COMBINE_TILE, issue, 0)

    def wait(r, c):
        row_copy(0, r, 0).wait()
        row_copy(1, r, 0).wait()
        return c

    lax.fori_loop(0, COMBINE_TILE, wait, 0)
    y = ALPHA * s2_ref[...] + buf_ref[0] + buf_ref[1]
    o_ref[...] = _layer_norm(y, lng_ref[...], lnb_ref[...])


def _const_spec(shape):
    nd = len(shape)
    return pl.BlockSpec(shape, lambda *_: (0,) * nd)


def _rope_tables(lp):
    pos = jnp.maximum(jnp.arange(lp, dtype=_f32) - PADL, 0.0)
    inv_freq = ROPE_BASE ** (-jnp.arange(0, MLA_ROPE, 2, dtype=_f32) / MLA_ROPE)
    ang = pos[:, None] * inv_freq[None, :]
    cos, sin = jnp.cos(ang), jnp.sin(ang)
    half = MLA_ROPE // 2
    ones = jnp.ones((lp, MLA_NOPE), _f32)
    zeros_n = jnp.zeros((lp, MLA_NOPE), _f32)
    zeros_h = jnp.zeros((lp, half), _f32)
    tail1 = jnp.ones((lp, LANES - MLA_QDIM), _f32)
    tail0 = jnp.zeros((lp, LANES - MLA_QDIM), _f32)
    cos_t = jnp.concatenate([ones, cos, cos, tail1], axis=1)
    sin_lo = jnp.concatenate([zeros_n, -sin, zeros_h, tail0], axis=1)
    sin_hi = jnp.concatenate([zeros_n, zeros_h, sin, tail0], axis=1)
    return cos_t, sin_lo, sin_hi


def _pad_heads(w, width):
    k = w.shape[0]
    w = w.reshape(k, MLA_HEADS, width)
    w = jnp.pad(w, ((0, 0), (0, 0), (0, HEAD_PAD - width)))
    return w.reshape(k, MLA_HEADS * HEAD_PAD)


def kernel(x, meta_tokens, ln_emb_g, ln_emb_b, w_in, gla_gate_w2, gla_gate_b, gla_norm_g, mla_q_norm_g, mla_w_uq, mla_kv_norm_g, mla_w_uk, mla_w_uv, w_branch_gla, w_branch_mla, w_out, ln_mix_g, ln_mix_b, router_group_w, router_group_b, router_expert_w, router_expert_b, expert_w_gate, expert_w_up, expert_w_down, ln_ffn_g, ln_ffn_b):
    bsz, seq, d = x.shape
    assert d == D_MODEL and seq % Q_TILE == 0 and w_in.shape[0] == DEPTH == 1
    lp = PADL + N_META + seq
    nt = lp // TILE
    ntok = bsz * seq
    row2 = lambda v: v.reshape(1, -1).astype(_f32)

    xcat = jnp.concatenate([
        jnp.zeros((bsz, PADL, d), x.dtype),
        jnp.broadcast_to(meta_tokens[None].astype(x.dtype), (bsz, N_META, d)), x], axis=1)
    wi = w_in[0]
    o_a = 2 * GLA_QK + 2 * GLA_VW
    o_cq = o_a + GLA_GATE_RANK
    o_ckv = o_cq + MLA_Q_RANK
    o_kr = o_ckv + MLA_KV_RANK
    o_ga = o_kr + MLA_ROPE
    w_a = jnp.pad(wi[:, o_a:o_cq], ((0, 0), (0, LANES - GLA_GATE_RANK)))
    w_kr = jnp.pad(wi[:, o_kr:o_ga], ((0, 0), (MLA_NOPE, LANES - MLA_QDIM)))
    w_all = jnp.concatenate([wi[:, :o_a], w_a, wi[:, o_cq:o_kr], w_kr, wi[:, o_ga:]], axis=1).astype(_bf16)
    assert w_all.shape == (d, _W_COLS)
    w2p = jnp.pad(gla_gate_w2[0], ((0, LANES - GLA_GATE_RANK), (0, 0))).astype(_bf16)
    wuq = _pad_heads(mla_w_uq[0], MLA_QDIM).astype(_bf16)
    wuk = _pad_heads(mla_w_uk[0], MLA_NOPE).astype(_bf16)
    wuv = mla_w_uv[0].astype(_bf16)
    cos_t, sin_lo, sin_hi = _rope_tables(lp)
    blk = np.arange(TILE)
    tril_chunks = jnp.asarray(
        ((blk[:, None] >= blk[None, :]) & (blk[:, None] // GLA_CHUNK == blk[None, :] // GLA_CHUNK)),
        dtype=_bf16)

    pad_map = lambda b, t: (b, t, 0)
    real_map = lambda b, t: (b, jnp.maximum(t - 1, 0), 0)
    tab_spec = pl.BlockSpec((TILE, LANES), lambda b, t: (t, 0))
    out_shapes = (
        jax.ShapeDtypeStruct((bsz, seq, d), _f32),
        jax.ShapeDtypeStruct((bsz, lp, GLA_QK), _bf16),
        jax.ShapeDtypeStruct((bsz, lp, GLA_QK), _bf16),
        jax.ShapeDtypeStruct((bsz, lp, GLA_QK), _bf16),
        jax.ShapeDtypeStruct((bsz, lp, GLA_VW), _bf16),
        jax.ShapeDtypeStruct((bsz * nt, TILE // GLA_CHUNK, GLA_QK), _f32),
        jax.ShapeDtypeStruct((bsz, seq, GLA_VW), _bf16),
        jax.ShapeDtypeStruct((bsz, seq, MLA_HEADS * HEAD_PAD), _bf16),
        jax.ShapeDtypeStruct((bsz, lp, MLA_HEADS * HEAD_PAD), _bf16),
        jax.ShapeDtypeStruct((bsz, lp, MLA_HEADS * MLA_DV), _bf16),
        jax.ShapeDtypeStruct((bsz, seq, d), _bf16),
        jax.ShapeDtypeStruct((bsz, seq, d), _bf16),
    )
    out_specs = (
        pl.BlockSpec((1, TILE, d), real_map),
        pl.BlockSpec((1, TILE, GLA_QK), pad_map),
        pl.BlockSpec((1, TILE, GLA_QK), pad_map),
        pl.BlockSpec((1, TILE, GLA_QK), pad_map),
        pl.BlockSpec((1, TILE, GLA_VW), pad_map),
        pl.BlockSpec((1, TILE // GLA_CHUNK, GLA_QK), lambda b, t: (b * nt + t, 0, 0)),
        pl.BlockSpec((1, TILE, GLA_VW), real_map),
        pl.BlockSpec((1, TILE, MLA_HEADS * HEAD_PAD), real_map),
        pl.BlockSpec((1, TILE, MLA_HEADS * HEAD_PAD), pad_map),
        pl.BlockSpec((1, TILE, MLA_HEADS * MLA_DV), pad_map),
        pl.BlockSpec((1, TILE, d), real_map),
        pl.BlockSpec((1, TILE, d), real_map),
    )
    (s_emb, qt, kt, ke, gv, dec, sr, qm, km, vm, gate_a, gate_b) = pl.pallas_call(
        _inproj_kernel,
        grid=(bsz, nt),
        in_specs=[
            pl.BlockSpec((1, TILE, d), pad_map),
            _const_spec((1, d)), _const_spec((1, d)),
            _const_spec((d, _W_COLS)),
            _const_spec((LANES, GLA_QK)), _const_spec((1, GLA_QK)),
            _const_spec((1, MLA_Q_RANK)), _const_spec((MLA_Q_RANK, MLA_HEADS * HEAD_PAD)),
            _const_spec((1, MLA_KV_RANK)), _const_spec((MLA_KV_RANK, MLA_HEADS * HEAD_PAD)),
            _const_spec((MLA_KV_RANK, MLA_HEADS * MLA_DV)),
            tab_spec, tab_spec, tab_spec,
            _const_spec((TILE, TILE)),
        ],
        out_specs=out_specs,
        out_shape=out_shapes,
        compiler_params=pltpu.CompilerParams(
            dimension_semantics=("arbitrary", "arbitrary"), vmem_limit_bytes=VMEM_LIMIT),
        name="inproj",
    )(xcat, row2(ln_emb_g), row2(ln_emb_b), w_all, w2p, row2(gla_gate_b[0]),
      row2(mla_q_norm_g[0]), wuq, row2(mla_kv_norm_g[0]), wuk, wuv, cos_t, sin_lo, sin_hi,
      tril_chunks)

    o_gla = pl.pallas_call(
        _gla_kernel,
        grid=(bsz, nt),
        in_specs=[
            pl.BlockSpec((1, TILE, GLA_QK), pad_map),
            pl.BlockSpec((1, TILE, GLA_QK), pad_map),
            pl.BlockSpec((1, TILE, GLA_QK), pad_map),
            pl.BlockSpec((1, TILE, GLA_VW), pad_map),
            pl.BlockSpec((1, TILE // GLA_CHUNK, GLA_QK), lambda b, t: (b * nt + t, 0, 0)),
            pl.BlockSpec((1, TILE, GLA_VW), real_map),
            _const_spec((1, GLA_DV)),
        ],
        out_specs=pl.BlockSpec((1, TILE, GLA_VW), real_map),
        out_shape=jax.ShapeDtypeStruct((bsz, seq, GLA_VW), _bf16),
        scratch_shapes=[pltpu.VMEM((GLA_HEADS, GLA_DV, GLA_DK), _f32)],
        compiler_params=pltpu.CompilerParams(
            dimension_semantics=("arbitrary", "arbitrary"), vmem_limit_bytes=VMEM_LIMIT),
        name="gla",
    )(qt, kt, ke, gv, dec, sr, row2(gla_norm_g[0]))

    pair = 2 * HEAD_PAD
    o_mla = pl.pallas_call(
        _mla_kernel,
        grid=(bsz, MLA_HEADS // 2, seq // Q_TILE),
        in_specs=[
            pl.BlockSpec((1, Q_TILE, pair), lambda b, hp, i: (b, i, hp)),
            pl.BlockSpec((1, lp, pair), lambda b, hp, i: (b, 0, hp)),
            pl.BlockSpec((1, lp, 2 * MLA_DV), lambda b, hp, i: (b, 0, hp)),
        ],
        out_specs=pl.BlockSpec((1, Q_TILE, 2 * MLA_DV), lambda b, hp, i: (b, i, hp)),
        out_shape=jax.ShapeDtypeStruct((bsz, seq, MLA_HEADS * MLA_DV), _bf16),
        scratch_shapes=[pltpu.VMEM((2, Q_TILE, LANES), _f32),
                        pltpu.VMEM((2, Q_TILE, 1), _f32),
                        pltpu.VMEM((2, Q_TILE, 1), _f32)],
        compiler_params=pltpu.CompilerParams(
            dimension_semantics=("arbitrary", "arbitrary", "arbitrary"),
            vmem_limit_bytes=VMEM_LIMIT),
        name="mla",
    )(qm, km, vm)

    rw = jnp.concatenate([router_group_w[0], router_expert_w[0]], axis=1)
    rw = jnp.pad(rw, ((0, 0), (0, LANES - rw.shape[1])))
    rwh = rw.astype(_bf16)
    rwl = (rw - rwh.astype(_f32)).astype(_bf16)
    rb = jnp.concatenate([router_group_b[0], router_expert_b[0]])
    rb = jnp.pad(rb, (0, LANES - rb.shape[0])).reshape(1, LANES)
    mi = np.arange(MERGE_TILE)
    tril_strict = jnp.asarray(mi[:, None] > mi[None, :], dtype=_bf16)
    flat = lambda a: a.reshape(ntok, a.shape[-1])
    tok_spec = lambda w: pl.BlockSpec((MERGE_TILE, w), lambda g: (g, 0))
    s2, info, cnt = pl.pallas_call(
        _merge_kernel,
        grid=(ntok // MERGE_TILE,),
        in_specs=[tok_spec(d), tok_spec(d), tok_spec(d), tok_spec(d), tok_spec(d),
                  _const_spec((d, d)), _const_spec((d, d)), _const_spec((d, d)),
                  _const_spec((1, d)), _const_spec((1, d)),
                  _const_spec((d, LANES)), _const_spec((d, LANES)), _const_spec((1, LANES)),
                  _const_spec((MERGE_TILE, MERGE_TILE))],
        out_specs=(tok_spec(d), tok_spec(LANES), _const_spec((8, LANES))),
        out_shape=(jax.ShapeDtypeStruct((ntok, d), _f32),
                   jax.ShapeDtypeStruct((ntok, LANES), _f32),
                   jax.ShapeDtypeStruct((8, LANES), _f32)),
        scratch_shapes=[pltpu.VMEM((8, LANES), _f32)],
        compiler_params=pltpu.CompilerParams(
            dimension_semantics=("arbitrary",), vmem_limit_bytes=VMEM_LIMIT),
        name="merge_router",
    )(flat(o_gla), flat(o_mla), flat(gate_a), flat(gate_b), flat(s_emb),
      w_branch_gla[0].astype(_bf16), w_branch_mla[0].astype(_bf16), w_out[0].astype(_bf16),
      row2(ln_mix_g[0]), row2(ln_mix_b[0]), rwh, rwl, rb, tril_strict)

    n_tiles = (2 * ntok + N_EXPERTS * (EXPERT_TILE - 1)) // EXPERT_TILE
    n_rows = n_tiles * EXPERT_TILE
    e_idx = info[:, 0:2].astype(jnp.int32)
    wts = info[:, 2:4]
    rank = info[:, 4:6].astype(jnp.int32)
    counts = cnt[0, :N_EXPERTS].astype(jnp.int32)
    padded = ((counts + EXPERT_TILE - 1) // EXPERT_TILE) * EXPERT_TILE
    ends = jnp.cumsum(padded)
    starts = ends - padded
    pos = (starts[e_idx] + rank).reshape(-1)
    tile_expert = jnp.minimum(
        jnp.searchsorted(ends, jnp.arange(n_tiles, dtype=jnp.int32) * EXPERT_TILE, side="right"),
        N_EXPERTS - 1).astype(jnp.int32)
    n_used = (ends[-1:] // EXPERT_TILE).astype(jnp.int32)
    tok_ids = jnp.repeat(jnp.arange(ntok, dtype=jnp.int32), 2)
    src_token = jnp.zeros((n_rows,), jnp.int32).at[pos].set(tok_ids)
    row_w = jnp.zeros((n_rows,), _f32).at[pos].set(wts.reshape(-1)).reshape(n_rows, 1)

    ff = EXPERT_FF
    wg = expert_w_gate[0].reshape(N_EXPERTS, d, ff).astype(_bf16)
    wu = expert_w_up[0].reshape(N_EXPERTS, d, ff).astype(_bf16)
    wd = expert_w_down[0].reshape(N_EXPERTS, ff, d).astype(_bf16)
    ys = pl.pallas_call(
        _expert_kernel,
        grid_spec=pltpu.PrefetchScalarGridSpec(
            num_scalar_prefetch=3,
            grid=(n_tiles,),
            in_specs=[
                pl.BlockSpec(memory_space=pl.ANY),
                pl.BlockSpec((EXPERT_TILE, 1), lambda u, te, nu, src: (u, 0)),
                pl.BlockSpec((1, d, ff), lambda u, te, nu, src: (te[u], 0, 0)),
                pl.BlockSpec((1, d, ff), lambda u, te, nu, src: (te[u], 0, 0)),
                pl.BlockSpec((1, ff, d), lambda u, te, nu, src: (te[u], 0, 0)),
            ],
            out_specs=pl.BlockSpec((EXPERT_TILE, d), lambda u, te, nu, src: (u, 0)),
            scratch_shapes=[pltpu.VMEM((EXPERT_TILE, d), _f32), pltpu.SemaphoreType.DMA(())],
        ),
        out_shape=jax.ShapeDtypeStruct((n_rows, d), _f32),
        compiler_params=pltpu.CompilerParams(
            dimension_semantics=("arbitrary",), vmem_limit_bytes=VMEM_LIMIT),
        name="experts",
    )(tile_expert, n_used, src_token, s2, row_w, wg, wu, wd)

    out = pl.pallas_call(
        _combine_kernel,
        grid_spec=pltpu.PrefetchScalarGridSpec(
            num_scalar_prefetch=1,
            grid=(ntok // COMBINE_TILE,),
            in_specs=[
                pl.BlockSpec((COMBINE_TILE, d), lambda g, p: (g, 0)),
                pl.BlockSpec(memory_space=pl.ANY),
                pl.BlockSpec((1, d), lambda g, p: (0, 0)),
                pl.BlockSpec((1, d), lambda g, p: (0, 0)),
            ],
            out_specs=pl.BlockSpec((COMBINE_TILE, d), lambda g, p: (g, 0)),
            scratch_shapes=[pltpu.VMEM((2, COMBINE_TILE, d), _f32), pltpu.SemaphoreType.DMA(())],
        ),
        out_shape=jax.ShapeDtypeStruct((ntok, d), _f32),
        compiler_params=pltpu.CompilerParams(
            dimension_semantics=("arbitrary",), vmem_limit_bytes=VMEM_LIMIT),
        name="combine_ln",
    )(pos, s2, ys, row2(ln_ffn_g[0]), row2(ln_ffn_b[0]))
    return out.reshape(bsz, seq, d)
```

```python
import functools

import jax
import jax.numpy as jnp
import numpy as np
from jax import lax
from jax.experimental import pallas as pl
from jax.experimental.pallas import tpu as pltpu

D_MODEL = 1024
N_META = 16
GLA_HEADS = 4
GLA_DK = 128
GLA_DV = 256
GLA_QK = GLA_HEADS * GLA_DK
GLA_VW = GLA_HEADS * GLA_DV
GLA_GATE_RANK = 16
GLA_GATE_TAU = 16.0
GLA_CHUNK = 64
MLA_HEADS = 16
MLA_Q_RANK = 384
MLA_KV_RANK = 256
MLA_NOPE = 64
MLA_ROPE = 32
MLA_DV = 64
MLA_QDIM = MLA_NOPE + MLA_ROPE
ROPE_BASE = 10000.0
N_GROUPS = 4
EXPERTS_PER_GROUP = 8
N_EXPERTS = N_GROUPS * EXPERTS_PER_GROUP
EXPERT_FF = 256
DEPTH = 1
ALPHA = (2.0 * DEPTH) ** 0.25
LN_EPS = 1e-5
RMS_EPS = 1e-6

LANES = 128
TILE = 256
PADL = TILE - N_META
HEAD_PAD = LANES
Q_TILE = 512
KV_TILE = 256
MERGE_TILE = 512
EXPERT_TILE = 256
COMBINE_TILE = 256
NEG = -1e30
LOG2E = 1.4426950408889634
VMEM_LIMIT = 56 * 1024 * 1024

_C_Q, _C_K, _C_V, _C_R = 0, 512, 1024, 2048
_C_A = 3072
_C_CQ = _C_A + LANES
_C_CKV = _C_CQ + MLA_Q_RANK
_C_KR = _C_CKV + MLA_KV_RANK
_C_GA = _C_KR + LANES
_C_GB = _C_GA + D_MODEL
_W_COLS = _C_GB + D_MODEL

_f32 = jnp.float32
_bf16 = jnp.bfloat16


def _dot(a, b):
    return jnp.dot(a, b, preferred_element_type=_f32)


def _dot_nt(a, b):
    return lax.dot_general(a, b, (((1,), (1,)), ((), ())), preferred_element_type=_f32)


def _dot_tn(a, b):
    return lax.dot_general(a, b, (((0,), (0,)), ((), ())), preferred_element_type=_f32)


def _layer_norm(x, g, b):
    mu = jnp.mean(x, axis=-1, keepdims=True)
    xc = x - mu
    var = jnp.mean(xc * xc, axis=-1, keepdims=True)
    return xc * lax.rsqrt(var + LN_EPS) * g + b


def _rms_norm(x, g):
    ms = jnp.mean(x * x, axis=-1, keepdims=True)
    return x * lax.rsqrt(ms + RMS_EPS) * g


def _sigmoid(x):
    return 1.0 / (1.0 + jnp.exp(-x))


def _rope(x, cos, sin_lo, sin_hi):
    half = MLA_ROPE // 2
    from_hi = pltpu.roll(x, LANES - half, 1)
    from_lo = pltpu.roll(x, half, 1)
    return x * cos + from_hi * sin_lo + from_lo * sin_hi


def _inproj_kernel(x_ref, lng_ref, lnb_ref, w_ref, w2_ref, gb_ref, qg_ref, wuqt_ref, kvg_ref,
                   wuk_ref, wuvt_ref, cos_ref, sl_ref, sh_ref, cost_ref, sint_ref, tril_ref,
                   s_ref, qt_ref, kt_ref, ke_ref, gv_ref, dec_ref, sr_ref, qm_ref, km_ref,
                   vm_ref, ga_ref, gbt_ref):
    t = pl.program_id(1)
    sn = _layer_norm(x_ref[0], lng_ref[...], lnb_ref[...])
    s_ref[0] = sn
    snb = sn.astype(_bf16)
    row = t * TILE + lax.broadcasted_iota(jnp.int32, (TILE, 1), 0)
    valid = row >= PADL

    a_lr = _dot(snb, w_ref[:, _C_A:_C_A + LANES])
    z = _dot(a_lr.astype(_bf16), w2_ref[...]) + gb_ref[...]
    la = (jnp.minimum(z, 0.0) - jnp.log1p(jnp.exp(-jnp.abs(z)))) * (1.0 / GLA_GATE_TAU)
    la = jnp.where(valid, la, 0.0)
    hi = la.astype(_bf16)
    r1 = la - hi.astype(_f32)
    mid = r1.astype(_bf16)
    lo = (r1 - mid.astype(_f32)).astype(_bf16)
    tril = tril_ref[...]
    bc = _dot(tril, hi) + _dot(tril, mid) + _dot(tril, lo)
    n_chunks = TILE // GLA_CHUNK
    lasts = [bc[c * GLA_CHUNK + GLA_CHUNK - 1:(c + 1) * GLA_CHUNK, :] for c in range(n_chunks)]
    for c in range(n_chunks):
        dec_ref[0, c:c + 1, :] = jnp.exp(lasts[c])
    b_last = jnp.concatenate(
        [jnp.broadcast_to(l, (GLA_CHUNK, GLA_QK)) for l in lasts], axis=0)
    gq = _dot(snb, w_ref[:, _C_Q:_C_Q + GLA_QK])
    gk = jnp.where(valid, _dot(snb, w_ref[:, _C_K:_C_K + GLA_QK]), 0.0)
    qt_ref[0] = (gq * (GLA_DK ** -0.5) * jnp.exp(bc)).astype(_bf16)
    kt_ref[0] = (gk * jnp.exp(-bc)).astype(_bf16)
    ke_ref[0] = (gk * jnp.exp(b_last - bc)).astype(_bf16)
    gv_ref[0] = jnp.where(valid, _dot(snb, w_ref[:, _C_V:_C_V + GLA_VW]), 0.0).astype(_bf16)
    r = _dot(snb, w_ref[:, _C_R:_C_R + GLA_VW])
    sr_ref[0] = (r * _sigmoid(r)).astype(_bf16)

    cos = cos_ref[...]
    sl = sl_ref[...]
    sh = sh_ref[...]
    cq = _dot(snb, w_ref[:, _C_CQ:_C_CQ + MLA_Q_RANK])
    cqn = _rms_norm(cq, qg_ref[...]).astype(_bf16)
    qft = _dot_nt(wuqt_ref[...], cqn)
    scale = (MLA_QDIM ** -0.5) * LOG2E
    cost = cost_ref[...]
    sint = sint_ref[...]
    half = MLA_ROPE // 2
    for h in range(MLA_HEADS):
        base = h * HEAD_PAD
        x1 = qft[base + MLA_NOPE:base + MLA_NOPE + half]
        x2 = qft[base + MLA_NOPE + half:base + MLA_QDIM]
        qm_ref[0, base:base + MLA_NOPE] = (qft[base:base + MLA_NOPE] * scale).astype(_bf16)
        qm_ref[0, base + MLA_NOPE:base + MLA_NOPE + half] = ((x1 * cost - x2 * sint) * scale).astype(_bf16)
        qm_ref[0, base + MLA_NOPE + half:base + MLA_QDIM] = ((x1 * sint + x2 * cost) * scale).astype(_bf16)
        qm_ref[0, base + MLA_QDIM:base + HEAD_PAD] = jnp.zeros((HEAD_PAD - MLA_QDIM, TILE), _bf16)
    ckv = _dot(snb, w_ref[:, _C_CKV:_C_CKV + MLA_KV_RANK])
    ckvn = _rms_norm(ckv, kvg_ref[...]).astype(_bf16)
    kf = _dot(ckvn, wuk_ref[...])
    kr = _rope(_dot(snb, w_ref[:, _C_KR:_C_KR + LANES]), cos, sl, sh)
    for h in range(MLA_HEADS):
        km_ref[0, :, h * HEAD_PAD:(h + 1) * HEAD_PAD] = (
            kf[:, h * HEAD_PAD:(h + 1) * HEAD_PAD] + kr).astype(_bf16)
    vm_ref[0] = _dot_nt(wuvt_ref[...], ckvn).astype(_bf16)

    ga_ref[0] = _sigmoid(_dot(snb, w_ref[:, _C_GA:_C_GA + D_MODEL])).astype(_bf16)
    gbt_ref[0] = _sigmoid(_dot(snb, w_ref[:, _C_GB:_C_GB + D_MODEL])).astype(_bf16)


def _gla_kernel(qt_ref, kt_ref, ke_ref, gv_ref, dec_ref, sr_ref, ng_ref, o_ref, st_ref):
    t = pl.program_id(1)

    @pl.when(t == 0)
    def _():
        st_ref[...] = jnp.zeros_like(st_ref)

    ri = lax.broadcasted_iota(jnp.int32, (GLA_CHUNK, GLA_CHUNK), 0)
    ci = lax.broadcasted_iota(jnp.int32, (GLA_CHUNK, GLA_CHUNK), 1)
    causal = ri >= ci
    ng = ng_ref[...]
    for c in range(TILE // GLA_CHUNK):
        rows = slice(c * GLA_CHUNK, (c + 1) * GLA_CHUNK)
        for h in range(GLA_HEADS):
            kc = slice(h * GLA_DK, (h + 1) * GLA_DK)
            vc = slice(h * GLA_DV, (h + 1) * GLA_DV)
            q = qt_ref[0, rows, kc]
            k = kt_ref[0, rows, kc]
            e = ke_ref[0, rows, kc]
            v = gv_ref[0, rows, vc]
            st = st_ref[h]
            att = jnp.where(causal, _dot_nt(q, k), 0.0)
            o = _dot(att.astype(_bf16), v) + _dot_nt(q, st.astype(_bf16))
            st_ref[h] = st * dec_ref[0, c:c + 1, kc] + _dot_tn(v, e)
            o = _rms_norm(o, ng) * sr_ref[0, rows, vc].astype(_f32)
            o_ref[0, rows, vc] = o.astype(_bf16)


def _mla_kernel(q_ref, k_ref, v_ref, o_ref, acc_ref):
    i = pl.program_id(2)
    heads = 2
    k_row = lax.broadcasted_iota(jnp.int32, (KV_TILE, Q_TILE), 0)
    q_pos = (TILE + i * Q_TILE) + lax.broadcasted_iota(jnp.int32, (KV_TILE, Q_TILE), 1)
    n_full = 2 * i + 1
    acc_ref[...] = jnp.zeros_like(acc_ref)

    def step(j, carry, mask):
        start = pl.multiple_of(j * KV_TILE, KV_TILE)
        out = []
        for h in range(heads):
            m_prev, l_prev = carry[h]
            kb = k_ref[0, pl.ds(start, KV_TILE), h * HEAD_PAD:(h + 1) * HEAD_PAD]
            vb = v_ref[0, h * MLA_DV:(h + 1) * MLA_DV, pl.ds(start, KV_TILE)]
            s = _dot(kb, q_ref[0, h * HEAD_PAD:(h + 1) * HEAD_PAD, :])
            if mask is not None:
                s = jnp.where(mask(j), s, NEG)
            m_new = jnp.maximum(m_prev, jnp.max(s, axis=0, keepdims=True))
            alpha = jnp.exp2(m_prev - m_new)
            p = jnp.exp2(s - m_new)
            l_new = alpha * l_prev + jnp.sum(p, axis=0, keepdims=True)
            acc_ref[h] = alpha * acc_ref[h] + _dot(vb, p.astype(_bf16))
            out.append((m_new, l_new))
        return tuple(out)

    init = tuple((jnp.full((1, Q_TILE), NEG, _f32), jnp.zeros((1, Q_TILE), _f32))
                 for _ in range(heads))
    carry = step(0, init, lambda j: k_row >= PADL)
    carry = lax.fori_loop(1, n_full, lambda j, c: step(j, c, None), carry)
    causal = lambda j: (j * KV_TILE + k_row) <= q_pos
    carry = step(n_full, carry, causal)
    carry = step(n_full + 1, carry, causal)
    ot = jnp.concatenate([acc_ref[h] / carry[h][1] for h in range(heads)], axis=0)
    o_ref[0] = ot.T.astype(_bf16)


def _merge_kernel(og_ref, om_ref, ga_ref, gbt_ref, s_ref, wbg_ref, wbm_ref, wo_ref, lng_ref,
                  lnb_ref, rwh_ref, rwl_ref, rb_ref, tril_ref,
                  s2_ref, info_ref, cnt_ref, carry_ref):
    step = pl.program_id(0)

    @pl.when(step == 0)
    def _():
        carry_ref[...] = jnp.zeros_like(carry_ref)

    ba = _dot(og_ref[...], wbg_ref[...])
    bb = _dot(om_ref[...], wbm_ref[...])
    merged = ga_ref[...].astype(_f32) * ba + gbt_ref[...].astype(_f32) * bb
    y = ALPHA * s_ref[...] + _dot(merged.astype(_bf16), wo_ref[...])
    s2 = _layer_norm(y, lng_ref[...], lnb_ref[...])
    s2_ref[...] = s2

    xh = s2.astype(_bf16)
    xl = (s2 - xh.astype(_f32)).astype(_bf16)
    logits = _dot(xh, rwh_ref[...]) + _dot(xl, rwh_ref[...]) + _dot(xh, rwl_ref[...]) + rb_ref[...]
    lane = lax.broadcasted_iota(jnp.int32, (MERGE_TILE, LANES), 1)
    is_g = lane < N_GROUPS
    gl = jnp.where(is_g, logits, NEG)
    gmax = jnp.max(gl, axis=-1, keepdims=True)
    gidx = jnp.min(jnp.where(gl == gmax, lane, LANES), axis=-1, keepdims=True)
    p_g = 1.0 / jnp.sum(jnp.where(is_g, jnp.exp(gl - gmax), 0.0), axis=-1, keepdims=True)
    lo = N_GROUPS + EXPERTS_PER_GROUP * gidx
    el = jnp.where((lane >= lo) & (lane < lo + EXPERTS_PER_GROUP), logits, NEG)
    v1 = jnp.max(el, axis=-1, keepdims=True)
    i1 = jnp.min(jnp.where(el == v1, lane, LANES), axis=-1, keepdims=True)
    el2 = jnp.where(lane == i1, NEG, el)
    v2 = jnp.max(el2, axis=-1, keepdims=True)
    i2 = jnp.min(jnp.where(el2 == v2, lane, LANES), axis=-1, keepdims=True)
    tt = jnp.exp(v2 - v1)
    p1 = 1.0 / (1.0 + tt)
    p2 = tt / (1.0 + tt)
    e1 = i1 - N_GROUPS
    e2 = i2 - N_GROUPS
    hit1 = lane == e1
    hit2 = lane == e2
    onehot = jnp.where(hit1 | hit2, 1.0, 0.0)
    before = _dot(tril_ref[...], onehot.astype(_bf16)) + carry_ref[0:1, :]
    r1 = jnp.sum(jnp.where(hit1, before, 0.0), axis=-1, keepdims=True)
    r2 = jnp.sum(jnp.where(hit2, before, 0.0), axis=-1, keepdims=True)
    new_carry = carry_ref[0:1, :] + jnp.sum(onehot, axis=0, keepdims=True)
    carry_ref[...] = jnp.broadcast_to(new_carry, carry_ref.shape)
    cnt_ref[...] = jnp.broadcast_to(new_carry, cnt_ref.shape)
    info = jnp.where(lane == 0, e1.astype(_f32),
           jnp.where(lane == 1, e2.astype(_f32),
           jnp.where(lane == 2, p_g * p1,
           jnp.where(lane == 3, p_g * p2,
           jnp.where(lane == 4, r1,
           jnp.where(lane == 5, r2, 0.0))))))
    info_ref[...] = info


def _expert_kernel(te_ref, nused_ref, src_ref, s2_hbm, rw_ref, wg_ref, wu_ref, wd_ref, o_ref,
                   buf_ref, sem):
    u = pl.program_id(0)

    def row_copy(r, tok):
        return pltpu.make_async_copy(s2_hbm.at[pl.ds(tok, 1)], buf_ref.at[pl.ds(r, 1)], sem)

    @pl.when(u < nused_ref[0])
    def _():
        def issue(r, c):
            row_copy(r, src_ref[u * EXPERT_TILE + r]).start()
            return c

        lax.fori_loop(0, EXPERT_TILE, issue, 0)

        def wait(r, c):
            row_copy(r, 0).wait()
            return c

        lax.fori_loop(0, EXPERT_TILE, wait, 0)
        x = buf_ref[...].astype(_bf16)
        a = _dot(x, wg_ref[0])
        up = _dot(x, wu_ref[0])
        hid = a * _sigmoid(a) * up * rw_ref[...]
        o_ref[...] = _dot(hid.astype(_bf16), wd_ref[0])

    @pl.when(u >= nused_ref[0])
    def _():
        o_ref[...] = jnp.zeros_like(o_ref)


def _combine_kernel(pos_ref, s2_ref, ys_hbm, lng_ref, lnb_ref, o_ref, buf_ref, sem):
    step = pl.program_id(0)

    def row_copy(k, r, src_row):
        return pltpu.make_async_copy(ys_hbm.at[pl.ds(src_row, 1)], buf_ref.at[k, pl.ds(r, 1)], sem)

    def issue(r, c):
        tok = step * COMBINE_TILE + r
        row_copy(0, r, pos_ref[2 * tok]).start()
        row_copy(1, r, pos_ref[2 * tok + 1]).start()
        return c

    lax.fori_loop(0, COMBINE_TILE, issue, 0)

    def wait(r, c):
        row_copy(0, r, 0).wait()
        row_copy(1, r, 0).wait()
        return c

    lax.fori_loop(0, COMBINE_TILE, wait, 0)
    y = ALPHA * s2_ref[...] + buf_ref[0] + buf_ref[1]
    o_ref[...] = _layer_norm(y, lng_ref[...], lnb_ref[...])


def _const_spec(shape):
    nd = len(shape)
    return pl.BlockSpec(shape, lambda *_: (0,) * nd)


def _rope_tables(lp):
    pos = jnp.maximum(jnp.arange(lp, dtype=_f32) - PADL, 0.0)
    inv_freq = ROPE_BASE ** (-jnp.arange(0, MLA_ROPE, 2, dtype=_f32) / MLA_ROPE)
    ang = pos[:, None] * inv_freq[None, :]
    cos, sin = jnp.cos(ang), jnp.sin(ang)
    half = MLA_ROPE // 2
    ones = jnp.ones((lp, MLA_NOPE), _f32)
    zeros_n = jnp.zeros((lp, MLA_NOPE), _f32)
    zeros_h = jnp.zeros((lp, half), _f32)
    tail1 = jnp.ones((lp, LANES - MLA_QDIM), _f32)
    tail0 = jnp.zeros((lp, LANES - MLA_QDIM), _f32)
    cos_t = jnp.concatenate([ones, cos, cos, tail1], axis=1)
    sin_lo = jnp.concatenate([zeros_n, -sin, zeros_h, tail0], axis=1)
    sin_hi = jnp.concatenate([zeros_n, zeros_h, sin, tail0], axis=1)
    return cos_t, sin_lo, sin_hi, cos.T, sin.T


def _pad_heads(w, width):
    k = w.shape[0]
    w = w.reshape(k, MLA_HEADS, width)
    w = jnp.pad(w, ((0, 0), (0, 0), (0, HEAD_PAD - width)))
    return w.reshape(k, MLA_HEADS * HEAD_PAD)


def kernel(x, meta_tokens, ln_emb_g, ln_emb_b, w_in, gla_gate_w2, gla_gate_b, gla_norm_g, mla_q_norm_g, mla_w_uq, mla_kv_norm_g, mla_w_uk, mla_w_uv, w_branch_gla, w_branch_mla, w_out, ln_mix_g, ln_mix_b, router_group_w, router_group_b, router_expert_w, router_expert_b, expert_w_gate, expert_w_up, expert_w_down, ln_ffn_g, ln_ffn_b):
    bsz, seq, d = x.shape
    assert d == D_MODEL and seq % Q_TILE == 0 and w_in.shape[0] == DEPTH == 1
    lp = PADL + N_META + seq
    nt = lp // TILE
    ntok = bsz * seq
    row2 = lambda v: v.reshape(1, -1).astype(_f32)

    xcat = jnp.concatenate([
        jnp.zeros((bsz, PADL, d), x.dtype),
        jnp.broadcast_to(meta_tokens[None].astype(x.dtype), (bsz, N_META, d)), x], axis=1)
    wi = w_in[0]
    o_a = 2 * GLA_QK + 2 * GLA_VW
    o_cq = o_a + GLA_GATE_RANK
    o_ckv = o_cq + MLA_Q_RANK
    o_kr = o_ckv + MLA_KV_RANK
    o_ga = o_kr + MLA_ROPE
    w_a = jnp.pad(wi[:, o_a:o_cq], ((0, 0), (0, LANES - GLA_GATE_RANK)))
    w_kr = jnp.pad(wi[:, o_kr:o_ga], ((0, 0), (MLA_NOPE, LANES - MLA_QDIM)))
    w_all = jnp.concatenate([wi[:, :o_a], w_a, wi[:, o_cq:o_kr], w_kr, wi[:, o_ga:]], axis=1).astype(_bf16)
    assert w_all.shape == (d, _W_COLS)
    w2p = jnp.pad(gla_gate_w2[0], ((0, LANES - GLA_GATE_RANK), (0, 0))).astype(_bf16)
    wuqt = _pad_heads(mla_w_uq[0], MLA_QDIM).T.astype(_bf16)
    wuk = _pad_heads(mla_w_uk[0], MLA_NOPE).astype(_bf16)
    wuvt = mla_w_uv[0].T.astype(_bf16)
    cos_t, sin_lo, sin_hi, cos_tr, sin_tr = _rope_tables(lp)
    blk = np.arange(TILE)
    tril_chunks = jnp.asarray(
        ((blk[:, None] >= blk[None, :]) & (blk[:, None] // GLA_CHUNK == blk[None, :] // GLA_CHUNK)),
        dtype=_bf16)

    pad_map = lambda b, t: (b, t, 0)
    real_map = lambda b, t: (b, jnp.maximum(t - 1, 0), 0)
    tab_spec = pl.BlockSpec((TILE, LANES), lambda b, t: (t, 0))
    tabt_spec = pl.BlockSpec((MLA_ROPE // 2, TILE), lambda b, t: (0, t))
    real_map_t = lambda b, t: (b, 0, jnp.maximum(t - 1, 0))
    out_shapes = (
        jax.ShapeDtypeStruct((bsz, seq, d), _f32),
        jax.ShapeDtypeStruct((bsz, lp, GLA_QK), _bf16),
        jax.ShapeDtypeStruct((bsz, lp, GLA_QK), _bf16),
        jax.ShapeDtypeStruct((bsz, lp, GLA_QK), _bf16),
        jax.ShapeDtypeStruct((bsz, lp, GLA_VW), _bf16),
        jax.ShapeDtypeStruct((bsz * nt, TILE // GLA_CHUNK, GLA_QK), _f32),
        jax.ShapeDtypeStruct((bsz, seq, GLA_VW), _bf16),
        jax.ShapeDtypeStruct((bsz, MLA_HEADS * HEAD_PAD, seq), _bf16),
        jax.ShapeDtypeStruct((bsz, lp, MLA_HEADS * HEAD_PAD), _bf16),
        jax.ShapeDtypeStruct((bsz, MLA_HEADS * MLA_DV, lp), _bf16),
        jax.ShapeDtypeStruct((bsz, seq, d), _bf16),
        jax.ShapeDtypeStruct((bsz, seq, d), _bf16),
    )
    out_specs = (
        pl.BlockSpec((1, TILE, d), real_map),
        pl.BlockSpec((1, TILE, GLA_QK), pad_map),
        pl.BlockSpec((1, TILE, GLA_QK), pad_map),
        pl.BlockSpec((1, TILE, GLA_QK), pad_map),
        pl.BlockSpec((1, TILE, GLA_VW), pad_map),
        pl.BlockSpec((1, TILE // GLA_CHUNK, GLA_QK), lambda b, t: (b * nt + t, 0, 0)),
        pl.BlockSpec((1, TILE, GLA_VW), real_map),
        pl.BlockSpec((1, MLA_HEADS * HEAD_PAD, TILE), real_map_t),
        pl.BlockSpec((1, TILE, MLA_HEADS * HEAD_PAD), pad_map),
        pl.BlockSpec((1, MLA_HEADS * MLA_DV, TILE), lambda b, t: (b, 0, t)),
        pl.BlockSpec((1, TILE, d), real_map),
        pl.BlockSpec((1, TILE, d), real_map),
    )
    (s_emb, qt, kt, ke, gv, dec, sr, qm, km, vm, gate_a, gate_b) = pl.pallas_call(
        _inproj_kernel,
        grid=(bsz, nt),
        in_specs=[
            pl.BlockSpec((1, TILE, d), pad_map),
            _const_spec((1, d)), _const_spec((1, d)),
            _const_spec((d, _W_COLS)),
            _const_spec((LANES, GLA_QK)), _const_spec((1, GLA_QK)),
            _const_spec((1, MLA_Q_RANK)), _const_spec((MLA_HEADS * HEAD_PAD, MLA_Q_RANK)),
            _const_spec((1, MLA_KV_RANK)), _const_spec((MLA_KV_RANK, MLA_HEADS * HEAD_PAD)),
            _const_spec((MLA_HEADS * MLA_DV, MLA_KV_RANK)),
            tab_spec, tab_spec, tab_spec, tabt_spec, tabt_spec,
            _const_spec((TILE, TILE)),
        ],
        out_specs=out_specs,
        out_shape=out_shapes,
        compiler_params=pltpu.CompilerParams(
            dimension_semantics=("arbitrary", "arbitrary"), vmem_limit_bytes=VMEM_LIMIT),
        name="inproj",
    )(xcat, row2(ln_emb_g), row2(ln_emb_b), w_all, w2p, row2(gla_gate_b[0]),
      row2(mla_q_norm_g[0]), wuqt, row2(mla_kv_norm_g[0]), wuk, wuvt, cos_t, sin_lo, sin_hi,
      cos_tr, sin_tr, tril_chunks)

    o_gla = pl.pallas_call(
        _gla_kernel,
        grid=(bsz, nt),
        in_specs=[
            pl.BlockSpec((1, TILE, GLA_QK), pad_map),
            pl.BlockSpec((1, TILE, GLA_QK), pad_map),
            pl.BlockSpec((1, TILE, GLA_QK), pad_map),
            pl.BlockSpec((1, TILE, GLA_VW), pad_map),
            pl.BlockSpec((1, TILE // GLA_CHUNK, GLA_QK), lambda b, t: (b * nt + t, 0, 0)),
            pl.BlockSpec((1, TILE, GLA_VW), real_map),
            _const_spec((1, GLA_DV)),
        ],
        out_specs=pl.BlockSpec((1, TILE, GLA_VW), real_map),
        out_shape=jax.ShapeDtypeStruct((bsz, seq, GLA_VW), _bf16),
        scratch_shapes=[pltpu.VMEM((GLA_HEADS, GLA_DV, GLA_DK), _f32)],
        compiler_params=pltpu.CompilerParams(
            dimension_semantics=("arbitrary", "arbitrary"), vmem_limit_bytes=VMEM_LIMIT),
        name="gla",
    )(qt, kt, ke, gv, dec, sr, row2(gla_norm_g[0]))

    pair = 2 * HEAD_PAD
    o_mla = pl.pallas_call(
        _mla_kernel,
        grid=(bsz, MLA_HEADS // 2, seq // Q_TILE),
        in_specs=[
            pl.BlockSpec((1, pair, Q_TILE), lambda b, hp, i: (b, hp, i)),
            pl.BlockSpec((1, lp, pair), lambda b, hp, i: (b, 0, hp)),
            pl.BlockSpec((1, 2 * MLA_DV, lp), lambda b, hp, i: (b, hp, 0)),
        ],
        out_specs=pl.BlockSpec((1, Q_TILE, 2 * MLA_DV), lambda b, hp, i: (b, i, hp)),
        out_shape=jax.ShapeDtypeStruct((bsz, seq, MLA_HEADS * MLA_DV), _bf16),
        scratch_shapes=[pltpu.VMEM((2, MLA_DV, Q_TILE), _f32)],
        compiler_params=pltpu.CompilerParams(
            dimension_semantics=("arbitrary", "arbitrary", "arbitrary"),
            vmem_limit_bytes=VMEM_LIMIT),
        name="mla",
    )(qm, km, vm)

    rw = jnp.concatenate([router_group_w[0], router_expert_w[0]], axis=1)
    rw = jnp.pad(rw, ((0, 0), (0, LANES - rw.shape[1])))
    rwh = rw.astype(_bf16)
    rwl = (rw - rwh.astype(_f32)).astype(_bf16)
    rb = jnp.concatenate([router_group_b[0], router_expert_b[0]])
    rb = jnp.pad(rb, (0, LANES - rb.shape[0])).reshape(1, LANES)
    mi = np.arange(MERGE_TILE)
    tril_strict = jnp.asarray(mi[:, None] > mi[None, :], dtype=_bf16)
    flat = lambda a: a.reshape(ntok, a.shape[-1])
    tok_spec = lambda w: pl.BlockSpec((MERGE_TILE, w), lambda g: (g, 0))
    s2, info, cnt = pl.pallas_call(
        _merge_kernel,
        grid=(ntok // MERGE_TILE,),
        in_specs=[tok_spec(d), tok_spec(d), tok_spec(d), tok_spec(d), tok_spec(d),
                  _const_spec((d, d)), _const_spec((d, d)), _const_spec((d, d)),
                  _const_spec((1, d)), _const_spec((1, d)),
                  _const_spec((d, LANES)), _const_spec((d, LANES)), _const_spec((1, LANES)),
                  _const_spec((MERGE_TILE, MERGE_TILE))],
        out_specs=(tok_spec(d), tok_spec(LANES), _const_spec((8, LANES))),
        out_shape=(jax.ShapeDtypeStruct((ntok, d), _f32),
                   jax.ShapeDtypeStruct((ntok, LANES), _f32),
                   jax.ShapeDtypeStruct((8, LANES), _f32)),
        scratch_shapes=[pltpu.VMEM((8, LANES), _f32)],
        compiler_params=pltpu.CompilerParams(
            dimension_semantics=("arbitrary",), vmem_limit_bytes=VMEM_LIMIT),
        name="merge_router",
    )(flat(o_gla), flat(o_mla), flat(gate_a), flat(gate_b), flat(s_emb),
      w_branch_gla[0].astype(_bf16), w_branch_mla[0].astype(_bf16), w_out[0].astype(_bf16),
      row2(ln_mix_g[0]), row2(ln_mix_b[0]), rwh, rwl, rb, tril_strict)

    n_tiles = (2 * ntok + N_EXPERTS * (EXPERT_TILE - 1)) // EXPERT_TILE
    n_rows = n_tiles * EXPERT_TILE
    e_idx = info[:, 0:2].astype(jnp.int32)
    wts = info[:, 2:4]
    rank = info[:, 4:6].astype(jnp.int32)
    counts = cnt[0, :N_EXPERTS].astype(jnp.int32)
    padded = ((counts + EXPERT_TILE - 1) // EXPERT_TILE) * EXPERT_TILE
    ends = jnp.cumsum(padded)
    starts = ends - padded
    pos = (starts[e_idx] + rank).reshape(-1)
    tile_start = jnp.arange(n_tiles, dtype=jnp.int32) * EXPERT_TILE
    tile_expert = jnp.minimum(
        jnp.sum((ends[None, :] <= tile_start[:, None]).astype(jnp.int32), axis=1), N_EXPERTS - 1)
    n_used = (ends[-1:] // EXPERT_TILE).astype(jnp.int32)
    tok_ids = jnp.repeat(jnp.arange(ntok, dtype=jnp.int32), 2)
    src_token = jnp.zeros((n_rows,), jnp.int32).at[pos].set(tok_ids)
    row_w = jnp.zeros((n_rows,), _f32).at[pos].set(wts.reshape(-1)).reshape(n_rows, 1)

    ff = EXPERT_FF
    wg = expert_w_gate[0].reshape(N_EXPERTS, d, ff).astype(_bf16)
    wu = expert_w_up[0].reshape(N_EXPERTS, d, ff).astype(_bf16)
    wd = expert_w_down[0].reshape(N_EXPERTS, ff, d).astype(_bf16)
    ys = pl.pallas_call(
        _expert_kernel,
        grid_spec=pltpu.PrefetchScalarGridSpec(
            num_scalar_prefetch=3,
            grid=(n_tiles,),
            in_specs=[
                pl.BlockSpec(memory_space=pl.ANY),
                pl.BlockSpec((EXPERT_TILE, 1), lambda u, te, nu, src: (u, 0)),
                pl.BlockSpec((1, d, ff), lambda u, te, nu, src: (te[u], 0, 0)),
                pl.BlockSpec((1, d, ff), lambda u, te, nu, src: (te[u], 0, 0)),
                pl.BlockSpec((1, ff, d), lambda u, te, nu, src: (te[u], 0, 0)),
            ],
            out_specs=pl.BlockSpec((EXPERT_TILE, d), lambda u, te, nu, src: (u, 0)),
            scratch_shapes=[pltpu.VMEM((EXPERT_TILE, d), _f32), pltpu.SemaphoreType.DMA(())],
        ),
        out_shape=jax.ShapeDtypeStruct((n_rows, d), _f32),
        compiler_params=pltpu.CompilerParams(
            dimension_semantics=("arbitrary",), vmem_limit_bytes=VMEM_LIMIT),
        name="experts",
    )(tile_expert, n_used, src_token, s2, row_w, wg, wu, wd)

    out = pl.pallas_call(
        _combine_kernel,
        grid_spec=pltpu.PrefetchScalarGridSpec(
            num_scalar_prefetch=1,
            grid=(ntok // COMBINE_TILE,),
            in_specs=[
                pl.BlockSpec((COMBINE_TILE, d), lambda g, p: (g, 0)),
                pl.BlockSpec(memory_space=pl.ANY),
                pl.BlockSpec((1, d), lambda g, p: (0, 0)),
                pl.BlockSpec((1, d), lambda g, p: (0, 0)),
            ],
            out_specs=pl.BlockSpec((COMBINE_TILE, d), lambda g, p: (g, 0)),
            scratch_shapes=[pltpu.VMEM((2, COMBINE_TILE, d), _f32), pltpu.SemaphoreType.DMA(())],
        ),
        out_shape=jax.ShapeDtypeStruct((ntok, d), _f32),
        compiler_params=pltpu.CompilerParams(
            dimension_semantics=("arbitrary",), vmem_limit_bytes=VMEM_LIMIT),
        name="combine_ln",
    )(pos, s2, ys, row2(ln_ffn_g[0]), row2(ln_ffn_b[0]))
    return out.reshape(bsz, seq, d)
```

```python
import jax
import jax.numpy as jnp
import numpy as np
from jax import lax
from jax.experimental import pallas as pl
from jax.experimental.pallas import tpu as pltpu

D_MODEL = 1024
N_META = 16
GLA_HEADS = 4
GLA_DK = 128
GLA_DV = 256
GLA_QK = GLA_HEADS * GLA_DK
GLA_VW = GLA_HEADS * GLA_DV
GLA_GATE_RANK = 16
GLA_GATE_TAU = 16.0
GLA_CHUNK = 64
MLA_HEADS = 16
MLA_Q_RANK = 384
MLA_KV_RANK = 256
MLA_NOPE = 64
MLA_ROPE = 32
MLA_DV = 64
MLA_QDIM = MLA_NOPE + MLA_ROPE
ROPE_BASE = 10000.0
N_GROUPS = 4
EXPERTS_PER_GROUP = 8
N_EXPERTS = N_GROUPS * EXPERTS_PER_GROUP
EXPERT_FF = 256
DEPTH = 1
ALPHA = (2.0 * DEPTH) ** 0.25
LN_EPS = 1e-5
RMS_EPS = 1e-6

LANES = 128
TILE = 256
PADL = TILE - N_META
HEAD_PAD = LANES
BF16_ROWS = 16
Q_TILE = 512
KV_TILE = Q_TILE
V_AUG = MLA_DV + BF16_ROWS
MERGE_TILE = 512
EXPERT_TILE = 256
DISPATCH_TILE = 512
COMBINE_TILE = 256
NEG = -1e30
LOG2E = 1.4426950408889634
VMEM_LIMIT = 56 * 1024 * 1024

_C_Q, _C_K, _C_V, _C_R = 0, 512, 1024, 2048
_C_A = 3072
_C_CQ = _C_A + LANES
_C_CKV = _C_CQ + MLA_Q_RANK
_C_KR = _C_CKV + MLA_KV_RANK
_C_GA = _C_KR + LANES
_C_GB = _C_GA + D_MODEL
_W_COLS = _C_GB + D_MODEL

_f32 = jnp.float32
_bf16 = jnp.bfloat16


def _dot(a, b):
    return jnp.dot(a, b, preferred_element_type=_f32)


def _dot_nt(a, b):
    return lax.dot_general(a, b, (((1,), (1,)), ((), ())), preferred_element_type=_f32)


def _dot_tn(a, b):
    return lax.dot_general(a, b, (((0,), (0,)), ((), ())), preferred_element_type=_f32)


def _layer_norm(x, g, b):
    mu = jnp.mean(x, axis=-1, keepdims=True)
    xc = x - mu
    var = jnp.mean(xc * xc, axis=-1, keepdims=True)
    return xc * lax.rsqrt(var + LN_EPS) * g + b


def _rms_norm(x, g):
    ms = jnp.mean(x * x, axis=-1, keepdims=True)
    return x * lax.rsqrt(ms + RMS_EPS) * g


def _sigmoid(x):
    return 1.0 / (1.0 + jnp.exp(-x))


def _rope(x, cos, sin_lo, sin_hi):
    half = MLA_ROPE // 2
    from_hi = pltpu.roll(x, LANES - half, 1)
    from_lo = pltpu.roll(x, half, 1)
    return x * cos + from_hi * sin_lo + from_lo * sin_hi


def _inproj_kernel(x_ref, lng_ref, lnb_ref, w_ref, w2_ref, gb_ref, qg_ref, wuqt_ref, kvg_ref,
                   wuk_ref, wuvt_ref, cos_ref, sl_ref, sh_ref, cost_ref, sint_ref, tril_ref,
                   s_ref, qt_ref, kt_ref, ke_ref, gv_ref, dec_ref, sr_ref, qm_ref, km_ref,
                   vm_ref, ga_ref, gbt_ref):
    t = pl.program_id(1)
    sn = _layer_norm(x_ref[0], lng_ref[...], lnb_ref[...])
    s_ref[0] = sn
    snb = sn.astype(_bf16)
    row = t * TILE + lax.broadcasted_iota(jnp.int32, (TILE, 1), 0)
    valid = row >= PADL

    a_lr = _dot(snb, w_ref[:, _C_A:_C_A + LANES])
    z = _dot(a_lr.astype(_bf16), w2_ref[...]) + gb_ref[...]
    la = (jnp.minimum(z, 0.0) - jnp.log1p(jnp.exp(-jnp.abs(z)))) * (1.0 / GLA_GATE_TAU)
    la = jnp.where(valid, la, 0.0)
    hi = la.astype(_bf16)
    r1 = la - hi.astype(_f32)
    mid = r1.astype(_bf16)
    lo = (r1 - mid.astype(_f32)).astype(_bf16)
    tril = tril_ref[...]
    bc = _dot(tril, hi) + _dot(tril, mid) + _dot(tril, lo)
    n_chunks = TILE // GLA_CHUNK
    lasts = [bc[c * GLA_CHUNK + GLA_CHUNK - 1:(c + 1) * GLA_CHUNK, :] for c in range(n_chunks)]
    for c in range(n_chunks):
        dec_ref[0, c:c + 1, :] = jnp.exp(lasts[c])
    b_last = jnp.concatenate(
        [jnp.broadcast_to(l, (GLA_CHUNK, GLA_QK)) for l in lasts], axis=0)
    gq = _dot(snb, w_ref[:, _C_Q:_C_Q + GLA_QK])
    gk = jnp.where(valid, _dot(snb, w_ref[:, _C_K:_C_K + GLA_QK]), 0.0)
    qt_ref[0] = (gq * (GLA_DK ** -0.5) * jnp.exp(bc)).astype(_bf16)
    kt_ref[0] = (gk * jnp.exp(-bc)).astype(_bf16)
    ke_ref[0] = (gk * jnp.exp(b_last - bc)).astype(_bf16)
    gv_ref[0] = jnp.where(valid, _dot(snb, w_ref[:, _C_V:_C_V + GLA_VW]), 0.0).astype(_bf16)
    r = _dot(snb, w_ref[:, _C_R:_C_R + GLA_VW])
    sr_ref[0] = (r * _sigmoid(r)).astype(_bf16)

    cos = cos_ref[...]
    sl = sl_ref[...]
    sh = sh_ref[...]
    cq = _dot(snb, w_ref[:, _C_CQ:_C_CQ + MLA_Q_RANK])
    cqn = _rms_norm(cq, qg_ref[...]).astype(_bf16)
    qft = _dot_nt(wuqt_ref[...], cqn)
    scale = (MLA_QDIM ** -0.5) * LOG2E
    cost = cost_ref[...]
    sint = sint_ref[...]
    half = MLA_ROPE // 2
    for h in range(MLA_HEADS):
        base = h * HEAD_PAD
        x1 = qft[base + MLA_NOPE:base + MLA_NOPE + half]
        x2 = qft[base + MLA_NOPE + half:base + MLA_QDIM]
        qm_ref[0, base:base + MLA_NOPE] = (qft[base:base + MLA_NOPE] * scale).astype(_bf16)
        qm_ref[0, base + MLA_NOPE:base + MLA_NOPE + half] = ((x1 * cost - x2 * sint) * scale).astype(_bf16)
        qm_ref[0, base + MLA_NOPE + half:base + MLA_QDIM] = ((x1 * sint + x2 * cost) * scale).astype(_bf16)
        qm_ref[0, base + MLA_QDIM:base + HEAD_PAD] = jnp.zeros((HEAD_PAD - MLA_QDIM, TILE), _bf16)
    ckv = _dot(snb, w_ref[:, _C_CKV:_C_CKV + MLA_KV_RANK])
    ckvn = _rms_norm(ckv, kvg_ref[...]).astype(_bf16)
    kf = _dot(ckvn, wuk_ref[...])
    kr = _rope(_dot(snb, w_ref[:, _C_KR:_C_KR + LANES]), cos, sl, sh)
    for h in range(MLA_HEADS):
        km_ref[0, :, h * HEAD_PAD:(h + 1) * HEAD_PAD] = (
            kf[:, h * HEAD_PAD:(h + 1) * HEAD_PAD] + kr).astype(_bf16)
    vt = _dot_nt(wuvt_ref[...], ckvn)
    for h in range(MLA_HEADS):
        vm_ref[0, h * V_AUG:h * V_AUG + MLA_DV] = vt[h * MLA_DV:(h + 1) * MLA_DV].astype(_bf16)
        vm_ref[0, h * V_AUG + MLA_DV:(h + 1) * V_AUG] = jnp.ones((V_AUG - MLA_DV, TILE), _bf16)

    ga_ref[0] = _sigmoid(_dot(snb, w_ref[:, _C_GA:_C_GA + D_MODEL])).astype(_bf16)
    gbt_ref[0] = _sigmoid(_dot(snb, w_ref[:, _C_GB:_C_GB + D_MODEL])).astype(_bf16)


def _gla_kernel(qt_ref, kt_ref, ke_ref, gv_ref, dec_ref, sr_ref, ng_ref, o_ref, st_ref):
    t = pl.program_id(1)

    @pl.when(t == 0)
    def _():
        st_ref[...] = jnp.zeros_like(st_ref)

    ri = lax.broadcasted_iota(jnp.int32, (GLA_CHUNK, GLA_CHUNK), 0)
    ci = lax.broadcasted_iota(jnp.int32, (GLA_CHUNK, GLA_CHUNK), 1)
    causal = ri >= ci
    ng = ng_ref[...]
    for c in range(TILE // GLA_CHUNK):
        rows = slice(c * GLA_CHUNK, (c + 1) * GLA_CHUNK)
        for h in range(GLA_HEADS):
            kc = slice(h * GLA_DK, (h + 1) * GLA_DK)
            vc = slice(h * GLA_DV, (h + 1) * GLA_DV)
            q = qt_ref[0, rows, kc]
            k = kt_ref[0, rows, kc]
            e = ke_ref[0, rows, kc]
            v = gv_ref[0, rows, vc]
            st = st_ref[h]
            att = jnp.where(causal, _dot_nt(q, k), 0.0)
            o = _dot(att.astype(_bf16), v) + _dot_nt(q, st.astype(_bf16))
            st_ref[h] = st * dec_ref[0, c:c + 1, kc] + _dot_tn(v, e)
            o = _rms_norm(o, ng) * sr_ref[0, rows, vc].astype(_f32)
            o_ref[0, rows, vc] = o.astype(_bf16)


def _mla_kernel(q_ref, k_ref, v_ref, o_ref, acc_ref):
    i = pl.program_id(2)
    heads = 2
    q_t = [q_ref[0, h * HEAD_PAD:(h + 1) * HEAD_PAD, :] for h in range(heads)]

    ms = []
    for h in range(heads):
        kb = k_ref[0, PADL:TILE, h * HEAD_PAD:(h + 1) * HEAD_PAD]
        s = _dot(kb, q_t[h])
        m0 = jnp.max(s, axis=0, keepdims=True)
        p = jnp.concatenate([jnp.zeros((PADL, Q_TILE), _bf16), jnp.exp2(s - m0).astype(_bf16)], axis=0)
        acc_ref[h] = _dot(v_ref[0, h * V_AUG:(h + 1) * V_AUG, 0:TILE], p)
        ms.append(m0)

    def tile_start(j):
        return pl.multiple_of(TILE + j * KV_TILE, TILE)

    def scores(j, h):
        return _dot(k_ref[0, pl.ds(tile_start(j), KV_TILE), h * HEAD_PAD:(h + 1) * HEAD_PAD], q_t[h])

    def accumulate(j, h, s, m_prev):
        vb = v_ref[0, h * V_AUG:(h + 1) * V_AUG, pl.ds(tile_start(j), KV_TILE)]
        m_new = jnp.maximum(m_prev, jnp.max(s, axis=0, keepdims=True))
        alpha = jnp.exp2(m_prev - m_new)
        p = jnp.exp2(s - m_new).astype(_bf16)
        acc_ref[h] = alpha * acc_ref[h] + _dot(vb, p)
        return m_new

    def body(j, ms):
        s = [scores(j, h) for h in range(heads)]
        return tuple(accumulate(j, h, s[h], ms[h]) for h in range(heads))

    ms = lax.fori_loop(0, i, body, tuple(ms))
    k_row = lax.broadcasted_iota(jnp.int32, (KV_TILE, Q_TILE), 0)
    q_col = lax.broadcasted_iota(jnp.int32, (KV_TILE, Q_TILE), 1)
    s = [jnp.where(k_row <= q_col, scores(i, h), NEG) for h in range(heads)]
    for h in range(heads):
        accumulate(i, h, s[h], ms[h])
    outs = []
    for h in range(heads):
        a = acc_ref[h]
        outs.append(a[0:MLA_DV] / a[MLA_DV:MLA_DV + 1])
    o_ref[0] = jnp.concatenate(outs, axis=0).T.astype(_bf16)


def _merge_kernel(og_ref, om_ref, ga_ref, gbt_ref, s_ref, wbg_ref, wbm_ref, wo_ref, lng_ref,
                  lnb_ref, rwh_ref, rwl_ref, rb_ref, tril_ref,
                  s2_ref, info_ref, cnt_ref, carry_ref):
    step = pl.program_id(0)

    @pl.when(step == 0)
    def _():
        carry_ref[...] = jnp.zeros_like(carry_ref)

    ba = _dot(og_ref[...], wbg_ref[...])
    bb = _dot(om_ref[...], wbm_ref[...])
    merged = ga_ref[...].astype(_f32) * ba + gbt_ref[...].astype(_f32) * bb
    y = ALPHA * s_ref[...] + _dot(merged.astype(_bf16), wo_ref[...])
    s2 = _layer_norm(y, lng_ref[...], lnb_ref[...])
    s2_ref[...] = s2

    xh = s2.astype(_bf16)
    xl = (s2 - xh.astype(_f32)).astype(_bf16)
    logits = _dot(xh, rwh_ref[...]) + _dot(xl, rwh_ref[...]) + _dot(xh, rwl_ref[...]) + rb_ref[...]
    lane = lax.broadcasted_iota(jnp.int32, (MERGE_TILE, LANES), 1)
    is_g = lane < N_GROUPS
    gl = jnp.where(is_g, logits, NEG)
    gmax = jnp.max(gl, axis=-1, keepdims=True)
    gidx = jnp.min(jnp.where(gl == gmax, lane, LANES), axis=-1, keepdims=True)
    p_g = 1.0 / jnp.sum(jnp.where(is_g, jnp.exp(gl - gmax), 0.0), axis=-1, keepdims=True)
    lo = N_GROUPS + EXPERTS_PER_GROUP * gidx
    el = jnp.where((lane >= lo) & (lane < lo + EXPERTS_PER_GROUP), logits, NEG)
    v1 = jnp.max(el, axis=-1, keepdims=True)
    i1 = jnp.min(jnp.where(el == v1, lane, LANES), axis=-1, keepdims=True)
    el2 = jnp.where(lane == i1, NEG, el)
    v2 = jnp.max(el2, axis=-1, keepdims=True)
    i2 = jnp.min(jnp.where(el2 == v2, lane, LANES), axis=-1, keepdims=True)
    tt = jnp.exp(v2 - v1)
    p1 = 1.0 / (1.0 + tt)
    p2 = tt / (1.0 + tt)
    e1 = i1 - N_GROUPS
    e2 = i2 - N_GROUPS
    hit1 = lane == e1
    hit2 = lane == e2
    onehot = jnp.where(hit1 | hit2, 1.0, 0.0)
    before = _dot(tril_ref[...], onehot.astype(_bf16)) + carry_ref[0:1, :]
    r1 = jnp.sum(jnp.where(hit1, before, 0.0), axis=-1, keepdims=True)
    r2 = jnp.sum(jnp.where(hit2, before, 0.0), axis=-1, keepdims=True)
    new_carry = carry_ref[0:1, :] + jnp.sum(onehot, axis=0, keepdims=True)
    carry_ref[...] = jnp.broadcast_to(new_carry, carry_ref.shape)
    cnt_ref[...] = jnp.broadcast_to(new_carry, cnt_ref.shape)
    info = jnp.where(lane == 0, e1.astype(_f32),
           jnp.where(lane == 1, e2.astype(_f32),
           jnp.where(lane == 2, p_g * p1,
           jnp.where(lane == 3, p_g * p2,
           jnp.where(lane == 4, r1,
           jnp.where(lane == 5, r2, 0.0))))))
    info_ref[...] = info


def _dispatch_kernel(pos_ref, zrow_ref, nused_ref, s2_hbm, xs_hbm, zero_ref, sems):
    g = pl.program_id(0)
    last = pl.num_programs(0) - 1
    zero_sem = sems.at[2]

    def zero_copy(row):
        row = pl.multiple_of(row, EXPERT_TILE)
        return pltpu.make_async_copy(zero_ref, xs_hbm.at[pl.ds(row, EXPERT_TILE)], zero_sem)

    @pl.when(g == 0)
    def _():
        zero_ref[...] = jnp.zeros_like(zero_ref)

        def start(e, c):
            @pl.when(zrow_ref[e] >= 0)
            def _():
                zero_copy(zrow_ref[e]).start()
            return c

        def wait(e, c):
            @pl.when(zrow_ref[e] >= 0)
            def _():
                zero_copy(0).wait()
            return c

        def start_tail(u, c):
            zero_copy(u * EXPERT_TILE).start()
            return c

        def wait_tail(u, c):
            zero_copy(0).wait()
            return c

        n_tiles = xs_hbm.shape[0] // EXPERT_TILE
        lax.fori_loop(0, N_EXPERTS, start, 0)
        lax.fori_loop(nused_ref[0], n_tiles, start_tail, 0)
        lax.fori_loop(0, N_EXPERTS, wait, 0)
        lax.fori_loop(nused_ref[0], n_tiles, wait_tail, 0)

    slot = g % 2

    def issue(r, c):
        tok = g * DISPATCH_TILE + r
        for k in range(2):
            pltpu.make_async_copy(s2_hbm.at[pl.ds(tok, 1)],
                                  xs_hbm.at[pl.ds(pos_ref[2 * tok + k], 1)], sems.at[slot]).start()
        return c

    lax.fori_loop(0, DISPATCH_TILE, issue, 0, unroll=8)

    def wait_step(s):
        pltpu.make_async_copy(s2_hbm.at[pl.ds(0, 2 * DISPATCH_TILE)],
                              xs_hbm.at[pl.ds(0, 2 * DISPATCH_TILE)], sems.at[s]).wait()

    @pl.when(g > 0)
    def _():
        wait_step(1 - slot)

    @pl.when(g == last)
    def _():
        wait_step(slot)


def _expert_kernel(te_ref, nused_ref, x_ref, wg_ref, wu_ref, wd_ref, o_ref):
    u = pl.program_id(0)

    @pl.when(u < nused_ref[0])
    def _():
        x = x_ref[...].astype(_bf16)
        a = _dot(x, wg_ref[0])
        up = _dot(x, wu_ref[0])
        hid = a * _sigmoid(a) * up
        o_ref[...] = _dot(hid.astype(_bf16), wd_ref[0])

    @pl.when(u >= nused_ref[0])
    def _():
        o_ref[...] = jnp.zeros_like(o_ref)


def _combine_kernel(pos_ref, s2_ref, info_ref, ys_hbm, lng_ref, lnb_ref, o_ref, buf_ref, sems):
    g = pl.program_id(0)
    last = pl.num_programs(0) - 1

    def issue_tile(tile, slot):
        def body(r, c):
            tok = tile * COMBINE_TILE + r
            for k in range(2):
                pltpu.make_async_copy(ys_hbm.at[pl.ds(pos_ref[2 * tok + k], 1)],
                                      buf_ref.at[slot, k, pl.ds(r, 1)], sems.at[slot]).start()
            return c

        lax.fori_loop(0, COMBINE_TILE, body, 0, unroll=8)

    @pl.when(g == 0)
    def _():
        issue_tile(0, 0)

    @pl.when(g < last)
    def _():
        issue_tile(g + 1, (g + 1) % 2)

    slot = g % 2
    for k in range(2):
        pltpu.make_async_copy(ys_hbm.at[pl.ds(0, COMBINE_TILE)], buf_ref.at[slot, k],
                              sems.at[slot]).wait()
    info = info_ref[...]
    y = (ALPHA * s2_ref[...] + info[:, 2:3] * buf_ref[slot, 0] + info[:, 3:4] * buf_ref[slot, 1])
    o_ref[...] = _layer_norm(y, lng_ref[...], lnb_ref[...])


def _const_spec(shape):
    nd = len(shape)
    return pl.BlockSpec(shape, lambda *_: (0,) * nd)


def _rope_tables(lp):
    pos = jnp.maximum(jnp.arange(lp, dtype=_f32) - PADL, 0.0)
    inv_freq = ROPE_BASE ** (-jnp.arange(0, MLA_ROPE, 2, dtype=_f32) / MLA_ROPE)
    ang = pos[:, None] * inv_freq[None, :]
    cos, sin = jnp.cos(ang), jnp.sin(ang)
    half = MLA_ROPE // 2
    ones = jnp.ones((lp, MLA_NOPE), _f32)
    zeros_n = jnp.zeros((lp, MLA_NOPE), _f32)
    zeros_h = jnp.zeros((lp, half), _f32)
    tail1 = jnp.ones((lp, LANES - MLA_QDIM), _f32)
    tail0 = jnp.zeros((lp, LANES - MLA_QDIM), _f32)
    cos_t = jnp.concatenate([ones, cos, cos, tail1], axis=1)
    sin_lo = jnp.concatenate([zeros_n, -sin, zeros_h, tail0], axis=1)
    sin_hi = jnp.concatenate([zeros_n, zeros_h, sin, tail0], axis=1)
    return cos_t, sin_lo, sin_hi, cos.T, sin.T


def _pad_heads(w, width):
    k = w.shape[0]
    w = w.reshape(k, MLA_HEADS, width)
    w = jnp.pad(w, ((0, 0), (0, 0), (0, HEAD_PAD - width)))
    return w.reshape(k, MLA_HEADS * HEAD_PAD)


def kernel(x, meta_tokens, ln_emb_g, ln_emb_b, w_in, gla_gate_w2, gla_gate_b, gla_norm_g, mla_q_norm_g, mla_w_uq, mla_kv_norm_g, mla_w_uk, mla_w_uv, w_branch_gla, w_branch_mla, w_out, ln_mix_g, ln_mix_b, router_group_w, router_group_b, router_expert_w, router_expert_b, expert_w_gate, expert_w_up, expert_w_down, ln_ffn_g, ln_ffn_b):
    bsz, seq, d = x.shape
    assert d == D_MODEL and seq % Q_TILE == 0 and w_in.shape[0] == DEPTH == 1
    lp = PADL + N_META + seq
    nt = lp // TILE
    ntok = bsz * seq
    row2 = lambda v: v.reshape(1, -1).astype(_f32)

    xcat = jnp.concatenate([
        jnp.zeros((bsz, PADL, d), x.dtype),
        jnp.broadcast_to(meta_tokens[None].astype(x.dtype), (bsz, N_META, d)), x], axis=1)
    wi = w_in[0]
    o_a = 2 * GLA_QK + 2 * GLA_VW
    o_cq = o_a + GLA_GATE_RANK
    o_ckv = o_cq + MLA_Q_RANK
    o_kr = o_ckv + MLA_KV_RANK
    o_ga = o_kr + MLA_ROPE
    w_a = jnp.pad(wi[:, o_a:o_cq], ((0, 0), (0, LANES - GLA_GATE_RANK)))
    w_kr = jnp.pad(wi[:, o_kr:o_ga], ((0, 0), (MLA_NOPE, LANES - MLA_QDIM)))
    w_all = jnp.concatenate([wi[:, :o_a], w_a, wi[:, o_cq:o_kr], w_kr, wi[:, o_ga:]], axis=1).astype(_bf16)
    assert w_all.shape == (d, _W_COLS)
    w2p = jnp.pad(gla_gate_w2[0], ((0, LANES - GLA_GATE_RANK), (0, 0))).astype(_bf16)
    wuqt = _pad_heads(mla_w_uq[0], MLA_QDIM).T.astype(_bf16)
    wuk = _pad_heads(mla_w_uk[0], MLA_NOPE).astype(_bf16)
    wuvt = mla_w_uv[0].T.astype(_bf16)
    cos_t, sin_lo, sin_hi, cos_tr, sin_tr = _rope_tables(lp)
    blk = np.arange(TILE)
    tril_chunks = jnp.asarray(
        ((blk[:, None] >= blk[None, :]) & (blk[:, None] // GLA_CHUNK == blk[None, :] // GLA_CHUNK)),
        dtype=_bf16)

    pad_map = lambda b, t: (b, t, 0)
    real_map = lambda b, t: (b, jnp.maximum(t - 1, 0), 0)
    tab_spec = pl.BlockSpec((TILE, LANES), lambda b, t: (t, 0))
    tabt_spec = pl.BlockSpec((MLA_ROPE // 2, TILE), lambda b, t: (0, t))
    real_map_t = lambda b, t: (b, 0, jnp.maximum(t - 1, 0))
    out_shapes = (
        jax.ShapeDtypeStruct((bsz, seq, d), _f32),
        jax.ShapeDtypeStruct((bsz, lp, GLA_QK), _bf16),
        jax.ShapeDtypeStruct((bsz, lp, GLA_QK), _bf16),
        jax.ShapeDtypeStruct((bsz, lp, GLA_QK), _bf16),
        jax.ShapeDtypeStruct((bsz, lp, GLA_VW), _bf16),
        jax.ShapeDtypeStruct((bsz * nt, TILE // GLA_CHUNK, GLA_QK), _f32),
        jax.ShapeDtypeStruct((bsz, seq, GLA_VW), _bf16),
        jax.ShapeDtypeStruct((bsz, MLA_HEADS * HEAD_PAD, seq), _bf16),
        jax.ShapeDtypeStruct((bsz, lp, MLA_HEADS * HEAD_PAD), _bf16),
        jax.ShapeDtypeStruct((bsz, MLA_HEADS * V_AUG, lp), _bf16),
        jax.ShapeDtypeStruct((bsz, seq, d), _bf16),
        jax.ShapeDtypeStruct((bsz, seq, d), _bf16),
    )
    out_specs = (
        pl.BlockSpec((1, TILE, d), real_map),
        pl.BlockSpec((1, TILE, GLA_QK), pad_map),
        pl.BlockSpec((1, TILE, GLA_QK), pad_map),
        pl.BlockSpec((1, TILE, GLA_QK), pad_map),
        pl.BlockSpec((1, TILE, GLA_VW), pad_map),
        pl.BlockSpec((1, TILE // GLA_CHUNK, GLA_QK), lambda b, t: (b * nt + t, 0, 0)),
        pl.BlockSpec((1, TILE, GLA_VW), real_map),
        pl.BlockSpec((1, MLA_HEADS * HEAD_PAD, TILE), real_map_t),
        pl.BlockSpec((1, TILE, MLA_HEADS * HEAD_PAD), pad_map),
        pl.BlockSpec((1, MLA_HEADS * V_AUG, TILE), lambda b, t: (b, 0, t)),
        pl.BlockSpec((1, TILE, d), real_map),
        pl.BlockSpec((1, TILE, d), real_map),
    )
    (s_emb, qt, kt, ke, gv, dec, sr, qm, km, vm, gate_a, gate_b) = pl.pallas_call(
        _inproj_kernel,
        grid=(bsz, nt),
        in_specs=[
            pl.BlockSpec((1, TILE, d), pad_map),
            _const_spec((1, d)), _const_spec((1, d)),
            _const_spec((d, _W_COLS)),
            _const_spec((LANES, GLA_QK)), _const_spec((1, GLA_QK)),
            _const_spec((1, MLA_Q_RANK)), _const_spec((MLA_HEADS * HEAD_PAD, MLA_Q_RANK)),
            _const_spec((1, MLA_KV_RANK)), _const_spec((MLA_KV_RANK, MLA_HEADS * HEAD_PAD)),
            _const_spec((MLA_HEADS * MLA_DV, MLA_KV_RANK)),
            tab_spec, tab_spec, tab_spec, tabt_spec, tabt_spec,
            _const_spec((TILE, TILE)),
        ],
        out_specs=out_specs,
        out_shape=out_shapes,
        compiler_params=pltpu.CompilerParams(
            dimension_semantics=("arbitrary", "arbitrary"), vmem_limit_bytes=VMEM_LIMIT),
        name="inproj",
    )(xcat, row2(ln_emb_g), row2(ln_emb_b), w_all, w2p, row2(gla_gate_b[0]),
      row2(mla_q_norm_g[0]), wuqt, row2(mla_kv_norm_g[0]), wuk, wuvt, cos_t, sin_lo, sin_hi,
      cos_tr, sin_tr, tril_chunks)

    o_gla = pl.pallas_call(
        _gla_kernel,
        grid=(bsz, nt),
        in_specs=[
            pl.BlockSpec((1, TILE, GLA_QK), pad_map),
            pl.BlockSpec((1, TILE, GLA_QK), pad_map),
            pl.BlockSpec((1, TILE, GLA_QK), pad_map),
            pl.BlockSpec((1, TILE, GLA_VW), pad_map),
            pl.BlockSpec((1, TILE // GLA_CHUNK, GLA_QK), lambda b, t: (b * nt + t, 0, 0)),
            pl.BlockSpec((1, TILE, GLA_VW), real_map),
            _const_spec((1, GLA_DV)),
        ],
        out_specs=pl.BlockSpec((1, TILE, GLA_VW), real_map),
        out_shape=jax.ShapeDtypeStruct((bsz, seq, GLA_VW), _bf16),
        scratch_shapes=[pltpu.VMEM((GLA_HEADS, GLA_DV, GLA_DK), _f32)],
        compiler_params=pltpu.CompilerParams(
            dimension_semantics=("arbitrary", "arbitrary"), vmem_limit_bytes=VMEM_LIMIT),
        name="gla",
    )(qt, kt, ke, gv, dec, sr, row2(gla_norm_g[0]))

    pair = 2 * HEAD_PAD
    o_mla = pl.pallas_call(
        _mla_kernel,
        grid=(bsz, MLA_HEADS // 2, seq // Q_TILE),
        in_specs=[
            pl.BlockSpec((1, pair, Q_TILE), lambda b, hp, i: (b, hp, i)),
            pl.BlockSpec((1, lp, pair), lambda b, hp, i: (b, 0, hp)),
            pl.BlockSpec((1, 2 * V_AUG, lp), lambda b, hp, i: (b, hp, 0)),
        ],
        out_specs=pl.BlockSpec((1, Q_TILE, 2 * MLA_DV), lambda b, hp, i: (b, i, hp)),
        out_shape=jax.ShapeDtypeStruct((bsz, seq, MLA_HEADS * MLA_DV), _bf16),
        scratch_shapes=[pltpu.VMEM((2, V_AUG, Q_TILE), _f32)],
        compiler_params=pltpu.CompilerParams(
            dimension_semantics=("arbitrary", "arbitrary", "arbitrary"),
            vmem_limit_bytes=VMEM_LIMIT),
        name="mla",
    )(qm, km, vm)

    rw = jnp.concatenate([router_group_w[0], router_expert_w[0]], axis=1)
    rw = jnp.pad(rw, ((0, 0), (0, LANES - rw.shape[1])))
    rwh = rw.astype(_bf16)
    rwl = (rw - rwh.astype(_f32)).astype(_bf16)
    rb = jnp.concatenate([router_group_b[0], router_expert_b[0]])
    rb = jnp.pad(rb, (0, LANES - rb.shape[0])).reshape(1, LANES)
    mi = np.arange(MERGE_TILE)
    tril_strict = jnp.asarray(mi[:, None] > mi[None, :], dtype=_bf16)
    flat = lambda a: a.reshape(ntok, a.shape[-1])
    tok_spec = lambda w: pl.BlockSpec((MERGE_TILE, w), lambda g: (g, 0))
    s2, info, cnt = pl.pallas_call(
        _merge_kernel,
        grid=(ntok // MERGE_TILE,),
        in_specs=[tok_spec(d), tok_spec(d), tok_spec(d), tok_spec(d), tok_spec(d),
                  _const_spec((d, d)), _const_spec((d, d)), _const_spec((d, d)),
                  _const_spec((1, d)), _const_spec((1, d)),
                  _const_spec((d, LANES)), _const_spec((d, LANES)), _const_spec((1, LANES)),
                  _const_spec((MERGE_TILE, MERGE_TILE))],
        out_specs=(tok_spec(d), tok_spec(LANES), _const_spec((8, LANES))),
        out_shape=(jax.ShapeDtypeStruct((ntok, d), _f32),
                   jax.ShapeDtypeStruct((ntok, LANES), _f32),
                   jax.ShapeDtypeStruct((8, LANES), _f32)),
        scratch_shapes=[pltpu.VMEM((8, LANES), _f32)],
        compiler_params=pltpu.CompilerParams(
            dimension_semantics=("arbitrary",), vmem_limit_bytes=VMEM_LIMIT),
        name="merge_router",
    )(flat(o_gla), flat(o_mla), flat(gate_a), flat(gate_b), flat(s_emb),
      w_branch_gla[0].astype(_bf16), w_branch_mla[0].astype(_bf16), w_out[0].astype(_bf16),
      row2(ln_mix_g[0]), row2(ln_mix_b[0]), rwh, rwl, rb, tril_strict)

    n_tiles = (2 * ntok + N_EXPERTS * (EXPERT_TILE - 1)) // EXPERT_TILE
    n_rows = n_tiles * EXPERT_TILE
    e_idx = info[:, 0:2].astype(jnp.int32)
    rank = info[:, 4:6].astype(jnp.int32)
    counts = cnt[0, :N_EXPERTS].astype(jnp.int32)
    padded = ((counts + EXPERT_TILE - 1) // EXPERT_TILE) * EXPERT_TILE
    ends = jnp.cumsum(padded)
    starts = ends - padded
    pos = (starts[e_idx] + rank).reshape(-1)
    tile_start = jnp.arange(n_tiles, dtype=jnp.int32) * EXPERT_TILE
    tile_expert = jnp.minimum(
        jnp.sum((ends[None, :] <= tile_start[:, None]).astype(jnp.int32), axis=1), N_EXPERTS - 1)
    n_used = (ends[-1:] // EXPERT_TILE).astype(jnp.int32)
    zero_row = jnp.where(padded > 0, ends - EXPERT_TILE, -1).astype(jnp.int32)

    any_spec = pl.BlockSpec(memory_space=pl.ANY)
    xs = pl.pallas_call(
        _dispatch_kernel,
        grid_spec=pltpu.PrefetchScalarGridSpec(
            num_scalar_prefetch=3,
            grid=(ntok // DISPATCH_TILE,),
            in_specs=[any_spec],
            out_specs=any_spec,
            scratch_shapes=[pltpu.VMEM((EXPERT_TILE, d), _f32), pltpu.SemaphoreType.DMA((3,))],
        ),
        out_shape=jax.ShapeDtypeStruct((n_rows, d), _f32),
        compiler_params=pltpu.CompilerParams(
            dimension_semantics=("arbitrary",), vmem_limit_bytes=VMEM_LIMIT, has_side_effects=True),
        name="dispatch",
    )(pos, zero_row, n_used, s2)

    ff = EXPERT_FF
    wg = expert_w_gate[0].reshape(N_EXPERTS, d, ff).astype(_bf16)
    wu = expert_w_up[0].reshape(N_EXPERTS, d, ff).astype(_bf16)
    wd = expert_w_down[0].reshape(N_EXPERTS, ff, d).astype(_bf16)
    ys = pl.pallas_call(
        _expert_kernel,
        grid_spec=pltpu.PrefetchScalarGridSpec(
            num_scalar_prefetch=2,
            grid=(n_tiles,),
            in_specs=[
                pl.BlockSpec((EXPERT_TILE, d), lambda u, te, nu: (jnp.minimum(u, nu[0] - 1), 0)),
                pl.BlockSpec((1, d, ff), lambda u, te, nu: (te[u], 0, 0)),
                pl.BlockSpec((1, d, ff), lambda u, te, nu: (te[u], 0, 0)),
                pl.BlockSpec((1, ff, d), lambda u, te, nu: (te[u], 0, 0)),
            ],
            out_specs=pl.BlockSpec((EXPERT_TILE, d), lambda u, te, nu: (u, 0)),
        ),
        out_shape=jax.ShapeDtypeStruct((n_rows, d), _f32),
        compiler_params=pltpu.CompilerParams(
            dimension_semantics=("arbitrary",), vmem_limit_bytes=VMEM_LIMIT),
        name="experts",
    )(tile_expert, n_used, xs, wg, wu, wd)

    out = pl.pallas_call(
        _combine_kernel,
        grid_spec=pltpu.PrefetchScalarGridSpec(
            num_scalar_prefetch=1,
            grid=(ntok // COMBINE_TILE,),
            in_specs=[
                pl.BlockSpec((COMBINE_TILE, d), lambda g, p: (g, 0)),
                pl.BlockSpec((COMBINE_TILE, LANES), lambda g, p: (g, 0)),
                any_spec,
                pl.BlockSpec((1, d), lambda g, p: (0, 0)),
                pl.BlockSpec((1, d), lambda g, p: (0, 0)),
            ],
            out_specs=pl.BlockSpec((COMBINE_TILE, d), lambda g, p: (g, 0)),
            scratch_shapes=[pltpu.VMEM((2, 2, COMBINE_TILE, d), _f32),
                            pltpu.SemaphoreType.DMA((2,))],
        ),
        out_shape=jax.ShapeDtypeStruct((ntok, d), _f32),
        compiler_params=pltpu.CompilerParams(
            dimension_semantics=("arbitrary",), vmem_limit_bytes=VMEM_LIMIT),
        name="combine_ln",
    )(pos, s2, info, ys, row2(ln_ffn_g[0]), row2(ln_ffn_b[0]))
    return out.reshape(bsz, seq, d)
```

```python
import jax
import jax.numpy as jnp
import numpy as np
from jax import lax
from jax.experimental import pallas as pl
from jax.experimental.pallas import tpu as pltpu

D_MODEL = 1024
N_META = 16
GLA_HEADS = 4
GLA_DK = 128
GLA_DV = 256
GLA_QK = GLA_HEADS * GLA_DK
GLA_VW = GLA_HEADS * GLA_DV
GLA_GATE_RANK = 16
GLA_GATE_TAU = 16.0
GLA_CHUNK = 64
MLA_HEADS = 16
MLA_Q_RANK = 384
MLA_KV_RANK = 256
MLA_NOPE = 64
MLA_ROPE = 32
MLA_DV = 64
MLA_QDIM = MLA_NOPE + MLA_ROPE
ROPE_BASE = 10000.0
N_GROUPS = 4
EXPERTS_PER_GROUP = 8
N_EXPERTS = N_GROUPS * EXPERTS_PER_GROUP
EXPERT_FF = 256
DEPTH = 1
ALPHA = (2.0 * DEPTH) ** 0.25
LN_EPS = 1e-5
RMS_EPS = 1e-6

LANES = 128
TILE = 256
PADL = TILE - N_META
HEAD_PAD = LANES
BF16_ROWS = 16
Q_TILE = 512
KV_TILE = Q_TILE
V_AUG = MLA_DV + BF16_ROWS
MERGE_TILE = 512
EXPERT_TILE = 256
DISPATCH_TILE = 512
COMBINE_TILE = 256
NEG = -1e30
LOG2E = 1.4426950408889634
VMEM_LIMIT = 56 * 1024 * 1024

_C_Q, _C_K, _C_V, _C_R = 0, 512, 1024, 2048
_C_A = 3072
_C_CQ = _C_A + LANES
_C_CKV = _C_CQ + MLA_Q_RANK
_C_KR = _C_CKV + MLA_KV_RANK
_C_GA = _C_KR + LANES
_C_GB = _C_GA + D_MODEL
_W_COLS = _C_GB + D_MODEL

_f32 = jnp.float32
_bf16 = jnp.bfloat16


def _dot(a, b):
    return jnp.dot(a, b, preferred_element_type=_f32)


def _dot_nt(a, b):
    return lax.dot_general(a, b, (((1,), (1,)), ((), ())), preferred_element_type=_f32)


def _dot_tn(a, b):
    return lax.dot_general(a, b, (((0,), (0,)), ((), ())), preferred_element_type=_f32)


def _layer_norm(x, g, b):
    mu = jnp.mean(x, axis=-1, keepdims=True)
    xc = x - mu
    var = jnp.mean(xc * xc, axis=-1, keepdims=True)
    return xc * lax.rsqrt(var + LN_EPS) * g + b


def _rms_norm(x, g):
    ms = jnp.mean(x * x, axis=-1, keepdims=True)
    return x * lax.rsqrt(ms + RMS_EPS) * g


def _sigmoid(x):
    return 1.0 / (1.0 + jnp.exp(-x))


def _rope(x, cos, sin_lo, sin_hi):
    half = MLA_ROPE // 2
    from_hi = pltpu.roll(x, LANES - half, 1)
    from_lo = pltpu.roll(x, half, 1)
    return x * cos + from_hi * sin_lo + from_lo * sin_hi


def _inproj_kernel(x_ref, lng_ref, lnb_ref, w_ref, w2_ref, gb_ref, qg_ref, wuqt_ref, kvg_ref,
                   wuk_ref, wuvt_ref, cos_ref, sl_ref, sh_ref, cost_ref, sint_ref, tril_ref,
                   s_ref, qt_ref, kt_ref, ke_ref, gv_ref, dec_ref, sr_ref, qm_ref, km_ref,
                   vm_ref, ga_ref, gbt_ref):
    t = pl.program_id(1)
    sn = _layer_norm(x_ref[0], lng_ref[...], lnb_ref[...])
    s_ref[0] = sn
    snb = sn.astype(_bf16)
    row = t * TILE + lax.broadcasted_iota(jnp.int32, (TILE, 1), 0)
    valid = row >= PADL

    a_lr = _dot(snb, w_ref[:, _C_A:_C_A + LANES])
    z = _dot(a_lr.astype(_bf16), w2_ref[...]) + gb_ref[...]
    la = (jnp.minimum(z, 0.0) - jnp.log1p(jnp.exp(-jnp.abs(z)))) * (1.0 / GLA_GATE_TAU)
    la = jnp.where(valid, la, 0.0)
    hi = la.astype(_bf16)
    r1 = la - hi.astype(_f32)
    mid = r1.astype(_bf16)
    lo = (r1 - mid.astype(_f32)).astype(_bf16)
    tril = tril_ref[...]
    bc = _dot(tril, hi) + _dot(tril, mid) + _dot(tril, lo)
    n_chunks = TILE // GLA_CHUNK
    lasts = [bc[c * GLA_CHUNK + GLA_CHUNK - 1:(c + 1) * GLA_CHUNK, :] for c in range(n_chunks)]
    for c in range(n_chunks):
        dec_ref[0, c:c + 1, :] = jnp.exp(lasts[c])
    b_last = jnp.concatenate(
        [jnp.broadcast_to(l, (GLA_CHUNK, GLA_QK)) for l in lasts], axis=0)
    gq = _dot(snb, w_ref[:, _C_Q:_C_Q + GLA_QK])
    gk = jnp.where(valid, _dot(snb, w_ref[:, _C_K:_C_K + GLA_QK]), 0.0)
    qt_ref[0] = (gq * (GLA_DK ** -0.5) * jnp.exp(bc)).astype(_bf16)
    kt_ref[0] = (gk * jnp.exp(-bc)).astype(_bf16)
    ke_ref[0] = (gk * jnp.exp(b_last - bc)).astype(_bf16)
    gv_ref[0] = jnp.where(valid, _dot(snb, w_ref[:, _C_V:_C_V + GLA_VW]), 0.0).astype(_bf16)
    r = _dot(snb, w_ref[:, _C_R:_C_R + GLA_VW])
    sr_ref[0] = (r * _sigmoid(r)).astype(_bf16)

    cos = cos_ref[...]
    sl = sl_ref[...]
    sh = sh_ref[...]
    cq = _dot(snb, w_ref[:, _C_CQ:_C_CQ + MLA_Q_RANK])
    cqn = _rms_norm(cq, qg_ref[...]).astype(_bf16)
    qft = _dot_nt(wuqt_ref[...], cqn)
    scale = (MLA_QDIM ** -0.5) * LOG2E
    cost = cost_ref[...]
    sint = sint_ref[...]
    half = MLA_ROPE // 2
    for h in range(MLA_HEADS):
        base = h * HEAD_PAD
        x1 = qft[base + MLA_NOPE:base + MLA_NOPE + half]
        x2 = qft[base + MLA_NOPE + half:base + MLA_QDIM]
        qm_ref[0, base:base + MLA_NOPE] = (qft[base:base + MLA_NOPE] * scale).astype(_bf16)
        qm_ref[0, base + MLA_NOPE:base + MLA_NOPE + half] = ((x1 * cost - x2 * sint) * scale).astype(_bf16)
        qm_ref[0, base + MLA_NOPE + half:base + MLA_QDIM] = ((x1 * sint + x2 * cost) * scale).astype(_bf16)
        qm_ref[0, base + MLA_QDIM:base + HEAD_PAD] = jnp.zeros((HEAD_PAD - MLA_QDIM, TILE), _bf16)
    ckv = _dot(snb, w_ref[:, _C_CKV:_C_CKV + MLA_KV_RANK])
    ckvn = _rms_norm(ckv, kvg_ref[...]).astype(_bf16)
    kf = _dot(ckvn, wuk_ref[...])
    kr = _rope(_dot(snb, w_ref[:, _C_KR:_C_KR + LANES]), cos, sl, sh)
    for h in range(MLA_HEADS):
        km_ref[0, :, h * HEAD_PAD:(h + 1) * HEAD_PAD] = (
            kf[:, h * HEAD_PAD:(h + 1) * HEAD_PAD] + kr).astype(_bf16)
    vt = _dot_nt(wuvt_ref[...], ckvn)
    for h in range(MLA_HEADS):
        vm_ref[0, h * V_AUG:h * V_AUG + MLA_DV] = vt[h * MLA_DV:(h + 1) * MLA_DV].astype(_bf16)
        vm_ref[0, h * V_AUG + MLA_DV:(h + 1) * V_AUG] = jnp.ones((V_AUG - MLA_DV, TILE), _bf16)

    ga_ref[0] = _sigmoid(_dot(snb, w_ref[:, _C_GA:_C_GA + D_MODEL])).astype(_bf16)
    gbt_ref[0] = _sigmoid(_dot(snb, w_ref[:, _C_GB:_C_GB + D_MODEL])).astype(_bf16)


def _gla_kernel(qt_ref, kt_ref, ke_ref, gv_ref, dec_ref, sr_ref, ng_ref, o_ref, st_ref):
    t = pl.program_id(1)

    @pl.when(t == 0)
    def _():
        st_ref[...] = jnp.zeros_like(st_ref)

    ri = lax.broadcasted_iota(jnp.int32, (GLA_CHUNK, GLA_CHUNK), 0)
    ci = lax.broadcasted_iota(jnp.int32, (GLA_CHUNK, GLA_CHUNK), 1)
    causal = ri >= ci
    ng = ng_ref[...]
    for c in range(TILE // GLA_CHUNK):
        rows = slice(c * GLA_CHUNK, (c + 1) * GLA_CHUNK)
        for h in range(GLA_HEADS):
            kc = slice(h * GLA_DK, (h + 1) * GLA_DK)
            vc = slice(h * GLA_DV, (h + 1) * GLA_DV)
            q = qt_ref[0, rows, kc]
            k = kt_ref[0, rows, kc]
            e = ke_ref[0, rows, kc]
            v = gv_ref[0, rows, vc]
            st = st_ref[h]
            att = jnp.where(causal, _dot_nt(q, k), 0.0)
            o = _dot(att.astype(_bf16), v) + _dot_nt(q, st.astype(_bf16))
            st_ref[h] = st * dec_ref[0, c:c + 1, kc] + _dot_tn(v, e)
            o = _rms_norm(o, ng) * sr_ref[0, rows, vc].astype(_f32)
            o_ref[0, rows, vc] = o.astype(_bf16)


def _mla_kernel(q_ref, k_ref, v_ref, o_ref, acc_ref):
    i = pl.program_id(2)
    heads = 2
    q_t = [q_ref[0, h * HEAD_PAD:(h + 1) * HEAD_PAD, :] for h in range(heads)]

    ms = []
    for h in range(heads):
        kb = k_ref[0, PADL:TILE, h * HEAD_PAD:(h + 1) * HEAD_PAD]
        s = _dot(kb, q_t[h])
        m0 = jnp.max(s, axis=0, keepdims=True)
        p = jnp.concatenate([jnp.zeros((PADL, Q_TILE), _bf16), jnp.exp2(s - m0).astype(_bf16)], axis=0)
        acc_ref[h] = _dot(v_ref[0, h * V_AUG:(h + 1) * V_AUG, 0:TILE], p)
        ms.append(m0)

    def tile_start(j):
        return pl.multiple_of(TILE + j * KV_TILE, TILE)

    def scores(j, h):
        return _dot(k_ref[0, pl.ds(tile_start(j), KV_TILE), h * HEAD_PAD:(h + 1) * HEAD_PAD], q_t[h])

    def accumulate(j, h, s, m_prev):
        vb = v_ref[0, h * V_AUG:(h + 1) * V_AUG, pl.ds(tile_start(j), KV_TILE)]
        m_new = jnp.maximum(m_prev, jnp.max(s, axis=0, keepdims=True))
        alpha = jnp.exp2(m_prev - m_new)
        p = jnp.exp2(s - m_new).astype(_bf16)
        acc_ref[h] = alpha * acc_ref[h] + _dot(vb, p)
        return m_new

    def body(j, ms):
        s = [scores(j, h) for h in range(heads)]
        return tuple(accumulate(j, h, s[h], ms[h]) for h in range(heads))

    ms = lax.fori_loop(0, i, body, tuple(ms))
    k_row = lax.broadcasted_iota(jnp.int32, (KV_TILE, Q_TILE), 0)
    q_col = lax.broadcasted_iota(jnp.int32, (KV_TILE, Q_TILE), 1)
    s = [jnp.where(k_row <= q_col, scores(i, h), NEG) for h in range(heads)]
    for h in range(heads):
        accumulate(i, h, s[h], ms[h])
    outs = []
    for h in range(heads):
        a = acc_ref[h]
        outs.append(a[0:MLA_DV] / a[MLA_DV:MLA_DV + 1])
    o_ref[0] = jnp.concatenate(outs, axis=0).T.astype(_bf16)


def _merge_kernel(og_ref, om_ref, ga_ref, gbt_ref, s_ref, wbg_ref, wbm_ref, wo_ref, lng_ref,
                  lnb_ref, rwh_ref, rwl_ref, rb_ref, tril_ref,
                  s2_ref, info_ref, cnt_ref, carry_ref):
    step = pl.program_id(0)

    @pl.when(step == 0)
    def _():
        carry_ref[...] = jnp.zeros_like(carry_ref)

    ba = _dot(og_ref[...], wbg_ref[...])
    bb = _dot(om_ref[...], wbm_ref[...])
    merged = ga_ref[...].astype(_f32) * ba + gbt_ref[...].astype(_f32) * bb
    y = ALPHA * s_ref[...] + _dot(merged.astype(_bf16), wo_ref[...])
    s2 = _layer_norm(y, lng_ref[...], lnb_ref[...])
    s2_ref[...] = s2

    xh = s2.astype(_bf16)
    xl = (s2 - xh.astype(_f32)).astype(_bf16)
    logits = _dot(xh, rwh_ref[...]) + _dot(xl, rwh_ref[...]) + _dot(xh, rwl_ref[...]) + rb_ref[...]
    lane = lax.broadcasted_iota(jnp.int32, (MERGE_TILE, LANES), 1)
    is_g = lane < N_GROUPS
    gl = jnp.where(is_g, logits, NEG)
    gmax = jnp.max(gl, axis=-1, keepdims=True)
    gidx = jnp.min(jnp.where(gl == gmax, lane, LANES), axis=-1, keepdims=True)
    p_g = 1.0 / jnp.sum(jnp.where(is_g, jnp.exp(gl - gmax), 0.0), axis=-1, keepdims=True)
    lo = N_GROUPS + EXPERTS_PER_GROUP * gidx
    el = jnp.where((lane >= lo) & (lane < lo + EXPERTS_PER_GROUP), logits, NEG)
    v1 = jnp.max(el, axis=-1, keepdims=True)
    i1 = jnp.min(jnp.where(el == v1, lane, LANES), axis=-1, keepdims=True)
    el2 = jnp.where(lane == i1, NEG, el)
    v2 = jnp.max(el2, axis=-1, keepdims=True)
    i2 = jnp.min(jnp.where(el2 == v2, lane, LANES), axis=-1, keepdims=True)
    tt = jnp.exp(v2 - v1)
    p1 = 1.0 / (1.0 + tt)
    p2 = tt / (1.0 + tt)
    e1 = i1 - N_GROUPS
    e2 = i2 - N_GROUPS
    hit1 = lane == e1
    hit2 = lane == e2
    onehot = jnp.where(hit1 | hit2, 1.0, 0.0)
    before = _dot(tril_ref[...], onehot.astype(_bf16)) + carry_ref[0:1, :]
    r1 = jnp.sum(jnp.where(hit1, before, 0.0), axis=-1, keepdims=True)
    r2 = jnp.sum(jnp.where(hit2, before, 0.0), axis=-1, keepdims=True)
    new_carry = carry_ref[0:1, :] + jnp.sum(onehot, axis=0, keepdims=True)
    carry_ref[...] = jnp.broadcast_to(new_carry, carry_ref.shape)
    cnt_ref[...] = jnp.broadcast_to(new_carry, cnt_ref.shape)
    info = jnp.where(lane == 0, e1.astype(_f32),
           jnp.where(lane == 1, e2.astype(_f32),
           jnp.where(lane == 2, p_g * p1,
           jnp.where(lane == 3, p_g * p2,
           jnp.where(lane == 4, r1,
           jnp.where(lane == 5, r2, 0.0))))))
    info_ref[...] = info


def _dispatch_kernel(pos_ref, zrow_ref, nused_ref, s2_ref, xs_hbm, zero_ref, sems):
    g = pl.program_id(0)
    zero_sem = sems.at[1]
    row_sem = sems.at[0]

    def zero_copy(row):
        row = pl.multiple_of(row, EXPERT_TILE)
        return pltpu.make_async_copy(zero_ref, xs_hbm.at[pl.ds(row, EXPERT_TILE)], zero_sem)

    @pl.when(g == 0)
    def _():
        zero_ref[...] = jnp.zeros_like(zero_ref)

        def start(e, c):
            @pl.when(zrow_ref[e] >= 0)
            def _():
                zero_copy(zrow_ref[e]).start()
            return c

        def wait(e, c):
            @pl.when(zrow_ref[e] >= 0)
            def _():
                zero_copy(0).wait()
            return c

        def start_tail(u, c):
            zero_copy(u * EXPERT_TILE).start()
            return c

        def wait_tail(u, c):
            zero_copy(0).wait()
            return c

        n_tiles = xs_hbm.shape[0] // EXPERT_TILE
        lax.fori_loop(0, N_EXPERTS, start, 0)
        lax.fori_loop(nused_ref[0], n_tiles, start_tail, 0)
        lax.fori_loop(0, N_EXPERTS, wait, 0)
        lax.fori_loop(nused_ref[0], n_tiles, wait_tail, 0)

    def issue(r, c):
        tok = g * DISPATCH_TILE + r
        for k in range(2):
            pltpu.make_async_copy(s2_ref.at[pl.ds(r, 1)],
                                  xs_hbm.at[pl.ds(pos_ref[2 * tok + k], 1)], row_sem).start()
        return c

    lax.fori_loop(0, DISPATCH_TILE, issue, 0, unroll=8)
    for k in range(2):
        pltpu.make_async_copy(s2_ref, xs_hbm.at[pl.ds(0, DISPATCH_TILE)], row_sem).wait()


def _expert_kernel(te_ref, nused_ref, x_ref, wg_ref, wu_ref, wd_ref, o_ref):
    u = pl.program_id(0)

    @pl.when(u < nused_ref[0])
    def _():
        x = x_ref[...].astype(_bf16)
        a = _dot(x, wg_ref[0])
        up = _dot(x, wu_ref[0])
        hid = a * _sigmoid(a) * up
        o_ref[...] = _dot(hid.astype(_bf16), wd_ref[0])

    @pl.when(u >= nused_ref[0])
    def _():
        o_ref[...] = jnp.zeros_like(o_ref)


def _combine_kernel(pos_ref, s2_ref, info_ref, ys_hbm, lng_ref, lnb_ref, o_ref, buf_ref, sems):
    g = pl.program_id(0)
    last = pl.num_programs(0) - 1

    def issue_tile(tile, slot):
        def body(r, c):
            tok = tile * COMBINE_TILE + r
            for k in range(2):
                pltpu.make_async_copy(ys_hbm.at[pl.ds(pos_ref[2 * tok + k], 1)],
                                      buf_ref.at[slot, k, pl.ds(r, 1)], sems.at[slot]).start()
            return c

        lax.fori_loop(0, COMBINE_TILE, body, 0, unroll=8)

    @pl.when(g == 0)
    def _():
        issue_tile(0, 0)

    @pl.when(g < last)
    def _():
        issue_tile(g + 1, (g + 1) % 2)

    slot = g % 2
    for k in range(2):
        pltpu.make_async_copy(ys_hbm.at[pl.ds(0, COMBINE_TILE)], buf_ref.at[slot, k],
                              sems.at[slot]).wait()
    info = info_ref[...]
    y = (ALPHA * s2_ref[...] + info[:, 2:3] * buf_ref[slot, 0] + info[:, 3:4] * buf_ref[slot, 1])
    o_ref[...] = _layer_norm(y, lng_ref[...], lnb_ref[...])


def _const_spec(shape):
    nd = len(shape)
    return pl.BlockSpec(shape, lambda *_: (0,) * nd)


def _rope_tables(lp):
    pos = jnp.maximum(jnp.arange(lp, dtype=_f32) - PADL, 0.0)
    inv_freq = ROPE_BASE ** (-jnp.arange(0, MLA_ROPE, 2, dtype=_f32) / MLA_ROPE)
    ang = pos[:, None] * inv_freq[None, :]
    cos, sin = jnp.cos(ang), jnp.sin(ang)
    half = MLA_ROPE // 2
    ones = jnp.ones((lp, MLA_NOPE), _f32)
    zeros_n = jnp.zeros((lp, MLA_NOPE), _f32)
    zeros_h = jnp.zeros((lp, half), _f32)
    tail1 = jnp.ones((lp, LANES - MLA_QDIM), _f32)
    tail0 = jnp.zeros((lp, LANES - MLA_QDIM), _f32)
    cos_t = jnp.concatenate([ones, cos, cos, tail1], axis=1)
    sin_lo = jnp.concatenate([zeros_n, -sin, zeros_h, tail0], axis=1)
    sin_hi = jnp.concatenate([zeros_n, zeros_h, sin, tail0], axis=1)
    return cos_t, sin_lo, sin_hi, cos.T, sin.T


def _pad_heads(w, width):
    k = w.shape[0]
    w = w.reshape(k, MLA_HEADS, width)
    w = jnp.pad(w, ((0, 0), (0, 0), (0, HEAD_PAD - width)))
    return w.reshape(k, MLA_HEADS * HEAD_PAD)


def kernel(x, meta_tokens, ln_emb_g, ln_emb_b, w_in, gla_gate_w2, gla_gate_b, gla_norm_g, mla_q_norm_g, mla_w_uq, mla_kv_norm_g, mla_w_uk, mla_w_uv, w_branch_gla, w_branch_mla, w_out, ln_mix_g, ln_mix_b, router_group_w, router_group_b, router_expert_w, router_expert_b, expert_w_gate, expert_w_up, expert_w_down, ln_ffn_g, ln_ffn_b):
    bsz, seq, d = x.shape
    assert d == D_MODEL and seq % Q_TILE == 0 and w_in.shape[0] == DEPTH == 1
    lp = PADL + N_META + seq
    nt = lp // TILE
    ntok = bsz * seq
    row2 = lambda v: v.reshape(1, -1).astype(_f32)

    xcat = jnp.concatenate([
        jnp.zeros((bsz, PADL, d), x.dtype),
        jnp.broadcast_to(meta_tokens[None].astype(x.dtype), (bsz, N_META, d)), x], axis=1)
    wi = w_in[0]
    o_a = 2 * GLA_QK + 2 * GLA_VW
    o_cq = o_a + GLA_GATE_RANK
    o_ckv = o_cq + MLA_Q_RANK
    o_kr = o_ckv + MLA_KV_RANK
    o_ga = o_kr + MLA_ROPE
    w_a = jnp.pad(wi[:, o_a:o_cq], ((0, 0), (0, LANES - GLA_GATE_RANK)))
    w_kr = jnp.pad(wi[:, o_kr:o_ga], ((0, 0), (MLA_NOPE, LANES - MLA_QDIM)))
    w_all = jnp.concatenate([wi[:, :o_a], w_a, wi[:, o_cq:o_kr], w_kr, wi[:, o_ga:]], axis=1).astype(_bf16)
    assert w_all.shape == (d, _W_COLS)
    w2p = jnp.pad(gla_gate_w2[0], ((0, LANES - GLA_GATE_RANK), (0, 0))).astype(_bf16)
    wuqt = _pad_heads(mla_w_uq[0], MLA_QDIM).T.astype(_bf16)
    wuk = _pad_heads(mla_w_uk[0], MLA_NOPE).astype(_bf16)
    wuvt = mla_w_uv[0].T.astype(_bf16)
    cos_t, sin_lo, sin_hi, cos_tr, sin_tr = _rope_tables(lp)
    blk = np.arange(TILE)
    tril_chunks = jnp.asarray(
        ((blk[:, None] >= blk[None, :]) & (blk[:, None] // GLA_CHUNK == blk[None, :] // GLA_CHUNK)),
        dtype=_bf16)

    pad_map = lambda b, t: (b, t, 0)
    real_map = lambda b, t: (b, jnp.maximum(t - 1, 0), 0)
    tab_spec = pl.BlockSpec((TILE, LANES), lambda b, t: (t, 0))
    tabt_spec = pl.BlockSpec((MLA_ROPE // 2, TILE), lambda b, t: (0, t))
    real_map_t = lambda b, t: (b, 0, jnp.maximum(t - 1, 0))
    out_shapes = (
        jax.ShapeDtypeStruct((bsz, seq, d), _f32),
        jax.ShapeDtypeStruct((bsz, lp, GLA_QK), _bf16),
        jax.ShapeDtypeStruct((bsz, lp, GLA_QK), _bf16),
        jax.ShapeDtypeStruct((bsz, lp, GLA_QK), _bf16),
        jax.ShapeDtypeStruct((bsz, lp, GLA_VW), _bf16),
        jax.ShapeDtypeStruct((bsz * nt, TILE // GLA_CHUNK, GLA_QK), _f32),
        jax.ShapeDtypeStruct((bsz, seq, GLA_VW), _bf16),
        jax.ShapeDtypeStruct((bsz, MLA_HEADS * HEAD_PAD, seq), _bf16),
        jax.ShapeDtypeStruct((bsz, lp, MLA_HEADS * HEAD_PAD), _bf16),
        jax.ShapeDtypeStruct((bsz, MLA_HEADS * V_AUG, lp), _bf16),
        jax.ShapeDtypeStruct((bsz, seq, d), _bf16),
        jax.ShapeDtypeStruct((bsz, seq, d), _bf16),
    )
    out_specs = (
        pl.BlockSpec((1, TILE, d), real_map),
        pl.BlockSpec((1, TILE, GLA_QK), pad_map),
        pl.BlockSpec((1, TILE, GLA_QK), pad_map),
        pl.BlockSpec((1, TILE, GLA_QK), pad_map),
        pl.BlockSpec((1, TILE, GLA_VW), pad_map),
        pl.BlockSpec((1, TILE // GLA_CHUNK, GLA_QK), lambda b, t: (b * nt + t, 0, 0)),
        pl.BlockSpec((1, TILE, GLA_VW), real_map),
        pl.BlockSpec((1, MLA_HEADS * HEAD_PAD, TILE), real_map_t),
        pl.BlockSpec((1, TILE, MLA_HEADS * HEAD_PAD), pad_map),
        pl.BlockSpec((1, MLA_HEADS * V_AUG, TILE), lambda b, t: (b, 0, t)),
        pl.BlockSpec((1, TILE, d), real_map),
        pl.BlockSpec((1, TILE, d), real_map),
    )
    (s_emb, qt, kt, ke, gv, dec, sr, qm, km, vm, gate_a, gate_b) = pl.pallas_call(
        _inproj_kernel,
        grid=(bsz, nt),
        in_specs=[
            pl.BlockSpec((1, TILE, d), pad_map),
            _const_spec((1, d)), _const_spec((1, d)),
            _const_spec((d, _W_COLS)),
            _const_spec((LANES, GLA_QK)), _const_spec((1, GLA_QK)),
            _const_spec((1, MLA_Q_RANK)), _const_spec((MLA_HEADS * HEAD_PAD, MLA_Q_RANK)),
            _const_spec((1, MLA_KV_RANK)), _const_spec((MLA_KV_RANK, MLA_HEADS * HEAD_PAD)),
            _const_spec((MLA_HEADS * MLA_DV, MLA_KV_RANK)),
            tab_spec, tab_spec, tab_spec, tabt_spec, tabt_spec,
            _const_spec((TILE, TILE)),
        ],
        out_specs=out_specs,
        out_shape=out_shapes,
        compiler_params=pltpu.CompilerParams(
            dimension_semantics=("arbitrary", "arbitrary"), vmem_limit_bytes=VMEM_LIMIT),
        name="inproj",
    )(xcat, row2(ln_emb_g), row2(ln_emb_b), w_all, w2p, row2(gla_gate_b[0]),
      row2(mla_q_norm_g[0]), wuqt, row2(mla_kv_norm_g[0]), wuk, wuvt, cos_t, sin_lo, sin_hi,
      cos_tr, sin_tr, tril_chunks)

    o_gla = pl.pallas_call(
        _gla_kernel,
        grid=(bsz, nt),
        in_specs=[
            pl.BlockSpec((1, TILE, GLA_QK), pad_map),
            pl.BlockSpec((1, TILE, GLA_QK), pad_map),
            pl.BlockSpec((1, TILE, GLA_QK), pad_map),
            pl.BlockSpec((1, TILE, GLA_VW), pad_map),
            pl.BlockSpec((1, TILE // GLA_CHUNK, GLA_QK), lambda b, t: (b * nt + t, 0, 0)),
            pl.BlockSpec((1, TILE, GLA_VW), real_map),
            _const_spec((1, GLA_DV)),
        ],
        out_specs=pl.BlockSpec((1, TILE, GLA_VW), real_map),
        out_shape=jax.ShapeDtypeStruct((bsz, seq, GLA_VW), _bf16),
        scratch_shapes=[pltpu.VMEM((GLA_HEADS, GLA_DV, GLA_DK), _f32)],
        compiler_params=pltpu.CompilerParams(
            dimension_semantics=("arbitrary", "arbitrary"), vmem_limit_bytes=VMEM_LIMIT),
        name="gla",
    )(qt, kt, ke, gv, dec, sr, row2(gla_norm_g[0]))

    pair = 2 * HEAD_PAD
    o_mla = pl.pallas_call(
        _mla_kernel,
        grid=(bsz, MLA_HEADS // 2, seq // Q_TILE),
        in_specs=[
            pl.BlockSpec((1, pair, Q_TILE), lambda b, hp, i: (b, hp, i)),
            pl.BlockSpec((1, lp, pair), lambda b, hp, i: (b, 0, hp)),
            pl.BlockSpec((1, 2 * V_AUG, lp), lambda b, hp, i: (b, hp, 0)),
        ],
        out_specs=pl.BlockSpec((1, Q_TILE, 2 * MLA_DV), lambda b, hp, i: (b, i, hp)),
        out_shape=jax.ShapeDtypeStruct((bsz, seq, MLA_HEADS * MLA_DV), _bf16),
        scratch_shapes=[pltpu.VMEM((2, V_AUG, Q_TILE), _f32)],
        compiler_params=pltpu.CompilerParams(
            dimension_semantics=("arbitrary", "arbitrary", "arbitrary"),
            vmem_limit_bytes=VMEM_LIMIT),
        name="mla",
    )(qm, km, vm)

    rw = jnp.concatenate([router_group_w[0], router_expert_w[0]], axis=1)
    rw = jnp.pad(rw, ((0, 0), (0, LANES - rw.shape[1])))
    rwh = rw.astype(_bf16)
    rwl = (rw - rwh.astype(_f32)).astype(_bf16)
    rb = jnp.concatenate([router_group_b[0], router_expert_b[0]])
    rb = jnp.pad(rb, (0, LANES - rb.shape[0])).reshape(1, LANES)
    mi = np.arange(MERGE_TILE)
    tril_strict = jnp.asarray(mi[:, None] > mi[None, :], dtype=_bf16)
    flat = lambda a: a.reshape(ntok, a.shape[-1])
    tok_spec = lambda w: pl.BlockSpec((MERGE_TILE, w), lambda g: (g, 0))
    s2, info, cnt = pl.pallas_call(
        _merge_kernel,
        grid=(ntok // MERGE_TILE,),
        in_specs=[tok_spec(d), tok_spec(d), tok_spec(d), tok_spec(d), tok_spec(d),
                  _const_spec((d, d)), _const_spec((d, d)), _const_spec((d, d)),
                  _const_spec((1, d)), _const_spec((1, d)),
                  _const_spec((d, LANES)), _const_spec((d, LANES)), _const_spec((1, LANES)),
                  _const_spec((MERGE_TILE, MERGE_TILE))],
        out_specs=(tok_spec(d), tok_spec(LANES), _const_spec((8, LANES))),
        out_shape=(jax.ShapeDtypeStruct((ntok, d), _f32),
                   jax.ShapeDtypeStruct((ntok, LANES), _f32),
                   jax.ShapeDtypeStruct((8, LANES), _f32)),
        scratch_shapes=[pltpu.VMEM((8, LANES), _f32)],
        compiler_params=pltpu.CompilerParams(
            dimension_semantics=("arbitrary",), vmem_limit_bytes=VMEM_LIMIT),
        name="merge_router",
    )(flat(o_gla), flat(o_mla), flat(gate_a), flat(gate_b), flat(s_emb),
      w_branch_gla[0].astype(_bf16), w_branch_mla[0].astype(_bf16), w_out[0].astype(_bf16),
      row2(ln_mix_g[0]), row2(ln_mix_b[0]), rwh, rwl, rb, tril_strict)

    n_tiles = (2 * ntok + N_EXPERTS * (EXPERT_TILE - 1)) // EXPERT_TILE
    n_rows = n_tiles * EXPERT_TILE
    e_idx = info[:, 0:2].astype(jnp.int32)
    rank = info[:, 4:6].astype(jnp.int32)
    counts = cnt[0, :N_EXPERTS].astype(jnp.int32)
    padded = ((counts + EXPERT_TILE - 1) // EXPERT_TILE) * EXPERT_TILE
    ends = jnp.cumsum(padded)
    starts = ends - padded
    pos = (starts[e_idx] + rank).reshape(-1)
    tile_start = jnp.arange(n_tiles, dtype=jnp.int32) * EXPERT_TILE
    tile_expert = jnp.minimum(
        jnp.sum((ends[None, :] <= tile_start[:, None]).astype(jnp.int32), axis=1), N_EXPERTS - 1)
    n_used = (ends[-1:] // EXPERT_TILE).astype(jnp.int32)
    zero_row = jnp.where(padded > 0, ends - EXPERT_TILE, -1).astype(jnp.int32)

    any_spec = pl.BlockSpec(memory_space=pl.ANY)
    xs = pl.pallas_call(
        _dispatch_kernel,
        grid_spec=pltpu.PrefetchScalarGridSpec(
            num_scalar_prefetch=3,
            grid=(ntok // DISPATCH_TILE,),
            in_specs=[pl.BlockSpec((DISPATCH_TILE, d), lambda g, p, z, nu: (g, 0))],
            out_specs=any_spec,
            scratch_shapes=[pltpu.VMEM((EXPERT_TILE, d), _f32), pltpu.SemaphoreType.DMA((2,))],
        ),
        out_shape=jax.ShapeDtypeStruct((n_rows, d), _f32),
        compiler_params=pltpu.CompilerParams(
            dimension_semantics=("arbitrary",), vmem_limit_bytes=VMEM_LIMIT, has_side_effects=True),
        name="dispatch",
    )(pos, zero_row, n_used, s2)

    ff = EXPERT_FF
    wg = expert_w_gate[0].reshape(N_EXPERTS, d, ff).astype(_bf16)
    wu = expert_w_up[0].reshape(N_EXPERTS, d, ff).astype(_bf16)
    wd = expert_w_down[0].reshape(N_EXPERTS, ff, d).astype(_bf16)
    ys = pl.pallas_call(
        _expert_kernel,
        grid_spec=pltpu.PrefetchScalarGridSpec(
            num_scalar_prefetch=2,
            grid=(n_tiles,),
            in_specs=[
                pl.BlockSpec((EXPERT_TILE, d), lambda u, te, nu: (jnp.minimum(u, nu[0] - 1), 0)),
                pl.BlockSpec((1, d, ff), lambda u, te, nu: (te[u], 0, 0)),
                pl.BlockSpec((1, d, ff), lambda u, te, nu: (te[u], 0, 0)),
                pl.BlockSpec((1, ff, d), lambda u, te, nu: (te[u], 0, 0)),
            ],
            out_specs=pl.BlockSpec((EXPERT_TILE, d), lambda u, te, nu: (u, 0)),
        ),
        out_shape=jax.ShapeDtypeStruct((n_rows, d), _f32),
        compiler_params=pltpu.CompilerParams(
            dimension_semantics=("arbitrary",), vmem_limit_bytes=VMEM_LIMIT),
        name="experts",
    )(tile_expert, n_used, xs, wg, wu, wd)

    out = pl.pallas_call(
        _combine_kernel,
        grid_spec=pltpu.PrefetchScalarGridSpec(
            num_scalar_prefetch=1,
            grid=(ntok // COMBINE_TILE,),
            in_specs=[
                pl.BlockSpec((COMBINE_TILE, d), lambda g, p: (g, 0)),
                pl.BlockSpec((COMBINE_TILE, LANES), lambda g, p: (g, 0)),
                any_spec,
                pl.BlockSpec((1, d), lambda g, p: (0, 0)),
                pl.BlockSpec((1, d), lambda g, p: (0, 0)),
            ],
            out_specs=pl.BlockSpec((COMBINE_TILE, d), lambda g, p: (g, 0)),
            scratch_shapes=[pltpu.VMEM((2, 2, COMBINE_TILE, d), _f32),
                            pltpu.SemaphoreType.DMA((2,))],
        ),
        out_shape=jax.ShapeDtypeStruct((ntok, d), _f32),
        compiler_params=pltpu.CompilerParams(
            dimension_semantics=("arbitrary",), vmem_limit_bytes=VMEM_LIMIT),
        name="combine_ln",
    )(pos, s2, info, ys, row2(ln_ffn_g[0]), row2(ln_ffn_b[0]))
    return out.reshape(bsz, seq, d)
```

```python
import jax
import jax.numpy as jnp
import numpy as np
from jax import lax
from jax.experimental import pallas as pl
from jax.experimental.pallas import tpu as pltpu

D_MODEL = 1024
N_META = 16
GLA_HEADS = 4
GLA_DK = 128
GLA_DV = 256
GLA_QK = GLA_HEADS * GLA_DK
GLA_VW = GLA_HEADS * GLA_DV
GLA_GATE_RANK = 16
GLA_GATE_TAU = 16.0
GLA_CHUNK = 64
MLA_HEADS = 16
MLA_Q_RANK = 384
MLA_KV_RANK = 256
MLA_NOPE = 64
MLA_ROPE = 32
MLA_DV = 64
MLA_QDIM = MLA_NOPE + MLA_ROPE
ROPE_BASE = 10000.0
N_GROUPS = 4
EXPERTS_PER_GROUP = 8
N_EXPERTS = N_GROUPS * EXPERTS_PER_GROUP
EXPERT_FF = 256
DEPTH = 1
ALPHA = (2.0 * DEPTH) ** 0.25
LN_EPS = 1e-5
RMS_EPS = 1e-6

LANES = 128
TILE = 256
PADL = TILE - N_META
HEAD_PAD = LANES
BF16_ROWS = 16
KV_TILE = 512
Q_TILE = 2 * KV_TILE
V_AUG = MLA_DV + BF16_ROWS
MERGE_TILE = 512
EXPERT_TILE = 256
DISPATCH_TILE = 512
COMBINE_TILE = 256
NEG = -1e30
LOG2E = 1.4426950408889634
BOUND_SLACK = 1.02
MIN_SOFTMAX_SUM = 2.0 ** -100
VMEM_LIMIT = 56 * 1024 * 1024

_C_Q, _C_K, _C_V, _C_R = 0, 512, 1024, 2048
_C_A = 3072
_C_CQ = _C_A + LANES
_C_CKV = _C_CQ + MLA_Q_RANK
_C_KR = _C_CKV + MLA_KV_RANK
_C_GA = _C_KR + LANES
_C_GB = _C_GA + D_MODEL
_W_COLS = _C_GB + D_MODEL

_f32 = jnp.float32
_bf16 = jnp.bfloat16


def _dot(a, b):
    return jnp.dot(a, b, preferred_element_type=_f32)


def _dot_nt(a, b):
    return lax.dot_general(a, b, (((1,), (1,)), ((), ())), preferred_element_type=_f32)


def _dot_tn(a, b):
    return lax.dot_general(a, b, (((0,), (0,)), ((), ())), preferred_element_type=_f32)


def _layer_norm(x, g, b):
    mu = jnp.mean(x, axis=-1, keepdims=True)
    xc = x - mu
    var = jnp.mean(xc * xc, axis=-1, keepdims=True)
    return xc * lax.rsqrt(var + LN_EPS) * g + b


def _rms_norm(x, g):
    ms = jnp.mean(x * x, axis=-1, keepdims=True)
    return x * lax.rsqrt(ms + RMS_EPS) * g


def _sigmoid(x):
    return 1.0 / (1.0 + jnp.exp(-x))


def _rope(x, cos, sin_lo, sin_hi):
    half = MLA_ROPE // 2
    from_hi = pltpu.roll(x, LANES - half, 1)
    from_lo = pltpu.roll(x, half, 1)
    return x * cos + from_hi * sin_lo + from_lo * sin_hi


def _inproj_kernel(x_ref, lng_ref, lnb_ref, w_ref, w2_ref, gb_ref, qg_ref, wuqt_ref, kvg_ref,
                   wuk_ref, wuvt_ref, cos_ref, sl_ref, sh_ref, cost_ref, sint_ref, tril_ref,
                   s_ref, qt_ref, kt_ref, ke_ref, gv_ref, dec_ref, sr_ref, qm_ref, km_ref,
                   vm_ref, ga_ref, gbt_ref):
    t = pl.program_id(1)
    sn = _layer_norm(x_ref[0], lng_ref[...], lnb_ref[...])
    s_ref[0] = sn
    snb = sn.astype(_bf16)
    row = t * TILE + lax.broadcasted_iota(jnp.int32, (TILE, 1), 0)
    valid = row >= PADL

    a_lr = _dot(snb, w_ref[:, _C_A:_C_A + LANES])
    z = _dot(a_lr.astype(_bf16), w2_ref[...]) + gb_ref[...]
    la = (jnp.minimum(z, 0.0) - jnp.log1p(jnp.exp(-jnp.abs(z)))) * (1.0 / GLA_GATE_TAU)
    la = jnp.where(valid, la, 0.0)
    hi = la.astype(_bf16)
    r1 = la - hi.astype(_f32)
    mid = r1.astype(_bf16)
    lo = (r1 - mid.astype(_f32)).astype(_bf16)
    tril = tril_ref[...]
    bc = _dot(tril, hi) + _dot(tril, mid) + _dot(tril, lo)
    n_chunks = TILE // GLA_CHUNK
    lasts = [bc[c * GLA_CHUNK + GLA_CHUNK - 1:(c + 1) * GLA_CHUNK, :] for c in range(n_chunks)]
    for c in range(n_chunks):
        dec_ref[0, c:c + 1, :] = jnp.exp(lasts[c])
    b_last = jnp.concatenate(
        [jnp.broadcast_to(l, (GLA_CHUNK, GLA_QK)) for l in lasts], axis=0)
    gq = _dot(snb, w_ref[:, _C_Q:_C_Q + GLA_QK])
    gk = jnp.where(valid, _dot(snb, w_ref[:, _C_K:_C_K + GLA_QK]), 0.0)
    qt_ref[0] = (gq * (GLA_DK ** -0.5) * jnp.exp(bc)).astype(_bf16)
    kt_ref[0] = (gk * jnp.exp(-bc)).astype(_bf16)
    ke_ref[0] = (gk * jnp.exp(b_last - bc)).astype(_bf16)
    gv_ref[0] = jnp.where(valid, _dot(snb, w_ref[:, _C_V:_C_V + GLA_VW]), 0.0).astype(_bf16)
    r = _dot(snb, w_ref[:, _C_R:_C_R + GLA_VW])
    sr_ref[0] = (r * _sigmoid(r)).astype(_bf16)

    cos = cos_ref[...]
    sl = sl_ref[...]
    sh = sh_ref[...]
    cq = _dot(snb, w_ref[:, _C_CQ:_C_CQ + MLA_Q_RANK])
    cqn = _rms_norm(cq, qg_ref[...]).astype(_bf16)
    qft = _dot_nt(wuqt_ref[...], cqn)
    scale = (MLA_QDIM ** -0.5) * LOG2E
    cost = cost_ref[...]
    sint = sint_ref[...]
    half = MLA_ROPE // 2
    for h in range(MLA_HEADS):
        base = h * HEAD_PAD
        x1 = qft[base + MLA_NOPE:base + MLA_NOPE + half]
        x2 = qft[base + MLA_NOPE + half:base + MLA_QDIM]
        qm_ref[0, base:base + MLA_NOPE] = (qft[base:base + MLA_NOPE] * scale).astype(_bf16)
        qm_ref[0, base + MLA_NOPE:base + MLA_NOPE + half] = ((x1 * cost - x2 * sint) * scale).astype(_bf16)
        qm_ref[0, base + MLA_NOPE + half:base + MLA_QDIM] = ((x1 * sint + x2 * cost) * scale).astype(_bf16)
        qm_ref[0, base + MLA_QDIM:base + HEAD_PAD] = jnp.zeros((HEAD_PAD - MLA_QDIM, TILE), _bf16)
    ckv = _dot(snb, w_ref[:, _C_CKV:_C_CKV + MLA_KV_RANK])
    ckvn = _rms_norm(ckv, kvg_ref[...]).astype(_bf16)
    kf = _dot(ckvn, wuk_ref[...])
    kr = _rope(_dot(snb, w_ref[:, _C_KR:_C_KR + LANES]), cos, sl, sh)
    for h in range(MLA_HEADS):
        km_ref[0, :, h * HEAD_PAD:(h + 1) * HEAD_PAD] = (
            kf[:, h * HEAD_PAD:(h + 1) * HEAD_PAD] + kr).astype(_bf16)
    vt = _dot_nt(wuvt_ref[...], ckvn)
    for h in range(MLA_HEADS):
        vm_ref[0, h * V_AUG:h * V_AUG + MLA_DV] = vt[h * MLA_DV:(h + 1) * MLA_DV].astype(_bf16)
        vm_ref[0, h * V_AUG + MLA_DV:(h + 1) * V_AUG] = jnp.ones((V_AUG - MLA_DV, TILE), _bf16)

    ga_ref[0] = _sigmoid(_dot(snb, w_ref[:, _C_GA:_C_GA + D_MODEL])).astype(_bf16)
    gbt_ref[0] = _sigmoid(_dot(snb, w_ref[:, _C_GB:_C_GB + D_MODEL])).astype(_bf16)


def _gla_kernel(qt_ref, kt_ref, ke_ref, gv_ref, dec_ref, sr_ref, ng_ref, o_ref, st_ref):
    t = pl.program_id(1)

    @pl.when(t == 0)
    def _():
        st_ref[...] = jnp.zeros_like(st_ref)

    ri = lax.broadcasted_iota(jnp.int32, (GLA_CHUNK, GLA_CHUNK), 0)
    ci = lax.broadcasted_iota(jnp.int32, (GLA_CHUNK, GLA_CHUNK), 1)
    causal = ri >= ci
    ng = ng_ref[...]
    for c in range(TILE // GLA_CHUNK):
        rows = slice(c * GLA_CHUNK, (c + 1) * GLA_CHUNK)
        for h in range(GLA_HEADS):
            kc = slice(h * GLA_DK, (h + 1) * GLA_DK)
            vc = slice(h * GLA_DV, (h + 1) * GLA_DV)
            q = qt_ref[0, rows, kc]
            k = kt_ref[0, rows, kc]
            e = ke_ref[0, rows, kc]
            v = gv_ref[0, rows, vc]
            st = st_ref[h]
            att = jnp.where(causal, _dot_nt(q, k), 0.0)
            o = _dot(att.astype(_bf16), v) + _dot_nt(q, st.astype(_bf16))
            st_ref[h] = st * dec_ref[0, c:c + 1, kc] + _dot_tn(v, e)
            o = _rms_norm(o, ng) * sr_ref[0, rows, vc].astype(_f32)
            o_ref[0, rows, vc] = o.astype(_bf16)


MLA_PAIR = 2


def _mla_tile_start(j):
    return pl.multiple_of(TILE + j * KV_TILE, TILE)


def _mla_finish(o_ref, acc_ref):
    outs = []
    for h in range(MLA_PAIR):
        a = acc_ref[h]
        outs.append(a[0:MLA_DV] / a[MLA_DV:MLA_DV + 1])
    o_ref[0] = jnp.concatenate(outs, axis=0).T.astype(_bf16)


def _mla_kernel(q_ref, k_ref, v_ref, o_ref, acc_ref, knorm_ref):
    i = pl.program_id(2)
    heads = MLA_PAIR
    ones = jnp.ones((HEAD_PAD, LANES), _bf16)

    @pl.when(i == 0)
    def _():
        for h in range(heads):
            kk = k_ref[0, :, h * HEAD_PAD:(h + 1) * HEAD_PAD].astype(_f32)
            hi = (kk * kk).astype(_bf16)
            knorm_ref[h] = jnp.max(_dot(hi, ones), axis=0, keepdims=True)

    half = slice(KV_TILE, Q_TILE)
    q_t = [q_ref[0, h * HEAD_PAD:(h + 1) * HEAD_PAD, :] for h in range(heads)]
    q_hi = [q_ref[0, h * HEAD_PAD:(h + 1) * HEAD_PAD, half] for h in range(heads)]

    def score_bound(h, q):
        qq = q.astype(_f32)
        qn2 = jnp.sum(qq * qq, axis=0, keepdims=True)
        return jnp.sqrt(qn2 * knorm_ref[h][:, 0:1]) * BOUND_SLACK

    bound = [score_bound(h, q_t[h]) for h in range(heads)]
    bound_hi = [score_bound(h, q_hi[h]) for h in range(heads)]

    def keys(j, h):
        return k_ref[0, pl.ds(_mla_tile_start(j), KV_TILE), h * HEAD_PAD:(h + 1) * HEAD_PAD]

    def weighted(j, h, p):
        return _dot(v_ref[0, h * V_AUG:(h + 1) * V_AUG, pl.ds(_mla_tile_start(j), KV_TILE)], p)

    k_row = lax.broadcasted_iota(jnp.int32, (KV_TILE, Q_TILE), 0)
    q_col = lax.broadcasted_iota(jnp.int32, (KV_TILE, Q_TILE), 1)
    k_row_sq = lax.broadcasted_iota(jnp.int32, (KV_TILE, KV_TILE), 0)
    q_col_sq = lax.broadcasted_iota(jnp.int32, (KV_TILE, KV_TILE), 1)
    s_meta = [_dot(k_ref[0, PADL:TILE, h * HEAD_PAD:(h + 1) * HEAD_PAD], q_t[h]) for h in range(heads)]
    s_lo = [jnp.where(k_row <= q_col, _dot(keys(2 * i, h), q_t[h]), NEG) for h in range(heads)]
    s_hi = [jnp.where(k_row_sq <= q_col_sq, _dot(keys(2 * i + 1, h), q_hi[h]), NEG)
            for h in range(heads)]
    for h in range(heads):
        p_meta = jnp.concatenate([jnp.zeros((PADL, Q_TILE), _bf16),
                                  jnp.exp2(s_meta[h] - bound[h]).astype(_bf16)], axis=0)
        acc_ref[h] = (_dot(v_ref[0, h * V_AUG:(h + 1) * V_AUG, 0:TILE], p_meta)
                      + weighted(2 * i, h, jnp.exp2(s_lo[h] - bound[h]).astype(_bf16)))
        acc_ref[h, :, half] += weighted(2 * i + 1, h, jnp.exp2(s_hi[h] - bound_hi[h]).astype(_bf16))

    def body(j, c):
        s = [_dot(keys(j, h), q_t[h]) for h in range(heads)]
        for h in range(heads):
            acc_ref[h] += weighted(j, h, jnp.exp2(s[h] - bound[h]).astype(_bf16))
        return c

    lax.fori_loop(0, 2 * i, body, 0)
    _mla_finish(o_ref, acc_ref)

    l_min = jnp.minimum(jnp.min(acc_ref[0, MLA_DV:MLA_DV + 1, :]), jnp.min(acc_ref[1, MLA_DV:MLA_DV + 1, :]))

    @pl.when(jnp.logical_not(l_min >= MIN_SOFTMAX_SUM))
    def _():
        _mla_exact(q_ref, k_ref, v_ref, o_ref, acc_ref)


def _mla_exact(q_ref, k_ref, v_ref, o_ref, acc_ref):
    i = pl.program_id(2)
    heads = MLA_PAIR
    q_t = [q_ref[0, h * HEAD_PAD:(h + 1) * HEAD_PAD, :] for h in range(heads)]

    ms = []
    for h in range(heads):
        kb = k_ref[0, PADL:TILE, h * HEAD_PAD:(h + 1) * HEAD_PAD]
        s = _dot(kb, q_t[h])
        m0 = jnp.max(s, axis=0, keepdims=True)
        p = jnp.concatenate([jnp.zeros((PADL, Q_TILE), _bf16), jnp.exp2(s - m0).astype(_bf16)], axis=0)
        acc_ref[h] = _dot(v_ref[0, h * V_AUG:(h + 1) * V_AUG, 0:TILE], p)
        ms.append(m0)

    k_row = lax.broadcasted_iota(jnp.int32, (KV_TILE, Q_TILE), 0)
    q_col = lax.broadcasted_iota(jnp.int32, (KV_TILE, Q_TILE), 1)

    def body(j, ms):
        visible = (j - 2 * i) * KV_TILE + k_row <= q_col
        out = []
        for h in range(heads):
            s = _dot(k_ref[0, pl.ds(_mla_tile_start(j), KV_TILE), h * HEAD_PAD:(h + 1) * HEAD_PAD],
                     q_t[h])
            s = jnp.where(visible, s, NEG)
            vb = v_ref[0, h * V_AUG:(h + 1) * V_AUG, pl.ds(_mla_tile_start(j), KV_TILE)]
            m_new = jnp.maximum(ms[h], jnp.max(s, axis=0, keepdims=True))
            alpha = jnp.exp2(ms[h] - m_new)
            acc_ref[h] = alpha * acc_ref[h] + _dot(vb, jnp.exp2(s - m_new).astype(_bf16))
            out.append(m_new)
        return tuple(out)

    lax.fori_loop(0, 2 * i + 2, body, tuple(ms))
    _mla_finish(o_ref, acc_ref)


def _merge_kernel(og_ref, om_ref, ga_ref, gbt_ref, s_ref, wbg_ref, wbm_ref, wo_ref, lng_ref,
                  lnb_ref, rwh_ref, rwl_ref, rb_ref, tril_ref,
                  s2_ref, info_ref, cnt_ref, carry_ref):
    step = pl.program_id(0)

    @pl.when(step == 0)
    def _():
        carry_ref[...] = jnp.zeros_like(carry_ref)

    ba = _dot(og_ref[...], wbg_ref[...])
    bb = _dot(om_ref[...], wbm_ref[...])
    merged = ga_ref[...].astype(_f32) * ba + gbt_ref[...].astype(_f32) * bb
    y = ALPHA * s_ref[...] + _dot(merged.astype(_bf16), wo_ref[...])
    s2 = _layer_norm(y, lng_ref[...], lnb_ref[...])
    s2_ref[...] = s2

    xh = s2.astype(_bf16)
    xl = (s2 - xh.astype(_f32)).astype(_bf16)
    logits = _dot(xh, rwh_ref[...]) + _dot(xl, rwh_ref[...]) + _dot(xh, rwl_ref[...]) + rb_ref[...]
    lane = lax.broadcasted_iota(jnp.int32, (MERGE_TILE, LANES), 1)
    is_g = lane < N_GROUPS
    gl = jnp.where(is_g, logits, NEG)
    gmax = jnp.max(gl, axis=-1, keepdims=True)
    gidx = jnp.min(jnp.where(gl == gmax, lane, LANES), axis=-1, keepdims=True)
    p_g = 1.0 / jnp.sum(jnp.where(is_g, jnp.exp(gl - gmax), 0.0), axis=-1, keepdims=True)
    lo = N_GROUPS + EXPERTS_PER_GROUP * gidx
    el = jnp.where((lane >= lo) & (lane < lo + EXPERTS_PER_GROUP), logits, NEG)
    v1 = jnp.max(el, axis=-1, keepdims=True)
    i1 = jnp.min(jnp.where(el == v1, lane, LANES), axis=-1, keepdims=True)
    el2 = jnp.where(lane == i1, NEG, el)
    v2 = jnp.max(el2, axis=-1, keepdims=True)
    i2 = jnp.min(jnp.where(el2 == v2, lane, LANES), axis=-1, keepdims=True)
    tt = jnp.exp(v2 - v1)
    p1 = 1.0 / (1.0 + tt)
    p2 = tt / (1.0 + tt)
    e1 = i1 - N_GROUPS
    e2 = i2 - N_GROUPS
    hit1 = lane == e1
    hit2 = lane == e2
    onehot = jnp.where(hit1 | hit2, 1.0, 0.0)
    before = _dot(tril_ref[...], onehot.astype(_bf16)) + carry_ref[0:1, :]
    r1 = jnp.sum(jnp.where(hit1, before, 0.0), axis=-1, keepdims=True)
    r2 = jnp.sum(jnp.where(hit2, before, 0.0), axis=-1, keepdims=True)
    new_carry = carry_ref[0:1, :] + jnp.sum(onehot, axis=0, keepdims=True)
    carry_ref[...] = jnp.broadcast_to(new_carry, carry_ref.shape)
    cnt_ref[...] = jnp.broadcast_to(new_carry, cnt_ref.shape)
    info = jnp.where(lane == 0, e1.astype(_f32),
           jnp.where(lane == 1, e2.astype(_f32),
           jnp.where(lane == 2, p_g * p1,
           jnp.where(lane == 3, p_g * p2,
           jnp.where(lane == 4, r1,
           jnp.where(lane == 5, r2, 0.0))))))
    info_ref[...] = info


def _dispatch_kernel(pos_ref, zrow_ref, nused_ref, s2_ref, xs_hbm, zero_ref, sems):
    g = pl.program_id(0)
    zero_sem = sems.at[1]
    row_sem = sems.at[0]

    def zero_copy(row):
        row = pl.multiple_of(row, EXPERT_TILE)
        return pltpu.make_async_copy(zero_ref, xs_hbm.at[pl.ds(row, EXPERT_TILE)], zero_sem)

    @pl.when(g == 0)
    def _():
        zero_ref[...] = jnp.zeros_like(zero_ref)

        def start(e, c):
            @pl.when(zrow_ref[e] >= 0)
            def _():
                zero_copy(zrow_ref[e]).start()
            return c

        def wait(e, c):
            @pl.when(zrow_ref[e] >= 0)
            def _():
                zero_copy(0).wait()
            return c

        def start_tail(u, c):
            zero_copy(u * EXPERT_TILE).start()
            return c

        def wait_tail(u, c):
            zero_copy(0).wait()
            return c

        n_tiles = xs_hbm.shape[0] // EXPERT_TILE
        lax.fori_loop(0, N_EXPERTS, start, 0)
        lax.fori_loop(nused_ref[0], n_tiles, start_tail, 0)
        lax.fori_loop(0, N_EXPERTS, wait, 0)
        lax.fori_loop(nused_ref[0], n_tiles, wait_tail, 0)

    def issue(r, c):
        tok = g * DISPATCH_TILE + r
        for k in range(2):
            pltpu.make_async_copy(s2_ref.at[pl.ds(r, 1)],
                                  xs_hbm.at[pl.ds(pos_ref[2 * tok + k], 1)], row_sem).start()
        return c

    lax.fori_loop(0, DISPATCH_TILE, issue, 0, unroll=8)
    for k in range(2):
        pltpu.make_async_copy(s2_ref, xs_hbm.at[pl.ds(0, DISPATCH_TILE)], row_sem).wait()


def _expert_kernel(te_ref, nused_ref, x_ref, wg_ref, wu_ref, wd_ref, o_ref):
    u = pl.program_id(0)

    @pl.when(u < nused_ref[0])
    def _():
        x = x_ref[...].astype(_bf16)
        a = _dot(x, wg_ref[0])
        up = _dot(x, wu_ref[0])
        hid = a * _sigmoid(a) * up
        o_ref[...] = _dot(hid.astype(_bf16), wd_ref[0])

    @pl.when(u >= nused_ref[0])
    def _():
        o_ref[...] = jnp.zeros_like(o_ref)


def _combine_kernel(pos_ref, s2_ref, info_ref, ys_hbm, lng_ref, lnb_ref, o_ref, buf_ref, sems):
    g = pl.program_id(0)
    last = pl.num_programs(0) - 1

    def issue_tile(tile, slot):
        def body(r, c):
            tok = tile * COMBINE_TILE + r
            for k in range(2):
                pltpu.make_async_copy(ys_hbm.at[pl.ds(pos_ref[2 * tok + k], 1)],
                                      buf_ref.at[slot, k, pl.ds(r, 1)], sems.at[slot]).start()
            return c

        lax.fori_loop(0, COMBINE_TILE, body, 0, unroll=8)

    @pl.when(g == 0)
    def _():
        issue_tile(0, 0)

    @pl.when(g < last)
    def _():
        issue_tile(g + 1, (g + 1) % 2)

    slot = g % 2
    for k in range(2):
        pltpu.make_async_copy(ys_hbm.at[pl.ds(0, COMBINE_TILE)], buf_ref.at[slot, k],
                              sems.at[slot]).wait()
    info = info_ref[...]
    y = (ALPHA * s2_ref[...] + info[:, 2:3] * buf_ref[slot, 0] + info[:, 3:4] * buf_ref[slot, 1])
    o_ref[...] = _layer_norm(y, lng_ref[...], lnb_ref[...])


def _const_spec(shape):
    nd = len(shape)
    return pl.BlockSpec(shape, lambda *_: (0,) * nd)


def _rope_tables(lp):
    pos = jnp.maximum(jnp.arange(lp, dtype=_f32) - PADL, 0.0)
    inv_freq = ROPE_BASE ** (-jnp.arange(0, MLA_ROPE, 2, dtype=_f32) / MLA_ROPE)
    ang = pos[:, None] * inv_freq[None, :]
    cos, sin = jnp.cos(ang), jnp.sin(ang)
    half = MLA_ROPE // 2
    ones = jnp.ones((lp, MLA_NOPE), _f32)
    zeros_n = jnp.zeros((lp, MLA_NOPE), _f32)
    zeros_h = jnp.zeros((lp, half), _f32)
    tail1 = jnp.ones((lp, LANES - MLA_QDIM), _f32)
    tail0 = jnp.zeros((lp, LANES - MLA_QDIM), _f32)
    cos_t = jnp.concatenate([ones, cos, cos, tail1], axis=1)
    sin_lo = jnp.concatenate([zeros_n, -sin, zeros_h, tail0], axis=1)
    sin_hi = jnp.concatenate([zeros_n, zeros_h, sin, tail0], axis=1)
    return cos_t, sin_lo, sin_hi, cos.T, sin.T


def _pad_heads(w, width):
    k = w.shape[0]
    w = w.reshape(k, MLA_HEADS, width)
    w = jnp.pad(w, ((0, 0), (0, 0), (0, HEAD_PAD - width)))
    return w.reshape(k, MLA_HEADS * HEAD_PAD)


def kernel(x, meta_tokens, ln_emb_g, ln_emb_b, w_in, gla_gate_w2, gla_gate_b, gla_norm_g, mla_q_norm_g, mla_w_uq, mla_kv_norm_g, mla_w_uk, mla_w_uv, w_branch_gla, w_branch_mla, w_out, ln_mix_g, ln_mix_b, router_group_w, router_group_b, router_expert_w, router_expert_b, expert_w_gate, expert_w_up, expert_w_down, ln_ffn_g, ln_ffn_b):
    bsz, seq, d = x.shape
    assert d == D_MODEL and seq % Q_TILE == 0 and w_in.shape[0] == DEPTH == 1
    lp = PADL + N_META + seq
    nt = lp // TILE
    ntok = bsz * seq
    row2 = lambda v: v.reshape(1, -1).astype(_f32)

    xcat = jnp.concatenate([
        jnp.zeros((bsz, PADL, d), x.dtype),
        jnp.broadcast_to(meta_tokens[None].astype(x.dtype), (bsz, N_META, d)), x], axis=1)
    wi = w_in[0]
    o_a = 2 * GLA_QK + 2 * GLA_VW
    o_cq = o_a + GLA_GATE_RANK
    o_ckv = o_cq + MLA_Q_RANK
    o_kr = o_ckv + MLA_KV_RANK
    o_ga = o_kr + MLA_ROPE
    w_a = jnp.pad(wi[:, o_a:o_cq], ((0, 0), (0, LANES - GLA_GATE_RANK)))
    w_kr = jnp.pad(wi[:, o_kr:o_ga], ((0, 0), (MLA_NOPE, LANES - MLA_QDIM)))
    w_all = jnp.concatenate([wi[:, :o_a], w_a, wi[:, o_cq:o_kr], w_kr, wi[:, o_ga:]], axis=1).astype(_bf16)
    assert w_all.shape == (d, _W_COLS)
    w2p = jnp.pad(gla_gate_w2[0], ((0, LANES - GLA_GATE_RANK), (0, 0))).astype(_bf16)
    wuqt = _pad_heads(mla_w_uq[0], MLA_QDIM).T.astype(_bf16)
    wuk = _pad_heads(mla_w_uk[0], MLA_NOPE).astype(_bf16)
    wuvt = mla_w_uv[0].T.astype(_bf16)
    cos_t, sin_lo, sin_hi, cos_tr, sin_tr = _rope_tables(lp)
    blk = np.arange(TILE)
    tril_chunks = jnp.asarray(
        ((blk[:, None] >= blk[None, :]) & (blk[:, None] // GLA_CHUNK == blk[None, :] // GLA_CHUNK)),
        dtype=_bf16)

    pad_map = lambda b, t: (b, t, 0)
    real_map = lambda b, t: (b, jnp.maximum(t - 1, 0), 0)
    tab_spec = pl.BlockSpec((TILE, LANES), lambda b, t: (t, 0))
    tabt_spec = pl.BlockSpec((MLA_ROPE // 2, TILE), lambda b, t: (0, t))
    real_map_t = lambda b, t: (b, 0, jnp.maximum(t - 1, 0))
    out_shapes = (
        jax.ShapeDtypeStruct((bsz, seq, d), _f32),
        jax.ShapeDtypeStruct((bsz, lp, GLA_QK), _bf16),
        jax.ShapeDtypeStruct((bsz, lp, GLA_QK), _bf16),
        jax.ShapeDtypeStruct((bsz, lp, GLA_QK), _bf16),
        jax.ShapeDtypeStruct((bsz, lp, GLA_VW), _bf16),
        jax.ShapeDtypeStruct((bsz * nt, TILE // GLA_CHUNK, GLA_QK), _f32),
        jax.ShapeDtypeStruct((bsz, seq, GLA_VW), _bf16),
        jax.ShapeDtypeStruct((bsz, MLA_HEADS * HEAD_PAD, seq), _bf16),
        jax.ShapeDtypeStruct((bsz, lp, MLA_HEADS * HEAD_PAD), _bf16),
        jax.ShapeDtypeStruct((bsz, MLA_HEADS * V_AUG, lp), _bf16),
        jax.ShapeDtypeStruct((bsz, seq, d), _bf16),
        jax.ShapeDtypeStruct((bsz, seq, d), _bf16),
    )
    out_specs = (
        pl.BlockSpec((1, TILE, d), real_map),
        pl.BlockSpec((1, TILE, GLA_QK), pad_map),
        pl.BlockSpec((1, TILE, GLA_QK), pad_map),
        pl.BlockSpec((1, TILE, GLA_QK), pad_map),
        pl.BlockSpec((1, TILE, GLA_VW), pad_map),
        pl.BlockSpec((1, TILE // GLA_CHUNK, GLA_QK), lambda b, t: (b * nt + t, 0, 0)),
        pl.BlockSpec((1, TILE, GLA_VW), real_map),
        pl.BlockSpec((1, MLA_HEADS * HEAD_PAD, TILE), real_map_t),
        pl.BlockSpec((1, TILE, MLA_HEADS * HEAD_PAD), pad_map),
        pl.BlockSpec((1, MLA_HEADS * V_AUG, TILE), lambda b, t: (b, 0, t)),
        pl.BlockSpec((1, TILE, d), real_map),
        pl.BlockSpec((1, TILE, d), real_map),
    )
    (s_emb, qt, kt, ke, gv, dec, sr, qm, km, vm, gate_a, gate_b) = pl.pallas_call(
        _inproj_kernel,
        grid=(bsz, nt),
        in_specs=[
            pl.BlockSpec((1, TILE, d), pad_map),
            _const_spec((1, d)), _const_spec((1, d)),
            _const_spec((d, _W_COLS)),
            _const_spec((LANES, GLA_QK)), _const_spec((1, GLA_QK)),
            _const_spec((1, MLA_Q_RANK)), _const_spec((MLA_HEADS * HEAD_PAD, MLA_Q_RANK)),
            _const_spec((1, MLA_KV_RANK)), _const_spec((MLA_KV_RANK, MLA_HEADS * HEAD_PAD)),
            _const_spec((MLA_HEADS * MLA_DV, MLA_KV_RANK)),
            tab_spec, tab_spec, tab_spec, tabt_spec, tabt_spec,
            _const_spec((TILE, TILE)),
        ],
        out_specs=out_specs,
        out_shape=out_shapes,
        compiler_params=pltpu.CompilerParams(
            dimension_semantics=("arbitrary", "arbitrary"), vmem_limit_bytes=VMEM_LIMIT),
        name="inproj",
    )(xcat, row2(ln_emb_g), row2(ln_emb_b), w_all, w2p, row2(gla_gate_b[0]),
      row2(mla_q_norm_g[0]), wuqt, row2(mla_kv_norm_g[0]), wuk, wuvt, cos_t, sin_lo, sin_hi,
      cos_tr, sin_tr, tril_chunks)

    o_gla = pl.pallas_call(
        _gla_kernel,
        grid=(bsz, nt),
        in_specs=[
            pl.BlockSpec((1, TILE, GLA_QK), pad_map),
            pl.BlockSpec((1, TILE, GLA_QK), pad_map),
            pl.BlockSpec((1, TILE, GLA_QK), pad_map),
            pl.BlockSpec((1, TILE, GLA_VW), pad_map),
            pl.BlockSpec((1, TILE // GLA_CHUNK, GLA_QK), lambda b, t: (b * nt + t, 0, 0)),
            pl.BlockSpec((1, TILE, GLA_VW), real_map),
            _const_spec((1, GLA_DV)),
        ],
        out_specs=pl.BlockSpec((1, TILE, GLA_VW), real_map),
        out_shape=jax.ShapeDtypeStruct((bsz, seq, GLA_VW), _bf16),
        scratch_shapes=[pltpu.VMEM((GLA_HEADS, GLA_DV, GLA_DK), _f32)],
        compiler_params=pltpu.CompilerParams(
            dimension_semantics=("arbitrary", "arbitrary"), vmem_limit_bytes=VMEM_LIMIT),
        name="gla",
    )(qt, kt, ke, gv, dec, sr, row2(gla_norm_g[0]))

    pair = 2 * HEAD_PAD
    o_mla = pl.pallas_call(
        _mla_kernel,
        grid=(bsz, MLA_HEADS // 2, seq // Q_TILE),
        in_specs=[
            pl.BlockSpec((1, pair, Q_TILE), lambda b, hp, i: (b, hp, i)),
            pl.BlockSpec((1, lp, pair), lambda b, hp, i: (b, 0, hp)),
            pl.BlockSpec((1, 2 * V_AUG, lp), lambda b, hp, i: (b, hp, 0)),
        ],
        out_specs=pl.BlockSpec((1, Q_TILE, 2 * MLA_DV), lambda b, hp, i: (b, i, hp)),
        out_shape=jax.ShapeDtypeStruct((bsz, seq, MLA_HEADS * MLA_DV), _bf16),
        scratch_shapes=[pltpu.VMEM((MLA_PAIR, V_AUG, Q_TILE), _f32),
                        pltpu.VMEM((MLA_PAIR, 1, LANES), _f32)],
        compiler_params=pltpu.CompilerParams(
            dimension_semantics=("arbitrary", "arbitrary", "arbitrary"),
            vmem_limit_bytes=VMEM_LIMIT),
        name="mla",
    )(qm, km, vm)

    rw = jnp.concatenate([router_group_w[0], router_expert_w[0]], axis=1)
    rw = jnp.pad(rw, ((0, 0), (0, LANES - rw.shape[1])))
    rwh = rw.astype(_bf16)
    rwl = (rw - rwh.astype(_f32)).astype(_bf16)
    rb = jnp.concatenate([router_group_b[0], router_expert_b[0]])
    rb = jnp.pad(rb, (0, LANES - rb.shape[0])).reshape(1, LANES)
    mi = np.arange(MERGE_TILE)
    tril_strict = jnp.asarray(mi[:, None] > mi[None, :], dtype=_bf16)
    flat = lambda a: a.reshape(ntok, a.shape[-1])
    tok_spec = lambda w: pl.BlockSpec((MERGE_TILE, w), lambda g: (g, 0))
    s2, info, cnt = pl.pallas_call(
        _merge_kernel,
        grid=(ntok // MERGE_TILE,),
        in_specs=[tok_spec(d), tok_spec(d), tok_spec(d), tok_spec(d), tok_spec(d),
                  _const_spec((d, d)), _const_spec((d, d)), _const_spec((d, d)),
                  _const_spec((1, d)), _const_spec((1, d)),
                  _const_spec((d, LANES)), _const_spec((d, LANES)), _const_spec((1, LANES)),
                  _const_spec((MERGE_TILE, MERGE_TILE))],
        out_specs=(tok_spec(d), tok_spec(LANES), _const_spec((8, LANES))),
        out_shape=(jax.ShapeDtypeStruct((ntok, d), _f32),
                   jax.ShapeDtypeStruct((ntok, LANES), _f32),
                   jax.ShapeDtypeStruct((8, LANES), _f32)),
        scratch_shapes=[pltpu.VMEM((8, LANES), _f32)],
        compiler_params=pltpu.CompilerParams(
            dimension_semantics=("arbitrary",), vmem_limit_bytes=VMEM_LIMIT),
        name="merge_router",
    )(flat(o_gla), flat(o_mla), flat(gate_a), flat(gate_b), flat(s_emb),
      w_branch_gla[0].astype(_bf16), w_branch_mla[0].astype(_bf16), w_out[0].astype(_bf16),
      row2(ln_mix_g[0]), row2(ln_mix_b[0]), rwh, rwl, rb, tril_strict)

    n_tiles = (2 * ntok + N_EXPERTS * (EXPERT_TILE - 1)) // EXPERT_TILE
    n_rows = n_tiles * EXPERT_TILE
    e_idx = info[:, 0:2].astype(jnp.int32)
    rank = info[:, 4:6].astype(jnp.int32)
    counts = cnt[0, :N_EXPERTS].astype(jnp.int32)
    padded = ((counts + EXPERT_TILE - 1) // EXPERT_TILE) * EXPERT_TILE
    ends = jnp.cumsum(padded)
    starts = ends - padded
    pos = (starts[e_idx] + rank).reshape(-1)
    tile_start = jnp.arange(n_tiles, dtype=jnp.int32) * EXPERT_TILE
    tile_expert = jnp.minimum(
        jnp.sum((ends[None, :] <= tile_start[:, None]).astype(jnp.int32), axis=1), N_EXPERTS - 1)
    n_used = (ends[-1:] // EXPERT_TILE).astype(jnp.int32)
    zero_row = jnp.where(padded > 0, ends - EXPERT_TILE, -1).astype(jnp.int32)

    any_spec = pl.BlockSpec(memory_space=pl.ANY)
    xs = pl.pallas_call(
        _dispatch_kernel,
        grid_spec=pltpu.PrefetchScalarGridSpec(
            num_scalar_prefetch=3,
            grid=(ntok // DISPATCH_TILE,),
            in_specs=[pl.BlockSpec((DISPATCH_TILE, d), lambda g, p, z, nu: (g, 0))],
            out_specs=any_spec,
            scratch_shapes=[pltpu.VMEM((EXPERT_TILE, d), _f32), pltpu.SemaphoreType.DMA((2,))],
        ),
        out_shape=jax.ShapeDtypeStruct((n_rows, d), _f32),
        compiler_params=pltpu.CompilerParams(
            dimension_semantics=("arbitrary",), vmem_limit_bytes=VMEM_LIMIT, has_side_effects=True),
        name="dispatch",
    )(pos, zero_row, n_used, s2)

    ff = EXPERT_FF
    wg = expert_w_gate[0].reshape(N_EXPERTS, d, ff).astype(_bf16)
    wu = expert_w_up[0].reshape(N_EXPERTS, d, ff).astype(_bf16)
    wd = expert_w_down[0].reshape(N_EXPERTS, ff, d).astype(_bf16)
    ys = pl.pallas_call(
        _expert_kernel,
        grid_spec=pltpu.PrefetchScalarGridSpec(
            num_scalar_prefetch=2,
            grid=(n_tiles,),
            in_specs=[
                pl.BlockSpec((EXPERT_TILE, d), lambda u, te, nu: (jnp.minimum(u, nu[0] - 1), 0)),
                pl.BlockSpec((1, d, ff), lambda u, te, nu: (te[u], 0, 0)),
                pl.BlockSpec((1, d, ff), lambda u, te, nu: (te[u], 0, 0)),
                pl.BlockSpec((1, ff, d), lambda u, te, nu: (te[u], 0, 0)),
            ],
            out_specs=pl.BlockSpec((EXPERT_TILE, d), lambda u, te, nu: (u, 0)),
        ),
        out_shape=jax.ShapeDtypeStruct((n_rows, d), _f32),
        compiler_params=pltpu.CompilerParams(
            dimension_semantics=("arbitrary",), vmem_limit_bytes=VMEM_LIMIT),
        name="experts",
    )(tile_expert, n_used, xs, wg, wu, wd)

    out = pl.pallas_call(
        _combine_kernel,
        grid_spec=pltpu.PrefetchScalarGridSpec(
            num_scalar_prefetch=1,
            grid=(ntok // COMBINE_TILE,),
            in_specs=[
                pl.BlockSpec((COMBINE_TILE, d), lambda g, p: (g, 0)),
                pl.BlockSpec((COMBINE_TILE, LANES), lambda g, p: (g, 0)),
                any_spec,
                pl.BlockSpec((1, d), lambda g, p: (0, 0)),
                pl.BlockSpec((1, d), lambda g, p: (0, 0)),
            ],
            out_specs=pl.BlockSpec((COMBINE_TILE, d), lambda g, p: (g, 0)),
            scratch_shapes=[pltpu.VMEM((2, 2, COMBINE_TILE, d), _f32),
                            pltpu.SemaphoreType.DMA((2,))],
        ),
        out_shape=jax.ShapeDtypeStruct((ntok, d), _f32),
        compiler_params=pltpu.CompilerParams(
            dimension_semantics=("arbitrary",), vmem_limit_bytes=VMEM_LIMIT),
        name="combine_ln",
    )(pos, s2, info, ys, row2(ln_ffn_g[0]), row2(ln_ffn_b[0]))
    return out.reshape(bsz, seq, d)
```

```python
import jax
import jax.numpy as jnp
import numpy as np
from jax import lax
from jax.experimental import pallas as pl
from jax.experimental.pallas import tpu as pltpu

D_MODEL = 1024
N_META = 16
GLA_HEADS = 4
GLA_DK = 128
GLA_DV = 256
GLA_QK = GLA_HEADS * GLA_DK
GLA_VW = GLA_HEADS * GLA_DV
GLA_GATE_RANK = 16
GLA_GATE_TAU = 16.0
GLA_CHUNK = 64
MLA_HEADS = 16
MLA_Q_RANK = 384
MLA_KV_RANK = 256
MLA_NOPE = 64
MLA_ROPE = 32
MLA_DV = 64
MLA_QDIM = MLA_NOPE + MLA_ROPE
ROPE_BASE = 10000.0
N_GROUPS = 4
EXPERTS_PER_GROUP = 8
N_EXPERTS = N_GROUPS * EXPERTS_PER_GROUP
EXPERT_FF = 256
DEPTH = 1
ALPHA = (2.0 * DEPTH) ** 0.25
LN_EPS = 1e-5
RMS_EPS = 1e-6

LANES = 128
TILE = 256
PADL = TILE - N_META
HEAD_PAD = LANES
BF16_ROWS = 16
KV_TILE = 512
Q_TILE = 2 * KV_TILE
V_AUG = MLA_DV + BF16_ROWS
MERGE_TILE = 512
EXPERT_TILE = 256
DISPATCH_TILE = 512
COMBINE_TILE = 256
NEG = -1e30
LOG2E = 1.4426950408889634
BOUND_SLACK = 1.02
MIN_SOFTMAX_SUM = 2.0 ** -100
VMEM_LIMIT = 56 * 1024 * 1024

_C_Q, _C_K, _C_V, _C_R = 0, 512, 1024, 2048
_C_A = 3072
_C_CQ = _C_A + LANES
_C_CKV = _C_CQ + MLA_Q_RANK
_C_KR = _C_CKV + MLA_KV_RANK
_C_GA = _C_KR + LANES
_C_GB = _C_GA + D_MODEL
_W_COLS = _C_GB + D_MODEL

_f32 = jnp.float32
_bf16 = jnp.bfloat16


def _dot(a, b):
    return jnp.dot(a, b, preferred_element_type=_f32)


def _dot_nt(a, b):
    return lax.dot_general(a, b, (((1,), (1,)), ((), ())), preferred_element_type=_f32)


def _dot_tn(a, b):
    return lax.dot_general(a, b, (((0,), (0,)), ((), ())), preferred_element_type=_f32)


def _layer_norm(x, g, b):
    mu = jnp.mean(x, axis=-1, keepdims=True)
    xc = x - mu
    var = jnp.mean(xc * xc, axis=-1, keepdims=True)
    return xc * lax.rsqrt(var + LN_EPS) * g + b


def _rms_norm(x, g):
    ms = jnp.mean(x * x, axis=-1, keepdims=True)
    return x * lax.rsqrt(ms + RMS_EPS) * g


def _sigmoid(x):
    return 1.0 / (1.0 + jnp.exp(-x))


def _rope(x, cos, sin_lo, sin_hi):
    half = MLA_ROPE // 2
    from_hi = pltpu.roll(x, LANES - half, 1)
    from_lo = pltpu.roll(x, half, 1)
    return x * cos + from_hi * sin_lo + from_lo * sin_hi


def _inproj_kernel(x_ref, head_ref, lng_ref, lnb_ref, w_ref, w2_ref, gb_ref, qg_ref, wuqt_ref, kvg_ref,
                   wuk_ref, wuvt_ref, cos_ref, sl_ref, sh_ref, cost_ref, sint_ref, tril_ref,
                   s_ref, qt_ref, kt_ref, ke_ref, gv_ref, dec_ref, sr_ref, qm_ref, km_ref,
                   vm_ref, ga_ref, gbt_ref):
    t = pl.program_id(1)
    x_in = jnp.where(t == 0, head_ref[...], x_ref[0])
    sn = _layer_norm(x_in, lng_ref[...], lnb_ref[...])
    s_ref[0] = sn
    snb = sn.astype(_bf16)
    row = t * TILE + lax.broadcasted_iota(jnp.int32, (TILE, 1), 0)
    valid = row >= PADL

    a_lr = _dot(snb, w_ref[:, _C_A:_C_A + LANES])
    z = _dot(a_lr.astype(_bf16), w2_ref[...]) + gb_ref[...]
    la = (jnp.minimum(z, 0.0) - jnp.log1p(jnp.exp(-jnp.abs(z)))) * (1.0 / GLA_GATE_TAU)
    la = jnp.where(valid, la, 0.0)
    hi = la.astype(_bf16)
    r1 = la - hi.astype(_f32)
    mid = r1.astype(_bf16)
    lo = (r1 - mid.astype(_f32)).astype(_bf16)
    tril = tril_ref[...]
    bc = _dot(tril, hi) + _dot(tril, mid) + _dot(tril, lo)
    n_chunks = TILE // GLA_CHUNK
    lasts = [bc[c * GLA_CHUNK + GLA_CHUNK - 1:(c + 1) * GLA_CHUNK, :] for c in range(n_chunks)]
    for c in range(n_chunks):
        dec_ref[0, c:c + 1, :] = jnp.exp(lasts[c])
    b_last = jnp.concatenate(
        [jnp.broadcast_to(l, (GLA_CHUNK, GLA_QK)) for l in lasts], axis=0)
    gq = _dot(snb, w_ref[:, _C_Q:_C_Q + GLA_QK])
    gk = jnp.where(valid, _dot(snb, w_ref[:, _C_K:_C_K + GLA_QK]), 0.0)
    qt_ref[0] = (gq * (GLA_DK ** -0.5) * jnp.exp(bc)).astype(_bf16)
    kt_ref[0] = (gk * jnp.exp(-bc)).astype(_bf16)
    ke_ref[0] = (gk * jnp.exp(b_last - bc)).astype(_bf16)
    gv_ref[0] = jnp.where(valid, _dot(snb, w_ref[:, _C_V:_C_V + GLA_VW]), 0.0).astype(_bf16)
    r = _dot(snb, w_ref[:, _C_R:_C_R + GLA_VW])
    sr_ref[0] = (r * _sigmoid(r)).astype(_bf16)

    cos = cos_ref[...]
    sl = sl_ref[...]
    sh = sh_ref[...]
    cq = _dot(snb, w_ref[:, _C_CQ:_C_CQ + MLA_Q_RANK])
    cqn = _rms_norm(cq, qg_ref[...]).astype(_bf16)
    qft = _dot_nt(wuqt_ref[...], cqn)
    scale = (MLA_QDIM ** -0.5) * LOG2E
    cost = cost_ref[...]
    sint = sint_ref[...]
    half = MLA_ROPE // 2
    for h in range(MLA_HEADS):
        base = h * HEAD_PAD
        x1 = qft[base + MLA_NOPE:base + MLA_NOPE + half]
        x2 = qft[base + MLA_NOPE + half:base + MLA_QDIM]
        qm_ref[0, base:base + MLA_NOPE] = (qft[base:base + MLA_NOPE] * scale).astype(_bf16)
        qm_ref[0, base + MLA_NOPE:base + MLA_NOPE + half] = ((x1 * cost - x2 * sint) * scale).astype(_bf16)
        qm_ref[0, base + MLA_NOPE + half:base + MLA_QDIM] = ((x1 * sint + x2 * cost) * scale).astype(_bf16)
        qm_ref[0, base + MLA_QDIM:base + HEAD_PAD] = jnp.zeros((HEAD_PAD - MLA_QDIM, TILE), _bf16)
    ckv = _dot(snb, w_ref[:, _C_CKV:_C_CKV + MLA_KV_RANK])
    ckvn = _rms_norm(ckv, kvg_ref[...]).astype(_bf16)
    kf = _dot(ckvn, wuk_ref[...])
    kr = _rope(_dot(snb, w_ref[:, _C_KR:_C_KR + LANES]), cos, sl, sh)
    for h in range(MLA_HEADS):
        km_ref[0, :, h * HEAD_PAD:(h + 1) * HEAD_PAD] = (
            kf[:, h * HEAD_PAD:(h + 1) * HEAD_PAD] + kr).astype(_bf16)
    vt = _dot_nt(wuvt_ref[...], ckvn)
    for h in range(MLA_HEADS):
        vm_ref[0, h * V_AUG:h * V_AUG + MLA_DV] = vt[h * MLA_DV:(h + 1) * MLA_DV].astype(_bf16)
        vm_ref[0, h * V_AUG + MLA_DV:(h + 1) * V_AUG] = jnp.ones((V_AUG - MLA_DV, TILE), _bf16)

    ga_ref[0] = _sigmoid(_dot(snb, w_ref[:, _C_GA:_C_GA + D_MODEL])).astype(_bf16)
    gbt_ref[0] = _sigmoid(_dot(snb, w_ref[:, _C_GB:_C_GB + D_MODEL])).astype(_bf16)


def _gla_kernel(qt_ref, kt_ref, ke_ref, gv_ref, dec_ref, sr_ref, ng_ref, o_ref, st_ref):
    t = pl.program_id(1)

    @pl.when(t == 0)
    def _():
        st_ref[...] = jnp.zeros_like(st_ref)

    ri = lax.broadcasted_iota(jnp.int32, (GLA_CHUNK, GLA_CHUNK), 0)
    ci = lax.broadcasted_iota(jnp.int32, (GLA_CHUNK, GLA_CHUNK), 1)
    causal = ri >= ci
    ng = ng_ref[...]
    for c in range(TILE // GLA_CHUNK):
        rows = slice(c * GLA_CHUNK, (c + 1) * GLA_CHUNK)
        for h in range(GLA_HEADS):
            kc = slice(h * GLA_DK, (h + 1) * GLA_DK)
            vc = slice(h * GLA_DV, (h + 1) * GLA_DV)
            q = qt_ref[0, rows, kc]
            k = kt_ref[0, rows, kc]
            e = ke_ref[0, rows, kc]
            v = gv_ref[0, rows, vc]
            st = st_ref[h]
            att = jnp.where(causal, _dot_nt(q, k), 0.0)
            o = _dot(att.astype(_bf16), v) + _dot_nt(q, st.astype(_bf16))
            st_ref[h] = st * dec_ref[0, c:c + 1, kc] + _dot_tn(v, e)
            o = _rms_norm(o, ng) * sr_ref[0, rows, vc].astype(_f32)
            o_ref[0, rows, vc] = o.astype(_bf16)


MLA_PAIR = 2


def _mla_tile_start(j):
    return pl.multiple_of(TILE + j * KV_TILE, TILE)


def _mla_finish(o_ref, acc_ref):
    outs = []
    for h in range(MLA_PAIR):
        a = acc_ref[h]
        outs.append(a[0:MLA_DV] / a[MLA_DV:MLA_DV + 1])
    o_ref[0] = jnp.concatenate(outs, axis=0).T.astype(_bf16)


def _mla_kernel(q_ref, k_ref, v_ref, o_ref, acc_ref, knorm_ref):
    i = pl.program_id(2)
    heads = MLA_PAIR
    ones = jnp.ones((HEAD_PAD, LANES), _bf16)

    @pl.when(i == 0)
    def _():
        for h in range(heads):
            kk = k_ref[0, :, h * HEAD_PAD:(h + 1) * HEAD_PAD].astype(_f32)
            hi = (kk * kk).astype(_bf16)
            knorm_ref[h] = jnp.max(_dot(hi, ones), axis=0, keepdims=True)

    half = slice(KV_TILE, Q_TILE)
    q_t = [q_ref[0, h * HEAD_PAD:(h + 1) * HEAD_PAD, :] for h in range(heads)]
    q_hi = [q_ref[0, h * HEAD_PAD:(h + 1) * HEAD_PAD, half] for h in range(heads)]

    def score_bound(h, q):
        qq = q.astype(_f32)
        qn2 = jnp.sum(qq * qq, axis=0, keepdims=True)
        return jnp.sqrt(qn2 * knorm_ref[h][:, 0:1]) * BOUND_SLACK

    bound = [score_bound(h, q_t[h]) for h in range(heads)]
    bound_hi = [score_bound(h, q_hi[h]) for h in range(heads)]

    def keys(j, h):
        return k_ref[0, pl.ds(_mla_tile_start(j), KV_TILE), h * HEAD_PAD:(h + 1) * HEAD_PAD]

    def weighted(j, h, p):
        return _dot(v_ref[0, h * V_AUG:(h + 1) * V_AUG, pl.ds(_mla_tile_start(j), KV_TILE)], p)

    k_row = lax.broadcasted_iota(jnp.int32, (KV_TILE, Q_TILE), 0)
    q_col = lax.broadcasted_iota(jnp.int32, (KV_TILE, Q_TILE), 1)
    k_row_sq = lax.broadcasted_iota(jnp.int32, (KV_TILE, KV_TILE), 0)
    q_col_sq = lax.broadcasted_iota(jnp.int32, (KV_TILE, KV_TILE), 1)
    s_meta = [_dot(k_ref[0, PADL:TILE, h * HEAD_PAD:(h + 1) * HEAD_PAD], q_t[h]) for h in range(heads)]
    s_lo = [jnp.where(k_row <= q_col, _dot(keys(2 * i, h), q_t[h]), NEG) for h in range(heads)]
    s_hi = [jnp.where(k_row_sq <= q_col_sq, _dot(keys(2 * i + 1, h), q_hi[h]), NEG)
            for h in range(heads)]
    for h in range(heads):
        p_meta = jnp.concatenate([jnp.zeros((PADL, Q_TILE), _bf16),
                                  jnp.exp2(s_meta[h] - bound[h]).astype(_bf16)], axis=0)
        acc_ref[h] = (_dot(v_ref[0, h * V_AUG:(h + 1) * V_AUG, 0:TILE], p_meta)
                      + weighted(2 * i, h, jnp.exp2(s_lo[h] - bound[h]).astype(_bf16)))
        acc_ref[h, :, half] += weighted(2 * i + 1, h, jnp.exp2(s_hi[h] - bound_hi[h]).astype(_bf16))

    def body(j, c):
        s = [_dot(keys(j, h), q_t[h]) for h in range(heads)]
        for h in range(heads):
            acc_ref[h] += weighted(j, h, jnp.exp2(s[h] - bound[h]).astype(_bf16))
        return c

    lax.fori_loop(0, 2 * i, body, 0)
    _mla_finish(o_ref, acc_ref)

    l_min = jnp.minimum(jnp.min(acc_ref[0, MLA_DV:MLA_DV + 1, :]), jnp.min(acc_ref[1, MLA_DV:MLA_DV + 1, :]))

    @pl.when(jnp.logical_not(l_min >= MIN_SOFTMAX_SUM))
    def _():
        _mla_exact(q_ref, k_ref, v_ref, o_ref, acc_ref)


def _mla_exact(q_ref, k_ref, v_ref, o_ref, acc_ref):
    i = pl.program_id(2)
    heads = MLA_PAIR
    q_t = [q_ref[0, h * HEAD_PAD:(h + 1) * HEAD_PAD, :] for h in range(heads)]

    ms = []
    for h in range(heads):
        kb = k_ref[0, PADL:TILE, h * HEAD_PAD:(h + 1) * HEAD_PAD]
        s = _dot(kb, q_t[h])
        m0 = jnp.max(s, axis=0, keepdims=True)
        p = jnp.concatenate([jnp.zeros((PADL, Q_TILE), _bf16), jnp.exp2(s - m0).astype(_bf16)], axis=0)
        acc_ref[h] = _dot(v_ref[0, h * V_AUG:(h + 1) * V_AUG, 0:TILE], p)
        ms.append(m0)

    k_row = lax.broadcasted_iota(jnp.int32, (KV_TILE, Q_TILE), 0)
    q_col = lax.broadcasted_iota(jnp.int32, (KV_TILE, Q_TILE), 1)

    def body(j, ms):
        visible = (j - 2 * i) * KV_TILE + k_row <= q_col
        out = []
        for h in range(heads):
            s = _dot(k_ref[0, pl.ds(_mla_tile_start(j), KV_TILE), h * HEAD_PAD:(h + 1) * HEAD_PAD],
                     q_t[h])
            s = jnp.where(visible, s, NEG)
            vb = v_ref[0, h * V_AUG:(h + 1) * V_AUG, pl.ds(_mla_tile_start(j), KV_TILE)]
            m_new = jnp.maximum(ms[h], jnp.max(s, axis=0, keepdims=True))
            alpha = jnp.exp2(ms[h] - m_new)
            acc_ref[h] = alpha * acc_ref[h] + _dot(vb, jnp.exp2(s - m_new).astype(_bf16))
            out.append(m_new)
        return tuple(out)

    lax.fori_loop(0, 2 * i + 2, body, tuple(ms))
    _mla_finish(o_ref, acc_ref)


def _merge_kernel(og_ref, om_ref, ga_ref, gbt_ref, s_ref, wbg_ref, wbm_ref, wo_ref, lng_ref,
                  lnb_ref, rwh_ref, rwl_ref, rb_ref, tril_ref,
                  s2_ref, info_ref, cnt_ref, carry_ref):
    step = pl.program_id(0)

    @pl.when(step == 0)
    def _():
        carry_ref[...] = jnp.zeros_like(carry_ref)

    ba = _dot(og_ref[...], wbg_ref[...])
    bb = _dot(om_ref[...], wbm_ref[...])
    merged = ga_ref[...].astype(_f32) * ba + gbt_ref[...].astype(_f32) * bb
    y = ALPHA * s_ref[...] + _dot(merged.astype(_bf16), wo_ref[...])
    s2 = _layer_norm(y, lng_ref[...], lnb_ref[...])
    s2_ref[...] = s2

    xh = s2.astype(_bf16)
    xl = (s2 - xh.astype(_f32)).astype(_bf16)
    logits = _dot(xh, rwh_ref[...]) + _dot(xl, rwh_ref[...]) + _dot(xh, rwl_ref[...]) + rb_ref[...]
    lane = lax.broadcasted_iota(jnp.int32, (MERGE_TILE, LANES), 1)
    is_g = lane < N_GROUPS
    gl = jnp.where(is_g, logits, NEG)
    gmax = jnp.max(gl, axis=-1, keepdims=True)
    gidx = jnp.min(jnp.where(gl == gmax, lane, LANES), axis=-1, keepdims=True)
    p_g = 1.0 / jnp.sum(jnp.where(is_g, jnp.exp(gl - gmax), 0.0), axis=-1, keepdims=True)
    lo = N_GROUPS + EXPERTS_PER_GROUP * gidx
    el = jnp.where((lane >= lo) & (lane < lo + EXPERTS_PER_GROUP), logits, NEG)
    v1 = jnp.max(el, axis=-1, keepdims=True)
    i1 = jnp.min(jnp.where(el == v1, lane, LANES), axis=-1, keepdims=True)
    el2 = jnp.where(lane == i1, NEG, el)
    v2 = jnp.max(el2, axis=-1, keepdims=True)
    i2 = jnp.min(jnp.where(el2 == v2, lane, LANES), axis=-1, keepdims=True)
    tt = jnp.exp(v2 - v1)
    p1 = 1.0 / (1.0 + tt)
    p2 = tt / (1.0 + tt)
    e1 = i1 - N_GROUPS
    e2 = i2 - N_GROUPS
    hit1 = lane == e1
    hit2 = lane == e2
    onehot = jnp.where(hit1 | hit2, 1.0, 0.0)
    before = _dot(tril_ref[...], onehot.astype(_bf16)) + carry_ref[0:1, :]
    r1 = jnp.sum(jnp.where(hit1, before, 0.0), axis=-1, keepdims=True)
    r2 = jnp.sum(jnp.where(hit2, before, 0.0), axis=-1, keepdims=True)
    new_carry = carry_ref[0:1, :] + jnp.sum(onehot, axis=0, keepdims=True)
    carry_ref[...] = jnp.broadcast_to(new_carry, carry_ref.shape)
    cnt_ref[...] = jnp.broadcast_to(new_carry, cnt_ref.shape)
    info = jnp.where(lane == 0, e1.astype(_f32),
           jnp.where(lane == 1, e2.astype(_f32),
           jnp.where(lane == 2, p_g * p1,
           jnp.where(lane == 3, p_g * p2,
           jnp.where(lane == 4, r1,
           jnp.where(lane == 5, r2, 0.0))))))
    info_ref[...] = info


def _dispatch_kernel(pos_ref, zrow_ref, nused_ref, s2_ref, xs_hbm, zero_ref, sems):
    g = pl.program_id(0)
    zero_sem = sems.at[1]
    row_sem = sems.at[0]

    def zero_copy(row):
        row = pl.multiple_of(row, EXPERT_TILE)
        return pltpu.make_async_copy(zero_ref, xs_hbm.at[pl.ds(row, EXPERT_TILE)], zero_sem)

    @pl.when(g == 0)
    def _():
        zero_ref[...] = jnp.zeros_like(zero_ref)

        def start(e, c):
            @pl.when(zrow_ref[e] >= 0)
            def _():
                zero_copy(zrow_ref[e]).start()
            return c

        def wait(e, c):
            @pl.when(zrow_ref[e] >= 0)
            def _():
                zero_copy(0).wait()
            return c

        def start_tail(u, c):
            zero_copy(u * EXPERT_TILE).start()
            return c

        def wait_tail(u, c):
            zero_copy(0).wait()
            return c

        n_tiles = xs_hbm.shape[0] // EXPERT_TILE
        lax.fori_loop(0, N_EXPERTS, start, 0)
        lax.fori_loop(nused_ref[0], n_tiles, start_tail, 0)
        lax.fori_loop(0, N_EXPERTS, wait, 0)
        lax.fori_loop(nused_ref[0], n_tiles, wait_tail, 0)

    def issue(r, c):
        tok = g * DISPATCH_TILE + r
        for k in range(2):
            pltpu.make_async_copy(s2_ref.at[pl.ds(r, 1)],
                                  xs_hbm.at[pl.ds(pos_ref[2 * tok + k], 1)], row_sem).start()
        return c

    lax.fori_loop(0, DISPATCH_TILE, issue, 0, unroll=8)
    for k in range(2):
        pltpu.make_async_copy(s2_ref, xs_hbm.at[pl.ds(0, DISPATCH_TILE)], row_sem).wait()


def _expert_kernel(te_ref, nused_ref, x_ref, wg_ref, wu_ref, wd_ref, o_ref, wgb_ref, wub_ref, wdb_ref):
    u = pl.program_id(0)
    used = u < nused_ref[0]
    first_of_expert = jnp.logical_or(u == 0, te_ref[u] != te_ref[jnp.maximum(u - 1, 0)])

    @pl.when(jnp.logical_and(used, first_of_expert))
    def _():
        wgb_ref[...] = wg_ref[0].astype(_bf16)
        wub_ref[...] = wu_ref[0].astype(_bf16)
        wdb_ref[...] = wd_ref[0].astype(_bf16)

    @pl.when(used)
    def _():
        x = x_ref[...].astype(_bf16)
        a = _dot(x, wgb_ref[...])
        up = _dot(x, wub_ref[...])
        hid = a * _sigmoid(a) * up
        o_ref[...] = _dot(hid.astype(_bf16), wdb_ref[...])

    @pl.when(u >= nused_ref[0])
    def _():
        o_ref[...] = jnp.zeros_like(o_ref)


def _combine_kernel(pos_ref, s2_ref, info_ref, ys_hbm, lng_ref, lnb_ref, o_ref, buf_ref, sems):
    g = pl.program_id(0)
    last = pl.num_programs(0) - 1

    def issue_tile(tile, slot):
        def body(r, c):
            tok = tile * COMBINE_TILE + r
            for k in range(2):
                pltpu.make_async_copy(ys_hbm.at[pl.ds(pos_ref[2 * tok + k], 1)],
                                      buf_ref.at[slot, k, pl.ds(r, 1)], sems.at[slot]).start()
            return c

        lax.fori_loop(0, COMBINE_TILE, body, 0, unroll=8)

    @pl.when(g == 0)
    def _():
        issue_tile(0, 0)

    @pl.when(g < last)
    def _():
        issue_tile(g + 1, (g + 1) % 2)

    slot = g % 2
    for k in range(2):
        pltpu.make_async_copy(ys_hbm.at[pl.ds(0, COMBINE_TILE)], buf_ref.at[slot, k],
                              sems.at[slot]).wait()
    info = info_ref[...]
    y = (ALPHA * s2_ref[...] + info[:, 2:3] * buf_ref[slot, 0] + info[:, 3:4] * buf_ref[slot, 1])
    o_ref[...] = _layer_norm(y, lng_ref[...], lnb_ref[...])


def _const_spec(shape):
    nd = len(shape)
    return pl.BlockSpec(shape, lambda *_: (0,) * nd)


def _rope_tables(lp):
    pos = jnp.maximum(jnp.arange(lp, dtype=_f32) - PADL, 0.0)
    inv_freq = ROPE_BASE ** (-jnp.arange(0, MLA_ROPE, 2, dtype=_f32) / MLA_ROPE)
    ang = pos[:, None] * inv_freq[None, :]
    cos, sin = jnp.cos(ang), jnp.sin(ang)
    half = MLA_ROPE // 2
    ones = jnp.ones((lp, MLA_NOPE), _f32)
    zeros_n = jnp.zeros((lp, MLA_NOPE), _f32)
    zeros_h = jnp.zeros((lp, half), _f32)
    tail1 = jnp.ones((lp, LANES - MLA_QDIM), _f32)
    tail0 = jnp.zeros((lp, LANES - MLA_QDIM), _f32)
    cos_t = jnp.concatenate([ones, cos, cos, tail1], axis=1)
    sin_lo = jnp.concatenate([zeros_n, -sin, zeros_h, tail0], axis=1)
    sin_hi = jnp.concatenate([zeros_n, zeros_h, sin, tail0], axis=1)
    return cos_t, sin_lo, sin_hi, cos.T, sin.T


def _pad_heads(w, width):
    k = w.shape[0]
    w = w.reshape(k, MLA_HEADS, width)
    w = jnp.pad(w, ((0, 0), (0, 0), (0, HEAD_PAD - width)))
    return w.reshape(k, MLA_HEADS * HEAD_PAD)


def kernel(x, meta_tokens, ln_emb_g, ln_emb_b, w_in, gla_gate_w2, gla_gate_b, gla_norm_g, mla_q_norm_g, mla_w_uq, mla_kv_norm_g, mla_w_uk, mla_w_uv, w_branch_gla, w_branch_mla, w_out, ln_mix_g, ln_mix_b, router_group_w, router_group_b, router_expert_w, router_expert_b, expert_w_gate, expert_w_up, expert_w_down, ln_ffn_g, ln_ffn_b):
    bsz, seq, d = x.shape
    assert d == D_MODEL and seq % Q_TILE == 0 and w_in.shape[0] == DEPTH == 1
    lp = PADL + N_META + seq
    nt = lp // TILE
    ntok = bsz * seq
    row2 = lambda v: v.reshape(1, -1).astype(_f32)

    head_tile = jnp.concatenate([jnp.zeros((PADL, d), _f32), meta_tokens.astype(_f32)], axis=0)
    wi = w_in[0]
    o_a = 2 * GLA_QK + 2 * GLA_VW
    o_cq = o_a + GLA_GATE_RANK
    o_ckv = o_cq + MLA_Q_RANK
    o_kr = o_ckv + MLA_KV_RANK
    o_ga = o_kr + MLA_ROPE
    w_a = jnp.pad(wi[:, o_a:o_cq], ((0, 0), (0, LANES - GLA_GATE_RANK)))
    w_kr = jnp.pad(wi[:, o_kr:o_ga], ((0, 0), (MLA_NOPE, LANES - MLA_QDIM)))
    w_all = jnp.concatenate([wi[:, :o_a], w_a, wi[:, o_cq:o_kr], w_kr, wi[:, o_ga:]], axis=1).astype(_bf16)
    assert w_all.shape == (d, _W_COLS)
    w2p = jnp.pad(gla_gate_w2[0], ((0, LANES - GLA_GATE_RANK), (0, 0))).astype(_bf16)
    wuqt = _pad_heads(mla_w_uq[0], MLA_QDIM).T.astype(_bf16)
    wuk = _pad_heads(mla_w_uk[0], MLA_NOPE).astype(_bf16)
    wuvt = mla_w_uv[0].T.astype(_bf16)
    cos_t, sin_lo, sin_hi, cos_tr, sin_tr = _rope_tables(lp)
    blk = np.arange(TILE)
    tril_chunks = jnp.asarray(
        ((blk[:, None] >= blk[None, :]) & (blk[:, None] // GLA_CHUNK == blk[None, :] // GLA_CHUNK)),
        dtype=_bf16)

    pad_map = lambda b, t: (b, t, 0)
    real_map = lambda b, t: (b, jnp.maximum(t - 1, 0), 0)
    tab_spec = pl.BlockSpec((TILE, LANES), lambda b, t: (t, 0))
    tabt_spec = pl.BlockSpec((MLA_ROPE // 2, TILE), lambda b, t: (0, t))
    real_map_t = lambda b, t: (b, 0, jnp.maximum(t - 1, 0))
    out_shapes = (
        jax.ShapeDtypeStruct((bsz, seq, d), _f32),
        jax.ShapeDtypeStruct((bsz, lp, GLA_QK), _bf16),
        jax.ShapeDtypeStruct((bsz, lp, GLA_QK), _bf16),
        jax.ShapeDtypeStruct((bsz, lp, GLA_QK), _bf16),
        jax.ShapeDtypeStruct((bsz, lp, GLA_VW), _bf16),
        jax.ShapeDtypeStruct((bsz * nt, TILE // GLA_CHUNK, GLA_QK), _f32),
        jax.ShapeDtypeStruct((bsz, seq, GLA_VW), _bf16),
        jax.ShapeDtypeStruct((bsz, MLA_HEADS * HEAD_PAD, seq), _bf16),
        jax.ShapeDtypeStruct((bsz, lp, MLA_HEADS * HEAD_PAD), _bf16),
        jax.ShapeDtypeStruct((bsz, MLA_HEADS * V_AUG, lp), _bf16),
        jax.ShapeDtypeStruct((bsz, seq, d), _bf16),
        jax.ShapeDtypeStruct((bsz, seq, d), _bf16),
    )
    out_specs = (
        pl.BlockSpec((1, TILE, d), real_map),
        pl.BlockSpec((1, TILE, GLA_QK), pad_map),
        pl.BlockSpec((1, TILE, GLA_QK), pad_map),
        pl.BlockSpec((1, TILE, GLA_QK), pad_map),
        pl.BlockSpec((1, TILE, GLA_VW), pad_map),
        pl.BlockSpec((1, TILE // GLA_CHUNK, GLA_QK), lambda b, t: (b * nt + t, 0, 0)),
        pl.BlockSpec((1, TILE, GLA_VW), real_map),
        pl.BlockSpec((1, MLA_HEADS * HEAD_PAD, TILE), real_map_t),
        pl.BlockSpec((1, TILE, MLA_HEADS * HEAD_PAD), pad_map),
        pl.BlockSpec((1, MLA_HEADS * V_AUG, TILE), lambda b, t: (b, 0, t)),
        pl.BlockSpec((1, TILE, d), real_map),
        pl.BlockSpec((1, TILE, d), real_map),
    )
    (s_emb, qt, kt, ke, gv, dec, sr, qm, km, vm, gate_a, gate_b) = pl.pallas_call(
        _inproj_kernel,
        grid=(bsz, nt),
        in_specs=[
            pl.BlockSpec((1, TILE, d), real_map),
            _const_spec((TILE, d)),
            _const_spec((1, d)), _const_spec((1, d)),
            _const_spec((d, _W_COLS)),
            _const_spec((LANES, GLA_QK)), _const_spec((1, GLA_QK)),
            _const_spec((1, MLA_Q_RANK)), _const_spec((MLA_HEADS * HEAD_PAD, MLA_Q_RANK)),
            _const_spec((1, MLA_KV_RANK)), _const_spec((MLA_KV_RANK, MLA_HEADS * HEAD_PAD)),
            _const_spec((MLA_HEADS * MLA_DV, MLA_KV_RANK)),
            tab_spec, tab_spec, tab_spec, tabt_spec, tabt_spec,
            _const_spec((TILE, TILE)),
        ],
        out_specs=out_specs,
        out_shape=out_shapes,
        compiler_params=pltpu.CompilerParams(
            dimension_semantics=("arbitrary", "arbitrary"), vmem_limit_bytes=VMEM_LIMIT),
        name="inproj",
    )(x, head_tile, row2(ln_emb_g), row2(ln_emb_b), w_all, w2p, row2(gla_gate_b[0]),
      row2(mla_q_norm_g[0]), wuqt, row2(mla_kv_norm_g[0]), wuk, wuvt, cos_t, sin_lo, sin_hi,
      cos_tr, sin_tr, tril_chunks)

    o_gla = pl.pallas_call(
        _gla_kernel,
        grid=(bsz, nt),
        in_specs=[
            pl.BlockSpec((1, TILE, GLA_QK), pad_map),
            pl.BlockSpec((1, TILE, GLA_QK), pad_map),
            pl.BlockSpec((1, TILE, GLA_QK), pad_map),
            pl.BlockSpec((1, TILE, GLA_VW), pad_map),
            pl.BlockSpec((1, TILE // GLA_CHUNK, GLA_QK), lambda b, t: (b * nt + t, 0, 0)),
            pl.BlockSpec((1, TILE, GLA_VW), real_map),
            _const_spec((1, GLA_DV)),
        ],
        out_specs=pl.BlockSpec((1, TILE, GLA_VW), real_map),
        out_shape=jax.ShapeDtypeStruct((bsz, seq, GLA_VW), _bf16),
        scratch_shapes=[pltpu.VMEM((GLA_HEADS, GLA_DV, GLA_DK), _f32)],
        compiler_params=pltpu.CompilerParams(
            dimension_semantics=("arbitrary", "arbitrary"), vmem_limit_bytes=VMEM_LIMIT),
        name="gla",
    )(qt, kt, ke, gv, dec, sr, row2(gla_norm_g[0]))

    pair = 2 * HEAD_PAD
    o_mla = pl.pallas_call(
        _mla_kernel,
        grid=(bsz, MLA_HEADS // 2, seq // Q_TILE),
        in_specs=[
            pl.BlockSpec((1, pair, Q_TILE), lambda b, hp, i: (b, hp, i)),
            pl.BlockSpec((1, lp, pair), lambda b, hp, i: (b, 0, hp)),
            pl.BlockSpec((1, 2 * V_AUG, lp), lambda b, hp, i: (b, hp, 0)),
        ],
        out_specs=pl.BlockSpec((1, Q_TILE, 2 * MLA_DV), lambda b, hp, i: (b, i, hp)),
        out_shape=jax.ShapeDtypeStruct((bsz, seq, MLA_HEADS * MLA_DV), _bf16),
        scratch_shapes=[pltpu.VMEM((MLA_PAIR, V_AUG, Q_TILE), _f32),
                        pltpu.VMEM((MLA_PAIR, 1, LANES), _f32)],
        compiler_params=pltpu.CompilerParams(
            dimension_semantics=("arbitrary", "arbitrary", "arbitrary"),
            vmem_limit_bytes=VMEM_LIMIT),
        name="mla",
    )(qm, km, vm)

    rw = jnp.concatenate([router_group_w[0], router_expert_w[0]], axis=1)
    rw = jnp.pad(rw, ((0, 0), (0, LANES - rw.shape[1])))
    rwh = rw.astype(_bf16)
    rwl = (rw - rwh.astype(_f32)).astype(_bf16)
    rb = jnp.concatenate([router_group_b[0], router_expert_b[0]])
    rb = jnp.pad(rb, (0, LANES - rb.shape[0])).reshape(1, LANES)
    mi = np.arange(MERGE_TILE)
    tril_strict = jnp.asarray(mi[:, None] > mi[None, :], dtype=_bf16)
    flat = lambda a: a.reshape(ntok, a.shape[-1])
    tok_spec = lambda w: pl.BlockSpec((MERGE_TILE, w), lambda g: (g, 0))
    s2, info, cnt = pl.pallas_call(
        _merge_kernel,
        grid=(ntok // MERGE_TILE,),
        in_specs=[tok_spec(d), tok_spec(d), tok_spec(d), tok_spec(d), tok_spec(d),
                  _const_spec((d, d)), _const_spec((d, d)), _const_spec((d, d)),
                  _const_spec((1, d)), _const_spec((1, d)),
                  _const_spec((d, LANES)), _const_spec((d, LANES)), _const_spec((1, LANES)),
                  _const_spec((MERGE_TILE, MERGE_TILE))],
        out_specs=(tok_spec(d), tok_spec(LANES), _const_spec((8, LANES))),
        out_shape=(jax.ShapeDtypeStruct((ntok, d), _f32),
                   jax.ShapeDtypeStruct((ntok, LANES), _f32),
                   jax.ShapeDtypeStruct((8, LANES), _f32)),
        scratch_shapes=[pltpu.VMEM((8, LANES), _f32)],
        compiler_params=pltpu.CompilerParams(
            dimension_semantics=("arbitrary",), vmem_limit_bytes=VMEM_LIMIT),
        name="merge_router",
    )(flat(o_gla), flat(o_mla), flat(gate_a), flat(gate_b), flat(s_emb),
      w_branch_gla[0].astype(_bf16), w_branch_mla[0].astype(_bf16), w_out[0].astype(_bf16),
      row2(ln_mix_g[0]), row2(ln_mix_b[0]), rwh, rwl, rb, tril_strict)

    n_tiles = (2 * ntok + N_EXPERTS * (EXPERT_TILE - 1)) // EXPERT_TILE
    n_rows = n_tiles * EXPERT_TILE
    e_idx = info[:, 0:2].astype(jnp.int32)
    rank = info[:, 4:6].astype(jnp.int32)
    counts = cnt[0, :N_EXPERTS].astype(jnp.int32)
    padded = ((counts + EXPERT_TILE - 1) // EXPERT_TILE) * EXPERT_TILE
    ends = jnp.cumsum(padded)
    starts = ends - padded
    pos = (starts[e_idx] + rank).reshape(-1)
    tile_start = jnp.arange(n_tiles, dtype=jnp.int32) * EXPERT_TILE
    tile_expert = jnp.minimum(
        jnp.sum((ends[None, :] <= tile_start[:, None]).astype(jnp.int32), axis=1), N_EXPERTS - 1)
    n_used = (ends[-1:] // EXPERT_TILE).astype(jnp.int32)
    zero_row = jnp.where(padded > 0, ends - EXPERT_TILE, -1).astype(jnp.int32)

    any_spec = pl.BlockSpec(memory_space=pl.ANY)
    xs = pl.pallas_call(
        _dispatch_kernel,
        grid_spec=pltpu.PrefetchScalarGridSpec(
            num_scalar_prefetch=3,
            grid=(ntok // DISPATCH_TILE,),
            in_specs=[pl.BlockSpec((DISPATCH_TILE, d), lambda g, p, z, nu: (g, 0))],
            out_specs=any_spec,
            scratch_shapes=[pltpu.VMEM((EXPERT_TILE, d), _f32), pltpu.SemaphoreType.DMA((2,))],
        ),
        out_shape=jax.ShapeDtypeStruct((n_rows, d), _f32),
        compiler_params=pltpu.CompilerParams(
            dimension_semantics=("arbitrary",), vmem_limit_bytes=VMEM_LIMIT, has_side_effects=True),
        name="dispatch",
    )(pos, zero_row, n_used, s2)

    ff = EXPERT_FF
    wg = expert_w_gate[0].reshape(N_EXPERTS, d, ff)
    wu = expert_w_up[0].reshape(N_EXPERTS, d, ff)
    wd = expert_w_down[0].reshape(N_EXPERTS, ff, d)
    ys = pl.pallas_call(
        _expert_kernel,
        grid_spec=pltpu.PrefetchScalarGridSpec(
            num_scalar_prefetch=2,
            grid=(n_tiles,),
            in_specs=[
                pl.BlockSpec((EXPERT_TILE, d), lambda u, te, nu: (jnp.minimum(u, nu[0] - 1), 0)),
                pl.BlockSpec((1, d, ff), lambda u, te, nu: (te[u], 0, 0)),
                pl.BlockSpec((1, d, ff), lambda u, te, nu: (te[u], 0, 0)),
                pl.BlockSpec((1, ff, d), lambda u, te, nu: (te[u], 0, 0)),
            ],
            out_specs=pl.BlockSpec((EXPERT_TILE, d), lambda u, te, nu: (u, 0)),
            scratch_shapes=[pltpu.VMEM((d, ff), _bf16), pltpu.VMEM((d, ff), _bf16),
                            pltpu.VMEM((ff, d), _bf16)],
        ),
        out_shape=jax.ShapeDtypeStruct((n_rows, d), _f32),
        compiler_params=pltpu.CompilerParams(
            dimension_semantics=("arbitrary",), vmem_limit_bytes=VMEM_LIMIT),
        name="experts",
    )(tile_expert, n_used, xs, wg, wu, wd)

    out = pl.pallas_call(
        _combine_kernel,
        grid_spec=pltpu.PrefetchScalarGridSpec(
            num_scalar_prefetch=1,
            grid=(ntok // COMBINE_TILE,),
            in_specs=[
                pl.BlockSpec((COMBINE_TILE, d), lambda g, p: (g, 0)),
                pl.BlockSpec((COMBINE_TILE, LANES), lambda g, p: (g, 0)),
                any_spec,
                pl.BlockSpec((1, d), lambda g, p: (0, 0)),
                pl.BlockSpec((1, d), lambda g, p: (0, 0)),
            ],
            out_specs=pl.BlockSpec((COMBINE_TILE, d), lambda g, p: (g, 0)),
            scratch_shapes=[pltpu.VMEM((2, 2, COMBINE_TILE, d), _f32),
                            pltpu.SemaphoreType.DMA((2,))],
        ),
        out_shape=jax.ShapeDtypeStruct((ntok, d), _f32),
        compiler_params=pltpu.CompilerParams(
            dimension_semantics=("arbitrary",), vmem_limit_bytes=VMEM_LIMIT),
        name="combine_ln",
    )(pos, s2, info, ys, row2(ln_ffn_g[0]), row2(ln_ffn_b[0]))
    return out.reshape(bsz, seq, d)
```

```python
import jax
import jax.numpy as jnp
import numpy as np
from jax import lax
from jax.experimental import pallas as pl
from jax.experimental.pallas import tpu as pltpu

D_MODEL = 1024
N_META = 16
GLA_HEADS = 4
GLA_DK = 128
GLA_DV = 256
GLA_QK = GLA_HEADS * GLA_DK
GLA_VW = GLA_HEADS * GLA_DV
GLA_GATE_RANK = 16
GLA_GATE_TAU = 16.0
GLA_CHUNK = 64
MLA_HEADS = 16
MLA_Q_RANK = 384
MLA_KV_RANK = 256
MLA_NOPE = 64
MLA_ROPE = 32
MLA_DV = 64
MLA_QDIM = MLA_NOPE + MLA_ROPE
ROPE_BASE = 10000.0
N_GROUPS = 4
EXPERTS_PER_GROUP = 8
N_EXPERTS = N_GROUPS * EXPERTS_PER_GROUP
EXPERT_FF = 256
DEPTH = 1
ALPHA = (2.0 * DEPTH) ** 0.25
LN_EPS = 1e-5
RMS_EPS = 1e-6

LANES = 128
SUBLANES = 8
TILE = 256
PADL = TILE - N_META
HEAD_PAD = LANES
BF16_ROWS = 16
KV_TILE = 512
Q_TILE = 2 * KV_TILE
V_AUG = MLA_DV + BF16_ROWS
MERGE_TILE = 512
EXPERT_TILE = 256
DISPATCH_TILE = 512
COMBINE_TILE = 256
NEG = -1e30
LOG2E = 1.4426950408889634
BOUND_SLACK = 1.02
MIN_SOFTMAX_SUM = 2.0 ** -100
VMEM_LIMIT = 56 * 1024 * 1024

_C_Q, _C_K, _C_V, _C_R = 0, 512, 1024, 2048
_C_A = 3072
_C_CQ = _C_A + LANES
_C_CKV = _C_CQ + MLA_Q_RANK
_C_KR = _C_CKV + MLA_KV_RANK
_C_GA = _C_KR + LANES
_C_GB = _C_GA + D_MODEL
_W_COLS = _C_GB + D_MODEL

_f32 = jnp.float32
_bf16 = jnp.bfloat16


def _dot(a, b):
    return jnp.dot(a, b, preferred_element_type=_f32)


def _dot_nt(a, b):
    return lax.dot_general(a, b, (((1,), (1,)), ((), ())), preferred_element_type=_f32)


def _dot_tn(a, b):
    return lax.dot_general(a, b, (((0,), (0,)), ((), ())), preferred_element_type=_f32)


def _layer_norm(x, g, b):
    mu = jnp.mean(x, axis=-1, keepdims=True)
    xc = x - mu
    var = jnp.mean(xc * xc, axis=-1, keepdims=True)
    return xc * lax.rsqrt(var + LN_EPS) * g + b


def _rms_norm(x, g):
    ms = jnp.mean(x * x, axis=-1, keepdims=True)
    return x * lax.rsqrt(ms + RMS_EPS) * g


def _sigmoid(x):
    return 1.0 / (1.0 + jnp.exp(-x))


def _tt_load(ref, n, lead=()):
    return jnp.concatenate(
        [ref[lead + (pl.ds(a, n, stride=SUBLANES), slice(None))] for a in range(SUBLANES)], axis=1)


def _tt_store(ref, x):
    n = x.shape[0]
    for a in range(SUBLANES):
        ref[pl.ds(a, n, stride=SUBLANES), :] = x[:, a * LANES:(a + 1) * LANES]


def _tt_rows(tok):
    return pl.ds(pl.multiple_of(tok * SUBLANES, SUBLANES), SUBLANES)


def _rope(x, cos, sin_lo, sin_hi):
    half = MLA_ROPE // 2
    from_hi = pltpu.roll(x, LANES - half, 1)
    from_lo = pltpu.roll(x, half, 1)
    return x * cos + from_hi * sin_lo + from_lo * sin_hi


def _inproj_kernel(x_ref, head_ref, lng_ref, lnb_ref, w_ref, w2_ref, gb_ref, qg_ref, wuqt_ref, kvg_ref,
                   wuk_ref, wuvt_ref, cos_ref, sl_ref, sh_ref, cost_ref, sint_ref, tril_ref,
                   s_ref, qt_ref, kt_ref, ke_ref, gv_ref, dec_ref, sr_ref, qm_ref, km_ref,
                   vm_ref, ga_ref, gbt_ref):
    t = pl.program_id(1)
    x_in = jnp.where(t == 0, head_ref[...], x_ref[0])
    sn = _layer_norm(x_in, lng_ref[...], lnb_ref[...])
    s_ref[0] = sn
    snb = sn.astype(_bf16)
    row = t * TILE + lax.broadcasted_iota(jnp.int32, (TILE, 1), 0)
    valid = row >= PADL

    a_lr = _dot(snb, w_ref[:, _C_A:_C_A + LANES])
    z = _dot(a_lr.astype(_bf16), w2_ref[...]) + gb_ref[...]
    la = (jnp.minimum(z, 0.0) - jnp.log1p(jnp.exp(-jnp.abs(z)))) * (1.0 / GLA_GATE_TAU)
    la = jnp.where(valid, la, 0.0)
    hi = la.astype(_bf16)
    r1 = la - hi.astype(_f32)
    mid = r1.astype(_bf16)
    lo = (r1 - mid.astype(_f32)).astype(_bf16)
    tril = tril_ref[...]
    bc = _dot(tril, hi) + _dot(tril, mid) + _dot(tril, lo)
    n_chunks = TILE // GLA_CHUNK
    lasts = [bc[c * GLA_CHUNK + GLA_CHUNK - 1:(c + 1) * GLA_CHUNK, :] for c in range(n_chunks)]
    for c in range(n_chunks):
        dec_ref[0, c:c + 1, :] = jnp.exp(lasts[c])
    b_last = jnp.concatenate(
        [jnp.broadcast_to(l, (GLA_CHUNK, GLA_QK)) for l in lasts], axis=0)
    gq = _dot(snb, w_ref[:, _C_Q:_C_Q + GLA_QK])
    gk = jnp.where(valid, _dot(snb, w_ref[:, _C_K:_C_K + GLA_QK]), 0.0)
    qt_ref[0] = (gq * (GLA_DK ** -0.5) * jnp.exp(bc)).astype(_bf16)
    kt_ref[0] = (gk * jnp.exp(-bc)).astype(_bf16)
    ke_ref[0] = (gk * jnp.exp(b_last - bc)).astype(_bf16)
    gv_ref[0] = jnp.where(valid, _dot(snb, w_ref[:, _C_V:_C_V + GLA_VW]), 0.0).astype(_bf16)
    r = _dot(snb, w_ref[:, _C_R:_C_R + GLA_VW])
    sr_ref[0] = (r * _sigmoid(r)).astype(_bf16)

    cos = cos_ref[...]
    sl = sl_ref[...]
    sh = sh_ref[...]
    cq = _dot(snb, w_ref[:, _C_CQ:_C_CQ + MLA_Q_RANK])
    cqn = _rms_norm(cq, qg_ref[...]).astype(_bf16)
    qft = _dot_nt(wuqt_ref[...], cqn)
    scale = (MLA_QDIM ** -0.5) * LOG2E
    cost = cost_ref[...]
    sint = sint_ref[...]
    half = MLA_ROPE // 2
    for h in range(MLA_HEADS):
        base = h * HEAD_PAD
        x1 = qft[base + MLA_NOPE:base + MLA_NOPE + half]
        x2 = qft[base + MLA_NOPE + half:base + MLA_QDIM]
        qm_ref[0, base:base + MLA_NOPE] = (qft[base:base + MLA_NOPE] * scale).astype(_bf16)
        qm_ref[0, base + MLA_NOPE:base + MLA_NOPE + half] = ((x1 * cost - x2 * sint) * scale).astype(_bf16)
        qm_ref[0, base + MLA_NOPE + half:base + MLA_QDIM] = ((x1 * sint + x2 * cost) * scale).astype(_bf16)
        qm_ref[0, base + MLA_QDIM:base + HEAD_PAD] = jnp.zeros((HEAD_PAD - MLA_QDIM, TILE), _bf16)
    ckv = _dot(snb, w_ref[:, _C_CKV:_C_CKV + MLA_KV_RANK])
    ckvn = _rms_norm(ckv, kvg_ref[...]).astype(_bf16)
    kf = _dot(ckvn, wuk_ref[...])
    kr = _rope(_dot(snb, w_ref[:, _C_KR:_C_KR + LANES]), cos, sl, sh)
    for h in range(MLA_HEADS):
        km_ref[0, :, h * HEAD_PAD:(h + 1) * HEAD_PAD] = (
            kf[:, h * HEAD_PAD:(h + 1) * HEAD_PAD] + kr).astype(_bf16)
    vt = _dot_nt(wuvt_ref[...], ckvn)
    for h in range(MLA_HEADS):
        vm_ref[0, h * V_AUG:h * V_AUG + MLA_DV] = vt[h * MLA_DV:(h + 1) * MLA_DV].astype(_bf16)
        vm_ref[0, h * V_AUG + MLA_DV:(h + 1) * V_AUG] = jnp.ones((V_AUG - MLA_DV, TILE), _bf16)

    ga_ref[0] = _sigmoid(_dot(snb, w_ref[:, _C_GA:_C_GA + D_MODEL])).astype(_bf16)
    gbt_ref[0] = _sigmoid(_dot(snb, w_ref[:, _C_GB:_C_GB + D_MODEL])).astype(_bf16)


def _gla_kernel(qt_ref, kt_ref, ke_ref, gv_ref, dec_ref, sr_ref, ng_ref, o_ref, st_ref):
    t = pl.program_id(1)

    @pl.when(t == 0)
    def _():
        st_ref[...] = jnp.zeros_like(st_ref)

    ri = lax.broadcasted_iota(jnp.int32, (GLA_CHUNK, GLA_CHUNK), 0)
    ci = lax.broadcasted_iota(jnp.int32, (GLA_CHUNK, GLA_CHUNK), 1)
    causal = ri >= ci
    ng = ng_ref[...]
    for c in range(TILE // GLA_CHUNK):
        rows = slice(c * GLA_CHUNK, (c + 1) * GLA_CHUNK)
        for h in range(GLA_HEADS):
            kc = slice(h * GLA_DK, (h + 1) * GLA_DK)
            vc = slice(h * GLA_DV, (h + 1) * GLA_DV)
            q = qt_ref[0, rows, kc]
            k = kt_ref[0, rows, kc]
            e = ke_ref[0, rows, kc]
            v = gv_ref[0, rows, vc]
            st = st_ref[h]
            att = jnp.where(causal, _dot_nt(q, k), 0.0)
            o = _dot(att.astype(_bf16), v) + _dot_nt(q, st.astype(_bf16))
            st_ref[h] = st * dec_ref[0, c:c + 1, kc] + _dot_tn(v, e)
            o = _rms_norm(o, ng) * sr_ref[0, rows, vc].astype(_f32)
            o_ref[0, rows, vc] = o.astype(_bf16)


MLA_PAIR = 2


def _mla_tile_start(j):
    return pl.multiple_of(TILE + j * KV_TILE, TILE)


def _mla_finish(o_ref, acc_ref):
    outs = []
    for h in range(MLA_PAIR):
        a = acc_ref[h]
        outs.append(a[0:MLA_DV] / a[MLA_DV:MLA_DV + 1])
    o_ref[0] = jnp.concatenate(outs, axis=0).T.astype(_bf16)


def _mla_kernel(q_ref, k_ref, v_ref, o_ref, acc_ref, knorm_ref):
    i = pl.program_id(2)
    heads = MLA_PAIR
    ones = jnp.ones((HEAD_PAD, LANES), _bf16)

    @pl.when(i == 0)
    def _():
        for h in range(heads):
            kk = k_ref[0, :, h * HEAD_PAD:(h + 1) * HEAD_PAD].astype(_f32)
            hi = (kk * kk).astype(_bf16)
            knorm_ref[h] = jnp.max(_dot(hi, ones), axis=0, keepdims=True)

    half = slice(KV_TILE, Q_TILE)
    q_t = [q_ref[0, h * HEAD_PAD:(h + 1) * HEAD_PAD, :] for h in range(heads)]
    q_hi = [q_ref[0, h * HEAD_PAD:(h + 1) * HEAD_PAD, half] for h in range(heads)]

    def score_bound(h, q):
        qq = q.astype(_f32)
        qn2 = jnp.sum(qq * qq, axis=0, keepdims=True)
        return jnp.sqrt(qn2 * knorm_ref[h][:, 0:1]) * BOUND_SLACK

    bound = [score_bound(h, q_t[h]) for h in range(heads)]
    bound_hi = [score_bound(h, q_hi[h]) for h in range(heads)]

    def keys(j, h):
        return k_ref[0, pl.ds(_mla_tile_start(j), KV_TILE), h * HEAD_PAD:(h + 1) * HEAD_PAD]

    def weighted(j, h, p):
        return _dot(v_ref[0, h * V_AUG:(h + 1) * V_AUG, pl.ds(_mla_tile_start(j), KV_TILE)], p)

    k_row = lax.broadcasted_iota(jnp.int32, (KV_TILE, Q_TILE), 0)
    q_col = lax.broadcasted_iota(jnp.int32, (KV_TILE, Q_TILE), 1)
    k_row_sq = lax.broadcasted_iota(jnp.int32, (KV_TILE, KV_TILE), 0)
    q_col_sq = lax.broadcasted_iota(jnp.int32, (KV_TILE, KV_TILE), 1)
    s_meta = [_dot(k_ref[0, PADL:TILE, h * HEAD_PAD:(h + 1) * HEAD_PAD], q_t[h]) for h in range(heads)]
    s_lo = [jnp.where(k_row <= q_col, _dot(keys(2 * i, h), q_t[h]), NEG) for h in range(heads)]
    s_hi = [jnp.where(k_row_sq <= q_col_sq, _dot(keys(2 * i + 1, h), q_hi[h]), NEG)
            for h in range(heads)]
    for h in range(heads):
        p_meta = jnp.concatenate([jnp.zeros((PADL, Q_TILE), _bf16),
                                  jnp.exp2(s_meta[h] - bound[h]).astype(_bf16)], axis=0)
        acc_ref[h] = (_dot(v_ref[0, h * V_AUG:(h + 1) * V_AUG, 0:TILE], p_meta)
                      + weighted(2 * i, h, jnp.exp2(s_lo[h] - bound[h]).astype(_bf16)))
        acc_ref[h, :, half] += weighted(2 * i + 1, h, jnp.exp2(s_hi[h] - bound_hi[h]).astype(_bf16))

    def body(j, c):
        s = [_dot(keys(j, h), q_t[h]) for h in range(heads)]
        for h in range(heads):
            acc_ref[h] += weighted(j, h, jnp.exp2(s[h] - bound[h]).astype(_bf16))
        return c

    lax.fori_loop(0, 2 * i, body, 0)
    _mla_finish(o_ref, acc_ref)

    l_min = jnp.minimum(jnp.min(acc_ref[0, MLA_DV:MLA_DV + 1, :]), jnp.min(acc_ref[1, MLA_DV:MLA_DV + 1, :]))

    @pl.when(jnp.logical_not(l_min >= MIN_SOFTMAX_SUM))
    def _():
        _mla_exact(q_ref, k_ref, v_ref, o_ref, acc_ref)


def _mla_exact(q_ref, k_ref, v_ref, o_ref, acc_ref):
    i = pl.program_id(2)
    heads = MLA_PAIR
    q_t = [q_ref[0, h * HEAD_PAD:(h + 1) * HEAD_PAD, :] for h in range(heads)]

    ms = []
    for h in range(heads):
        kb = k_ref[0, PADL:TILE, h * HEAD_PAD:(h + 1) * HEAD_PAD]
        s = _dot(kb, q_t[h])
        m0 = jnp.max(s, axis=0, keepdims=True)
        p = jnp.concatenate([jnp.zeros((PADL, Q_TILE), _bf16), jnp.exp2(s - m0).astype(_bf16)], axis=0)
        acc_ref[h] = _dot(v_ref[0, h * V_AUG:(h + 1) * V_AUG, 0:TILE], p)
        ms.append(m0)

    k_row = lax.broadcasted_iota(jnp.int32, (KV_TILE, Q_TILE), 0)
    q_col = lax.broadcasted_iota(jnp.int32, (KV_TILE, Q_TILE), 1)

    def body(j, ms):
        visible = (j - 2 * i) * KV_TILE + k_row <= q_col
        out = []
        for h in range(heads):
            s = _dot(k_ref[0, pl.ds(_mla_tile_start(j), KV_TILE), h * HEAD_PAD:(h + 1) * HEAD_PAD],
                     q_t[h])
            s = jnp.where(visible, s, NEG)
            vb = v_ref[0, h * V_AUG:(h + 1) * V_AUG, pl.ds(_mla_tile_start(j), KV_TILE)]
            m_new = jnp.maximum(ms[h], jnp.max(s, axis=0, keepdims=True))
            alpha = jnp.exp2(ms[h] - m_new)
            acc_ref[h] = alpha * acc_ref[h] + _dot(vb, jnp.exp2(s - m_new).astype(_bf16))
            out.append(m_new)
        return tuple(out)

    lax.fori_loop(0, 2 * i + 2, body, tuple(ms))
    _mla_finish(o_ref, acc_ref)


def _merge_kernel(og_ref, om_ref, ga_ref, gbt_ref, s_ref, wbg_ref, wbm_ref, wo_ref, lng_ref,
                  lnb_ref, rwh_ref, rwl_ref, rb_ref, tril_ref,
                  s2_ref, info_ref, cnt_ref, carry_ref):
    step = pl.program_id(0)

    @pl.when(step == 0)
    def _():
        carry_ref[...] = jnp.zeros_like(carry_ref)

    ba = _dot(og_ref[...], wbg_ref[...])
    bb = _dot(om_ref[...], wbm_ref[...])
    merged = ga_ref[...].astype(_f32) * ba + gbt_ref[...].astype(_f32) * bb
    y = ALPHA * s_ref[...] + _dot(merged.astype(_bf16), wo_ref[...])
    s2 = _layer_norm(y, lng_ref[...], lnb_ref[...])
    _tt_store(s2_ref, s2)

    xh = s2.astype(_bf16)
    xl = (s2 - xh.astype(_f32)).astype(_bf16)
    logits = _dot(xh, rwh_ref[...]) + _dot(xl, rwh_ref[...]) + _dot(xh, rwl_ref[...]) + rb_ref[...]
    lane = lax.broadcasted_iota(jnp.int32, (MERGE_TILE, LANES), 1)
    is_g = lane < N_GROUPS
    gl = jnp.where(is_g, logits, NEG)
    gmax = jnp.max(gl, axis=-1, keepdims=True)
    gidx = jnp.min(jnp.where(gl == gmax, lane, LANES), axis=-1, keepdims=True)
    p_g = 1.0 / jnp.sum(jnp.where(is_g, jnp.exp(gl - gmax), 0.0), axis=-1, keepdims=True)
    lo = N_GROUPS + EXPERTS_PER_GROUP * gidx
    el = jnp.where((lane >= lo) & (lane < lo + EXPERTS_PER_GROUP), logits, NEG)
    v1 = jnp.max(el, axis=-1, keepdims=True)
    i1 = jnp.min(jnp.where(el == v1, lane, LANES), axis=-1, keepdims=True)
    el2 = jnp.where(lane == i1, NEG, el)
    v2 = jnp.max(el2, axis=-1, keepdims=True)
    i2 = jnp.min(jnp.where(el2 == v2, lane, LANES), axis=-1, keepdims=True)
    tt = jnp.exp(v2 - v1)
    p1 = 1.0 / (1.0 + tt)
    p2 = tt / (1.0 + tt)
    e1 = i1 - N_GROUPS
    e2 = i2 - N_GROUPS
    hit1 = lane == e1
    hit2 = lane == e2
    onehot = jnp.where(hit1 | hit2, 1.0, 0.0)
    before = _dot(tril_ref[...], onehot.astype(_bf16)) + carry_ref[0:1, :]
    r1 = jnp.sum(jnp.where(hit1, before, 0.0), axis=-1, keepdims=True)
    r2 = jnp.sum(jnp.where(hit2, before, 0.0), axis=-1, keepdims=True)
    new_carry = carry_ref[0:1, :] + jnp.sum(onehot, axis=0, keepdims=True)
    carry_ref[...] = jnp.broadcast_to(new_carry, carry_ref.shape)
    cnt_ref[...] = jnp.broadcast_to(new_carry, cnt_ref.shape)
    info = jnp.where(lane == 0, e1.astype(_f32),
           jnp.where(lane == 1, e2.astype(_f32),
           jnp.where(lane == 2, p_g * p1,
           jnp.where(lane == 3, p_g * p2,
           jnp.where(lane == 4, r1,
           jnp.where(lane == 5, r2, 0.0))))))
    info_ref[...] = info


def _dispatch_kernel(pos_ref, zrow_ref, nused_ref, s2_ref, xs_hbm, zero_ref, sems):
    g = pl.program_id(0)
    zero_sem = sems.at[1]
    row_sem = sems.at[0]

    tile_rows = EXPERT_TILE * SUBLANES

    def zero_copy(row):
        start = pl.multiple_of(row * SUBLANES, tile_rows)
        return pltpu.make_async_copy(zero_ref, xs_hbm.at[pl.ds(start, tile_rows)], zero_sem)

    @pl.when(g == 0)
    def _():
        zero_ref[...] = jnp.zeros_like(zero_ref)

        def start(e, c):
            @pl.when(zrow_ref[e] >= 0)
            def _():
                zero_copy(zrow_ref[e]).start()
            return c

        def wait(e, c):
            @pl.when(zrow_ref[e] >= 0)
            def _():
                zero_copy(0).wait()
            return c

        def start_tail(u, c):
            zero_copy(u * EXPERT_TILE).start()
            return c

        def wait_tail(u, c):
            zero_copy(0).wait()
            return c

        n_tiles = xs_hbm.shape[0] // tile_rows
        lax.fori_loop(0, N_EXPERTS, start, 0)
        lax.fori_loop(nused_ref[0], n_tiles, start_tail, 0)
        lax.fori_loop(0, N_EXPERTS, wait, 0)
        lax.fori_loop(nused_ref[0], n_tiles, wait_tail, 0)

    def issue(r, c):
        tok = g * DISPATCH_TILE + r
        for k in range(2):
            pltpu.make_async_copy(s2_ref.at[_tt_rows(r)],
                                  xs_hbm.at[_tt_rows(pos_ref[2 * tok + k])], row_sem).start()
        return c

    lax.fori_loop(0, DISPATCH_TILE, issue, 0, unroll=8)
    for k in range(2):
        pltpu.make_async_copy(s2_ref, xs_hbm.at[pl.ds(0, DISPATCH_TILE * SUBLANES)], row_sem).wait()


def _expert_kernel(te_ref, nused_ref, x_ref, wg_ref, wu_ref, wd_ref, o_ref, wgb_ref, wub_ref, wdb_ref):
    u = pl.program_id(0)
    used = u < nused_ref[0]
    first_of_expert = jnp.logical_or(u == 0, te_ref[u] != te_ref[jnp.maximum(u - 1, 0)])

    @pl.when(jnp.logical_and(used, first_of_expert))
    def _():
        wgb_ref[...] = wg_ref[0].astype(_bf16)
        wub_ref[...] = wu_ref[0].astype(_bf16)
        wdb_ref[...] = wd_ref[0].astype(_bf16)

    @pl.when(used)
    def _():
        x = _tt_load(x_ref, EXPERT_TILE).astype(_bf16)
        a = _dot(x, wgb_ref[...])
        up = _dot(x, wub_ref[...])
        hid = a * _sigmoid(a) * up
        _tt_store(o_ref, _dot(hid.astype(_bf16), wdb_ref[...]))

    @pl.when(u >= nused_ref[0])
    def _():
        o_ref[...] = jnp.zeros_like(o_ref)


def _combine_kernel(pos_ref, s2_ref, info_ref, ys_hbm, lng_ref, lnb_ref, o_ref, buf_ref, sems):
    g = pl.program_id(0)
    last = pl.num_programs(0) - 1

    def issue_tile(tile, slot):
        def body(r, c):
            tok = tile * COMBINE_TILE + r
            for k in range(2):
                pltpu.make_async_copy(ys_hbm.at[_tt_rows(pos_ref[2 * tok + k])],
                                      buf_ref.at[slot, k, _tt_rows(r)], sems.at[slot]).start()
            return c

        lax.fori_loop(0, COMBINE_TILE, body, 0, unroll=8)

    @pl.when(g == 0)
    def _():
        issue_tile(0, 0)

    @pl.when(g < last)
    def _():
        issue_tile(g + 1, (g + 1) % 2)

    slot = g % 2
    for k in range(2):
        pltpu.make_async_copy(ys_hbm.at[pl.ds(0, COMBINE_TILE * SUBLANES)], buf_ref.at[slot, k],
                              sems.at[slot]).wait()
    info = info_ref[...]
    y = (ALPHA * _tt_load(s2_ref, COMBINE_TILE)
         + info[:, 2:3] * _tt_load(buf_ref, COMBINE_TILE, (slot, 0))
         + info[:, 3:4] * _tt_load(buf_ref, COMBINE_TILE, (slot, 1)))
    o_ref[...] = _layer_norm(y, lng_ref[...], lnb_ref[...])


def _const_spec(shape):
    nd = len(shape)
    return pl.BlockSpec(shape, lambda *_: (0,) * nd)


def _rope_tables(lp):
    pos = jnp.maximum(jnp.arange(lp, dtype=_f32) - PADL, 0.0)
    inv_freq = ROPE_BASE ** (-jnp.arange(0, MLA_ROPE, 2, dtype=_f32) / MLA_ROPE)
    ang = pos[:, None] * inv_freq[None, :]
    cos, sin = jnp.cos(ang), jnp.sin(ang)
    half = MLA_ROPE // 2
    ones = jnp.ones((lp, MLA_NOPE), _f32)
    zeros_n = jnp.zeros((lp, MLA_NOPE), _f32)
    zeros_h = jnp.zeros((lp, half), _f32)
    tail1 = jnp.ones((lp, LANES - MLA_QDIM), _f32)
    tail0 = jnp.zeros((lp, LANES - MLA_QDIM), _f32)
    cos_t = jnp.concatenate([ones, cos, cos, tail1], axis=1)
    sin_lo = jnp.concatenate([zeros_n, -sin, zeros_h, tail0], axis=1)
    sin_hi = jnp.concatenate([zeros_n, zeros_h, sin, tail0], axis=1)
    return cos_t, sin_lo, sin_hi, cos.T, sin.T


def _pad_heads(w, width):
    k = w.shape[0]
    w = w.reshape(k, MLA_HEADS, width)
    w = jnp.pad(w, ((0, 0), (0, 0), (0, HEAD_PAD - width)))
    return w.reshape(k, MLA_HEADS * HEAD_PAD)


def kernel(x, meta_tokens, ln_emb_g, ln_emb_b, w_in, gla_gate_w2, gla_gate_b, gla_norm_g, mla_q_norm_g, mla_w_uq, mla_kv_norm_g, mla_w_uk, mla_w_uv, w_branch_gla, w_branch_mla, w_out, ln_mix_g, ln_mix_b, router_group_w, router_group_b, router_expert_w, router_expert_b, expert_w_gate, expert_w_up, expert_w_down, ln_ffn_g, ln_ffn_b):
    bsz, seq, d = x.shape
    assert d == D_MODEL and seq % Q_TILE == 0 and w_in.shape[0] == DEPTH == 1
    lp = PADL + N_META + seq
    nt = lp // TILE
    ntok = bsz * seq
    row2 = lambda v: v.reshape(1, -1).astype(_f32)

    head_tile = jnp.concatenate([jnp.zeros((PADL, d), _f32), meta_tokens.astype(_f32)], axis=0)
    wi = w_in[0]
    o_a = 2 * GLA_QK + 2 * GLA_VW
    o_cq = o_a + GLA_GATE_RANK
    o_ckv = o_cq + MLA_Q_RANK
    o_kr = o_ckv + MLA_KV_RANK
    o_ga = o_kr + MLA_ROPE
    w_a = jnp.pad(wi[:, o_a:o_cq], ((0, 0), (0, LANES - GLA_GATE_RANK)))
    w_kr = jnp.pad(wi[:, o_kr:o_ga], ((0, 0), (MLA_NOPE, LANES - MLA_QDIM)))
    w_all = jnp.concatenate([wi[:, :o_a], w_a, wi[:, o_cq:o_kr], w_kr, wi[:, o_ga:]], axis=1).astype(_bf16)
    assert w_all.shape == (d, _W_COLS)
    w2p = jnp.pad(gla_gate_w2[0], ((0, LANES - GLA_GATE_RANK), (0, 0))).astype(_bf16)
    wuqt = _pad_heads(mla_w_uq[0], MLA_QDIM).T.astype(_bf16)
    wuk = _pad_heads(mla_w_uk[0], MLA_NOPE).astype(_bf16)
    wuvt = mla_w_uv[0].T.astype(_bf16)
    cos_t, sin_lo, sin_hi, cos_tr, sin_tr = _rope_tables(lp)
    blk = np.arange(TILE)
    tril_chunks = jnp.asarray(
        ((blk[:, None] >= blk[None, :]) & (blk[:, None] // GLA_CHUNK == blk[None, :] // GLA_CHUNK)),
        dtype=_bf16)

    pad_map = lambda b, t: (b, t, 0)
    real_map = lambda b, t: (b, jnp.maximum(t - 1, 0), 0)
    tab_spec = pl.BlockSpec((TILE, LANES), lambda b, t: (t, 0))
    tabt_spec = pl.BlockSpec((MLA_ROPE // 2, TILE), lambda b, t: (0, t))
    real_map_t = lambda b, t: (b, 0, jnp.maximum(t - 1, 0))
    out_shapes = (
        jax.ShapeDtypeStruct((bsz, seq, d), _f32),
        jax.ShapeDtypeStruct((bsz, lp, GLA_QK), _bf16),
        jax.ShapeDtypeStruct((bsz, lp, GLA_QK), _bf16),
        jax.ShapeDtypeStruct((bsz, lp, GLA_QK), _bf16),
        jax.ShapeDtypeStruct((bsz, lp, GLA_VW), _bf16),
        jax.ShapeDtypeStruct((bsz * nt, TILE // GLA_CHUNK, GLA_QK), _f32),
        jax.ShapeDtypeStruct((bsz, seq, GLA_VW), _bf16),
        jax.ShapeDtypeStruct((bsz, MLA_HEADS * HEAD_PAD, seq), _bf16),
        jax.ShapeDtypeStruct((bsz, lp, MLA_HEADS * HEAD_PAD), _bf16),
        jax.ShapeDtypeStruct((bsz, MLA_HEADS * V_AUG, lp), _bf16),
        jax.ShapeDtypeStruct((bsz, seq, d), _bf16),
        jax.ShapeDtypeStruct((bsz, seq, d), _bf16),
    )
    out_specs = (
        pl.BlockSpec((1, TILE, d), real_map),
        pl.BlockSpec((1, TILE, GLA_QK), pad_map),
        pl.BlockSpec((1, TILE, GLA_QK), pad_map),
        pl.BlockSpec((1, TILE, GLA_QK), pad_map),
        pl.BlockSpec((1, TILE, GLA_VW), pad_map),
        pl.BlockSpec((1, TILE // GLA_CHUNK, GLA_QK), lambda b, t: (b * nt + t, 0, 0)),
        pl.BlockSpec((1, TILE, GLA_VW), real_map),
        pl.BlockSpec((1, MLA_HEADS * HEAD_PAD, TILE), real_map_t),
        pl.BlockSpec((1, TILE, MLA_HEADS * HEAD_PAD), pad_map),
        pl.BlockSpec((1, MLA_HEADS * V_AUG, TILE), lambda b, t: (b, 0, t)),
        pl.BlockSpec((1, TILE, d), real_map),
        pl.BlockSpec((1, TILE, d), real_map),
    )
    (s_emb, qt, kt, ke, gv, dec, sr, qm, km, vm, gate_a, gate_b) = pl.pallas_call(
        _inproj_kernel,
        grid=(bsz, nt),
        in_specs=[
            pl.BlockSpec((1, TILE, d), real_map),
            _const_spec((TILE, d)),
            _const_spec((1, d)), _const_spec((1, d)),
            _const_spec((d, _W_COLS)),
            _const_spec((LANES, GLA_QK)), _const_spec((1, GLA_QK)),
            _const_spec((1, MLA_Q_RANK)), _const_spec((MLA_HEADS * HEAD_PAD, MLA_Q_RANK)),
            _const_spec((1, MLA_KV_RANK)), _const_spec((MLA_KV_RANK, MLA_HEADS * HEAD_PAD)),
            _const_spec((MLA_HEADS * MLA_DV, MLA_KV_RANK)),
            tab_spec, tab_spec, tab_spec, tabt_spec, tabt_spec,
            _const_spec((TILE, TILE)),
        ],
        out_specs=out_specs,
        out_shape=out_shapes,
        compiler_params=pltpu.CompilerParams(
            dimension_semantics=("arbitrary", "arbitrary"), vmem_limit_bytes=VMEM_LIMIT),
        name="inproj",
    )(x, head_tile, row2(ln_emb_g), row2(ln_emb_b), w_all, w2p, row2(gla_gate_b[0]),
      row2(mla_q_norm_g[0]), wuqt, row2(mla_kv_norm_g[0]), wuk, wuvt, cos_t, sin_lo, sin_hi,
      cos_tr, sin_tr, tril_chunks)

    o_gla = pl.pallas_call(
        _gla_kernel,
        grid=(bsz, nt),
        in_specs=[
            pl.BlockSpec((1, TILE, GLA_QK), pad_map),
            pl.BlockSpec((1, TILE, GLA_QK), pad_map),
            pl.BlockSpec((1, TILE, GLA_QK), pad_map),
            pl.BlockSpec((1, TILE, GLA_VW), pad_map),
            pl.BlockSpec((1, TILE // GLA_CHUNK, GLA_QK), lambda b, t: (b * nt + t, 0, 0)),
            pl.BlockSpec((1, TILE, GLA_VW), real_map),
            _const_spec((1, GLA_DV)),
        ],
        out_specs=pl.BlockSpec((1, TILE, GLA_VW), real_map),
        out_shape=jax.ShapeDtypeStruct((bsz, seq, GLA_VW), _bf16),
        scratch_shapes=[pltpu.VMEM((GLA_HEADS, GLA_DV, GLA_DK), _f32)],
        compiler_params=pltpu.CompilerParams(
            dimension_semantics=("arbitrary", "arbitrary"), vmem_limit_bytes=VMEM_LIMIT),
        name="gla",
    )(qt, kt, ke, gv, dec, sr, row2(gla_norm_g[0]))

    pair = 2 * HEAD_PAD
    o_mla = pl.pallas_call(
        _mla_kernel,
        grid=(bsz, MLA_HEADS // 2, seq // Q_TILE),
        in_specs=[
            pl.BlockSpec((1, pair, Q_TILE), lambda b, hp, i: (b, hp, i)),
            pl.BlockSpec((1, lp, pair), lambda b, hp, i: (b, 0, hp)),
            pl.BlockSpec((1, 2 * V_AUG, lp), lambda b, hp, i: (b, hp, 0)),
        ],
        out_specs=pl.BlockSpec((1, Q_TILE, 2 * MLA_DV), lambda b, hp, i: (b, i, hp)),
        out_shape=jax.ShapeDtypeStruct((bsz, seq, MLA_HEADS * MLA_DV), _bf16),
        scratch_shapes=[pltpu.VMEM((MLA_PAIR, V_AUG, Q_TILE), _f32),
                        pltpu.VMEM((MLA_PAIR, 1, LANES), _f32)],
        compiler_params=pltpu.CompilerParams(
            dimension_semantics=("arbitrary", "arbitrary", "arbitrary"),
            vmem_limit_bytes=VMEM_LIMIT),
        name="mla",
    )(qm, km, vm)

    rw = jnp.concatenate([router_group_w[0], router_expert_w[0]], axis=1)
    rw = jnp.pad(rw, ((0, 0), (0, LANES - rw.shape[1])))
    rwh = rw.astype(_bf16)
    rwl = (rw - rwh.astype(_f32)).astype(_bf16)
    rb = jnp.concatenate([router_group_b[0], router_expert_b[0]])
    rb = jnp.pad(rb, (0, LANES - rb.shape[0])).reshape(1, LANES)
    mi = np.arange(MERGE_TILE)
    tril_strict = jnp.asarray(mi[:, None] > mi[None, :], dtype=_bf16)
    flat = lambda a: a.reshape(ntok, a.shape[-1])
    tok_spec = lambda w: pl.BlockSpec((MERGE_TILE, w), lambda g: (g, 0))
    tt_spec = lambda n, index_map: pl.BlockSpec((n * SUBLANES, LANES), index_map)
    s2, info, cnt = pl.pallas_call(
        _merge_kernel,
        grid=(ntok // MERGE_TILE,),
        in_specs=[tok_spec(d), tok_spec(d), tok_spec(d), tok_spec(d), tok_spec(d),
                  _const_spec((d, d)), _const_spec((d, d)), _const_spec((d, d)),
                  _const_spec((1, d)), _const_spec((1, d)),
                  _const_spec((d, LANES)), _const_spec((d, LANES)), _const_spec((1, LANES)),
                  _const_spec((MERGE_TILE, MERGE_TILE))],
        out_specs=(tt_spec(MERGE_TILE, lambda g: (g, 0)), tok_spec(LANES), _const_spec((8, LANES))),
        out_shape=(jax.ShapeDtypeStruct((ntok * SUBLANES, LANES), _f32),
                   jax.ShapeDtypeStruct((ntok, LANES), _f32),
                   jax.ShapeDtypeStruct((8, LANES), _f32)),
        scratch_shapes=[pltpu.VMEM((8, LANES), _f32)],
        compiler_params=pltpu.CompilerParams(
            dimension_semantics=("arbitrary",), vmem_limit_bytes=VMEM_LIMIT),
        name="merge_router",
    )(flat(o_gla), flat(o_mla), flat(gate_a), flat(gate_b), flat(s_emb),
      w_branch_gla[0].astype(_bf16), w_branch_mla[0].astype(_bf16), w_out[0].astype(_bf16),
      row2(ln_mix_g[0]), row2(ln_mix_b[0]), rwh, rwl, rb, tril_strict)

    n_tiles = (2 * ntok + N_EXPERTS * (EXPERT_TILE - 1)) // EXPERT_TILE
    n_rows = n_tiles * EXPERT_TILE
    e_idx = info[:, 0:2].astype(jnp.int32)
    rank = info[:, 4:6].astype(jnp.int32)
    counts = cnt[0, :N_EXPERTS].astype(jnp.int32)
    padded = ((counts + EXPERT_TILE - 1) // EXPERT_TILE) * EXPERT_TILE
    ends = jnp.cumsum(padded)
    starts = ends - padded
    pos = (starts[e_idx] + rank).reshape(-1)
    tile_start = jnp.arange(n_tiles, dtype=jnp.int32) * EXPERT_TILE
    tile_expert = jnp.minimum(
        jnp.sum((ends[None, :] <= tile_start[:, None]).astype(jnp.int32), axis=1), N_EXPERTS - 1)
    n_used = (ends[-1:] // EXPERT_TILE).astype(jnp.int32)
    zero_row = jnp.where(padded > 0, ends - EXPERT_TILE, -1).astype(jnp.int32)

    any_spec = pl.BlockSpec(memory_space=pl.ANY)
    xs = pl.pallas_call(
        _dispatch_kernel,
        grid_spec=pltpu.PrefetchScalarGridSpec(
            num_scalar_prefetch=3,
            grid=(ntok // DISPATCH_TILE,),
            in_specs=[tt_spec(DISPATCH_TILE, lambda g, p, z, nu: (g, 0))],
            out_specs=any_spec,
            scratch_shapes=[pltpu.VMEM((EXPERT_TILE * SUBLANES, LANES), _f32),
                            pltpu.SemaphoreType.DMA((2,))],
        ),
        out_shape=jax.ShapeDtypeStruct((n_rows * SUBLANES, LANES), _f32),
        compiler_params=pltpu.CompilerParams(
            dimension_semantics=("arbitrary",), vmem_limit_bytes=VMEM_LIMIT, has_side_effects=True),
        name="dispatch",
    )(pos, zero_row, n_used, s2)

    ff = EXPERT_FF
    wg = expert_w_gate[0].reshape(N_EXPERTS, d, ff)
    wu = expert_w_up[0].reshape(N_EXPERTS, d, ff)
    wd = expert_w_down[0].reshape(N_EXPERTS, ff, d)
    ys = pl.pallas_call(
        _expert_kernel,
        grid_spec=pltpu.PrefetchScalarGridSpec(
            num_scalar_prefetch=2,
            grid=(n_tiles,),
            in_specs=[
                tt_spec(EXPERT_TILE, lambda u, te, nu: (jnp.minimum(u, nu[0] - 1), 0)),
                pl.BlockSpec((1, d, ff), lambda u, te, nu: (te[u], 0, 0)),
                pl.BlockSpec((1, d, ff), lambda u, te, nu: (te[u], 0, 0)),
                pl.BlockSpec((1, ff, d), lambda u, te, nu: (te[u], 0, 0)),
            ],
            out_specs=tt_spec(EXPERT_TILE, lambda u, te, nu: (u, 0)),
            scratch_shapes=[pltpu.VMEM((d, ff), _bf16), pltpu.VMEM((d, ff), _bf16),
                            pltpu.VMEM((ff, d), _bf16)],
        ),
        out_shape=jax.ShapeDtypeStruct((n_rows * SUBLANES, LANES), _f32),
        compiler_params=pltpu.CompilerParams(
            dimension_semantics=("arbitrary",), vmem_limit_bytes=VMEM_LIMIT),
        name="experts",
    )(tile_expert, n_used, xs, wg, wu, wd)

    out = pl.pallas_call(
        _combine_kernel,
        grid_spec=pltpu.PrefetchScalarGridSpec(
            num_scalar_prefetch=1,
            grid=(ntok // COMBINE_TILE,),
            in_specs=[
                tt_spec(COMBINE_TILE, lambda g, p: (g, 0)),
                pl.BlockSpec((COMBINE_TILE, LANES), lambda g, p: (g, 0)),
                any_spec,
                pl.BlockSpec((1, d), lambda g, p: (0, 0)),
                pl.BlockSpec((1, d), lambda g, p: (0, 0)),
            ],
            out_specs=pl.BlockSpec((COMBINE_TILE, d), lambda g, p: (g, 0)),
            scratch_shapes=[pltpu.VMEM((2, 2, COMBINE_TILE * SUBLANES, LANES), _f32),
                            pltpu.SemaphoreType.DMA((2,))],
        ),
        out_shape=jax.ShapeDtypeStruct((ntok, d), _f32),
        compiler_params=pltpu.CompilerParams(
            dimension_semantics=("arbitrary",), vmem_limit_bytes=VMEM_LIMIT),
        name="combine_ln",
    )(pos, s2, info, ys, row2(ln_ffn_g[0]), row2(ln_ffn_b[0]))
    return out.reshape(bsz, seq, d)
```

```python
import jax
import jax.numpy as jnp
import numpy as np
from jax import lax
from jax.experimental import pallas as pl
from jax.experimental.pallas import tpu as pltpu

D_MODEL = 1024
N_META = 16
GLA_HEADS = 4
GLA_DK = 128
GLA_DV = 256
GLA_QK = GLA_HEADS * GLA_DK
GLA_VW = GLA_HEADS * GLA_DV
GLA_GATE_RANK = 16
GLA_GATE_TAU = 16.0
GLA_CHUNK = 64
MLA_HEADS = 16
MLA_Q_RANK = 384
MLA_KV_RANK = 256
MLA_NOPE = 64
MLA_ROPE = 32
MLA_DV = 64
MLA_QDIM = MLA_NOPE + MLA_ROPE
ROPE_BASE = 10000.0
N_GROUPS = 4
EXPERTS_PER_GROUP = 8
N_EXPERTS = N_GROUPS * EXPERTS_PER_GROUP
EXPERT_FF = 256
DEPTH = 1
ALPHA = (2.0 * DEPTH) ** 0.25
LN_EPS = 1e-5
RMS_EPS = 1e-6

LANES = 128
SUBLANES = 8
TILE = 256
PADL = TILE - N_META
HEAD_PAD = LANES
BF16_ROWS = 16
KV_TILE = 512
Q_TILE = 2 * KV_TILE
V_AUG = MLA_DV + BF16_ROWS
MERGE_TILE = 512
EXPERT_TILE = 256
DISPATCH_TILE = 512
COMBINE_TILE = 256
NEG = -1e30
LOG2E = 1.4426950408889634
BOUND_SLACK = 1.02
MIN_SOFTMAX_SUM = 2.0 ** -100
VMEM_LIMIT = 56 * 1024 * 1024

_C_Q, _C_K, _C_V, _C_R = 0, 512, 1024, 2048
_C_A = 3072
_C_CQ = _C_A + LANES
_C_CKV = _C_CQ + MLA_Q_RANK
_C_KR = _C_CKV + MLA_KV_RANK
_C_GA = _C_KR + LANES
_C_GB = _C_GA + D_MODEL
_W_COLS = _C_GB + D_MODEL

_f32 = jnp.float32
_bf16 = jnp.bfloat16


def _dot(a, b):
    return jnp.dot(a, b, preferred_element_type=_f32)


def _dot_nt(a, b):
    return lax.dot_general(a, b, (((1,), (1,)), ((), ())), preferred_element_type=_f32)


def _dot_tn(a, b):
    return lax.dot_general(a, b, (((0,), (0,)), ((), ())), preferred_element_type=_f32)


def _layer_norm(x, g, b):
    mu = jnp.mean(x, axis=-1, keepdims=True)
    xc = x - mu
    var = jnp.mean(xc * xc, axis=-1, keepdims=True)
    return xc * lax.rsqrt(var + LN_EPS) * g + b


def _rms_norm(x, g):
    ms = jnp.mean(x * x, axis=-1, keepdims=True)
    return x * lax.rsqrt(ms + RMS_EPS) * g


def _sigmoid(x):
    return 1.0 / (1.0 + jnp.exp(-x))


def _tt_load(ref, n, lead=()):
    return jnp.concatenate(
        [ref[lead + (pl.ds(a, n, stride=SUBLANES), slice(None))] for a in range(SUBLANES)], axis=1)


def _tt_store(ref, x):
    n = x.shape[0]
    for a in range(SUBLANES):
        ref[pl.ds(a, n, stride=SUBLANES), :] = x[:, a * LANES:(a + 1) * LANES]


def _tt_rows(tok):
    return pl.ds(pl.multiple_of(tok * SUBLANES, SUBLANES), SUBLANES)


def _rope(x, cos, sin_lo, sin_hi):
    half = MLA_ROPE // 2
    from_hi = pltpu.roll(x, LANES - half, 1)
    from_lo = pltpu.roll(x, half, 1)
    return x * cos + from_hi * sin_lo + from_lo * sin_hi


def _inproj_kernel(x_ref, head_ref, lng_ref, lnb_ref, w_ref, w2_ref, gb_ref, qg_ref, wuqt_ref, kvg_ref,
                   wuk_ref, wuvt_ref, cos_ref, sl_ref, sh_ref, cost_ref, sint_ref, tril_ref,
                   s_ref, qt_ref, kt_ref, ke_ref, gv_ref, dec_ref, sr_ref, qm_ref, km_ref,
                   vm_ref, ga_ref, gbt_ref):
    t = pl.program_id(1)
    x_in = jnp.where(t == 0, head_ref[...], x_ref[0])
    sn = _layer_norm(x_in, lng_ref[...], lnb_ref[...])
    s_ref[0] = sn
    snb = sn.astype(_bf16)
    row = t * TILE + lax.broadcasted_iota(jnp.int32, (TILE, 1), 0)
    valid = row >= PADL

    a_lr = _dot(snb, w_ref[:, _C_A:_C_A + LANES])
    z = _dot(a_lr.astype(_bf16), w2_ref[...]) + gb_ref[...]
    la = (jnp.minimum(z, 0.0) - jnp.log1p(jnp.exp(-jnp.abs(z)))) * (1.0 / GLA_GATE_TAU)
    la = jnp.where(valid, la, 0.0)
    hi = la.astype(_bf16)
    r1 = la - hi.astype(_f32)
    mid = r1.astype(_bf16)
    lo = (r1 - mid.astype(_f32)).astype(_bf16)
    tril = tril_ref[...]
    bc = _dot(tril, hi) + _dot(tril, mid) + _dot(tril, lo)
    n_chunks = TILE // GLA_CHUNK
    lasts = [bc[c * GLA_CHUNK + GLA_CHUNK - 1:(c + 1) * GLA_CHUNK, :] for c in range(n_chunks)]
    for c in range(n_chunks):
        dec_ref[0, c:c + 1, :] = jnp.exp(lasts[c])
    b_last = jnp.concatenate(
        [jnp.broadcast_to(l, (GLA_CHUNK, GLA_QK)) for l in lasts], axis=0)
    gq = _dot(snb, w_ref[:, _C_Q:_C_Q + GLA_QK])
    gk = jnp.where(valid, _dot(snb, w_ref[:, _C_K:_C_K + GLA_QK]), 0.0)
    qt_ref[0] = (gq * (GLA_DK ** -0.5) * jnp.exp(bc)).astype(_bf16)
    kt_ref[0] = (gk * jnp.exp(-bc)).astype(_bf16)
    ke_ref[0] = (gk * jnp.exp(b_last - bc)).astype(_bf16)
    gv_ref[0] = jnp.where(valid, _dot(snb, w_ref[:, _C_V:_C_V + GLA_VW]), 0.0).astype(_bf16)
    r = _dot(snb, w_ref[:, _C_R:_C_R + GLA_VW])
    sr_ref[0] = (r * _sigmoid(r)).astype(_bf16)

    cos = cos_ref[...]
    sl = sl_ref[...]
    sh = sh_ref[...]
    cq = _dot(snb, w_ref[:, _C_CQ:_C_CQ + MLA_Q_RANK])
    cqn = _rms_norm(cq, qg_ref[...]).astype(_bf16)
    qft = _dot_nt(wuqt_ref[...], cqn)
    scale = (MLA_QDIM ** -0.5) * LOG2E
    cost = cost_ref[...]
    sint = sint_ref[...]
    half = MLA_ROPE // 2
    for h in range(MLA_HEADS):
        base = h * HEAD_PAD
        x1 = qft[base + MLA_NOPE:base + MLA_NOPE + half]
        x2 = qft[base + MLA_NOPE + half:base + MLA_QDIM]
        qm_ref[0, base:base + MLA_NOPE] = (qft[base:base + MLA_NOPE] * scale).astype(_bf16)
        qm_ref[0, base + MLA_NOPE:base + MLA_NOPE + half] = ((x1 * cost - x2 * sint) * scale).astype(_bf16)
        qm_ref[0, base + MLA_NOPE + half:base + MLA_QDIM] = ((x1 * sint + x2 * cost) * scale).astype(_bf16)
        qm_ref[0, base + MLA_QDIM:base + HEAD_PAD] = jnp.zeros((HEAD_PAD - MLA_QDIM, TILE), _bf16)
    ckv = _dot(snb, w_ref[:, _C_CKV:_C_CKV + MLA_KV_RANK])
    ckvn = _rms_norm(ckv, kvg_ref[...]).astype(_bf16)
    kf = _dot(ckvn, wuk_ref[...])
    kr = _rope(_dot(snb, w_ref[:, _C_KR:_C_KR + LANES]), cos, sl, sh)
    for h in range(MLA_HEADS):
        km_ref[0, :, h * HEAD_PAD:(h + 1) * HEAD_PAD] = (
            kf[:, h * HEAD_PAD:(h + 1) * HEAD_PAD] + kr).astype(_bf16)
    vt = _dot_nt(wuvt_ref[...], ckvn)
    for h in range(MLA_HEADS):
        vm_ref[0, h * V_AUG:h * V_AUG + MLA_DV] = vt[h * MLA_DV:(h + 1) * MLA_DV].astype(_bf16)
        vm_ref[0, h * V_AUG + MLA_DV:(h + 1) * V_AUG] = jnp.ones((V_AUG - MLA_DV, TILE), _bf16)

    ga_ref[0] = _sigmoid(_dot(snb, w_ref[:, _C_GA:_C_GA + D_MODEL])).astype(_bf16)
    gbt_ref[0] = _sigmoid(_dot(snb, w_ref[:, _C_GB:_C_GB + D_MODEL])).astype(_bf16)


def _gla_kernel(qt_ref, kt_ref, ke_ref, gv_ref, dec_ref, sr_ref, ng_ref, o_ref, st_ref):
    t = pl.program_id(1)

    @pl.when(t == 0)
    def _():
        st_ref[...] = jnp.zeros_like(st_ref)

    ri = lax.broadcasted_iota(jnp.int32, (GLA_CHUNK, GLA_CHUNK), 0)
    ci = lax.broadcasted_iota(jnp.int32, (GLA_CHUNK, GLA_CHUNK), 1)
    causal = ri >= ci
    ng = ng_ref[...]
    for c in range(TILE // GLA_CHUNK):
        rows = slice(c * GLA_CHUNK, (c + 1) * GLA_CHUNK)
        for h in range(GLA_HEADS):
            kc = slice(h * GLA_DK, (h + 1) * GLA_DK)
            vc = slice(h * GLA_DV, (h + 1) * GLA_DV)
            q = qt_ref[0, rows, kc]
            k = kt_ref[0, rows, kc]
            e = ke_ref[0, rows, kc]
            v = gv_ref[0, rows, vc]
            st = st_ref[h]
            att = jnp.where(causal, _dot_nt(q, k), 0.0)
            o = _dot(att.astype(_bf16), v) + _dot_nt(q, st.astype(_bf16))
            st_ref[h] = st * dec_ref[0, c:c + 1, kc] + _dot_tn(v, e)
            o = _rms_norm(o, ng) * sr_ref[0, rows, vc].astype(_f32)
            o_ref[0, rows, vc] = o.astype(_bf16)


MLA_PAIR = 2


def _mla_tile_start(j):
    return pl.multiple_of(TILE + j * KV_TILE, TILE)


def _mla_finish(o_ref, acc_ref):
    outs = []
    for h in range(MLA_PAIR):
        a = acc_ref[h]
        outs.append(a[0:MLA_DV] / a[MLA_DV:MLA_DV + 1])
    o_ref[0] = jnp.concatenate(outs, axis=0).T.astype(_bf16)


def _mla_kernel(q_ref, k_ref, v_ref, o_ref, acc_ref, knorm_ref):
    i = pl.program_id(2)
    heads = MLA_PAIR
    ones = jnp.ones((HEAD_PAD, LANES), _bf16)

    @pl.when(i == 0)
    def _():
        for h in range(heads):
            kk = k_ref[0, :, h * HEAD_PAD:(h + 1) * HEAD_PAD].astype(_f32)
            hi = (kk * kk).astype(_bf16)
            knorm_ref[h] = jnp.max(_dot(hi, ones), axis=0, keepdims=True)

    half = slice(KV_TILE, Q_TILE)
    q_t = [q_ref[0, h * HEAD_PAD:(h + 1) * HEAD_PAD, :] for h in range(heads)]
    q_hi = [q_ref[0, h * HEAD_PAD:(h + 1) * HEAD_PAD, half] for h in range(heads)]

    def score_bound(h, q):
        qq = q.astype(_f32)
        qn2 = jnp.sum(qq * qq, axis=0, keepdims=True)
        return jnp.sqrt(qn2 * knorm_ref[h][:, 0:1]) * BOUND_SLACK

    bound = [score_bound(h, q_t[h]) for h in range(heads)]
    bound_hi = [score_bound(h, q_hi[h]) for h in range(heads)]

    def keys(j, h):
        return k_ref[0, pl.ds(_mla_tile_start(j), KV_TILE), h * HEAD_PAD:(h + 1) * HEAD_PAD]

    def weighted(j, h, p):
        return _dot(v_ref[0, h * V_AUG:(h + 1) * V_AUG, pl.ds(_mla_tile_start(j), KV_TILE)], p)

    k_row = lax.broadcasted_iota(jnp.int32, (KV_TILE, Q_TILE), 0)
    q_col = lax.broadcasted_iota(jnp.int32, (KV_TILE, Q_TILE), 1)
    k_row_sq = lax.broadcasted_iota(jnp.int32, (KV_TILE, KV_TILE), 0)
    q_col_sq = lax.broadcasted_iota(jnp.int32, (KV_TILE, KV_TILE), 1)
    s_meta = [_dot(k_ref[0, PADL:TILE, h * HEAD_PAD:(h + 1) * HEAD_PAD], q_t[h]) for h in range(heads)]
    s_lo = [jnp.where(k_row <= q_col, _dot(keys(2 * i, h), q_t[h]), NEG) for h in range(heads)]
    s_hi = [jnp.where(k_row_sq <= q_col_sq, _dot(keys(2 * i + 1, h), q_hi[h]), NEG)
            for h in range(heads)]
    for h in range(heads):
        p_meta = jnp.concatenate([jnp.zeros((PADL, Q_TILE), _bf16),
                                  jnp.exp2(s_meta[h] - bound[h]).astype(_bf16)], axis=0)
        acc_ref[h] = (_dot(v_ref[0, h * V_AUG:(h + 1) * V_AUG, 0:TILE], p_meta)
                      + weighted(2 * i, h, jnp.exp2(s_lo[h] - bound[h]).astype(_bf16)))
        acc_ref[h, :, half] += weighted(2 * i + 1, h, jnp.exp2(s_hi[h] - bound_hi[h]).astype(_bf16))

    def body(j, c):
        s = [_dot(keys(j, h), q_t[h]) for h in range(heads)]
        for h in range(heads):
            acc_ref[h] += weighted(j, h, jnp.exp2(s[h] - bound[h]).astype(_bf16))
        return c

    lax.fori_loop(0, 2 * i, body, 0)
    _mla_finish(o_ref, acc_ref)

    l_min = jnp.minimum(jnp.min(acc_ref[0, MLA_DV:MLA_DV + 1, :]), jnp.min(acc_ref[1, MLA_DV:MLA_DV + 1, :]))

    @pl.when(jnp.logical_not(l_min >= MIN_SOFTMAX_SUM))
    def _():
        _mla_exact(q_ref, k_ref, v_ref, o_ref, acc_ref)


def _mla_exact(q_ref, k_ref, v_ref, o_ref, acc_ref):
    i = pl.program_id(2)
    heads = MLA_PAIR
    q_t = [q_ref[0, h * HEAD_PAD:(h + 1) * HEAD_PAD, :] for h in range(heads)]

    ms = []
    for h in range(heads):
        kb = k_ref[0, PADL:TILE, h * HEAD_PAD:(h + 1) * HEAD_PAD]
        s = _dot(kb, q_t[h])
        m0 = jnp.max(s, axis=0, keepdims=True)
        p = jnp.concatenate([jnp.zeros((PADL, Q_TILE), _bf16), jnp.exp2(s - m0).astype(_bf16)], axis=0)
        acc_ref[h] = _dot(v_ref[0, h * V_AUG:(h + 1) * V_AUG, 0:TILE], p)
        ms.append(m0)

    k_row = lax.broadcasted_iota(jnp.int32, (KV_TILE, Q_TILE), 0)
    q_col = lax.broadcasted_iota(jnp.int32, (KV_TILE, Q_TILE), 1)

    def body(j, ms):
        visible = (j - 2 * i) * KV_TILE + k_row <= q_col
        out = []
        for h in range(heads):
            s = _dot(k_ref[0, pl.ds(_mla_tile_start(j), KV_TILE), h * HEAD_PAD:(h + 1) * HEAD_PAD],
                     q_t[h])
            s = jnp.where(visible, s, NEG)
            vb = v_ref[0, h * V_AUG:(h + 1) * V_AUG, pl.ds(_mla_tile_start(j), KV_TILE)]
            m_new = jnp.maximum(ms[h], jnp.max(s, axis=0, keepdims=True))
            alpha = jnp.exp2(ms[h] - m_new)
            acc_ref[h] = alpha * acc_ref[h] + _dot(vb, jnp.exp2(s - m_new).astype(_bf16))
            out.append(m_new)
        return tuple(out)

    lax.fori_loop(0, 2 * i + 2, body, tuple(ms))
    _mla_finish(o_ref, acc_ref)


def _merge_kernel(og_ref, om_ref, ga_ref, gbt_ref, s_ref, wbg_ref, wbm_ref, wo_ref, lng_ref,
                  lnb_ref, rwh_ref, rwl_ref, rb_ref, tril_ref,
                  s2_ref, info_ref, infot_ref, cnt_ref, carry_ref):
    step = pl.program_id(0)

    @pl.when(step == 0)
    def _():
        carry_ref[...] = jnp.zeros_like(carry_ref)

    ba = _dot(og_ref[...], wbg_ref[...])
    bb = _dot(om_ref[...], wbm_ref[...])
    merged = ga_ref[...].astype(_f32) * ba + gbt_ref[...].astype(_f32) * bb
    y = ALPHA * s_ref[...] + _dot(merged.astype(_bf16), wo_ref[...])
    s2 = _layer_norm(y, lng_ref[...], lnb_ref[...])
    _tt_store(s2_ref, s2)

    xh = s2.astype(_bf16)
    xl = (s2 - xh.astype(_f32)).astype(_bf16)
    logits = _dot(xh, rwh_ref[...]) + _dot(xl, rwh_ref[...]) + _dot(xh, rwl_ref[...]) + rb_ref[...]
    lane = lax.broadcasted_iota(jnp.int32, (MERGE_TILE, LANES), 1)
    is_g = lane < N_GROUPS
    gl = jnp.where(is_g, logits, NEG)
    gmax = jnp.max(gl, axis=-1, keepdims=True)
    gidx = jnp.min(jnp.where(gl == gmax, lane, LANES), axis=-1, keepdims=True)
    p_g = 1.0 / jnp.sum(jnp.where(is_g, jnp.exp(gl - gmax), 0.0), axis=-1, keepdims=True)
    lo = N_GROUPS + EXPERTS_PER_GROUP * gidx
    el = jnp.where((lane >= lo) & (lane < lo + EXPERTS_PER_GROUP), logits, NEG)
    v1 = jnp.max(el, axis=-1, keepdims=True)
    i1 = jnp.min(jnp.where(el == v1, lane, LANES), axis=-1, keepdims=True)
    el2 = jnp.where(lane == i1, NEG, el)
    v2 = jnp.max(el2, axis=-1, keepdims=True)
    i2 = jnp.min(jnp.where(el2 == v2, lane, LANES), axis=-1, keepdims=True)
    tt = jnp.exp(v2 - v1)
    p1 = 1.0 / (1.0 + tt)
    p2 = tt / (1.0 + tt)
    e1 = i1 - N_GROUPS
    e2 = i2 - N_GROUPS
    hit1 = lane == e1
    hit2 = lane == e2
    onehot = jnp.where(hit1 | hit2, 1.0, 0.0)
    before = _dot(tril_ref[...], onehot.astype(_bf16)) + carry_ref[0:1, :]
    r1 = jnp.sum(jnp.where(hit1, before, 0.0), axis=-1, keepdims=True)
    r2 = jnp.sum(jnp.where(hit2, before, 0.0), axis=-1, keepdims=True)
    new_carry = carry_ref[0:1, :] + jnp.sum(onehot, axis=0, keepdims=True)
    carry_ref[...] = jnp.broadcast_to(new_carry, carry_ref.shape)
    cnt_ref[...] = jnp.broadcast_to(new_carry, cnt_ref.shape)
    info = jnp.where(lane == 0, e1.astype(_f32),
           jnp.where(lane == 1, e2.astype(_f32),
           jnp.where(lane == 2, p_g * p1,
           jnp.where(lane == 3, p_g * p2,
           jnp.where(lane == 4, r1,
           jnp.where(lane == 5, r2, 0.0))))))
    info_ref[...] = info
    infot_ref[...] = info.T[0:SUBLANES]


def _dispatch_kernel(pos_ref, zrow_ref, nused_ref, s2_ref, xs_hbm, zero_ref, sems):
    g = pl.program_id(0)
    zero_sem = sems.at[1]
    row_sem = sems.at[0]

    tile_rows = EXPERT_TILE * SUBLANES

    def zero_copy(row):
        start = pl.multiple_of(row * SUBLANES, tile_rows)
        return pltpu.make_async_copy(zero_ref, xs_hbm.at[pl.ds(start, tile_rows)], zero_sem)

    @pl.when(g == 0)
    def _():
        zero_ref[...] = jnp.zeros_like(zero_ref)

        def start(e, c):
            @pl.when(zrow_ref[e] >= 0)
            def _():
                zero_copy(zrow_ref[e]).start()
            return c

        def wait(e, c):
            @pl.when(zrow_ref[e] >= 0)
            def _():
                zero_copy(0).wait()
            return c

        def start_tail(u, c):
            zero_copy(u * EXPERT_TILE).start()
            return c

        def wait_tail(u, c):
            zero_copy(0).wait()
            return c

        n_tiles = xs_hbm.shape[0] // tile_rows
        lax.fori_loop(0, N_EXPERTS, start, 0)
        lax.fori_loop(nused_ref[0], n_tiles, start_tail, 0)
        lax.fori_loop(0, N_EXPERTS, wait, 0)
        lax.fori_loop(nused_ref[0], n_tiles, wait_tail, 0)

    n_tok = pl.num_programs(0) * DISPATCH_TILE

    def issue(r, c):
        tok = g * DISPATCH_TILE + r
        for k in range(2):
            pltpu.make_async_copy(s2_ref.at[_tt_rows(r)],
                                  xs_hbm.at[_tt_rows(pos_ref[k * n_tok + tok])], row_sem).start(priority=k)
        return c

    lax.fori_loop(0, DISPATCH_TILE, issue, 0, unroll=8)
    for k in range(2):
        pltpu.make_async_copy(s2_ref, xs_hbm.at[pl.ds(0, DISPATCH_TILE * SUBLANES)], row_sem).wait()


def _expert_kernel(te_ref, nused_ref, x_ref, wg_ref, wu_ref, wd_ref, o_ref, wgb_ref, wub_ref, wdb_ref):
    u = pl.program_id(0)
    used = u < nused_ref[0]
    first_of_expert = jnp.logical_or(u == 0, te_ref[u] != te_ref[jnp.maximum(u - 1, 0)])

    @pl.when(jnp.logical_and(used, first_of_expert))
    def _():
        wgb_ref[...] = wg_ref[0].astype(_bf16)
        wub_ref[...] = wu_ref[0].astype(_bf16)
        wdb_ref[...] = wd_ref[0].astype(_bf16)

    @pl.when(used)
    def _():
        x = _tt_load(x_ref, EXPERT_TILE).astype(_bf16)
        a = _dot(x, wgb_ref[...])
        up = _dot(x, wub_ref[...])
        hid = a * _sigmoid(a) * up
        _tt_store(o_ref, _dot(hid.astype(_bf16), wdb_ref[...]))

    @pl.when(u >= nused_ref[0])
    def _():
        o_ref[...] = jnp.zeros_like(o_ref)


def _combine_kernel(pos_ref, s2_ref, info_ref, ys_hbm, lng_ref, lnb_ref, o_ref, buf_ref, sems):
    g = pl.program_id(0)
    last = pl.num_programs(0) - 1
    n_tok = pl.num_programs(0) * COMBINE_TILE

    def issue_tile(tile, slot):
        def body(r, c):
            tok = tile * COMBINE_TILE + r
            for k in range(2):
                pltpu.make_async_copy(ys_hbm.at[_tt_rows(pos_ref[k * n_tok + tok])],
                                      buf_ref.at[slot, k, _tt_rows(r)], sems.at[slot]).start(priority=k)
            return c

        lax.fori_loop(0, COMBINE_TILE, body, 0, unroll=8)

    @pl.when(g == 0)
    def _():
        issue_tile(0, 0)

    @pl.when(g < last)
    def _():
        issue_tile(g + 1, (g + 1) % 2)

    slot = g % 2
    for k in range(2):
        pltpu.make_async_copy(ys_hbm.at[pl.ds(0, COMBINE_TILE * SUBLANES)], buf_ref.at[slot, k],
                              sems.at[slot]).wait()
    info = info_ref[...]
    y = (ALPHA * _tt_load(s2_ref, COMBINE_TILE)
         + info[:, 2:3] * _tt_load(buf_ref, COMBINE_TILE, (slot, 0))
         + info[:, 3:4] * _tt_load(buf_ref, COMBINE_TILE, (slot, 1)))
    o_ref[...] = _layer_norm(y, lng_ref[...], lnb_ref[...])


def _const_spec(shape):
    nd = len(shape)
    return pl.BlockSpec(shape, lambda *_: (0,) * nd)


def _rope_tables(lp):
    pos = jnp.maximum(jnp.arange(lp, dtype=_f32) - PADL, 0.0)
    inv_freq = ROPE_BASE ** (-jnp.arange(0, MLA_ROPE, 2, dtype=_f32) / MLA_ROPE)
    ang = pos[:, None] * inv_freq[None, :]
    cos, sin = jnp.cos(ang), jnp.sin(ang)
    half = MLA_ROPE // 2
    ones = jnp.ones((lp, MLA_NOPE), _f32)
    zeros_n = jnp.zeros((lp, MLA_NOPE), _f32)
    zeros_h = jnp.zeros((lp, half), _f32)
    tail1 = jnp.ones((lp, LANES - MLA_QDIM), _f32)
    tail0 = jnp.zeros((lp, LANES - MLA_QDIM), _f32)
    cos_t = jnp.concatenate([ones, cos, cos, tail1], axis=1)
    sin_lo = jnp.concatenate([zeros_n, -sin, zeros_h, tail0], axis=1)
    sin_hi = jnp.concatenate([zeros_n, zeros_h, sin, tail0], axis=1)
    return cos_t, sin_lo, sin_hi, cos.T, sin.T


def _pad_heads(w, width):
    k = w.shape[0]
    w = w.reshape(k, MLA_HEADS, width)
    w = jnp.pad(w, ((0, 0), (0, 0), (0, HEAD_PAD - width)))
    return w.reshape(k, MLA_HEADS * HEAD_PAD)


def kernel(x, meta_tokens, ln_emb_g, ln_emb_b, w_in, gla_gate_w2, gla_gate_b, gla_norm_g, mla_q_norm_g, mla_w_uq, mla_kv_norm_g, mla_w_uk, mla_w_uv, w_branch_gla, w_branch_mla, w_out, ln_mix_g, ln_mix_b, router_group_w, router_group_b, router_expert_w, router_expert_b, expert_w_gate, expert_w_up, expert_w_down, ln_ffn_g, ln_ffn_b):
    bsz, seq, d = x.shape
    assert d == D_MODEL and seq % Q_TILE == 0 and w_in.shape[0] == DEPTH == 1
    lp = PADL + N_META + seq
    nt = lp // TILE
    ntok = bsz * seq
    row2 = lambda v: v.reshape(1, -1).astype(_f32)

    head_tile = jnp.concatenate([jnp.zeros((PADL, d), _f32), meta_tokens.astype(_f32)], axis=0)
    wi = w_in[0]
    o_a = 2 * GLA_QK + 2 * GLA_VW
    o_cq = o_a + GLA_GATE_RANK
    o_ckv = o_cq + MLA_Q_RANK
    o_kr = o_ckv + MLA_KV_RANK
    o_ga = o_kr + MLA_ROPE
    w_a = jnp.pad(wi[:, o_a:o_cq], ((0, 0), (0, LANES - GLA_GATE_RANK)))
    w_kr = jnp.pad(wi[:, o_kr:o_ga], ((0, 0), (MLA_NOPE, LANES - MLA_QDIM)))
    w_all = jnp.concatenate([wi[:, :o_a], w_a, wi[:, o_cq:o_kr], w_kr, wi[:, o_ga:]], axis=1).astype(_bf16)
    assert w_all.shape == (d, _W_COLS)
    w2p = jnp.pad(gla_gate_w2[0], ((0, LANES - GLA_GATE_RANK), (0, 0))).astype(_bf16)
    wuqt = _pad_heads(mla_w_uq[0], MLA_QDIM).T.astype(_bf16)
    wuk = _pad_heads(mla_w_uk[0], MLA_NOPE).astype(_bf16)
    wuvt = mla_w_uv[0].T.astype(_bf16)
    cos_t, sin_lo, sin_hi, cos_tr, sin_tr = _rope_tables(lp)
    blk = np.arange(TILE)
    tril_chunks = jnp.asarray(
        ((blk[:, None] >= blk[None, :]) & (blk[:, None] // GLA_CHUNK == blk[None, :] // GLA_CHUNK)),
        dtype=_bf16)

    pad_map = lambda b, t: (b, t, 0)
    real_map = lambda b, t: (b, jnp.maximum(t - 1, 0), 0)
    tab_spec = pl.BlockSpec((TILE, LANES), lambda b, t: (t, 0))
    tabt_spec = pl.BlockSpec((MLA_ROPE // 2, TILE), lambda b, t: (0, t))
    real_map_t = lambda b, t: (b, 0, jnp.maximum(t - 1, 0))
    out_shapes = (
        jax.ShapeDtypeStruct((bsz, seq, d), _f32),
        jax.ShapeDtypeStruct((bsz, lp, GLA_QK), _bf16),
        jax.ShapeDtypeStruct((bsz, lp, GLA_QK), _bf16),
        jax.ShapeDtypeStruct((bsz, lp, GLA_QK), _bf16),
        jax.ShapeDtypeStruct((bsz, lp, GLA_VW), _bf16),
        jax.ShapeDtypeStruct((bsz * nt, TILE // GLA_CHUNK, GLA_QK), _f32),
        jax.ShapeDtypeStruct((bsz, seq, GLA_VW), _bf16),
        jax.ShapeDtypeStruct((bsz, MLA_HEADS * HEAD_PAD, seq), _bf16),
        jax.ShapeDtypeStruct((bsz, lp, MLA_HEADS * HEAD_PAD), _bf16),
        jax.ShapeDtypeStruct((bsz, MLA_HEADS * V_AUG, lp), _bf16),
        jax.ShapeDtypeStruct((bsz, seq, d), _bf16),
        jax.ShapeDtypeStruct((bsz, seq, d), _bf16),
    )
    out_specs = (
        pl.BlockSpec((1, TILE, d), real_map),
        pl.BlockSpec((1, TILE, GLA_QK), pad_map),
        pl.BlockSpec((1, TILE, GLA_QK), pad_map),
        pl.BlockSpec((1, TILE, GLA_QK), pad_map),
        pl.BlockSpec((1, TILE, GLA_VW), pad_map),
        pl.BlockSpec((1, TILE // GLA_CHUNK, GLA_QK), lambda b, t: (b * nt + t, 0, 0)),
        pl.BlockSpec((1, TILE, GLA_VW), real_map),
        pl.BlockSpec((1, MLA_HEADS * HEAD_PAD, TILE), real_map_t),
        pl.BlockSpec((1, TILE, MLA_HEADS * HEAD_PAD), pad_map),
        pl.BlockSpec((1, MLA_HEADS * V_AUG, TILE), lambda b, t: (b, 0, t)),
        pl.BlockSpec((1, TILE, d), real_map),
        pl.BlockSpec((1, TILE, d), real_map),
    )
    (s_emb, qt, kt, ke, gv, dec, sr, qm, km, vm, gate_a, gate_b) = pl.pallas_call(
        _inproj_kernel,
        grid=(bsz, nt),
        in_specs=[
            pl.BlockSpec((1, TILE, d), real_map),
            _const_spec((TILE, d)),
            _const_spec((1, d)), _const_spec((1, d)),
            _const_spec((d, _W_COLS)),
            _const_spec((LANES, GLA_QK)), _const_spec((1, GLA_QK)),
            _const_spec((1, MLA_Q_RANK)), _const_spec((MLA_HEADS * HEAD_PAD, MLA_Q_RANK)),
            _const_spec((1, MLA_KV_RANK)), _const_spec((MLA_KV_RANK, MLA_HEADS * HEAD_PAD)),
            _const_spec((MLA_HEADS * MLA_DV, MLA_KV_RANK)),
            tab_spec, tab_spec, tab_spec, tabt_spec, tabt_spec,
            _const_spec((TILE, TILE)),
        ],
        out_specs=out_specs,
        out_shape=out_shapes,
        compiler_params=pltpu.CompilerParams(
            dimension_semantics=("arbitrary", "arbitrary"), vmem_limit_bytes=VMEM_LIMIT),
        name="inproj",
    )(x, head_tile, row2(ln_emb_g), row2(ln_emb_b), w_all, w2p, row2(gla_gate_b[0]),
      row2(mla_q_norm_g[0]), wuqt, row2(mla_kv_norm_g[0]), wuk, wuvt, cos_t, sin_lo, sin_hi,
      cos_tr, sin_tr, tril_chunks)

    o_gla = pl.pallas_call(
        _gla_kernel,
        grid=(bsz, nt),
        in_specs=[
            pl.BlockSpec((1, TILE, GLA_QK), pad_map),
            pl.BlockSpec((1, TILE, GLA_QK), pad_map),
            pl.BlockSpec((1, TILE, GLA_QK), pad_map),
            pl.BlockSpec((1, TILE, GLA_VW), pad_map),
            pl.BlockSpec((1, TILE // GLA_CHUNK, GLA_QK), lambda b, t: (b * nt + t, 0, 0)),
            pl.BlockSpec((1, TILE, GLA_VW), real_map),
            _const_spec((1, GLA_DV)),
        ],
        out_specs=pl.BlockSpec((1, TILE, GLA_VW), real_map),
        out_shape=jax.ShapeDtypeStruct((bsz, seq, GLA_VW), _bf16),
        scratch_shapes=[pltpu.VMEM((GLA_HEADS, GLA_DV, GLA_DK), _f32)],
        compiler_params=pltpu.CompilerParams(
            dimension_semantics=("arbitrary", "arbitrary"), vmem_limit_bytes=VMEM_LIMIT),
        name="gla",
    )(qt, kt, ke, gv, dec, sr, row2(gla_norm_g[0]))

    pair = 2 * HEAD_PAD
    o_mla = pl.pallas_call(
        _mla_kernel,
        grid=(bsz, MLA_HEADS // 2, seq // Q_TILE),
        in_specs=[
            pl.BlockSpec((1, pair, Q_TILE), lambda b, hp, i: (b, hp, i)),
            pl.BlockSpec((1, lp, pair), lambda b, hp, i: (b, 0, hp)),
            pl.BlockSpec((1, 2 * V_AUG, lp), lambda b, hp, i: (b, hp, 0)),
        ],
        out_specs=pl.BlockSpec((1, Q_TILE, 2 * MLA_DV), lambda b, hp, i: (b, i, hp)),
        out_shape=jax.ShapeDtypeStruct((bsz, seq, MLA_HEADS * MLA_DV), _bf16),
        scratch_shapes=[pltpu.VMEM((MLA_PAIR, V_AUG, Q_TILE), _f32),
                        pltpu.VMEM((MLA_PAIR, 1, LANES), _f32)],
        compiler_params=pltpu.CompilerParams(
            dimension_semantics=("arbitrary", "arbitrary", "arbitrary"),
            vmem_limit_bytes=VMEM_LIMIT),
        name="mla",
    )(qm, km, vm)

    rw = jnp.concatenate([router_group_w[0], router_expert_w[0]], axis=1)
    rw = jnp.pad(rw, ((0, 0), (0, LANES - rw.shape[1])))
    rwh = rw.astype(_bf16)
    rwl = (rw - rwh.astype(_f32)).astype(_bf16)
    rb = jnp.concatenate([router_group_b[0], router_expert_b[0]])
    rb = jnp.pad(rb, (0, LANES - rb.shape[0])).reshape(1, LANES)
    mi = np.arange(MERGE_TILE)
    tril_strict = jnp.asarray(mi[:, None] > mi[None, :], dtype=_bf16)
    flat = lambda a: a.reshape(ntok, a.shape[-1])
    tok_spec = lambda w: pl.BlockSpec((MERGE_TILE, w), lambda g: (g, 0))
    tt_spec = lambda n, index_map: pl.BlockSpec((n * SUBLANES, LANES), index_map)
    s2, info, info_t, cnt = pl.pallas_call(
        _merge_kernel,
        grid=(ntok // MERGE_TILE,),
        in_specs=[tok_spec(d), tok_spec(d), tok_spec(d), tok_spec(d), tok_spec(d),
                  _const_spec((d, d)), _const_spec((d, d)), _const_spec((d, d)),
                  _const_spec((1, d)), _const_spec((1, d)),
                  _const_spec((d, LANES)), _const_spec((d, LANES)), _const_spec((1, LANES)),
                  _const_spec((MERGE_TILE, MERGE_TILE))],
        out_specs=(tt_spec(MERGE_TILE, lambda g: (g, 0)), tok_spec(LANES),
                   pl.BlockSpec((SUBLANES, MERGE_TILE), lambda g: (0, g)), _const_spec((8, LANES))),
        out_shape=(jax.ShapeDtypeStruct((ntok * SUBLANES, LANES), _f32),
                   jax.ShapeDtypeStruct((ntok, LANES), _f32),
                   jax.ShapeDtypeStruct((SUBLANES, ntok), _f32),
                   jax.ShapeDtypeStruct((8, LANES), _f32)),
        scratch_shapes=[pltpu.VMEM((8, LANES), _f32)],
        compiler_params=pltpu.CompilerParams(
            dimension_semantics=("arbitrary",), vmem_limit_bytes=VMEM_LIMIT),
        name="merge_router",
    )(flat(o_gla), flat(o_mla), flat(gate_a), flat(gate_b), flat(s_emb),
      w_branch_gla[0].astype(_bf16), w_branch_mla[0].astype(_bf16), w_out[0].astype(_bf16),
      row2(ln_mix_g[0]), row2(ln_mix_b[0]), rwh, rwl, rb, tril_strict)

    n_tiles = (2 * ntok + N_EXPERTS * (EXPERT_TILE - 1)) // EXPERT_TILE
    n_rows = n_tiles * EXPERT_TILE
    e_idx = info_t[0:2].astype(jnp.int32)
    rank = info_t[4:6].astype(jnp.int32)
    counts = cnt[0, :N_EXPERTS].astype(jnp.int32)
    padded = ((counts + EXPERT_TILE - 1) // EXPERT_TILE) * EXPERT_TILE
    ends = jnp.cumsum(padded)
    starts = ends - padded
    pos = (starts[e_idx] + rank).reshape(-1)
    tile_start = jnp.arange(n_tiles, dtype=jnp.int32) * EXPERT_TILE
    tile_expert = jnp.minimum(
        jnp.sum((ends[None, :] <= tile_start[:, None]).astype(jnp.int32), axis=1), N_EXPERTS - 1)
    n_used = (ends[-1:] // EXPERT_TILE).astype(jnp.int32)
    zero_row = jnp.where(padded > 0, ends - EXPERT_TILE, -1).astype(jnp.int32)

    any_spec = pl.BlockSpec(memory_space=pl.ANY)
    xs = pl.pallas_call(
        _dispatch_kernel,
        grid_spec=pltpu.PrefetchScalarGridSpec(
            num_scalar_prefetch=3,
            grid=(ntok // DISPATCH_TILE,),
            in_specs=[tt_spec(DISPATCH_TILE, lambda g, p, z, nu: (g, 0))],
            out_specs=any_spec,
            scratch_shapes=[pltpu.VMEM((EXPERT_TILE * SUBLANES, LANES), _f32),
                            pltpu.SemaphoreType.DMA((2,))],
        ),
        out_shape=jax.ShapeDtypeStruct((n_rows * SUBLANES, LANES), _f32),
        compiler_params=pltpu.CompilerParams(
            dimension_semantics=("arbitrary",), vmem_limit_bytes=VMEM_LIMIT, has_side_effects=True),
        name="dispatch",
    )(pos, zero_row, n_used, s2)

    ff = EXPERT_FF
    wg = expert_w_gate[0].reshape(N_EXPERTS, d, ff)
    wu = expert_w_up[0].reshape(N_EXPERTS, d, ff)
    wd = expert_w_down[0].reshape(N_EXPERTS, ff, d)
    ys = pl.pallas_call(
        _expert_kernel,
        grid_spec=pltpu.PrefetchScalarGridSpec(
            num_scalar_prefetch=2,
            grid=(n_tiles,),
            in_specs=[
                tt_spec(EXPERT_TILE, lambda u, te, nu: (jnp.minimum(u, nu[0] - 1), 0)),
                pl.BlockSpec((1, d, ff), lambda u, te, nu: (te[u], 0, 0)),
                pl.BlockSpec((1, d, ff), lambda u, te, nu: (te[u], 0, 0)),
                pl.BlockSpec((1, ff, d), lambda u, te, nu: (te[u], 0, 0)),
            ],
            out_specs=tt_spec(EXPERT_TILE, lambda u, te, nu: (u, 0)),
            scratch_shapes=[pltpu.VMEM((d, ff), _bf16), pltpu.VMEM((d, ff), _bf16),
                            pltpu.VMEM((ff, d), _bf16)],
        ),
        out_shape=jax.ShapeDtypeStruct((n_rows * SUBLANES, LANES), _f32),
        compiler_params=pltpu.CompilerParams(
            dimension_semantics=("arbitrary",), vmem_limit_bytes=VMEM_LIMIT),
        name="experts",
    )(tile_expert, n_used, xs, wg, wu, wd)

    out = pl.pallas_call(
        _combine_kernel,
        grid_spec=pltpu.PrefetchScalarGridSpec(
            num_scalar_prefetch=1,
            grid=(ntok // COMBINE_TILE,),
            in_specs=[
                tt_spec(COMBINE_TILE, lambda g, p: (g, 0)),
                pl.BlockSpec((COMBINE_TILE, LANES), lambda g, p: (g, 0)),
                any_spec,
                pl.BlockSpec((1, d), lambda g, p: (0, 0)),
                pl.BlockSpec((1, d), lambda g, p: (0, 0)),
            ],
            out_specs=pl.BlockSpec((COMBINE_TILE, d), lambda g, p: (g, 0)),
            scratch_shapes=[pltpu.VMEM((2, 2, COMBINE_TILE * SUBLANES, LANES), _f32),
                            pltpu.SemaphoreType.DMA((2,))],
        ),
        out_shape=jax.ShapeDtypeStruct((ntok, d), _f32),
        compiler_params=pltpu.CompilerParams(
            dimension_semantics=("arbitrary",), vmem_limit_bytes=VMEM_LIMIT),
        name="combine_ln",
    )(pos, s2, info, ys, row2(ln_ffn_g[0]), row2(ln_ffn_b[0]))
    return out.reshape(bsz, seq, d)
```

```python
import jax
import jax.numpy as jnp
import numpy as np
from jax import lax
from jax.experimental import pallas as pl
from jax.experimental.pallas import tpu as pltpu

D_MODEL = 1024
N_META = 16
GLA_HEADS = 4
GLA_DK = 128
GLA_DV = 256
GLA_QK = GLA_HEADS * GLA_DK
GLA_VW = GLA_HEADS * GLA_DV
GLA_GATE_RANK = 16
GLA_GATE_TAU = 16.0
GLA_CHUNK = 64
MLA_HEADS = 16
MLA_Q_RANK = 384
MLA_KV_RANK = 256
MLA_NOPE = 64
MLA_ROPE = 32
MLA_DV = 64
MLA_QDIM = MLA_NOPE + MLA_ROPE
ROPE_BASE = 10000.0
N_GROUPS = 4
EXPERTS_PER_GROUP = 8
N_EXPERTS = N_GROUPS * EXPERTS_PER_GROUP
EXPERT_FF = 256
DEPTH = 1
ALPHA = (2.0 * DEPTH) ** 0.25
LN_EPS = 1e-5
RMS_EPS = 1e-6

LANES = 128
SUBLANES = 8
MXU_DIM = 256
TILE = 256
PADL = TILE - N_META
HEAD_PAD = LANES
BF16_ROWS = 16
KV_TILE = 512
Q_TILE = 2 * KV_TILE
V_AUG = MLA_DV + BF16_ROWS
MERGE_TILE = 512
ROUTE_BLOCK = 256
EXPERT_TILE = 256
DISPATCH_TILE = 512
COMBINE_TILE = 256
NEG = -1e30
LOG2E = 1.4426950408889634
BOUND_SLACK = 1.02
MIN_SOFTMAX_SUM = 2.0 ** -100
VMEM_LIMIT = 56 * 1024 * 1024

_C_Q, _C_K, _C_V, _C_R = 0, 512, 1024, 2048
_C_A = 3072
_C_CQ = _C_A + LANES
_C_CKV = _C_CQ + MLA_Q_RANK
_C_KR = _C_CKV + MLA_KV_RANK
_C_GA = _C_KR + LANES
_C_GB = _C_GA + D_MODEL
_W_COLS = _C_GB + D_MODEL

_f32 = jnp.float32
_bf16 = jnp.bfloat16


def _dot(a, b):
    return jnp.dot(a, b, preferred_element_type=_f32)


def _dot_nt(a, b):
    return lax.dot_general(a, b, (((1,), (1,)), ((), ())), preferred_element_type=_f32)


def _dot_tn(a, b):
    return lax.dot_general(a, b, (((0,), (0,)), ((), ())), preferred_element_type=_f32)


def _layer_norm(x, g, b):
    mu = jnp.mean(x, axis=-1, keepdims=True)
    xc = x - mu
    var = jnp.mean(xc * xc, axis=-1, keepdims=True)
    return xc * lax.rsqrt(var + LN_EPS) * g + b


def _rms_norm(x, g):
    ms = jnp.mean(x * x, axis=-1, keepdims=True)
    return x * lax.rsqrt(ms + RMS_EPS) * g


def _sigmoid(x):
    return 1.0 / (1.0 + jnp.exp(-x))


def _tt_load(ref, n, lead=()):
    return jnp.concatenate(
        [ref[lead + (pl.ds(a, n, stride=SUBLANES), slice(None))] for a in range(SUBLANES)], axis=1)


def _tt_store(ref, x):
    n = x.shape[0]
    for a in range(SUBLANES):
        ref[pl.ds(a, n, stride=SUBLANES), :] = x[:, a * LANES:(a + 1) * LANES]


def _tt_rows(tok):
    return pl.ds(pl.multiple_of(tok * SUBLANES, SUBLANES), SUBLANES)


def _rope(x, cos, sin_lo, sin_hi):
    half = MLA_ROPE // 2
    from_hi = pltpu.roll(x, LANES - half, 1)
    from_lo = pltpu.roll(x, half, 1)
    return x * cos + from_hi * sin_lo + from_lo * sin_hi


def _inproj_kernel(x_ref, head_ref, lng_ref, lnb_ref, w_ref, w2_ref, gb_ref, qg_ref, wuqt_ref, kvg_ref,
                   wuk_ref, wuvt_ref, cos_ref, sl_ref, sh_ref, cost_ref, sint_ref, tril_ref,
                   s_ref, qt_ref, kt_ref, ke_ref, gv_ref, dec_ref, sr_ref, qm_ref, km_ref,
                   vm_ref, ga_ref, gbt_ref):
    t = pl.program_id(1)
    x_in = jnp.where(t == 0, head_ref[...], x_ref[0])
    sn = _layer_norm(x_in, lng_ref[...], lnb_ref[...])
    s_ref[0] = sn
    snb = sn.astype(_bf16)
    row = t * TILE + lax.broadcasted_iota(jnp.int32, (TILE, 1), 0)
    valid = row >= PADL

    proj = lambda c0, width: _dot(snb, w_ref[:, c0:c0 + width])
    a_lr = proj(_C_A, LANES)
    cq = proj(_C_CQ, MLA_Q_RANK)
    ckv = proj(_C_CKV, MLA_KV_RANK)
    kr_raw = proj(_C_KR, LANES)
    gq = proj(_C_Q, GLA_QK)
    gk = proj(_C_K, GLA_QK)
    gv_ref[0] = jnp.where(valid, proj(_C_V, GLA_VW), 0.0).astype(_bf16)
    r = proj(_C_R, GLA_VW)
    sr_ref[0] = (r * _sigmoid(r)).astype(_bf16)
    ga_ref[0] = _sigmoid(proj(_C_GA, D_MODEL)).astype(_bf16)
    gbt_ref[0] = _sigmoid(proj(_C_GB, D_MODEL)).astype(_bf16)

    z = _dot(a_lr.astype(_bf16), w2_ref[...]) + gb_ref[...]
    cqn = _rms_norm(cq, qg_ref[...]).astype(_bf16)
    ckvn = _rms_norm(ckv, kvg_ref[...]).astype(_bf16)
    qft = _dot_nt(wuqt_ref[...], cqn)
    kf = _dot(ckvn, wuk_ref[...])
    vt = _dot_nt(wuvt_ref[...], ckvn)

    la = (jnp.minimum(z, 0.0) - jnp.log1p(jnp.exp(-jnp.abs(z)))) * (1.0 / GLA_GATE_TAU)
    la = jnp.where(valid, la, 0.0)
    hi = la.astype(_bf16)
    r1 = la - hi.astype(_f32)
    mid = r1.astype(_bf16)
    lo = (r1 - mid.astype(_f32)).astype(_bf16)
    tril = tril_ref[...]
    bc = _dot(tril, hi) + _dot(tril, mid) + _dot(tril, lo)
    n_chunks = TILE // GLA_CHUNK
    lasts = [bc[c * GLA_CHUNK + GLA_CHUNK - 1:(c + 1) * GLA_CHUNK, :] for c in range(n_chunks)]
    for c in range(n_chunks):
        dec_ref[0, c:c + 1, :] = jnp.exp(lasts[c])
    b_last = jnp.concatenate(
        [jnp.broadcast_to(l, (GLA_CHUNK, GLA_QK)) for l in lasts], axis=0)
    gk = jnp.where(valid, gk, 0.0)
    qt_ref[0] = (gq * (GLA_DK ** -0.5) * jnp.exp(bc)).astype(_bf16)
    kt_ref[0] = (gk * jnp.exp(-bc)).astype(_bf16)
    ke_ref[0] = (gk * jnp.exp(b_last - bc)).astype(_bf16)

    cos = cos_ref[...]
    sl = sl_ref[...]
    sh = sh_ref[...]
    scale = (MLA_QDIM ** -0.5) * LOG2E
    cost = cost_ref[...]
    sint = sint_ref[...]
    half = MLA_ROPE // 2
    for h in range(MLA_HEADS):
        base = h * HEAD_PAD
        x1 = qft[base + MLA_NOPE:base + MLA_NOPE + half]
        x2 = qft[base + MLA_NOPE + half:base + MLA_QDIM]
        qm_ref[0, base:base + MLA_NOPE] = (qft[base:base + MLA_NOPE] * scale).astype(_bf16)
        qm_ref[0, base + MLA_NOPE:base + MLA_NOPE + half] = ((x1 * cost - x2 * sint) * scale).astype(_bf16)
        qm_ref[0, base + MLA_NOPE + half:base + MLA_QDIM] = ((x1 * sint + x2 * cost) * scale).astype(_bf16)
        qm_ref[0, base + MLA_QDIM:base + HEAD_PAD] = jnp.zeros((HEAD_PAD - MLA_QDIM, TILE), _bf16)
    kr = _rope(kr_raw, cos, sl, sh)
    for h in range(MLA_HEADS):
        km_ref[0, :, h * HEAD_PAD:(h + 1) * HEAD_PAD] = (
            kf[:, h * HEAD_PAD:(h + 1) * HEAD_PAD] + kr).astype(_bf16)
    for h in range(MLA_HEADS):
        vm_ref[0, h * V_AUG:h * V_AUG + MLA_DV] = vt[h * MLA_DV:(h + 1) * MLA_DV].astype(_bf16)
        vm_ref[0, h * V_AUG + MLA_DV:(h + 1) * V_AUG] = jnp.ones((V_AUG - MLA_DV, TILE), _bf16)


def _gla_kernel(qt_ref, kt_ref, ke_ref, gv_ref, dec_ref, sr_ref, ng_ref, o_ref, st_ref):
    t = pl.program_id(1)

    @pl.when(t == 0)
    def _():
        st_ref[...] = jnp.zeros_like(st_ref)

    ri = lax.broadcasted_iota(jnp.int32, (GLA_CHUNK, GLA_CHUNK), 0)
    ci = lax.broadcasted_iota(jnp.int32, (GLA_CHUNK, GLA_CHUNK), 1)
    causal = ri >= ci
    ng = ng_ref[...]
    for c in range(TILE // GLA_CHUNK):
        rows = slice(c * GLA_CHUNK, (c + 1) * GLA_CHUNK)
        for h in range(GLA_HEADS):
            kc = slice(h * GLA_DK, (h + 1) * GLA_DK)
            vc = slice(h * GLA_DV, (h + 1) * GLA_DV)
            q = qt_ref[0, rows, kc]
            k = kt_ref[0, rows, kc]
            e = ke_ref[0, rows, kc]
            v = gv_ref[0, rows, vc]
            st = st_ref[h]
            att = jnp.where(causal, _dot_nt(q, k), 0.0)
            o = _dot(att.astype(_bf16), v) + _dot_nt(q, st.astype(_bf16))
            st_ref[h] = st * dec_ref[0, c:c + 1, kc] + _dot_tn(v, e)
            o = _rms_norm(o, ng) * sr_ref[0, rows, vc].astype(_f32)
            o_ref[0, rows, vc] = o.astype(_bf16)


MLA_PAIR = 2


def _mla_tile_start(j):
    return pl.multiple_of(TILE + j * KV_TILE, TILE)


def _mla_finish(o_ref, acc_ref):
    outs = []
    for h in range(MLA_PAIR):
        a = acc_ref[h]
        outs.append(a[0:MLA_DV] / a[MLA_DV:MLA_DV + 1])
    o_ref[0] = jnp.concatenate(outs, axis=0).T.astype(_bf16)


def _mla_kernel(q_ref, k_ref, v_ref, o_ref, acc_ref, knorm_ref):
    i = pl.program_id(2)
    heads = MLA_PAIR
    ones = jnp.ones((HEAD_PAD, LANES), _bf16)

    @pl.when(i == 0)
    def _():
        for h in range(heads):
            kk = k_ref[0, :, h * HEAD_PAD:(h + 1) * HEAD_PAD].astype(_f32)
            hi = (kk * kk).astype(_bf16)
            knorm_ref[h] = jnp.max(_dot(hi, ones), axis=0, keepdims=True)

    half = slice(KV_TILE, Q_TILE)
    q_t = [q_ref[0, h * HEAD_PAD:(h + 1) * HEAD_PAD, :] for h in range(heads)]
    q_hi = [q_ref[0, h * HEAD_PAD:(h + 1) * HEAD_PAD, half] for h in range(heads)]

    def score_bound(h, q):
        qq = q.astype(_f32)
        qn2 = jnp.sum(qq * qq, axis=0, keepdims=True)
        return jnp.sqrt(qn2 * knorm_ref[h][:, 0:1]) * BOUND_SLACK

    bound = [score_bound(h, q_t[h]) for h in range(heads)]
    bound_hi = [score_bound(h, q_hi[h]) for h in range(heads)]

    def keys(j, h):
        return k_ref[0, pl.ds(_mla_tile_start(j), KV_TILE), h * HEAD_PAD:(h + 1) * HEAD_PAD]

    def weighted(j, h, p):
        return _dot(v_ref[0, h * V_AUG:(h + 1) * V_AUG, pl.ds(_mla_tile_start(j), KV_TILE)], p)

    k_row = lax.broadcasted_iota(jnp.int32, (KV_TILE, Q_TILE), 0)
    q_col = lax.broadcasted_iota(jnp.int32, (KV_TILE, Q_TILE), 1)
    k_row_sq = lax.broadcasted_iota(jnp.int32, (KV_TILE, KV_TILE), 0)
    q_col_sq = lax.broadcasted_iota(jnp.int32, (KV_TILE, KV_TILE), 1)
    s_meta = [_dot(k_ref[0, PADL:TILE, h * HEAD_PAD:(h + 1) * HEAD_PAD], q_t[h]) for h in range(heads)]
    s_lo = [jnp.where(k_row <= q_col, _dot(keys(2 * i, h), q_t[h]), NEG) for h in range(heads)]
    s_hi = [jnp.where(k_row_sq <= q_col_sq, _dot(keys(2 * i + 1, h), q_hi[h]), NEG)
            for h in range(heads)]
    for h in range(heads):
        p_meta = jnp.concatenate([jnp.zeros((PADL, Q_TILE), _bf16),
                                  jnp.exp2(s_meta[h] - bound[h]).astype(_bf16)], axis=0)
        acc_ref[h] = (_dot(v_ref[0, h * V_AUG:(h + 1) * V_AUG, 0:TILE], p_meta)
                      + weighted(2 * i, h, jnp.exp2(s_lo[h] - bound[h]).astype(_bf16)))
        acc_ref[h, :, half] += weighted(2 * i + 1, h, jnp.exp2(s_hi[h] - bound_hi[h]).astype(_bf16))

    def body(j, c):
        s = [_dot(keys(j, h), q_t[h]) for h in range(heads)]
        for h in range(heads):
            acc_ref[h] += weighted(j, h, jnp.exp2(s[h] - bound[h]).astype(_bf16))
        return c

    lax.fori_loop(0, 2 * i, body, 0)
    _mla_finish(o_ref, acc_ref)

    l_min = jnp.minimum(jnp.min(acc_ref[0, MLA_DV:MLA_DV + 1, :]), jnp.min(acc_ref[1, MLA_DV:MLA_DV + 1, :]))

    @pl.when(jnp.logical_not(l_min >= MIN_SOFTMAX_SUM))
    def _():
        _mla_exact(q_ref, k_ref, v_ref, o_ref, acc_ref)


def _mla_exact(q_ref, k_ref, v_ref, o_ref, acc_ref):
    i = pl.program_id(2)
    heads = MLA_PAIR
    q_t = [q_ref[0, h * HEAD_PAD:(h + 1) * HEAD_PAD, :] for h in range(heads)]

    ms = []
    for h in range(heads):
        kb = k_ref[0, PADL:TILE, h * HEAD_PAD:(h + 1) * HEAD_PAD]
        s = _dot(kb, q_t[h])
        m0 = jnp.max(s, axis=0, keepdims=True)
        p = jnp.concatenate([jnp.zeros((PADL, Q_TILE), _bf16), jnp.exp2(s - m0).astype(_bf16)], axis=0)
        acc_ref[h] = _dot(v_ref[0, h * V_AUG:(h + 1) * V_AUG, 0:TILE], p)
        ms.append(m0)

    k_row = lax.broadcasted_iota(jnp.int32, (KV_TILE, Q_TILE), 0)
    q_col = lax.broadcasted_iota(jnp.int32, (KV_TILE, Q_TILE), 1)

    def body(j, ms):
        visible = (j - 2 * i) * KV_TILE + k_row <= q_col
        out = []
        for h in range(heads):
            s = _dot(k_ref[0, pl.ds(_mla_tile_start(j), KV_TILE), h * HEAD_PAD:(h + 1) * HEAD_PAD],
                     q_t[h])
            s = jnp.where(visible, s, NEG)
            vb = v_ref[0, h * V_AUG:(h + 1) * V_AUG, pl.ds(_mla_tile_start(j), KV_TILE)]
            m_new = jnp.maximum(ms[h], jnp.max(s, axis=0, keepdims=True))
            alpha = jnp.exp2(ms[h] - m_new)
            acc_ref[h] = alpha * acc_ref[h] + _dot(vb, jnp.exp2(s - m_new).astype(_bf16))
            out.append(m_new)
        return tuple(out)

    lax.fori_loop(0, 2 * i + 2, body, tuple(ms))
    _mla_finish(o_ref, acc_ref)


def _merge_kernel(og_ref, om_ref, ga_ref, gbt_ref, s_ref, wbg_ref, wbm_ref, wo_ref, lng_ref,
                  lnb_ref, rwh_ref, rwl_ref, rb_ref, tril_ref,
                  s2_ref, info_ref, infot_ref, cnt_ref, carry_ref):
    step = pl.program_id(0)

    @pl.when(step == 0)
    def _():
        carry_ref[...] = jnp.zeros_like(carry_ref)

    n_blk = MERGE_TILE // ROUTE_BLOCK
    blocks = [slice(i * ROUTE_BLOCK, (i + 1) * ROUTE_BLOCK) for i in range(n_blk)]
    d = wo_ref.shape[0]
    col_blocks = [slice(c, c + MXU_DIM) for c in range(0, d, MXU_DIM)]
    merged = []
    for rows in blocks:
        merged.append(jnp.concatenate(
            [(ga_ref[rows, cols].astype(_f32) * _dot(og_ref[rows, :], wbg_ref[:, cols])
              + gbt_ref[rows, cols].astype(_f32) * _dot(om_ref[rows, :], wbm_ref[:, cols])
              ).astype(_bf16) for cols in col_blocks], axis=1))
    s2 = []
    for i, rows in enumerate(blocks):
        y = ALPHA * s_ref[rows, :] + _dot(merged[i], wo_ref[...])
        s2.append(_layer_norm(y, lng_ref[...], lnb_ref[...]))
    logits = []
    for i in range(n_blk):
        xh = s2[i].astype(_bf16)
        xl = (s2[i] - xh.astype(_f32)).astype(_bf16)
        logits.append(_dot(xh, rwh_ref[...]) + _dot(xl, rwh_ref[...]) + _dot(xh, rwl_ref[...])
                      + rb_ref[...])
    _tt_store(s2_ref, jnp.concatenate(s2, axis=0))

    lane = lax.broadcasted_iota(jnp.int32, (ROUTE_BLOCK, LANES), 1)
    is_g = lane < N_GROUPS
    carry = carry_ref[0:1, :]
    infos = []
    for i in range(n_blk):
        gl = jnp.where(is_g, logits[i], NEG)
        gmax = jnp.max(gl, axis=-1, keepdims=True)
        gidx = jnp.min(jnp.where(gl == gmax, lane, LANES), axis=-1, keepdims=True)
        p_g = 1.0 / jnp.sum(jnp.where(is_g, jnp.exp(gl - gmax), 0.0), axis=-1, keepdims=True)
        lo = N_GROUPS + EXPERTS_PER_GROUP * gidx
        el = jnp.where((lane >= lo) & (lane < lo + EXPERTS_PER_GROUP), logits[i], NEG)
        v1 = jnp.max(el, axis=-1, keepdims=True)
        i1 = jnp.min(jnp.where(el == v1, lane, LANES), axis=-1, keepdims=True)
        el2 = jnp.where(lane == i1, NEG, el)
        v2 = jnp.max(el2, axis=-1, keepdims=True)
        i2 = jnp.min(jnp.where(el2 == v2, lane, LANES), axis=-1, keepdims=True)
        tt = jnp.exp(v2 - v1)
        p1 = 1.0 / (1.0 + tt)
        p2 = tt / (1.0 + tt)
        e1 = i1 - N_GROUPS
        e2 = i2 - N_GROUPS
        hit1 = lane == e1
        hit2 = lane == e2
        onehot = jnp.where(hit1 | hit2, 1.0, 0.0)
        before = _dot(tril_ref[...], onehot.astype(_bf16)) + carry
        r1 = jnp.sum(jnp.where(hit1, before, 0.0), axis=-1, keepdims=True)
        r2 = jnp.sum(jnp.where(hit2, before, 0.0), axis=-1, keepdims=True)
        carry = carry + jnp.sum(onehot, axis=0, keepdims=True)
        infos.append(jnp.where(lane == 0, e1.astype(_f32),
                     jnp.where(lane == 1, e2.astype(_f32),
                     jnp.where(lane == 2, p_g * p1,
                     jnp.where(lane == 3, p_g * p2,
                     jnp.where(lane == 4, r1,
                     jnp.where(lane == 5, r2, 0.0)))))))
    carry_ref[...] = jnp.broadcast_to(carry, carry_ref.shape)
    cnt_ref[...] = jnp.broadcast_to(carry, cnt_ref.shape)
    info = jnp.concatenate(infos, axis=0)
    info_ref[...] = info
    infot_ref[...] = info.T[0:SUBLANES]


def _dispatch_kernel(pos_ref, zrow_ref, nused_ref, s2_ref, xs_hbm, zero_ref, sems):
    g = pl.program_id(0)
    zero_sem = sems.at[1]
    row_sem = sems.at[0]

    tile_rows = EXPERT_TILE * SUBLANES

    def zero_copy(row):
        start = pl.multiple_of(row * SUBLANES, tile_rows)
        return pltpu.make_async_copy(zero_ref, xs_hbm.at[pl.ds(start, tile_rows)], zero_sem)

    @pl.when(g == 0)
    def _():
        zero_ref[...] = jnp.zeros_like(zero_ref)

        def start(e, c):
            @pl.when(zrow_ref[e] >= 0)
            def _():
                zero_copy(zrow_ref[e]).start()
            return c

        def wait(e, c):
            @pl.when(zrow_ref[e] >= 0)
            def _():
                zero_copy(0).wait()
            return c

        def start_tail(u, c):
            zero_copy(u * EXPERT_TILE).start()
            return c

        def wait_tail(u, c):
            zero_copy(0).wait()
            return c

        n_tiles = xs_hbm.shape[0] // tile_rows
        lax.fori_loop(0, N_EXPERTS, start, 0)
        lax.fori_loop(nused_ref[0], n_tiles, start_tail, 0)
        lax.fori_loop(0, N_EXPERTS, wait, 0)
        lax.fori_loop(nused_ref[0], n_tiles, wait_tail, 0)

    n_tok = pl.num_programs(0) * DISPATCH_TILE

    def issue(r, c):
        tok = g * DISPATCH_TILE + r
        for k in range(2):
            pltpu.make_async_copy(s2_ref.at[_tt_rows(r)],
                                  xs_hbm.at[_tt_rows(pos_ref[k * n_tok + tok])], row_sem).start(priority=k)
        return c

    lax.fori_loop(0, DISPATCH_TILE, issue, 0, unroll=8)
    for k in range(2):
        pltpu.make_async_copy(s2_ref, xs_hbm.at[pl.ds(0, DISPATCH_TILE * SUBLANES)], row_sem).wait()


def _expert_kernel(te_ref, nused_ref, x_ref, wg_ref, wu_ref, wd_ref, o_ref, wgb_ref, wub_ref, wdb_ref):
    u = pl.program_id(0)
    used = u < nused_ref[0]
    first_of_expert = jnp.logical_or(u == 0, te_ref[u] != te_ref[jnp.maximum(u - 1, 0)])

    @pl.when(jnp.logical_and(used, first_of_expert))
    def _():
        wgb_ref[...] = wg_ref[0].astype(_bf16)
        wub_ref[...] = wu_ref[0].astype(_bf16)
        wdb_ref[...] = wd_ref[0].astype(_bf16)

    @pl.when(used)
    def _():
        x = _tt_load(x_ref, EXPERT_TILE).astype(_bf16)
        a = _dot(x, wgb_ref[...])
        up = _dot(x, wub_ref[...])
        hid = a * _sigmoid(a) * up
        _tt_store(o_ref, _dot(hid.astype(_bf16), wdb_ref[...]))

    @pl.when(u >= nused_ref[0])
    def _():
        o_ref[...] = jnp.zeros_like(o_ref)


def _combine_kernel(pos_ref, s2_ref, info_ref, ys_hbm, lng_ref, lnb_ref, o_ref, buf_ref, sems):
    g = pl.program_id(0)
    last = pl.num_programs(0) - 1
    n_tok = pl.num_programs(0) * COMBINE_TILE

    def issue_tile(tile, slot):
        def body(r, c):
            tok = tile * COMBINE_TILE + r
            for k in range(2):
                pltpu.make_async_copy(ys_hbm.at[_tt_rows(pos_ref[k * n_tok + tok])],
                                      buf_ref.at[slot, k, _tt_rows(r)], sems.at[slot]).start(priority=k)
            return c

        lax.fori_loop(0, COMBINE_TILE, body, 0, unroll=8)

    @pl.when(g == 0)
    def _():
        issue_tile(0, 0)

    @pl.when(g < last)
    def _():
        issue_tile(g + 1, (g + 1) % 2)

    slot = g % 2
    for k in range(2):
        pltpu.make_async_copy(ys_hbm.at[pl.ds(0, COMBINE_TILE * SUBLANES)], buf_ref.at[slot, k],
                              sems.at[slot]).wait()
    info = info_ref[...]
    y = (ALPHA * _tt_load(s2_ref, COMBINE_TILE)
         + info[:, 2:3] * _tt_load(buf_ref, COMBINE_TILE, (slot, 0))
         + info[:, 3:4] * _tt_load(buf_ref, COMBINE_TILE, (slot, 1)))
    o_ref[...] = _layer_norm(y, lng_ref[...], lnb_ref[...])


def _const_spec(shape):
    nd = len(shape)
    return pl.BlockSpec(shape, lambda *_: (0,) * nd)


def _rope_tables(lp):
    pos = jnp.maximum(jnp.arange(lp, dtype=_f32) - PADL, 0.0)
    inv_freq = ROPE_BASE ** (-jnp.arange(0, MLA_ROPE, 2, dtype=_f32) / MLA_ROPE)
    ang = pos[:, None] * inv_freq[None, :]
    cos, sin = jnp.cos(ang), jnp.sin(ang)
    half = MLA_ROPE // 2
    ones = jnp.ones((lp, MLA_NOPE), _f32)
    zeros_n = jnp.zeros((lp, MLA_NOPE), _f32)
    zeros_h = jnp.zeros((lp, half), _f32)
    tail1 = jnp.ones((lp, LANES - MLA_QDIM), _f32)
    tail0 = jnp.zeros((lp, LANES - MLA_QDIM), _f32)
    cos_t = jnp.concatenate([ones, cos, cos, tail1], axis=1)
    sin_lo = jnp.concatenate([zeros_n, -sin, zeros_h, tail0], axis=1)
    sin_hi = jnp.concatenate([zeros_n, zeros_h, sin, tail0], axis=1)
    return cos_t, sin_lo, sin_hi, cos.T, sin.T


def _pad_heads(w, width):
    k = w.shape[0]
    w = w.reshape(k, MLA_HEADS, width)
    w = jnp.pad(w, ((0, 0), (0, 0), (0, HEAD_PAD - width)))
    return w.reshape(k, MLA_HEADS * HEAD_PAD)


def kernel(x, meta_tokens, ln_emb_g, ln_emb_b, w_in, gla_gate_w2, gla_gate_b, gla_norm_g, mla_q_norm_g, mla_w_uq, mla_kv_norm_g, mla_w_uk, mla_w_uv, w_branch_gla, w_branch_mla, w_out, ln_mix_g, ln_mix_b, router_group_w, router_group_b, router_expert_w, router_expert_b, expert_w_gate, expert_w_up, expert_w_down, ln_ffn_g, ln_ffn_b):
    bsz, seq, d = x.shape
    assert d == D_MODEL and seq % Q_TILE == 0 and w_in.shape[0] == DEPTH == 1
    lp = PADL + N_META + seq
    nt = lp // TILE
    ntok = bsz * seq
    row2 = lambda v: v.reshape(1, -1).astype(_f32)

    head_tile = jnp.concatenate([jnp.zeros((PADL, d), _f32), meta_tokens.astype(_f32)], axis=0)
    wi = w_in[0]
    o_a = 2 * GLA_QK + 2 * GLA_VW
    o_cq = o_a + GLA_GATE_RANK
    o_ckv = o_cq + MLA_Q_RANK
    o_kr = o_ckv + MLA_KV_RANK
    o_ga = o_kr + MLA_ROPE
    w_a = jnp.pad(wi[:, o_a:o_cq], ((0, 0), (0, LANES - GLA_GATE_RANK)))
    w_kr = jnp.pad(wi[:, o_kr:o_ga], ((0, 0), (MLA_NOPE, LANES - MLA_QDIM)))
    w_all = jnp.concatenate([wi[:, :o_a], w_a, wi[:, o_cq:o_kr], w_kr, wi[:, o_ga:]], axis=1).astype(_bf16)
    assert w_all.shape == (d, _W_COLS)
    w2p = jnp.pad(gla_gate_w2[0], ((0, LANES - GLA_GATE_RANK), (0, 0))).astype(_bf16)
    wuqt = _pad_heads(mla_w_uq[0], MLA_QDIM).T.astype(_bf16)
    wuk = _pad_heads(mla_w_uk[0], MLA_NOPE).astype(_bf16)
    wuvt = mla_w_uv[0].T.astype(_bf16)
    cos_t, sin_lo, sin_hi, cos_tr, sin_tr = _rope_tables(lp)
    blk = np.arange(TILE)
    tril_chunks = jnp.asarray(
        ((blk[:, None] >= blk[None, :]) & (blk[:, None] // GLA_CHUNK == blk[None, :] // GLA_CHUNK)),
        dtype=_bf16)

    pad_map = lambda b, t: (b, t, 0)
    real_map = lambda b, t: (b, jnp.maximum(t - 1, 0), 0)
    tab_spec = pl.BlockSpec((TILE, LANES), lambda b, t: (t, 0))
    tabt_spec = pl.BlockSpec((MLA_ROPE // 2, TILE), lambda b, t: (0, t))
    real_map_t = lambda b, t: (b, 0, jnp.maximum(t - 1, 0))
    out_shapes = (
        jax.ShapeDtypeStruct((bsz, seq, d), _f32),
        jax.ShapeDtypeStruct((bsz, lp, GLA_QK), _bf16),
        jax.ShapeDtypeStruct((bsz, lp, GLA_QK), _bf16),
        jax.ShapeDtypeStruct((bsz, lp, GLA_QK), _bf16),
        jax.ShapeDtypeStruct((bsz, lp, GLA_VW), _bf16),
        jax.ShapeDtypeStruct((bsz * nt, TILE // GLA_CHUNK, GLA_QK), _f32),
        jax.ShapeDtypeStruct((bsz, seq, GLA_VW), _bf16),
        jax.ShapeDtypeStruct((bsz, MLA_HEADS * HEAD_PAD, seq), _bf16),
        jax.ShapeDtypeStruct((bsz, lp, MLA_HEADS * HEAD_PAD), _bf16),
        jax.ShapeDtypeStruct((bsz, MLA_HEADS * V_AUG, lp), _bf16),
        jax.ShapeDtypeStruct((bsz, seq, d), _bf16),
        jax.ShapeDtypeStruct((bsz, seq, d), _bf16),
    )
    out_specs = (
        pl.BlockSpec((1, TILE, d), real_map),
        pl.BlockSpec((1, TILE, GLA_QK), pad_map),
        pl.BlockSpec((1, TILE, GLA_QK), pad_map),
        pl.BlockSpec((1, TILE, GLA_QK), pad_map),
        pl.BlockSpec((1, TILE, GLA_VW), pad_map),
        pl.BlockSpec((1, TILE // GLA_CHUNK, GLA_QK), lambda b, t: (b * nt + t, 0, 0)),
        pl.BlockSpec((1, TILE, GLA_VW), real_map),
        pl.BlockSpec((1, MLA_HEADS * HEAD_PAD, TILE), real_map_t),
        pl.BlockSpec((1, TILE, MLA_HEADS * HEAD_PAD), pad_map),
        pl.BlockSpec((1, MLA_HEADS * V_AUG, TILE), lambda b, t: (b, 0, t)),
        pl.BlockSpec((1, TILE, d), real_map),
        pl.BlockSpec((1, TILE, d), real_map),
    )
    (s_emb, qt, kt, ke, gv, dec, sr, qm, km, vm, gate_a, gate_b) = pl.pallas_call(
        _inproj_kernel,
        grid=(bsz, nt),
        in_specs=[
            pl.BlockSpec((1, TILE, d), real_map),
            _const_spec((TILE, d)),
            _const_spec((1, d)), _const_spec((1, d)),
            _const_spec((d, _W_COLS)),
            _const_spec((LANES, GLA_QK)), _const_spec((1, GLA_QK)),
            _const_spec((1, MLA_Q_RANK)), _const_spec((MLA_HEADS * HEAD_PAD, MLA_Q_RANK)),
            _const_spec((1, MLA_KV_RANK)), _const_spec((MLA_KV_RANK, MLA_HEADS * HEAD_PAD)),
            _const_spec((MLA_HEADS * MLA_DV, MLA_KV_RANK)),
            tab_spec, tab_spec, tab_spec, tabt_spec, tabt_spec,
            _const_spec((TILE, TILE)),
        ],
        out_specs=out_specs,
        out_shape=out_shapes,
        compiler_params=pltpu.CompilerParams(
            dimension_semantics=("arbitrary", "arbitrary"), vmem_limit_bytes=VMEM_LIMIT),
        name="inproj",
    )(x, head_tile, row2(ln_emb_g), row2(ln_emb_b), w_all, w2p, row2(gla_gate_b[0]),
      row2(mla_q_norm_g[0]), wuqt, row2(mla_kv_norm_g[0]), wuk, wuvt, cos_t, sin_lo, sin_hi,
      cos_tr, sin_tr, tril_chunks)

    o_gla = pl.pallas_call(
        _gla_kernel,
        grid=(bsz, nt),
        in_specs=[
            pl.BlockSpec((1, TILE, GLA_QK), pad_map),
            pl.BlockSpec((1, TILE, GLA_QK), pad_map),
            pl.BlockSpec((1, TILE, GLA_QK), pad_map),
            pl.BlockSpec((1, TILE, GLA_VW), pad_map),
            pl.BlockSpec((1, TILE // GLA_CHUNK, GLA_QK), lambda b, t: (b * nt + t, 0, 0)),
            pl.BlockSpec((1, TILE, GLA_VW), real_map),
            _const_spec((1, GLA_DV)),
        ],
        out_specs=pl.BlockSpec((1, TILE, GLA_VW), real_map),
        out_shape=jax.ShapeDtypeStruct((bsz, seq, GLA_VW), _bf16),
        scratch_shapes=[pltpu.VMEM((GLA_HEADS, GLA_DV, GLA_DK), _f32)],
        compiler_params=pltpu.CompilerParams(
            dimension_semantics=("arbitrary", "arbitrary"), vmem_limit_bytes=VMEM_LIMIT),
        name="gla",
    )(qt, kt, ke, gv, dec, sr, row2(gla_norm_g[0]))

    pair = 2 * HEAD_PAD
    o_mla = pl.pallas_call(
        _mla_kernel,
        grid=(bsz, MLA_HEADS // 2, seq // Q_TILE),
        in_specs=[
            pl.BlockSpec((1, pair, Q_TILE), lambda b, hp, i: (b, hp, i)),
            pl.BlockSpec((1, lp, pair), lambda b, hp, i: (b, 0, hp)),
            pl.BlockSpec((1, 2 * V_AUG, lp), lambda b, hp, i: (b, hp, 0)),
        ],
        out_specs=pl.BlockSpec((1, Q_TILE, 2 * MLA_DV), lambda b, hp, i: (b, i, hp)),
        out_shape=jax.ShapeDtypeStruct((bsz, seq, MLA_HEADS * MLA_DV), _bf16),
        scratch_shapes=[pltpu.VMEM((MLA_PAIR, V_AUG, Q_TILE), _f32),
                        pltpu.VMEM((MLA_PAIR, 1, LANES), _f32)],
        compiler_params=pltpu.CompilerParams(
            dimension_semantics=("arbitrary", "arbitrary", "arbitrary"),
            vmem_limit_bytes=VMEM_LIMIT),
        name="mla",
    )(qm, km, vm)

    rw = jnp.concatenate([router_group_w[0], router_expert_w[0]], axis=1)
    rw = jnp.pad(rw, ((0, 0), (0, LANES - rw.shape[1])))
    rwh = rw.astype(_bf16)
    rwl = (rw - rwh.astype(_f32)).astype(_bf16)
    rb = jnp.concatenate([router_group_b[0], router_expert_b[0]])
    rb = jnp.pad(rb, (0, LANES - rb.shape[0])).reshape(1, LANES)
    mi = np.arange(ROUTE_BLOCK)
    tril_strict = jnp.asarray(mi[:, None] > mi[None, :], dtype=_bf16)
    flat = lambda a: a.reshape(ntok, a.shape[-1])
    tok_spec = lambda w: pl.BlockSpec((MERGE_TILE, w), lambda g: (g, 0))
    tt_spec = lambda n, index_map: pl.BlockSpec((n * SUBLANES, LANES), index_map)
    s2, info, info_t, cnt = pl.pallas_call(
        _merge_kernel,
        grid=(ntok // MERGE_TILE,),
        in_specs=[tok_spec(d), tok_spec(d), tok_spec(d), tok_spec(d), tok_spec(d),
                  _const_spec((d, d)), _const_spec((d, d)), _const_spec((d, d)),
                  _const_spec((1, d)), _const_spec((1, d)),
                  _const_spec((d, LANES)), _const_spec((d, LANES)), _const_spec((1, LANES)),
                  _const_spec((ROUTE_BLOCK, ROUTE_BLOCK))],
        out_specs=(tt_spec(MERGE_TILE, lambda g: (g, 0)), tok_spec(LANES),
                   pl.BlockSpec((SUBLANES, MERGE_TILE), lambda g: (0, g)), _const_spec((8, LANES))),
        out_shape=(jax.ShapeDtypeStruct((ntok * SUBLANES, LANES), _f32),
                   jax.ShapeDtypeStruct((ntok, LANES), _f32),
                   jax.ShapeDtypeStruct((SUBLANES, ntok), _f32),
                   jax.ShapeDtypeStruct((8, LANES), _f32)),
        scratch_shapes=[pltpu.VMEM((8, LANES), _f32)],
        compiler_params=pltpu.CompilerParams(
            dimension_semantics=("arbitrary",), vmem_limit_bytes=VMEM_LIMIT),
        name="merge_router",
    )(flat(o_gla), flat(o_mla), flat(gate_a), flat(gate_b), flat(s_emb),
      w_branch_gla[0].astype(_bf16), w_branch_mla[0].astype(_bf16), w_out[0].astype(_bf16),
      row2(ln_mix_g[0]), row2(ln_mix_b[0]), rwh, rwl, rb, tril_strict)

    n_tiles = (2 * ntok + N_EXPERTS * (EXPERT_TILE - 1)) // EXPERT_TILE
    n_rows = n_tiles * EXPERT_TILE
    e_idx = info_t[0:2].astype(jnp.int32)
    rank = info_t[4:6].astype(jnp.int32)
    counts = cnt[0, :N_EXPERTS].astype(jnp.int32)
    padded = ((counts + EXPERT_TILE - 1) // EXPERT_TILE) * EXPERT_TILE
    ends = jnp.cumsum(padded)
    starts = ends - padded
    expert_ids = jnp.arange(N_EXPERTS, dtype=jnp.int32)[:, None, None]
    start_of = jnp.sum(jnp.where(e_idx[None] == expert_ids, starts[:, None, None], 0), axis=0)
    pos = (start_of + rank).reshape(-1)
    tile_start = jnp.arange(n_tiles, dtype=jnp.int32) * EXPERT_TILE
    tile_expert = jnp.minimum(
        jnp.sum((ends[None, :] <= tile_start[:, None]).astype(jnp.int32), axis=1), N_EXPERTS - 1)
    n_used = (ends[-1:] // EXPERT_TILE).astype(jnp.int32)
    zero_row = jnp.where(padded > 0, ends - EXPERT_TILE, -1).astype(jnp.int32)

    any_spec = pl.BlockSpec(memory_space=pl.ANY)
    xs = pl.pallas_call(
        _dispatch_kernel,
        grid_spec=pltpu.PrefetchScalarGridSpec(
            num_scalar_prefetch=3,
            grid=(ntok // DISPATCH_TILE,),
            in_specs=[tt_spec(DISPATCH_TILE, lambda g, p, z, nu: (g, 0))],
            out_specs=any_spec,
            scratch_shapes=[pltpu.VMEM((EXPERT_TILE * SUBLANES, LANES), _f32),
                            pltpu.SemaphoreType.DMA((2,))],
        ),
        out_shape=jax.ShapeDtypeStruct((n_rows * SUBLANES, LANES), _f32),
        compiler_params=pltpu.CompilerParams(
            dimension_semantics=("arbitrary",), vmem_limit_bytes=VMEM_LIMIT, has_side_effects=True),
        name="dispatch",
    )(pos, zero_row, n_used, s2)

    ff = EXPERT_FF
    wg = expert_w_gate[0].reshape(N_EXPERTS, d, ff)
    wu = expert_w_up[0].reshape(N_EXPERTS, d, ff)
    wd = expert_w_down[0].reshape(N_EXPERTS, ff, d)
    ys = pl.pallas_call(
        _expert_kernel,
        grid_spec=pltpu.PrefetchScalarGridSpec(
            num_scalar_prefetch=2,
            grid=(n_tiles,),
            in_specs=[
                tt_spec(EXPERT_TILE, lambda u, te, nu: (jnp.minimum(u, nu[0] - 1), 0)),
                pl.BlockSpec((1, d, ff), lambda u, te, nu: (te[u], 0, 0)),
                pl.BlockSpec((1, d, ff), lambda u, te, nu: (te[u], 0, 0)),
                pl.BlockSpec((1, ff, d), lambda u, te, nu: (te[u], 0, 0)),
            ],
            out_specs=tt_spec(EXPERT_TILE, lambda u, te, nu: (u, 0)),
            scratch_shapes=[pltpu.VMEM((d, ff), _bf16), pltpu.VMEM((d, ff), _bf16),
                            pltpu.VMEM((ff, d), _bf16)],
        ),
        out_shape=jax.ShapeDtypeStruct((n_rows * SUBLANES, LANES), _f32),
        compiler_params=pltpu.CompilerParams(
            dimension_semantics=("arbitrary",), vmem_limit_bytes=VMEM_LIMIT),
        name="experts",
    )(tile_expert, n_used, xs, wg, wu, wd)

    out = pl.pallas_call(
        _combine_kernel,
        grid_spec=pltpu.PrefetchScalarGridSpec(
            num_scalar_prefetch=1,
            grid=(ntok // COMBINE_TILE,),
            in_specs=[
                tt_spec(COMBINE_TILE, lambda g, p: (g, 0)),
                pl.BlockSpec((COMBINE_TILE, LANES), lambda g, p: (g, 0)),
                any_spec,
                pl.BlockSpec((1, d), lambda g, p: (0, 0)),
                pl.BlockSpec((1, d), lambda g, p: (0, 0)),
            ],
            out_specs=pl.BlockSpec((COMBINE_TILE, d), lambda g, p: (g, 0)),
            scratch_shapes=[pltpu.VMEM((2, 2, COMBINE_TILE * SUBLANES, LANES), _f32),
                            pltpu.SemaphoreType.DMA((2,))],
        ),
        out_shape=jax.ShapeDtypeStruct((ntok, d), _f32),
        compiler_params=pltpu.CompilerParams(
            dimension_semantics=("arbitrary",), vmem_limit_bytes=VMEM_LIMIT),
        name="combine_ln",
    )(pos, s2, info, ys, row2(ln_ffn_g[0]), row2(ln_ffn_b[0]))
    return out.reshape(bsz, seq, d)
```

```python
import jax
import jax.numpy as jnp
import numpy as np
from jax import lax
from jax.experimental import pallas as pl
from jax.experimental.pallas import tpu as pltpu

D_MODEL = 1024
N_META = 16
GLA_HEADS = 4
GLA_DK = 128
GLA_DV = 256
GLA_QK = GLA_HEADS * GLA_DK
GLA_VW = GLA_HEADS * GLA_DV
GLA_GATE_RANK = 16
GLA_GATE_TAU = 16.0
GLA_CHUNK = 64
MLA_HEADS = 16
MLA_Q_RANK = 384
MLA_KV_RANK = 256
MLA_NOPE = 64
MLA_ROPE = 32
MLA_DV = 64
MLA_QDIM = MLA_NOPE + MLA_ROPE
ROPE_BASE = 10000.0
N_GROUPS = 4
EXPERTS_PER_GROUP = 8
N_EXPERTS = N_GROUPS * EXPERTS_PER_GROUP
EXPERT_FF = 256
DEPTH = 1
ALPHA = (2.0 * DEPTH) ** 0.25
LN_EPS = 1e-5
RMS_EPS = 1e-6

LANES = 128
SUBLANES = 8
MXU_DIM = 256
TILE = 256
PADL = TILE - N_META
HEAD_PAD = LANES
BF16_ROWS = 16
KV_TILE = 512
Q_TILE = 2 * KV_TILE
V_AUG = MLA_DV + BF16_ROWS
MERGE_TILE = 512
ROUTE_BLOCK = 256
EXPERT_TILE = 256
DISPATCH_TILE = 512
COMBINE_TILE = 256
NEG = -1e30
LOG2E = 1.4426950408889634
BOUND_SLACK = 1.02
MIN_SOFTMAX_SUM = 2.0 ** -100
VMEM_LIMIT = 56 * 1024 * 1024

_C_Q, _C_K, _C_V, _C_R = 0, 512, 1024, 2048
_C_A = 3072
_C_CQ = _C_A + LANES
_C_CKV = _C_CQ + MLA_Q_RANK
_C_KR = _C_CKV + MLA_KV_RANK
_C_GA = _C_KR + LANES
_C_GB = _C_GA + D_MODEL
_W_COLS = _C_GB + D_MODEL

_f32 = jnp.float32
_bf16 = jnp.bfloat16


def _dot(a, b):
    return jnp.dot(a, b, preferred_element_type=_f32)


def _dot_nt(a, b):
    return lax.dot_general(a, b, (((1,), (1,)), ((), ())), preferred_element_type=_f32)


def _dot_tn(a, b):
    return lax.dot_general(a, b, (((0,), (0,)), ((), ())), preferred_element_type=_f32)


def _layer_norm(x, g, b):
    mu = jnp.mean(x, axis=-1, keepdims=True)
    xc = x - mu
    var = jnp.mean(xc * xc, axis=-1, keepdims=True)
    return xc * lax.rsqrt(var + LN_EPS) * g + b


def _rms_norm(x, g):
    ms = jnp.mean(x * x, axis=-1, keepdims=True)
    return x * lax.rsqrt(ms + RMS_EPS) * g


def _sigmoid(x):
    return 1.0 / (1.0 + jnp.exp(-x))


def _tt_load(ref, n, lead=()):
    return jnp.concatenate(
        [ref[lead + (pl.ds(a, n, stride=SUBLANES), slice(None))] for a in range(SUBLANES)], axis=1)


def _tt_store(ref, x):
    n = x.shape[0]
    for a in range(SUBLANES):
        ref[pl.ds(a, n, stride=SUBLANES), :] = x[:, a * LANES:(a + 1) * LANES]


def _tt_rows(tok):
    return pl.ds(pl.multiple_of(tok * SUBLANES, SUBLANES), SUBLANES)


def _rope(x, cos, sin_lo, sin_hi):
    half = MLA_ROPE // 2
    from_hi = pltpu.roll(x, LANES - half, 1)
    from_lo = pltpu.roll(x, half, 1)
    return x * cos + from_hi * sin_lo + from_lo * sin_hi


def _inproj_kernel(x_ref, head_ref, lng_ref, lnb_ref, w_ref, w2_ref, gb_ref, qg_ref, wuqt_ref, kvg_ref,
                   wuk_ref, wuvt_ref, cos_ref, sl_ref, sh_ref, cost_ref, sint_ref, tril_ref,
                   s_ref, qt_ref, kt_ref, ke_ref, gv_ref, dec_ref, sr_ref, qm_ref, km_ref,
                   vm_ref, ga_ref, gbt_ref):
    t = pl.program_id(1)
    x_in = jnp.where(t == 0, head_ref[...], x_ref[0])
    sn = _layer_norm(x_in, lng_ref[...], lnb_ref[...])
    s_ref[0] = sn
    snb = sn.astype(_bf16)
    row = t * TILE + lax.broadcasted_iota(jnp.int32, (TILE, 1), 0)
    valid = row >= PADL

    proj = lambda c0, width: _dot(snb, w_ref[:, c0:c0 + width])
    a_lr = proj(_C_A, LANES)
    cq = proj(_C_CQ, MLA_Q_RANK)
    ckv = proj(_C_CKV, MLA_KV_RANK)
    kr_raw = proj(_C_KR, LANES)
    gq = proj(_C_Q, GLA_QK)
    gk = proj(_C_K, GLA_QK)
    gv_ref[0] = jnp.where(valid, proj(_C_V, GLA_VW), 0.0).astype(_bf16)
    r = proj(_C_R, GLA_VW)
    sr_ref[0] = (r * _sigmoid(r)).astype(_bf16)
    ga_ref[0] = _sigmoid(proj(_C_GA, D_MODEL)).astype(_bf16)
    gbt_ref[0] = _sigmoid(proj(_C_GB, D_MODEL)).astype(_bf16)

    z = _dot(a_lr.astype(_bf16), w2_ref[...]) + gb_ref[...]
    cqn = _rms_norm(cq, qg_ref[...]).astype(_bf16)
    ckvn = _rms_norm(ckv, kvg_ref[...]).astype(_bf16)
    qft = _dot_nt(wuqt_ref[...], cqn)
    kf = _dot(ckvn, wuk_ref[...])
    vt = _dot_nt(wuvt_ref[...], ckvn)

    la = (jnp.minimum(z, 0.0) - jnp.log1p(jnp.exp(-jnp.abs(z)))) * (1.0 / GLA_GATE_TAU)
    la = jnp.where(valid, la, 0.0)
    hi = la.astype(_bf16)
    r1 = la - hi.astype(_f32)
    mid = r1.astype(_bf16)
    lo = (r1 - mid.astype(_f32)).astype(_bf16)
    tril = tril_ref[...]
    bc = _dot(tril, hi) + _dot(tril, mid) + _dot(tril, lo)
    n_chunks = TILE // GLA_CHUNK
    lasts = [bc[c * GLA_CHUNK + GLA_CHUNK - 1:(c + 1) * GLA_CHUNK, :] for c in range(n_chunks)]
    for c in range(n_chunks):
        dec_ref[0, c:c + 1, :] = jnp.exp(lasts[c])
    b_last = jnp.concatenate(
        [jnp.broadcast_to(l, (GLA_CHUNK, GLA_QK)) for l in lasts], axis=0)
    gk = jnp.where(valid, gk, 0.0)
    qt_ref[0] = (gq * (GLA_DK ** -0.5) * jnp.exp(bc)).astype(_bf16)
    kt_ref[0] = (gk * jnp.exp(-bc)).astype(_bf16)
    ke_ref[0] = (gk * jnp.exp(b_last - bc)).astype(_bf16)

    cos = cos_ref[...]
    sl = sl_ref[...]
    sh = sh_ref[...]
    scale = (MLA_QDIM ** -0.5) * LOG2E
    cost = cost_ref[...]
    sint = sint_ref[...]
    half = MLA_ROPE // 2
    for h in range(MLA_HEADS):
        base = h * HEAD_PAD
        x1 = qft[base + MLA_NOPE:base + MLA_NOPE + half]
        x2 = qft[base + MLA_NOPE + half:base + MLA_QDIM]
        qm_ref[0, base:base + MLA_NOPE] = (qft[base:base + MLA_NOPE] * scale).astype(_bf16)
        qm_ref[0, base + MLA_NOPE:base + MLA_NOPE + half] = ((x1 * cost - x2 * sint) * scale).astype(_bf16)
        qm_ref[0, base + MLA_NOPE + half:base + MLA_QDIM] = ((x1 * sint + x2 * cost) * scale).astype(_bf16)
        qm_ref[0, base + MLA_QDIM:base + HEAD_PAD] = jnp.zeros((HEAD_PAD - MLA_QDIM, TILE), _bf16)
    kr = _rope(kr_raw, cos, sl, sh)
    for h in range(MLA_HEADS):
        km_ref[0, :, h * HEAD_PAD:(h + 1) * HEAD_PAD] = (
            kf[:, h * HEAD_PAD:(h + 1) * HEAD_PAD] + kr).astype(_bf16)
    for h in range(MLA_HEADS):
        vm_ref[0, h * V_AUG:h * V_AUG + MLA_DV] = vt[h * MLA_DV:(h + 1) * MLA_DV].astype(_bf16)
        vm_ref[0, h * V_AUG + MLA_DV:(h + 1) * V_AUG] = jnp.ones((V_AUG - MLA_DV, TILE), _bf16)


def _gla_kernel(qt_ref, kt_ref, ke_ref, gv_ref, dec_ref, sr_ref, ng_ref, o_ref, st_ref):
    t = pl.program_id(1)

    @pl.when(t == 0)
    def _():
        st_ref[...] = jnp.zeros_like(st_ref)

    ri = lax.broadcasted_iota(jnp.int32, (TILE, TILE), 0)
    ci = lax.broadcasted_iota(jnp.int32, (TILE, TILE), 1)
    visible = (ri >= ci) & (ri // GLA_CHUNK == ci // GLA_CHUNK)
    ng = ng_ref[...]
    n_chunks = TILE // GLA_CHUNK
    chunk_rows = [slice(c * GLA_CHUNK, (c + 1) * GLA_CHUNK) for c in range(n_chunks)]
    k_cols = [slice(h * GLA_DK, (h + 1) * GLA_DK) for h in range(GLA_HEADS)]
    v_cols = [slice(h * GLA_DV, (h + 1) * GLA_DV) for h in range(GLA_HEADS)]
    att = [jnp.where(visible, _dot_nt(qt_ref[0, :, k_cols[h]], kt_ref[0, :, k_cols[h]]), 0.0)
           for h in range(GLA_HEADS)]
    upd = [[_dot_tn(gv_ref[0, rows, v_cols[h]], ke_ref[0, rows, k_cols[h]]) for rows in chunk_rows]
           for h in range(GLA_HEADS)]
    o_intra = [_dot(att[h].astype(_bf16), gv_ref[0, :, v_cols[h]]) for h in range(GLA_HEADS)]
    for h in range(GLA_HEADS):
        st = st_ref[h]
        for c, rows in enumerate(chunk_rows):
            o = o_intra[h][rows] + _dot_nt(qt_ref[0, rows, k_cols[h]], st.astype(_bf16))
            st = st * dec_ref[0, c:c + 1, k_cols[h]] + upd[h][c]
            o = _rms_norm(o, ng) * sr_ref[0, rows, v_cols[h]].astype(_f32)
            o_ref[0, rows, v_cols[h]] = o.astype(_bf16)
        st_ref[h] = st


MLA_PAIR = 2


def _mla_tile_start(j):
    return pl.multiple_of(TILE + j * KV_TILE, TILE)


def _mla_finish(o_ref, acc_ref):
    outs = []
    for h in range(MLA_PAIR):
        a = acc_ref[h]
        outs.append(a[0:MLA_DV] / a[MLA_DV:MLA_DV + 1])
    o_ref[0] = jnp.concatenate(outs, axis=0).T.astype(_bf16)


def _mla_kernel(q_ref, k_ref, v_ref, o_ref, acc_ref, knorm_ref):
    i = pl.program_id(2)
    heads = MLA_PAIR
    ones = jnp.ones((HEAD_PAD, LANES), _bf16)

    @pl.when(i == 0)
    def _():
        for h in range(heads):
            kk = k_ref[0, :, h * HEAD_PAD:(h + 1) * HEAD_PAD].astype(_f32)
            hi = (kk * kk).astype(_bf16)
            knorm_ref[h] = jnp.max(_dot(hi, ones), axis=0, keepdims=True)

    half = slice(KV_TILE, Q_TILE)
    q_t = [q_ref[0, h * HEAD_PAD:(h + 1) * HEAD_PAD, :] for h in range(heads)]
    q_hi = [q_ref[0, h * HEAD_PAD:(h + 1) * HEAD_PAD, half] for h in range(heads)]

    def score_bound(h, q):
        qq = q.astype(_f32)
        qn2 = jnp.sum(qq * qq, axis=0, keepdims=True)
        return jnp.sqrt(qn2 * knorm_ref[h][:, 0:1]) * BOUND_SLACK

    bound = [score_bound(h, q_t[h]) for h in range(heads)]
    bound_hi = [score_bound(h, q_hi[h]) for h in range(heads)]

    def keys(j, h):
        return k_ref[0, pl.ds(_mla_tile_start(j), KV_TILE), h * HEAD_PAD:(h + 1) * HEAD_PAD]

    def weighted(j, h, p):
        return _dot(v_ref[0, h * V_AUG:(h + 1) * V_AUG, pl.ds(_mla_tile_start(j), KV_TILE)], p)

    k_row = lax.broadcasted_iota(jnp.int32, (KV_TILE, Q_TILE), 0)
    q_col = lax.broadcasted_iota(jnp.int32, (KV_TILE, Q_TILE), 1)
    k_row_sq = lax.broadcasted_iota(jnp.int32, (KV_TILE, KV_TILE), 0)
    q_col_sq = lax.broadcasted_iota(jnp.int32, (KV_TILE, KV_TILE), 1)
    s_meta = [_dot(k_ref[0, PADL:TILE, h * HEAD_PAD:(h + 1) * HEAD_PAD], q_t[h]) for h in range(heads)]
    s_lo = [jnp.where(k_row <= q_col, _dot(keys(2 * i, h), q_t[h]), NEG) for h in range(heads)]
    s_hi = [jnp.where(k_row_sq <= q_col_sq, _dot(keys(2 * i + 1, h), q_hi[h]), NEG)
            for h in range(heads)]
    for h in range(heads):
        p_meta = jnp.concatenate([jnp.zeros((PADL, Q_TILE), _bf16),
                                  jnp.exp2(s_meta[h] - bound[h]).astype(_bf16)], axis=0)
        acc_ref[h] = (_dot(v_ref[0, h * V_AUG:(h + 1) * V_AUG, 0:TILE], p_meta)
                      + weighted(2 * i, h, jnp.exp2(s_lo[h] - bound[h]).astype(_bf16)))
        acc_ref[h, :, half] += weighted(2 * i + 1, h, jnp.exp2(s_hi[h] - bound_hi[h]).astype(_bf16))

    def body(jj, c):
        tiles = (2 * jj, 2 * jj + 1)
        s = [[_dot(keys(j, h), q_t[h]) for h in range(heads)] for j in tiles]
        for h in range(heads):
            acc_ref[h] += sum(weighted(j, h, jnp.exp2(s[n][h] - bound[h]).astype(_bf16))
                              for n, j in enumerate(tiles))
        return c

    lax.fori_loop(0, i, body, 0)
    _mla_finish(o_ref, acc_ref)

    l_min = jnp.minimum(jnp.min(acc_ref[0, MLA_DV:MLA_DV + 1, :]), jnp.min(acc_ref[1, MLA_DV:MLA_DV + 1, :]))

    @pl.when(jnp.logical_not(l_min >= MIN_SOFTMAX_SUM))
    def _():
        _mla_exact(q_ref, k_ref, v_ref, o_ref, acc_ref)


def _mla_exact(q_ref, k_ref, v_ref, o_ref, acc_ref):
    i = pl.program_id(2)
    heads = MLA_PAIR
    q_t = [q_ref[0, h * HEAD_PAD:(h + 1) * HEAD_PAD, :] for h in range(heads)]

    ms = []
    for h in range(heads):
        kb = k_ref[0, PADL:TILE, h * HEAD_PAD:(h + 1) * HEAD_PAD]
        s = _dot(kb, q_t[h])
        m0 = jnp.max(s, axis=0, keepdims=True)
        p = jnp.concatenate([jnp.zeros((PADL, Q_TILE), _bf16), jnp.exp2(s - m0).astype(_bf16)], axis=0)
        acc_ref[h] = _dot(v_ref[0, h * V_AUG:(h + 1) * V_AUG, 0:TILE], p)
        ms.append(m0)

    k_row = lax.broadcasted_iota(jnp.int32, (KV_TILE, Q_TILE), 0)
    q_col = lax.broadcasted_iota(jnp.int32, (KV_TILE, Q_TILE), 1)

    def body(j, ms):
        visible = (j - 2 * i) * KV_TILE + k_row <= q_col
        out = []
        for h in range(heads):
            s = _dot(k_ref[0, pl.ds(_mla_tile_start(j), KV_TILE), h * HEAD_PAD:(h + 1) * HEAD_PAD],
                     q_t[h])
            s = jnp.where(visible, s, NEG)
            vb = v_ref[0, h * V_AUG:(h + 1) * V_AUG, pl.ds(_mla_tile_start(j), KV_TILE)]
            m_new = jnp.maximum(ms[h], jnp.max(s, axis=0, keepdims=True))
            alpha = jnp.exp2(ms[h] - m_new)
            acc_ref[h] = alpha * acc_ref[h] + _dot(vb, jnp.exp2(s - m_new).astype(_bf16))
            out.append(m_new)
        return tuple(out)

    lax.fori_loop(0, 2 * i + 2, body, tuple(ms))
    _mla_finish(o_ref, acc_ref)


def _merge_kernel(og_ref, om_ref, ga_ref, gbt_ref, s_ref, wbg_ref, wbm_ref, wo_ref, lng_ref,
                  lnb_ref, rwh_ref, rwl_ref, rb_ref, tril_ref,
                  s2_ref, info_ref, infot_ref, cnt_ref, carry_ref):
    step = pl.program_id(0)

    @pl.when(step == 0)
    def _():
        carry_ref[...] = jnp.zeros_like(carry_ref)

    n_blk = MERGE_TILE // ROUTE_BLOCK
    blocks = [slice(i * ROUTE_BLOCK, (i + 1) * ROUTE_BLOCK) for i in range(n_blk)]
    d = wo_ref.shape[0]
    col_blocks = [slice(c, c + MXU_DIM) for c in range(0, d, MXU_DIM)]
    merged = []
    for rows in blocks:
        merged.append(jnp.concatenate(
            [(ga_ref[rows, cols].astype(_f32) * _dot(og_ref[rows, :], wbg_ref[:, cols])
              + gbt_ref[rows, cols].astype(_f32) * _dot(om_ref[rows, :], wbm_ref[:, cols])
              ).astype(_bf16) for cols in col_blocks], axis=1))
    s2 = []
    for i, rows in enumerate(blocks):
        y = ALPHA * s_ref[rows, :] + _dot(merged[i], wo_ref[...])
        s2.append(_layer_norm(y, lng_ref[...], lnb_ref[...]))
    logits = []
    for i in range(n_blk):
        xh = s2[i].astype(_bf16)
        xl = (s2[i] - xh.astype(_f32)).astype(_bf16)
        logits.append(_dot(xh, rwh_ref[...]) + _dot(xl, rwh_ref[...]) + _dot(xh, rwl_ref[...])
                      + rb_ref[...])
    _tt_store(s2_ref, jnp.concatenate(s2, axis=0))

    lane = lax.broadcasted_iota(jnp.int32, (ROUTE_BLOCK, LANES), 1)
    is_g = lane < N_GROUPS
    carry = carry_ref[0:1, :]
    infos = []
    for i in range(n_blk):
        gl = jnp.where(is_g, logits[i], NEG)
        gmax = jnp.max(gl, axis=-1, keepdims=True)
        gidx = jnp.min(jnp.where(gl == gmax, lane, LANES), axis=-1, keepdims=True)
        p_g = 1.0 / jnp.sum(jnp.where(is_g, jnp.exp(gl - gmax), 0.0), axis=-1, keepdims=True)
        lo = N_GROUPS + EXPERTS_PER_GROUP * gidx
        el = jnp.where((lane >= lo) & (lane < lo + EXPERTS_PER_GROUP), logits[i], NEG)
        v1 = jnp.max(el, axis=-1, keepdims=True)
        i1 = jnp.min(jnp.where(el == v1, lane, LANES), axis=-1, keepdims=True)
        el2 = jnp.where(lane == i1, NEG, el)
        v2 = jnp.max(el2, axis=-1, keepdims=True)
        i2 = jnp.min(jnp.where(el2 == v2, lane, LANES), axis=-1, keepdims=True)
        tt = jnp.exp(v2 - v1)
        p1 = 1.0 / (1.0 + tt)
        p2 = tt / (1.0 + tt)
        e1 = i1 - N_GROUPS
        e2 = i2 - N_GROUPS
        hit1 = lane == e1
        hit2 = lane == e2
        onehot = jnp.where(hit1 | hit2, 1.0, 0.0)
        before = _dot(tril_ref[...], onehot.astype(_bf16)) + carry
        r1 = jnp.sum(jnp.where(hit1, before, 0.0), axis=-1, keepdims=True)
        r2 = jnp.sum(jnp.where(hit2, before, 0.0), axis=-1, keepdims=True)
        carry = carry + jnp.sum(onehot, axis=0, keepdims=True)
        infos.append(jnp.where(lane == 0, e1.astype(_f32),
                     jnp.where(lane == 1, e2.astype(_f32),
                     jnp.where(lane == 2, p_g * p1,
                     jnp.where(lane == 3, p_g * p2,
                     jnp.where(lane == 4, r1,
                     jnp.where(lane == 5, r2, 0.0)))))))
    carry_ref[...] = jnp.broadcast_to(carry, carry_ref.shape)
    cnt_ref[...] = jnp.broadcast_to(carry, cnt_ref.shape)
    info = jnp.concatenate(infos, axis=0)
    info_ref[...] = info
    infot_ref[...] = info.T[0:SUBLANES]


def _dispatch_kernel(pos_ref, zrow_ref, nused_ref, s2_ref, xs_hbm, zero_ref, sems):
    g = pl.program_id(0)
    zero_sem = sems.at[1]
    row_sem = sems.at[0]

    tile_rows = EXPERT_TILE * SUBLANES

    def zero_copy(row):
        start = pl.multiple_of(row * SUBLANES, tile_rows)
        return pltpu.make_async_copy(zero_ref, xs_hbm.at[pl.ds(start, tile_rows)], zero_sem)

    @pl.when(g == 0)
    def _():
        zero_ref[...] = jnp.zeros_like(zero_ref)

        def start(e, c):
            @pl.when(zrow_ref[e] >= 0)
            def _():
                zero_copy(zrow_ref[e]).start()
            return c

        def wait(e, c):
            @pl.when(zrow_ref[e] >= 0)
            def _():
                zero_copy(0).wait()
            return c

        def start_tail(u, c):
            zero_copy(u * EXPERT_TILE).start()
            return c

        def wait_tail(u, c):
            zero_copy(0).wait()
            return c

        n_tiles = xs_hbm.shape[0] // tile_rows
        lax.fori_loop(0, N_EXPERTS, start, 0)
        lax.fori_loop(nused_ref[0], n_tiles, start_tail, 0)
        lax.fori_loop(0, N_EXPERTS, wait, 0)
        lax.fori_loop(nused_ref[0], n_tiles, wait_tail, 0)

    n_tok = pl.num_programs(0) * DISPATCH_TILE

    def issue(r, c):
        tok = g * DISPATCH_TILE + r
        for k in range(2):
            pltpu.make_async_copy(s2_ref.at[_tt_rows(r)],
                                  xs_hbm.at[_tt_rows(pos_ref[k * n_tok + tok])], row_sem).start(priority=k)
        return c

    lax.fori_loop(0, DISPATCH_TILE, issue, 0, unroll=8)
    for k in range(2):
        pltpu.make_async_copy(s2_ref, xs_hbm.at[pl.ds(0, DISPATCH_TILE * SUBLANES)], row_sem).wait()


def _expert_kernel(te_ref, nused_ref, x_ref, wg_ref, wu_ref, wd_ref, o_ref, wgb_ref, wub_ref, wdb_ref):
    u = pl.program_id(0)
    used = u < nused_ref[0]
    first_of_expert = jnp.logical_or(u == 0, te_ref[u] != te_ref[jnp.maximum(u - 1, 0)])

    @pl.when(jnp.logical_and(used, first_of_expert))
    def _():
        wgb_ref[...] = wg_ref[0].astype(_bf16)
        wub_ref[...] = wu_ref[0].astype(_bf16)
        wdb_ref[...] = wd_ref[0].astype(_bf16)

    @pl.when(used)
    def _():
        x = _tt_load(x_ref, EXPERT_TILE).astype(_bf16)
        a = _dot(x, wgb_ref[...])
        up = _dot(x, wub_ref[...])
        hid = a * _sigmoid(a) * up
        _tt_store(o_ref, _dot(hid.astype(_bf16), wdb_ref[...]))

    @pl.when(u >= nused_ref[0])
    def _():
        o_ref[...] = jnp.zeros_like(o_ref)


def _combine_kernel(pos_ref, s2_ref, info_ref, ys_hbm, lng_ref, lnb_ref, o_ref, buf_ref, sems):
    g = pl.program_id(0)
    last = pl.num_programs(0) - 1
    n_tok = pl.num_programs(0) * COMBINE_TILE

    def start_copies(tile, slot, r):
        tok = tile * COMBINE_TILE + r
        for k in range(2):
            pltpu.make_async_copy(ys_hbm.at[_tt_rows(pos_ref[k * n_tok + tok])],
                                  buf_ref.at[slot, k, _tt_rows(r)], sems.at[slot]).start(priority=k)

    def wait_slot(slot):
        for k in range(2):
            pltpu.make_async_copy(ys_hbm.at[pl.ds(0, COMBINE_TILE * SUBLANES)], buf_ref.at[slot, k],
                                  sems.at[slot]).wait()

    @pl.when(g == 0)
    def _():
        def body(r, c):
            start_copies(0, 0, r)
            return c

        lax.fori_loop(0, COMBINE_TILE, body, 0, unroll=8)

    slot = g % 2
    wait_slot(slot)
    info = info_ref[...]
    y = (ALPHA * _tt_load(s2_ref, COMBINE_TILE)
         + info[:, 2:3] * _tt_load(buf_ref, COMBINE_TILE, (slot, 0))
         + info[:, 3:4] * _tt_load(buf_ref, COMBINE_TILE, (slot, 1)))
    o_ref[...] = _layer_norm(y, lng_ref[...], lnb_ref[...])

    nxt = jnp.minimum(g + 1, last)
    for r in range(COMBINE_TILE):
        start_copies(nxt, 1 - slot, r)

    @pl.when(g == last)
    def _():
        wait_slot(1 - slot)


def _const_spec(shape):
    nd = len(shape)
    return pl.BlockSpec(shape, lambda *_: (0,) * nd)


def _rope_tables(lp):
    pos = jnp.maximum(jnp.arange(lp, dtype=_f32) - PADL, 0.0)
    inv_freq = ROPE_BASE ** (-jnp.arange(0, MLA_ROPE, 2, dtype=_f32) / MLA_ROPE)
    ang = pos[:, None] * inv_freq[None, :]
    cos, sin = jnp.cos(ang), jnp.sin(ang)
    half = MLA_ROPE // 2
    ones = jnp.ones((lp, MLA_NOPE), _f32)
    zeros_n = jnp.zeros((lp, MLA_NOPE), _f32)
    zeros_h = jnp.zeros((lp, half), _f32)
    tail1 = jnp.ones((lp, LANES - MLA_QDIM), _f32)
    tail0 = jnp.zeros((lp, LANES - MLA_QDIM), _f32)
    cos_t = jnp.concatenate([ones, cos, cos, tail1], axis=1)
    sin_lo = jnp.concatenate([zeros_n, -sin, zeros_h, tail0], axis=1)
    sin_hi = jnp.concatenate([zeros_n, zeros_h, sin, tail0], axis=1)
    return cos_t, sin_lo, sin_hi, cos.T, sin.T


def _pad_heads(w, width):
    k = w.shape[0]
    w = w.reshape(k, MLA_HEADS, width)
    w = jnp.pad(w, ((0, 0), (0, 0), (0, HEAD_PAD - width)))
    return w.reshape(k, MLA_HEADS * HEAD_PAD)


def kernel(x, meta_tokens, ln_emb_g, ln_emb_b, w_in, gla_gate_w2, gla_gate_b, gla_norm_g, mla_q_norm_g, mla_w_uq, mla_kv_norm_g, mla_w_uk, mla_w_uv, w_branch_gla, w_branch_mla, w_out, ln_mix_g, ln_mix_b, router_group_w, router_group_b, router_expert_w, router_expert_b, expert_w_gate, expert_w_up, expert_w_down, ln_ffn_g, ln_ffn_b):
    bsz, seq, d = x.shape
    assert d == D_MODEL and seq % Q_TILE == 0 and w_in.shape[0] == DEPTH == 1
    lp = PADL + N_META + seq
    nt = lp // TILE
    ntok = bsz * seq
    row2 = lambda v: v.reshape(1, -1).astype(_f32)

    head_tile = jnp.concatenate([jnp.zeros((PADL, d), _f32), meta_tokens.astype(_f32)], axis=0)
    wi = w_in[0]
    o_a = 2 * GLA_QK + 2 * GLA_VW
    o_cq = o_a + GLA_GATE_RANK
    o_ckv = o_cq + MLA_Q_RANK
    o_kr = o_ckv + MLA_KV_RANK
    o_ga = o_kr + MLA_ROPE
    w_a = jnp.pad(wi[:, o_a:o_cq], ((0, 0), (0, LANES - GLA_GATE_RANK)))
    w_kr = jnp.pad(wi[:, o_kr:o_ga], ((0, 0), (MLA_NOPE, LANES - MLA_QDIM)))
    w_all = jnp.concatenate([wi[:, :o_a], w_a, wi[:, o_cq:o_kr], w_kr, wi[:, o_ga:]], axis=1).astype(_bf16)
    assert w_all.shape == (d, _W_COLS)
    w2p = jnp.pad(gla_gate_w2[0], ((0, LANES - GLA_GATE_RANK), (0, 0))).astype(_bf16)
    wuqt = _pad_heads(mla_w_uq[0], MLA_QDIM).T.astype(_bf16)
    wuk = _pad_heads(mla_w_uk[0], MLA_NOPE).astype(_bf16)
    wuvt = mla_w_uv[0].T.astype(_bf16)
    cos_t, sin_lo, sin_hi, cos_tr, sin_tr = _rope_tables(lp)
    blk = np.arange(TILE)
    tril_chunks = jnp.asarray(
        ((blk[:, None] >= blk[None, :]) & (blk[:, None] // GLA_CHUNK == blk[None, :] // GLA_CHUNK)),
        dtype=_bf16)

    pad_map = lambda b, t: (b, t, 0)
    real_map = lambda b, t: (b, jnp.maximum(t - 1, 0), 0)
    tab_spec = pl.BlockSpec((TILE, LANES), lambda b, t: (t, 0))
    tabt_spec = pl.BlockSpec((MLA_ROPE // 2, TILE), lambda b, t: (0, t))
    real_map_t = lambda b, t: (b, 0, jnp.maximum(t - 1, 0))
    out_shapes = (
        jax.ShapeDtypeStruct((bsz, seq, d), _f32),
        jax.ShapeDtypeStruct((bsz, lp, GLA_QK), _bf16),
        jax.ShapeDtypeStruct((bsz, lp, GLA_QK), _bf16),
        jax.ShapeDtypeStruct((bsz, lp, GLA_QK), _bf16),
        jax.ShapeDtypeStruct((bsz, lp, GLA_VW), _bf16),
        jax.ShapeDtypeStruct((bsz * nt, TILE // GLA_CHUNK, GLA_QK), _f32),
        jax.ShapeDtypeStruct((bsz, seq, GLA_VW), _bf16),
        jax.ShapeDtypeStruct((bsz, MLA_HEADS * HEAD_PAD, seq), _bf16),
        jax.ShapeDtypeStruct((bsz, lp, MLA_HEADS * HEAD_PAD), _bf16),
        jax.ShapeDtypeStruct((bsz, MLA_HEADS * V_AUG, lp), _bf16),
        jax.ShapeDtypeStruct((bsz, seq, d), _bf16),
        jax.ShapeDtypeStruct((bsz, seq, d), _bf16),
    )
    out_specs = (
        pl.BlockSpec((1, TILE, d), real_map),
        pl.BlockSpec((1, TILE, GLA_QK), pad_map),
        pl.BlockSpec((1, TILE, GLA_QK), pad_map),
        pl.BlockSpec((1, TILE, GLA_QK), pad_map),
        pl.BlockSpec((1, TILE, GLA_VW), pad_map),
        pl.BlockSpec((1, TILE // GLA_CHUNK, GLA_QK), lambda b, t: (b * nt + t, 0, 0)),
        pl.BlockSpec((1, TILE, GLA_VW), real_map),
        pl.BlockSpec((1, MLA_HEADS * HEAD_PAD, TILE), real_map_t),
        pl.BlockSpec((1, TILE, MLA_HEADS * HEAD_PAD), pad_map),
        pl.BlockSpec((1, MLA_HEADS * V_AUG, TILE), lambda b, t: (b, 0, t)),
        pl.BlockSpec((1, TILE, d), real_map),
        pl.BlockSpec((1, TILE, d), real_map),
    )
    (s_emb, qt, kt, ke, gv, dec, sr, qm, km, vm, gate_a, gate_b) = pl.pallas_call(
        _inproj_kernel,
        grid=(bsz, nt),
        in_specs=[
            pl.BlockSpec((1, TILE, d), real_map),
            _const_spec((TILE, d)),
            _const_spec((1, d)), _const_spec((1, d)),
            _const_spec((d, _W_COLS)),
            _const_spec((LANES, GLA_QK)), _const_spec((1, GLA_QK)),
            _const_spec((1, MLA_Q_RANK)), _const_spec((MLA_HEADS * HEAD_PAD, MLA_Q_RANK)),
            _const_spec((1, MLA_KV_RANK)), _const_spec((MLA_KV_RANK, MLA_HEADS * HEAD_PAD)),
            _const_spec((MLA_HEADS * MLA_DV, MLA_KV_RANK)),
            tab_spec, tab_spec, tab_spec, tabt_spec, tabt_spec,
            _const_spec((TILE, TILE)),
        ],
        out_specs=out_specs,
        out_shape=out_shapes,
        compiler_params=pltpu.CompilerParams(
            dimension_semantics=("arbitrary", "arbitrary"), vmem_limit_bytes=VMEM_LIMIT),
        name="inproj",
    )(x, head_tile, row2(ln_emb_g), row2(ln_emb_b), w_all, w2p, row2(gla_gate_b[0]),
      row2(mla_q_norm_g[0]), wuqt, row2(mla_kv_norm_g[0]), wuk, wuvt, cos_t, sin_lo, sin_hi,
      cos_tr, sin_tr, tril_chunks)

    o_gla = pl.pallas_call(
        _gla_kernel,
        grid=(bsz, nt),
        in_specs=[
            pl.BlockSpec((1, TILE, GLA_QK), pad_map),
            pl.BlockSpec((1, TILE, GLA_QK), pad_map),
            pl.BlockSpec((1, TILE, GLA_QK), pad_map),
            pl.BlockSpec((1, TILE, GLA_VW), pad_map),
            pl.BlockSpec((1, TILE // GLA_CHUNK, GLA_QK), lambda b, t: (b * nt + t, 0, 0)),
            pl.BlockSpec((1, TILE, GLA_VW), real_map),
            _const_spec((1, GLA_DV)),
        ],
        out_specs=pl.BlockSpec((1, TILE, GLA_VW), real_map),
        out_shape=jax.ShapeDtypeStruct((bsz, seq, GLA_VW), _bf16),
        scratch_shapes=[pltpu.VMEM((GLA_HEADS, GLA_DV, GLA_DK), _f32)],
        compiler_params=pltpu.CompilerParams(
            dimension_semantics=("arbitrary", "arbitrary"), vmem_limit_bytes=VMEM_LIMIT),
        name="gla",
    )(qt, kt, ke, gv, dec, sr, row2(gla_norm_g[0]))

    pair = 2 * HEAD_PAD
    o_mla = pl.pallas_call(
        _mla_kernel,
        grid=(bsz, MLA_HEADS // 2, seq // Q_TILE),
        in_specs=[
            pl.BlockSpec((1, pair, Q_TILE), lambda b, hp, i: (b, hp, i)),
            pl.BlockSpec((1, lp, pair), lambda b, hp, i: (b, 0, hp)),
            pl.BlockSpec((1, 2 * V_AUG, lp), lambda b, hp, i: (b, hp, 0)),
        ],
        out_specs=pl.BlockSpec((1, Q_TILE, 2 * MLA_DV), lambda b, hp, i: (b, i, hp)),
        out_shape=jax.ShapeDtypeStruct((bsz, seq, MLA_HEADS * MLA_DV), _bf16),
        scratch_shapes=[pltpu.VMEM((MLA_PAIR, V_AUG, Q_TILE), _f32),
                        pltpu.VMEM((MLA_PAIR, 1, LANES), _f32)],
        compiler_params=pltpu.CompilerParams(
            dimension_semantics=("arbitrary", "arbitrary", "arbitrary"),
            vmem_limit_bytes=VMEM_LIMIT),
        name="mla",
    )(qm, km, vm)

    rw = jnp.concatenate([router_group_w[0], router_expert_w[0]], axis=1)
    rw = jnp.pad(rw, ((0, 0), (0, LANES - rw.shape[1])))
    rwh = rw.astype(_bf16)
    rwl = (rw - rwh.astype(_f32)).astype(_bf16)
    rb = jnp.concatenate([router_group_b[0], router_expert_b[0]])
    rb = jnp.pad(rb, (0, LANES - rb.shape[0])).reshape(1, LANES)
    mi = np.arange(ROUTE_BLOCK)
    tril_strict = jnp.asarray(mi[:, None] > mi[None, :], dtype=_bf16)
    flat = lambda a: a.reshape(ntok, a.shape[-1])
    tok_spec = lambda w: pl.BlockSpec((MERGE_TILE, w), lambda g: (g, 0))
    tt_spec = lambda n, index_map: pl.BlockSpec((n * SUBLANES, LANES), index_map)
    s2, info, info_t, cnt = pl.pallas_call(
        _merge_kernel,
        grid=(ntok // MERGE_TILE,),
        in_specs=[tok_spec(d), tok_spec(d), tok_spec(d), tok_spec(d), tok_spec(d),
                  _const_spec((d, d)), _const_spec((d, d)), _const_spec((d, d)),
                  _const_spec((1, d)), _const_spec((1, d)),
                  _const_spec((d, LANES)), _const_spec((d, LANES)), _const_spec((1, LANES)),
                  _const_spec((ROUTE_BLOCK, ROUTE_BLOCK))],
        out_specs=(tt_spec(MERGE_TILE, lambda g: (g, 0)), tok_spec(LANES),
                   pl.BlockSpec((SUBLANES, MERGE_TILE), lambda g: (0, g)), _const_spec((8, LANES))),
        out_shape=(jax.ShapeDtypeStruct((ntok * SUBLANES, LANES), _f32),
                   jax.ShapeDtypeStruct((ntok, LANES), _f32),
                   jax.ShapeDtypeStruct((SUBLANES, ntok), _f32),
                   jax.ShapeDtypeStruct((8, LANES), _f32)),
        scratch_shapes=[pltpu.VMEM((8, LANES), _f32)],
        compiler_params=pltpu.CompilerParams(
            dimension_semantics=("arbitrary",), vmem_limit_bytes=VMEM_LIMIT),
        name="merge_router",
    )(flat(o_gla), flat(o_mla), flat(gate_a), flat(gate_b), flat(s_emb),
      w_branch_gla[0].astype(_bf16), w_branch_mla[0].astype(_bf16), w_out[0].astype(_bf16),
      row2(ln_mix_g[0]), row2(ln_mix_b[0]), rwh, rwl, rb, tril_strict)

    n_tiles = (2 * ntok + N_EXPERTS * (EXPERT_TILE - 1)) // EXPERT_TILE
    n_rows = n_tiles * EXPERT_TILE
    e_idx = info_t[0:2].astype(jnp.int32)
    rank = info_t[4:6].astype(jnp.int32)
    counts = cnt[0, :N_EXPERTS].astype(jnp.int32)
    padded = ((counts + EXPERT_TILE - 1) // EXPERT_TILE) * EXPERT_TILE
    ends = jnp.cumsum(padded)
    starts = ends - padded
    expert_ids = jnp.arange(N_EXPERTS, dtype=jnp.int32)[:, None, None]
    start_of = jnp.sum(jnp.where(e_idx[None] == expert_ids, starts[:, None, None], 0), axis=0)
    pos = (start_of + rank).reshape(-1)
    tile_start = jnp.arange(n_tiles, dtype=jnp.int32) * EXPERT_TILE
    tile_expert = jnp.minimum(
        jnp.sum((ends[None, :] <= tile_start[:, None]).astype(jnp.int32), axis=1), N_EXPERTS - 1)
    n_used = (ends[-1:] // EXPERT_TILE).astype(jnp.int32)
    zero_row = jnp.where(padded > 0, ends - EXPERT_TILE, -1).astype(jnp.int32)

    any_spec = pl.BlockSpec(memory_space=pl.ANY)
    xs = pl.pallas_call(
        _dispatch_kernel,
        grid_spec=pltpu.PrefetchScalarGridSpec(
            num_scalar_prefetch=3,
            grid=(ntok // DISPATCH_TILE,),
            in_specs=[tt_spec(DISPATCH_TILE, lambda g, p, z, nu: (g, 0))],
            out_specs=any_spec,
            scratch_shapes=[pltpu.VMEM((EXPERT_TILE * SUBLANES, LANES), _f32),
                            pltpu.SemaphoreType.DMA((2,))],
        ),
        out_shape=jax.ShapeDtypeStruct((n_rows * SUBLANES, LANES), _f32),
        compiler_params=pltpu.CompilerParams(
            dimension_semantics=("arbitrary",), vmem_limit_bytes=VMEM_LIMIT, has_side_effects=True),
        name="dispatch",
    )(pos, zero_row, n_used, s2)

    ff = EXPERT_FF
    wg = expert_w_gate[0].reshape(N_EXPERTS, d, ff)
    wu = expert_w_up[0].reshape(N_EXPERTS, d, ff)
    wd = expert_w_down[0].reshape(N_EXPERTS, ff, d)
    ys = pl.pallas_call(
        _expert_kernel,
        grid_spec=pltpu.PrefetchScalarGridSpec(
            num_scalar_prefetch=2,
            grid=(n_tiles,),
            in_specs=[
                tt_spec(EXPERT_TILE, lambda u, te, nu: (jnp.minimum(u, nu[0] - 1), 0)),
                pl.BlockSpec((1, d, ff), lambda u, te, nu: (te[u], 0, 0)),
                pl.BlockSpec((1, d, ff), lambda u, te, nu: (te[u], 0, 0)),
                pl.BlockSpec((1, ff, d), lambda u, te, nu: (te[u], 0, 0)),
            ],
            out_specs=tt_spec(EXPERT_TILE, lambda u, te, nu: (u, 0)),
            scratch_shapes=[pltpu.VMEM((d, ff), _bf16), pltpu.VMEM((d, ff), _bf16),
                            pltpu.VMEM((ff, d), _bf16)],
        ),
        out_shape=jax.ShapeDtypeStruct((n_rows * SUBLANES, LANES), _f32),
        compiler_params=pltpu.CompilerParams(
            dimension_semantics=("arbitrary",), vmem_limit_bytes=VMEM_LIMIT),
        name="experts",
    )(tile_expert, n_used, xs, wg, wu, wd)

    out = pl.pallas_call(
        _combine_kernel,
        grid_spec=pltpu.PrefetchScalarGridSpec(
            num_scalar_prefetch=1,
            grid=(ntok // COMBINE_TILE,),
            in_specs=[
                tt_spec(COMBINE_TILE, lambda g, p: (g, 0)),
                pl.BlockSpec((COMBINE_TILE, LANES), lambda g, p: (g, 0)),
                any_spec,
                pl.BlockSpec((1, d), lambda g, p: (0, 0)),
                pl.BlockSpec((1, d), lambda g, p: (0, 0)),
            ],
            out_specs=pl.BlockSpec((COMBINE_TILE, d), lambda g, p: (g, 0)),
            scratch_shapes=[pltpu.VMEM((2, 2, COMBINE_TILE * SUBLANES, LANES), _f32),
                            pltpu.SemaphoreType.DMA((2,))],
        ),
        out_shape=jax.ShapeDtypeStruct((ntok, d), _f32),
        compiler_params=pltpu.CompilerParams(
            dimension_semantics=("arbitrary",), vmem_limit_bytes=VMEM_LIMIT),
        name="combine_ln",
    )(pos, s2, info, ys, row2(ln_ffn_g[0]), row2(ln_ffn_b[0]))
    return out.reshape(bsz, seq, d)
```

```python
import jax
import jax.numpy as jnp
import numpy as np
from jax import lax
from jax.experimental import pallas as pl
from jax.experimental.pallas import tpu as pltpu

D_MODEL = 1024
N_META = 16
GLA_HEADS = 4
GLA_DK = 128
GLA_DV = 256
GLA_QK = GLA_HEADS * GLA_DK
GLA_VW = GLA_HEADS * GLA_DV
GLA_GATE_RANK = 16
GLA_GATE_TAU = 16.0
GLA_CHUNK = 64
MLA_HEADS = 16
MLA_Q_RANK = 384
MLA_KV_RANK = 256
MLA_NOPE = 64
MLA_ROPE = 32
MLA_DV = 64
MLA_QDIM = MLA_NOPE + MLA_ROPE
ROPE_BASE = 10000.0
N_GROUPS = 4
EXPERTS_PER_GROUP = 8
N_EXPERTS = N_GROUPS * EXPERTS_PER_GROUP
EXPERT_FF = 256
DEPTH = 1
ALPHA = (2.0 * DEPTH) ** 0.25
LN_EPS = 1e-5
RMS_EPS = 1e-6

LANES = 128
SUBLANES = 8
MXU_DIM = 256
TILE = 256
PADL = TILE - N_META
HEAD_PAD = LANES
BF16_ROWS = 16
KV_TILE = 512
Q_TILE = 2 * KV_TILE
V_AUG = MLA_DV + BF16_ROWS
MERGE_TILE = 512
ROUTE_BLOCK = 256
EXPERT_TILE = 256
DISPATCH_TILE = 512
COMBINE_TILE = 256
COMBINE_SLOTS = 3
NEG = -1e30
LOG2E = 1.4426950408889634
BOUND_SLACK = 1.02
MIN_SOFTMAX_SUM = 2.0 ** -100
VMEM_LIMIT = 56 * 1024 * 1024

_C_Q, _C_K, _C_V, _C_R = 0, 512, 1024, 2048
_C_A = 3072
_C_CQ = _C_A + LANES
_C_CKV = _C_CQ + MLA_Q_RANK
_C_KR = _C_CKV + MLA_KV_RANK
_C_GA = _C_KR + LANES
_C_GB = _C_GA + D_MODEL
_W_COLS = _C_GB + D_MODEL

_f32 = jnp.float32
_bf16 = jnp.bfloat16


def _dot(a, b):
    return jnp.dot(a, b, preferred_element_type=_f32)


def _dot_nt(a, b):
    return lax.dot_general(a, b, (((1,), (1,)), ((), ())), preferred_element_type=_f32)


def _dot_tn(a, b):
    return lax.dot_general(a, b, (((0,), (0,)), ((), ())), preferred_element_type=_f32)


def _layer_norm(x, g, b):
    mu = jnp.mean(x, axis=-1, keepdims=True)
    xc = x - mu
    var = jnp.mean(xc * xc, axis=-1, keepdims=True)
    return xc * lax.rsqrt(var + LN_EPS) * g + b


def _rms_norm(x, g):
    ms = jnp.mean(x * x, axis=-1, keepdims=True)
    return x * lax.rsqrt(ms + RMS_EPS) * g


def _sigmoid(x):
    return 1.0 / (1.0 + jnp.exp(-x))


def _tt_load(ref, n, lead=()):
    return jnp.concatenate(
        [ref[lead + (pl.ds(a, n, stride=SUBLANES), slice(None))] for a in range(SUBLANES)], axis=1)


def _tt_store(ref, x):
    n = x.shape[0]
    for a in range(SUBLANES):
        ref[pl.ds(a, n, stride=SUBLANES), :] = x[:, a * LANES:(a + 1) * LANES]


def _tt_rows(tok):
    return pl.ds(pl.multiple_of(tok * SUBLANES, SUBLANES), SUBLANES)


def _rope(x, cos, sin_lo, sin_hi):
    half = MLA_ROPE // 2
    from_hi = pltpu.roll(x, LANES - half, 1)
    from_lo = pltpu.roll(x, half, 1)
    return x * cos + from_hi * sin_lo + from_lo * sin_hi


def _inproj_kernel(x_ref, head_ref, lng_ref, lnb_ref, w_ref, w2_ref, gb_ref, qg_ref, wuqt_ref, kvg_ref,
                   wuk_ref, wuvt_ref, cos_ref, sl_ref, sh_ref, cost_ref, sint_ref, tril_ref,
                   s_ref, qt_ref, kt_ref, ke_ref, gv_ref, dec_ref, sr_ref, qm_ref, km_ref,
                   vm_ref, ga_ref, gbt_ref):
    t = pl.program_id(1)
    x_in = jnp.where(t == 0, head_ref[...], x_ref[0])
    sn = _layer_norm(x_in, lng_ref[...], lnb_ref[...])
    s_ref[0] = sn
    snb = sn.astype(_bf16)
    row = t * TILE + lax.broadcasted_iota(jnp.int32, (TILE, 1), 0)
    valid = row >= PADL

    proj = lambda c0, width: _dot(snb, w_ref[:, c0:c0 + width])
    a_lr = proj(_C_A, LANES)
    cq = proj(_C_CQ, MLA_Q_RANK)
    ckv = proj(_C_CKV, MLA_KV_RANK)
    kr_raw = proj(_C_KR, LANES)
    gq = proj(_C_Q, GLA_QK)
    gk = proj(_C_K, GLA_QK)
    gv_ref[0] = jnp.where(valid, proj(_C_V, GLA_VW), 0.0).astype(_bf16)
    r = proj(_C_R, GLA_VW)
    sr_ref[0] = (r * _sigmoid(r)).astype(_bf16)
    ga_ref[0] = _sigmoid(proj(_C_GA, D_MODEL)).astype(_bf16)
    gbt_ref[0] = _sigmoid(proj(_C_GB, D_MODEL)).astype(_bf16)

    z = _dot(a_lr.astype(_bf16), w2_ref[...]) + gb_ref[...]
    cqn = _rms_norm(cq, qg_ref[...]).astype(_bf16)
    ckvn = _rms_norm(ckv, kvg_ref[...]).astype(_bf16)
    qft = _dot_nt(wuqt_ref[...], cqn)
    kf = _dot(ckvn, wuk_ref[...])
    vt = _dot_nt(wuvt_ref[...], ckvn)

    la = (jnp.minimum(z, 0.0) - jnp.log1p(jnp.exp(-jnp.abs(z)))) * (1.0 / GLA_GATE_TAU)
    la = jnp.where(valid, la, 0.0)
    hi = la.astype(_bf16)
    r1 = la - hi.astype(_f32)
    mid = r1.astype(_bf16)
    lo = (r1 - mid.astype(_f32)).astype(_bf16)
    tril = tril_ref[...]
    bc = _dot(tril, hi) + _dot(tril, mid) + _dot(tril, lo)
    n_chunks = TILE // GLA_CHUNK
    lasts = [bc[c * GLA_CHUNK + GLA_CHUNK - 1:(c + 1) * GLA_CHUNK, :] for c in range(n_chunks)]
    for c in range(n_chunks):
        dec_ref[0, c:c + 1, :] = jnp.exp(lasts[c])
    b_last = jnp.concatenate(
        [jnp.broadcast_to(l, (GLA_CHUNK, GLA_QK)) for l in lasts], axis=0)
    gk = jnp.where(valid, gk, 0.0)
    qt_ref[0] = (gq * (GLA_DK ** -0.5) * jnp.exp(bc)).astype(_bf16)
    kt_ref[0] = (gk * jnp.exp(-bc)).astype(_bf16)
    ke_ref[0] = (gk * jnp.exp(b_last - bc)).astype(_bf16)

    cos = cos_ref[...]
    sl = sl_ref[...]
    sh = sh_ref[...]
    scale = (MLA_QDIM ** -0.5) * LOG2E
    cost = cost_ref[...]
    sint = sint_ref[...]
    half = MLA_ROPE // 2
    for h in range(MLA_HEADS):
        base = h * HEAD_PAD
        x1 = qft[base + MLA_NOPE:base + MLA_NOPE + half]
        x2 = qft[base + MLA_NOPE + half:base + MLA_QDIM]
        qm_ref[0, base:base + MLA_NOPE] = (qft[base:base + MLA_NOPE] * scale).astype(_bf16)
        qm_ref[0, base + MLA_NOPE:base + MLA_NOPE + half] = ((x1 * cost - x2 * sint) * scale).astype(_bf16)
        qm_ref[0, base + MLA_NOPE + half:base + MLA_QDIM] = ((x1 * sint + x2 * cost) * scale).astype(_bf16)
        qm_ref[0, base + MLA_QDIM:base + HEAD_PAD] = jnp.zeros((HEAD_PAD - MLA_QDIM, TILE), _bf16)
    kr = _rope(kr_raw, cos, sl, sh)
    for h in range(MLA_HEADS):
        km_ref[0, :, h * HEAD_PAD:(h + 1) * HEAD_PAD] = (
            kf[:, h * HEAD_PAD:(h + 1) * HEAD_PAD] + kr).astype(_bf16)
    for h in range(MLA_HEADS):
        vm_ref[0, h * V_AUG:h * V_AUG + MLA_DV] = vt[h * MLA_DV:(h + 1) * MLA_DV].astype(_bf16)
        vm_ref[0, h * V_AUG + MLA_DV:(h + 1) * V_AUG] = jnp.ones((V_AUG - MLA_DV, TILE), _bf16)


def _gla_kernel(qt_ref, kt_ref, ke_ref, gv_ref, dec_ref, sr_ref, ng_ref, o_ref, st_ref):
    t = pl.program_id(1)

    @pl.when(t == 0)
    def _():
        st_ref[...] = jnp.zeros_like(st_ref)

    ri = lax.broadcasted_iota(jnp.int32, (TILE, TILE), 0)
    ci = lax.broadcasted_iota(jnp.int32, (TILE, TILE), 1)
    visible = (ri >= ci) & (ri // GLA_CHUNK == ci // GLA_CHUNK)
    ng = ng_ref[...]
    n_chunks = TILE // GLA_CHUNK
    chunk_rows = [slice(c * GLA_CHUNK, (c + 1) * GLA_CHUNK) for c in range(n_chunks)]
    k_cols = [slice(h * GLA_DK, (h + 1) * GLA_DK) for h in range(GLA_HEADS)]
    v_cols = [slice(h * GLA_DV, (h + 1) * GLA_DV) for h in range(GLA_HEADS)]
    att = [jnp.where(visible, _dot_nt(qt_ref[0, :, k_cols[h]], kt_ref[0, :, k_cols[h]]), 0.0)
           for h in range(GLA_HEADS)]
    upd = [[_dot_tn(gv_ref[0, rows, v_cols[h]], ke_ref[0, rows, k_cols[h]]) for rows in chunk_rows]
           for h in range(GLA_HEADS)]
    o_intra = [_dot(att[h].astype(_bf16), gv_ref[0, :, v_cols[h]]) for h in range(GLA_HEADS)]
    for h in range(GLA_HEADS):
        st = st_ref[h]
        for c, rows in enumerate(chunk_rows):
            o = o_intra[h][rows] + _dot_nt(qt_ref[0, rows, k_cols[h]], st.astype(_bf16))
            st = st * dec_ref[0, c:c + 1, k_cols[h]] + upd[h][c]
            o = _rms_norm(o, ng) * sr_ref[0, rows, v_cols[h]].astype(_f32)
            o_ref[0, rows, v_cols[h]] = o.astype(_bf16)
        st_ref[h] = st


MLA_PAIR = 2


def _mla_tile_start(j):
    return pl.multiple_of(TILE + j * KV_TILE, TILE)


def _mla_finish(o_ref, acc_ref):
    outs = []
    for h in range(MLA_PAIR):
        a = acc_ref[h]
        outs.append(a[0:MLA_DV] / a[MLA_DV:MLA_DV + 1])
    o_ref[0] = jnp.concatenate(outs, axis=0).T.astype(_bf16)


def _mla_kernel(q_ref, k_ref, v_ref, o_ref, acc_ref, knorm_ref):
    i = pl.program_id(2)
    heads = MLA_PAIR
    ones = jnp.ones((HEAD_PAD, LANES), _bf16)

    @pl.when(i == 0)
    def _():
        for h in range(heads):
            kk = k_ref[0, :, h * HEAD_PAD:(h + 1) * HEAD_PAD].astype(_f32)
            hi = (kk * kk).astype(_bf16)
            knorm_ref[h] = jnp.max(_dot(hi, ones), axis=0, keepdims=True)

    half = slice(KV_TILE, Q_TILE)
    q_t = [q_ref[0, h * HEAD_PAD:(h + 1) * HEAD_PAD, :] for h in range(heads)]
    q_hi = [q_ref[0, h * HEAD_PAD:(h + 1) * HEAD_PAD, half] for h in range(heads)]

    def score_bound(h, q):
        qq = q.astype(_f32)
        qn2 = jnp.sum(qq * qq, axis=0, keepdims=True)
        return jnp.sqrt(qn2 * knorm_ref[h][:, 0:1]) * BOUND_SLACK

    bound = [score_bound(h, q_t[h]) for h in range(heads)]
    bound_hi = [score_bound(h, q_hi[h]) for h in range(heads)]

    def keys(j, h):
        return k_ref[0, pl.ds(_mla_tile_start(j), KV_TILE), h * HEAD_PAD:(h + 1) * HEAD_PAD]

    def weighted(j, h, p):
        return _dot(v_ref[0, h * V_AUG:(h + 1) * V_AUG, pl.ds(_mla_tile_start(j), KV_TILE)], p)

    k_row = lax.broadcasted_iota(jnp.int32, (KV_TILE, Q_TILE), 0)
    q_col = lax.broadcasted_iota(jnp.int32, (KV_TILE, Q_TILE), 1)
    k_row_sq = lax.broadcasted_iota(jnp.int32, (KV_TILE, KV_TILE), 0)
    q_col_sq = lax.broadcasted_iota(jnp.int32, (KV_TILE, KV_TILE), 1)
    s_meta = [_dot(k_ref[0, PADL:TILE, h * HEAD_PAD:(h + 1) * HEAD_PAD], q_t[h]) for h in range(heads)]
    s_lo = [jnp.where(k_row <= q_col, _dot(keys(2 * i, h), q_t[h]), NEG) for h in range(heads)]
    s_hi = [jnp.where(k_row_sq <= q_col_sq, _dot(keys(2 * i + 1, h), q_hi[h]), NEG)
            for h in range(heads)]
    for h in range(heads):
        p_meta = jnp.concatenate([jnp.zeros((PADL, Q_TILE), _bf16),
                                  jnp.exp2(s_meta[h] - bound[h]).astype(_bf16)], axis=0)
        acc_ref[h] = (_dot(v_ref[0, h * V_AUG:(h + 1) * V_AUG, 0:TILE], p_meta)
                      + weighted(2 * i, h, jnp.exp2(s_lo[h] - bound[h]).astype(_bf16)))
        acc_ref[h, :, half] += weighted(2 * i + 1, h, jnp.exp2(s_hi[h] - bound_hi[h]).astype(_bf16))

    def body(jj, c):
        tiles = (2 * jj, 2 * jj + 1)
        s = [[_dot(keys(j, h), q_t[h]) for h in range(heads)] for j in tiles]
        for h in range(heads):
            acc_ref[h] += sum(weighted(j, h, jnp.exp2(s[n][h] - bound[h]).astype(_bf16))
                              for n, j in enumerate(tiles))
        return c

    lax.fori_loop(0, i, body, 0)
    _mla_finish(o_ref, acc_ref)

    l_min = jnp.minimum(jnp.min(acc_ref[0, MLA_DV:MLA_DV + 1, :]), jnp.min(acc_ref[1, MLA_DV:MLA_DV + 1, :]))

    @pl.when(jnp.logical_not(l_min >= MIN_SOFTMAX_SUM))
    def _():
        _mla_exact(q_ref, k_ref, v_ref, o_ref, acc_ref)


def _mla_exact(q_ref, k_ref, v_ref, o_ref, acc_ref):
    i = pl.program_id(2)
    heads = MLA_PAIR
    q_t = [q_ref[0, h * HEAD_PAD:(h + 1) * HEAD_PAD, :] for h in range(heads)]

    ms = []
    for h in range(heads):
        kb = k_ref[0, PADL:TILE, h * HEAD_PAD:(h + 1) * HEAD_PAD]
        s = _dot(kb, q_t[h])
        m0 = jnp.max(s, axis=0, keepdims=True)
        p = jnp.concatenate([jnp.zeros((PADL, Q_TILE), _bf16), jnp.exp2(s - m0).astype(_bf16)], axis=0)
        acc_ref[h] = _dot(v_ref[0, h * V_AUG:(h + 1) * V_AUG, 0:TILE], p)
        ms.append(m0)

    k_row = lax.broadcasted_iota(jnp.int32, (KV_TILE, Q_TILE), 0)
    q_col = lax.broadcasted_iota(jnp.int32, (KV_TILE, Q_TILE), 1)

    def body(j, ms):
        visible = (j - 2 * i) * KV_TILE + k_row <= q_col
        out = []
        for h in range(heads):
            s = _dot(k_ref[0, pl.ds(_mla_tile_start(j), KV_TILE), h * HEAD_PAD:(h + 1) * HEAD_PAD],
                     q_t[h])
            s = jnp.where(visible, s, NEG)
            vb = v_ref[0, h * V_AUG:(h + 1) * V_AUG, pl.ds(_mla_tile_start(j), KV_TILE)]
            m_new = jnp.maximum(ms[h], jnp.max(s, axis=0, keepdims=True))
            alpha = jnp.exp2(ms[h] - m_new)
            acc_ref[h] = alpha * acc_ref[h] + _dot(vb, jnp.exp2(s - m_new).astype(_bf16))
            out.append(m_new)
        return tuple(out)

    lax.fori_loop(0, 2 * i + 2, body, tuple(ms))
    _mla_finish(o_ref, acc_ref)


def _merge_kernel(og_ref, om_ref, ga_ref, gbt_ref, s_ref, wbg_ref, wbm_ref, wo_ref, lng_ref,
                  lnb_ref, rwh_ref, rwl_ref, rb_ref, tril_ref,
                  s2_ref, info_ref, infot_ref, cnt_ref, carry_ref):
    step = pl.program_id(0)

    @pl.when(step == 0)
    def _():
        carry_ref[...] = jnp.zeros_like(carry_ref)

    n_blk = MERGE_TILE // ROUTE_BLOCK
    blocks = [slice(i * ROUTE_BLOCK, (i + 1) * ROUTE_BLOCK) for i in range(n_blk)]
    d = wo_ref.shape[0]
    col_blocks = [slice(c, c + MXU_DIM) for c in range(0, d, MXU_DIM)]
    merged = []
    for rows in blocks:
        merged.append(jnp.concatenate(
            [(ga_ref[rows, cols].astype(_f32) * _dot(og_ref[rows, :], wbg_ref[:, cols])
              + gbt_ref[rows, cols].astype(_f32) * _dot(om_ref[rows, :], wbm_ref[:, cols])
              ).astype(_bf16) for cols in col_blocks], axis=1))
    s2 = []
    for i, rows in enumerate(blocks):
        y = ALPHA * s_ref[rows, :] + _dot(merged[i], wo_ref[...])
        s2.append(_layer_norm(y, lng_ref[...], lnb_ref[...]))
    logits = []
    for i in range(n_blk):
        xh = s2[i].astype(_bf16)
        xl = (s2[i] - xh.astype(_f32)).astype(_bf16)
        logits.append(_dot(xh, rwh_ref[...]) + _dot(xl, rwh_ref[...]) + _dot(xh, rwl_ref[...])
                      + rb_ref[...])
    _tt_store(s2_ref, jnp.concatenate(s2, axis=0))

    lane = lax.broadcasted_iota(jnp.int32, (ROUTE_BLOCK, LANES), 1)
    is_g = lane < N_GROUPS
    carry = carry_ref[0:1, :]
    infos = []
    for i in range(n_blk):
        gl = jnp.where(is_g, logits[i], NEG)
        gmax = jnp.max(gl, axis=-1, keepdims=True)
        gidx = jnp.min(jnp.where(gl == gmax, lane, LANES), axis=-1, keepdims=True)
        p_g = 1.0 / jnp.sum(jnp.where(is_g, jnp.exp(gl - gmax), 0.0), axis=-1, keepdims=True)
        lo = N_GROUPS + EXPERTS_PER_GROUP * gidx
        el = jnp.where((lane >= lo) & (lane < lo + EXPERTS_PER_GROUP), logits[i], NEG)
        v1 = jnp.max(el, axis=-1, keepdims=True)
        i1 = jnp.min(jnp.where(el == v1, lane, LANES), axis=-1, keepdims=True)
        el2 = jnp.where(lane == i1, NEG, el)
        v2 = jnp.max(el2, axis=-1, keepdims=True)
        i2 = jnp.min(jnp.where(el2 == v2, lane, LANES), axis=-1, keepdims=True)
        tt = jnp.exp(v2 - v1)
        p1 = 1.0 / (1.0 + tt)
        p2 = tt / (1.0 + tt)
        e1 = i1 - N_GROUPS
        e2 = i2 - N_GROUPS
        hit1 = lane == e1
        hit2 = lane == e2
        onehot = jnp.where(hit1 | hit2, 1.0, 0.0)
        before = _dot(tril_ref[...], onehot.astype(_bf16)) + carry
        r1 = jnp.sum(jnp.where(hit1, before, 0.0), axis=-1, keepdims=True)
        r2 = jnp.sum(jnp.where(hit2, before, 0.0), axis=-1, keepdims=True)
        carry = carry + jnp.sum(onehot, axis=0, keepdims=True)
        infos.append(jnp.where(lane == 0, e1.astype(_f32),
                     jnp.where(lane == 1, e2.astype(_f32),
                     jnp.where(lane == 2, p_g * p1,
                     jnp.where(lane == 3, p_g * p2,
                     jnp.where(lane == 4, r1,
                     jnp.where(lane == 5, r2, 0.0)))))))
    carry_ref[...] = jnp.broadcast_to(carry, carry_ref.shape)
    cnt_ref[...] = jnp.broadcast_to(carry, cnt_ref.shape)
    info = jnp.concatenate(infos, axis=0)
    info_ref[...] = info
    infot_ref[...] = info.T[0:SUBLANES]


def _dispatch_kernel(pos_ref, zrow_ref, nused_ref, s2_ref, xs_hbm, zero_ref, sems):
    g = pl.program_id(0)
    zero_sem = sems.at[1]
    row_sem = sems.at[0]

    tile_rows = EXPERT_TILE * SUBLANES

    def zero_copy(row):
        start = pl.multiple_of(row * SUBLANES, tile_rows)
        return pltpu.make_async_copy(zero_ref, xs_hbm.at[pl.ds(start, tile_rows)], zero_sem)

    @pl.when(g == 0)
    def _():
        zero_ref[...] = jnp.zeros_like(zero_ref)

        def start(e, c):
            @pl.when(zrow_ref[e] >= 0)
            def _():
                zero_copy(zrow_ref[e]).start()
            return c

        def wait(e, c):
            @pl.when(zrow_ref[e] >= 0)
            def _():
                zero_copy(0).wait()
            return c

        def start_tail(u, c):
            zero_copy(u * EXPERT_TILE).start()
            return c

        def wait_tail(u, c):
            zero_copy(0).wait()
            return c

        n_tiles = xs_hbm.shape[0] // tile_rows
        lax.fori_loop(0, N_EXPERTS, start, 0)
        lax.fori_loop(nused_ref[0], n_tiles, start_tail, 0)
        lax.fori_loop(0, N_EXPERTS, wait, 0)
        lax.fori_loop(nused_ref[0], n_tiles, wait_tail, 0)

    n_tok = pl.num_programs(0) * DISPATCH_TILE

    def issue(r, c):
        tok = g * DISPATCH_TILE + r
        for k in range(2):
            pltpu.make_async_copy(s2_ref.at[_tt_rows(r)],
                                  xs_hbm.at[_tt_rows(pos_ref[k * n_tok + tok])], row_sem).start(priority=k)
        return c

    lax.fori_loop(0, DISPATCH_TILE, issue, 0, unroll=8)
    for k in range(2):
        pltpu.make_async_copy(s2_ref, xs_hbm.at[pl.ds(0, DISPATCH_TILE * SUBLANES)], row_sem).wait()


def _expert_kernel(te_ref, nused_ref, x_ref, wg_ref, wu_ref, wd_ref, o_ref, wgb_ref, wub_ref, wdb_ref):
    u = pl.program_id(0)
    used = u < nused_ref[0]
    first_of_expert = jnp.logical_or(u == 0, te_ref[u] != te_ref[jnp.maximum(u - 1, 0)])

    @pl.when(jnp.logical_and(used, first_of_expert))
    def _():
        wgb_ref[...] = wg_ref[0].astype(_bf16)
        wub_ref[...] = wu_ref[0].astype(_bf16)
        wdb_ref[...] = wd_ref[0].astype(_bf16)

    @pl.when(used)
    def _():
        x = _tt_load(x_ref, EXPERT_TILE).astype(_bf16)
        a = _dot(x, wgb_ref[...])
        up = _dot(x, wub_ref[...])
        hid = a * _sigmoid(a) * up
        _tt_store(o_ref, _dot(hid.astype(_bf16), wdb_ref[...]))

    @pl.when(u >= nused_ref[0])
    def _():
        o_ref[...] = jnp.zeros_like(o_ref)


def _combine_kernel(pos_ref, s2_ref, info_ref, ys_hbm, lng_ref, lnb_ref, o_ref, buf_ref, sems):
    g = pl.program_id(0)
    last = pl.num_programs(0) - 1
    n_tok = pl.num_programs(0) * COMBINE_TILE

    def start_copies(tile, slot, r):
        tok = tile * COMBINE_TILE + r
        for k in range(2):
            pltpu.make_async_copy(ys_hbm.at[_tt_rows(pos_ref[k * n_tok + tok])],
                                  buf_ref.at[slot, k, _tt_rows(r)], sems.at[slot]).start(priority=k)

    def wait_slot(slot):
        for k in range(2):
            pltpu.make_async_copy(ys_hbm.at[pl.ds(0, COMBINE_TILE * SUBLANES)], buf_ref.at[slot, k],
                                  sems.at[slot]).wait()

    @pl.when(g == 0)
    def _():
        def body(r, c):
            for tile in range(COMBINE_SLOTS - 1):
                start_copies(tile, tile, r)
            return c

        lax.fori_loop(0, COMBINE_TILE, body, 0, unroll=8)

    slot = g % COMBINE_SLOTS
    wait_slot(slot)
    info = info_ref[...]
    y = (ALPHA * _tt_load(s2_ref, COMBINE_TILE)
         + info[:, 2:3] * _tt_load(buf_ref, COMBINE_TILE, (slot, 0))
         + info[:, 3:4] * _tt_load(buf_ref, COMBINE_TILE, (slot, 1)))
    o_ref[...] = _layer_norm(y, lng_ref[...], lnb_ref[...])

    ahead = COMBINE_SLOTS - 1
    ahead_slot = (g + ahead) % COMBINE_SLOTS
    for r in range(COMBINE_TILE):
        start_copies(jnp.minimum(g + ahead, last), ahead_slot, r)

    @pl.when(g == last)
    def _():
        for n in range(1, COMBINE_SLOTS):
            wait_slot((g + n) % COMBINE_SLOTS)


def _const_spec(shape):
    nd = len(shape)
    return pl.BlockSpec(shape, lambda *_: (0,) * nd)


def _rope_tables(lp):
    pos = jnp.maximum(jnp.arange(lp, dtype=_f32) - PADL, 0.0)
    inv_freq = ROPE_BASE ** (-jnp.arange(0, MLA_ROPE, 2, dtype=_f32) / MLA_ROPE)
    ang = pos[:, None] * inv_freq[None, :]
    cos, sin = jnp.cos(ang), jnp.sin(ang)
    half = MLA_ROPE // 2
    ones = jnp.ones((lp, MLA_NOPE), _f32)
    zeros_n = jnp.zeros((lp, MLA_NOPE), _f32)
    zeros_h = jnp.zeros((lp, half), _f32)
    tail1 = jnp.ones((lp, LANES - MLA_QDIM), _f32)
    tail0 = jnp.zeros((lp, LANES - MLA_QDIM), _f32)
    cos_t = jnp.concatenate([ones, cos, cos, tail1], axis=1)
    sin_lo = jnp.concatenate([zeros_n, -sin, zeros_h, tail0], axis=1)
    sin_hi = jnp.concatenate([zeros_n, zeros_h, sin, tail0], axis=1)
    return cos_t, sin_lo, sin_hi, cos.T, sin.T


def _pad_heads(w, width):
    k = w.shape[0]
    w = w.reshape(k, MLA_HEADS, width)
    w = jnp.pad(w, ((0, 0), (0, 0), (0, HEAD_PAD - width)))
    return w.reshape(k, MLA_HEADS * HEAD_PAD)


def kernel(x, meta_tokens, ln_emb_g, ln_emb_b, w_in, gla_gate_w2, gla_gate_b, gla_norm_g, mla_q_norm_g, mla_w_uq, mla_kv_norm_g, mla_w_uk, mla_w_uv, w_branch_gla, w_branch_mla, w_out, ln_mix_g, ln_mix_b, router_group_w, router_group_b, router_expert_w, router_expert_b, expert_w_gate, expert_w_up, expert_w_down, ln_ffn_g, ln_ffn_b):
    bsz, seq, d = x.shape
    assert d == D_MODEL and seq % Q_TILE == 0 and w_in.shape[0] == DEPTH == 1
    lp = PADL + N_META + seq
    nt = lp // TILE
    ntok = bsz * seq
    row2 = lambda v: v.reshape(1, -1).astype(_f32)

    head_tile = jnp.concatenate([jnp.zeros((PADL, d), _f32), meta_tokens.astype(_f32)], axis=0)
    wi = w_in[0]
    o_a = 2 * GLA_QK + 2 * GLA_VW
    o_cq = o_a + GLA_GATE_RANK
    o_ckv = o_cq + MLA_Q_RANK
    o_kr = o_ckv + MLA_KV_RANK
    o_ga = o_kr + MLA_ROPE
    w_a = jnp.pad(wi[:, o_a:o_cq], ((0, 0), (0, LANES - GLA_GATE_RANK)))
    w_kr = jnp.pad(wi[:, o_kr:o_ga], ((0, 0), (MLA_NOPE, LANES - MLA_QDIM)))
    w_all = jnp.concatenate([wi[:, :o_a], w_a, wi[:, o_cq:o_kr], w_kr, wi[:, o_ga:]], axis=1).astype(_bf16)
    assert w_all.shape == (d, _W_COLS)
    w2p = jnp.pad(gla_gate_w2[0], ((0, LANES - GLA_GATE_RANK), (0, 0))).astype(_bf16)
    wuqt = _pad_heads(mla_w_uq[0], MLA_QDIM).T.astype(_bf16)
    wuk = _pad_heads(mla_w_uk[0], MLA_NOPE).astype(_bf16)
    wuvt = mla_w_uv[0].T.astype(_bf16)
    cos_t, sin_lo, sin_hi, cos_tr, sin_tr = _rope_tables(lp)
    blk = np.arange(TILE)
    tril_chunks = jnp.asarray(
        ((blk[:, None] >= blk[None, :]) & (blk[:, None] // GLA_CHUNK == blk[None, :] // GLA_CHUNK)),
        dtype=_bf16)

    pad_map = lambda b, t: (b, t, 0)
    real_map = lambda b, t: (b, jnp.maximum(t - 1, 0), 0)
    real_map_t = lambda b, t: (b, 0, jnp.maximum(t - 1, 0))
    out_shapes = (
        jax.ShapeDtypeStruct((bsz, seq, d), _f32),
        jax.ShapeDtypeStruct((bsz, lp, GLA_QK), _bf16),
        jax.ShapeDtypeStruct((bsz, lp, GLA_QK), _bf16),
        jax.ShapeDtypeStruct((bsz, lp, GLA_QK), _bf16),
        jax.ShapeDtypeStruct((bsz, lp, GLA_VW), _bf16),
        jax.ShapeDtypeStruct((bsz * nt, TILE // GLA_CHUNK, GLA_QK), _f32),
        jax.ShapeDtypeStruct((bsz, seq, GLA_VW), _bf16),
        jax.ShapeDtypeStruct((bsz, MLA_HEADS * HEAD_PAD, seq), _bf16),
        jax.ShapeDtypeStruct((bsz, lp, MLA_HEADS * HEAD_PAD), _bf16),
        jax.ShapeDtypeStruct((bsz, MLA_HEADS * V_AUG, lp), _bf16),
        jax.ShapeDtypeStruct((bsz, seq, d), _bf16),
        jax.ShapeDtypeStruct((bsz, seq, d), _bf16),
    )
    out_specs = (
        pl.BlockSpec((1, TILE, d), real_map),
        pl.BlockSpec((1, TILE, GLA_QK), pad_map),
        pl.BlockSpec((1, TILE, GLA_QK), pad_map),
        pl.BlockSpec((1, TILE, GLA_QK), pad_map),
        pl.BlockSpec((1, TILE, GLA_VW), pad_map),
        pl.BlockSpec((1, TILE // GLA_CHUNK, GLA_QK), lambda b, t: (b * nt + t, 0, 0)),
        pl.BlockSpec((1, TILE, GLA_VW), real_map),
        pl.BlockSpec((1, MLA_HEADS * HEAD_PAD, TILE), real_map_t),
        pl.BlockSpec((1, TILE, MLA_HEADS * HEAD_PAD), pad_map),
        pl.BlockSpec((1, MLA_HEADS * V_AUG, TILE), lambda b, t: (b, 0, t)),
        pl.BlockSpec((1, TILE, d), real_map),
        pl.BlockSpec((1, TILE, d), real_map),
    )
    tab_spec = pl.BlockSpec((TILE, LANES), lambda b, t: (t, 0))
    tabt_spec = pl.BlockSpec((MLA_ROPE // 2, TILE), lambda b, t: (0, t))
    (s_emb, qt, kt, ke, gv, dec, sr, qm, km, vm, gate_a, gate_b) = pl.pallas_call(
        _inproj_kernel,
        grid=(bsz, nt),
        in_specs=[
            pl.BlockSpec((1, TILE, d), real_map),
            _const_spec((TILE, d)),
            _const_spec((1, d)), _const_spec((1, d)),
            _const_spec((d, _W_COLS)),
            _const_spec((LANES, GLA_QK)), _const_spec((1, GLA_QK)),
            _const_spec((1, MLA_Q_RANK)), _const_spec((MLA_HEADS * HEAD_PAD, MLA_Q_RANK)),
            _const_spec((1, MLA_KV_RANK)), _const_spec((MLA_KV_RANK, MLA_HEADS * HEAD_PAD)),
            _const_spec((MLA_HEADS * MLA_DV, MLA_KV_RANK)),
            tab_spec, tab_spec, tab_spec, tabt_spec, tabt_spec,
            _const_spec((TILE, TILE)),
        ],
        out_specs=out_specs,
        out_shape=out_shapes,
        compiler_params=pltpu.CompilerParams(
            dimension_semantics=("arbitrary", "arbitrary"), vmem_limit_bytes=VMEM_LIMIT),
        name="inproj",
    )(x, head_tile, row2(ln_emb_g), row2(ln_emb_b), w_all, w2p, row2(gla_gate_b[0]),
      row2(mla_q_norm_g[0]), wuqt, row2(mla_kv_norm_g[0]), wuk, wuvt, cos_t, sin_lo, sin_hi,
      cos_tr, sin_tr, tril_chunks)

    o_gla = pl.pallas_call(
        _gla_kernel,
        grid=(bsz, nt),
        in_specs=[
            pl.BlockSpec((1, TILE, GLA_QK), pad_map),
            pl.BlockSpec((1, TILE, GLA_QK), pad_map),
            pl.BlockSpec((1, TILE, GLA_QK), pad_map),
            pl.BlockSpec((1, TILE, GLA_VW), pad_map),
            pl.BlockSpec((1, TILE // GLA_CHUNK, GLA_QK), lambda b, t: (b * nt + t, 0, 0)),
            pl.BlockSpec((1, TILE, GLA_VW), real_map),
            _const_spec((1, GLA_DV)),
        ],
        out_specs=pl.BlockSpec((1, TILE, GLA_VW), real_map),
        out_shape=jax.ShapeDtypeStruct((bsz, seq, GLA_VW), _bf16),
        scratch_shapes=[pltpu.VMEM((GLA_HEADS, GLA_DV, GLA_DK), _f32)],
        compiler_params=pltpu.CompilerParams(
            dimension_semantics=("arbitrary", "arbitrary"), vmem_limit_bytes=VMEM_LIMIT),
        name="gla",
    )(qt, kt, ke, gv, dec, sr, row2(gla_norm_g[0]))

    pair = 2 * HEAD_PAD
    o_mla = pl.pallas_call(
        _mla_kernel,
        grid=(bsz, MLA_HEADS // 2, seq // Q_TILE),
        in_specs=[
            pl.BlockSpec((1, pair, Q_TILE), lambda b, hp, i: (b, hp, i)),
            pl.BlockSpec((1, lp, pair), lambda b, hp, i: (b, 0, hp)),
            pl.BlockSpec((1, 2 * V_AUG, lp), lambda b, hp, i: (b, hp, 0)),
        ],
        out_specs=pl.BlockSpec((1, Q_TILE, 2 * MLA_DV), lambda b, hp, i: (b, i, hp)),
        out_shape=jax.ShapeDtypeStruct((bsz, seq, MLA_HEADS * MLA_DV), _bf16),
        scratch_shapes=[pltpu.VMEM((MLA_PAIR, V_AUG, Q_TILE), _f32),
                        pltpu.VMEM((MLA_PAIR, 1, LANES), _f32)],
        compiler_params=pltpu.CompilerParams(
            dimension_semantics=("arbitrary", "arbitrary", "arbitrary"),
            vmem_limit_bytes=VMEM_LIMIT),
        name="mla",
    )(qm, km, vm)

    rw = jnp.concatenate([router_group_w[0], router_expert_w[0]], axis=1)
    rw = jnp.pad(rw, ((0, 0), (0, LANES - rw.shape[1])))
    rwh = rw.astype(_bf16)
    rwl = (rw - rwh.astype(_f32)).astype(_bf16)
    rb = jnp.concatenate([router_group_b[0], router_expert_b[0]])
    rb = jnp.pad(rb, (0, LANES - rb.shape[0])).reshape(1, LANES)
    mi = np.arange(ROUTE_BLOCK)
    tril_strict = jnp.asarray(mi[:, None] > mi[None, :], dtype=_bf16)
    flat = lambda a: a.reshape(ntok, a.shape[-1])
    tok_spec = lambda w: pl.BlockSpec((MERGE_TILE, w), lambda g: (g, 0))
    tt_spec = lambda n, index_map: pl.BlockSpec((n * SUBLANES, LANES), index_map)
    s2, info, info_t, cnt = pl.pallas_call(
        _merge_kernel,
        grid=(ntok // MERGE_TILE,),
        in_specs=[tok_spec(d), tok_spec(d), tok_spec(d), tok_spec(d), tok_spec(d),
                  _const_spec((d, d)), _const_spec((d, d)), _const_spec((d, d)),
                  _const_spec((1, d)), _const_spec((1, d)),
                  _const_spec((d, LANES)), _const_spec((d, LANES)), _const_spec((1, LANES)),
                  _const_spec((ROUTE_BLOCK, ROUTE_BLOCK))],
        out_specs=(tt_spec(MERGE_TILE, lambda g: (g, 0)), tok_spec(LANES),
                   pl.BlockSpec((SUBLANES, MERGE_TILE), lambda g: (0, g)), _const_spec((8, LANES))),
        out_shape=(jax.ShapeDtypeStruct((ntok * SUBLANES, LANES), _f32),
                   jax.ShapeDtypeStruct((ntok, LANES), _f32),
                   jax.ShapeDtypeStruct((SUBLANES, ntok), _f32),
                   jax.ShapeDtypeStruct((8, LANES), _f32)),
        scratch_shapes=[pltpu.VMEM((8, LANES), _f32)],
        compiler_params=pltpu.CompilerParams(
            dimension_semantics=("arbitrary",), vmem_limit_bytes=VMEM_LIMIT),
        name="merge_router",
    )(flat(o_gla), flat(o_mla), flat(gate_a), flat(gate_b), flat(s_emb),
      w_branch_gla[0].astype(_bf16), w_branch_mla[0].astype(_bf16), w_out[0].astype(_bf16),
      row2(ln_mix_g[0]), row2(ln_mix_b[0]), rwh, rwl, rb, tril_strict)

    n_tiles = (2 * ntok + N_EXPERTS * (EXPERT_TILE - 1)) // EXPERT_TILE
    n_rows = n_tiles * EXPERT_TILE
    e_idx = info_t[0:2].astype(jnp.int32)
    rank = info_t[4:6].astype(jnp.int32)
    counts = cnt[0, :N_EXPERTS].astype(jnp.int32)
    padded = ((counts + EXPERT_TILE - 1) // EXPERT_TILE) * EXPERT_TILE
    ends = jnp.cumsum(padded)
    starts = ends - padded
    expert_ids = jnp.arange(N_EXPERTS, dtype=jnp.int32)[:, None, None]
    start_of = jnp.sum(jnp.where(e_idx[None] == expert_ids, starts[:, None, None], 0), axis=0)
    pos = (start_of + rank).reshape(-1)
    tile_start = jnp.arange(n_tiles, dtype=jnp.int32) * EXPERT_TILE
    tile_expert = jnp.minimum(
        jnp.sum((ends[None, :] <= tile_start[:, None]).astype(jnp.int32), axis=1), N_EXPERTS - 1)
    n_used = (ends[-1:] // EXPERT_TILE).astype(jnp.int32)
    zero_row = jnp.where(padded > 0, ends - EXPERT_TILE, -1).astype(jnp.int32)

    any_spec = pl.BlockSpec(memory_space=pl.ANY)
    xs = pl.pallas_call(
        _dispatch_kernel,
        grid_spec=pltpu.PrefetchScalarGridSpec(
            num_scalar_prefetch=3,
            grid=(ntok // DISPATCH_TILE,),
            in_specs=[tt_spec(DISPATCH_TILE, lambda g, p, z, nu: (g, 0))],
            out_specs=any_spec,
            scratch_shapes=[pltpu.VMEM((EXPERT_TILE * SUBLANES, LANES), _f32),
                            pltpu.SemaphoreType.DMA((2,))],
        ),
        out_shape=jax.ShapeDtypeStruct((n_rows * SUBLANES, LANES), _f32),
        compiler_params=pltpu.CompilerParams(
            dimension_semantics=("arbitrary",), vmem_limit_bytes=VMEM_LIMIT, has_side_effects=True),
        name="dispatch",
    )(pos, zero_row, n_used, s2)

    ff = EXPERT_FF
    wg = expert_w_gate[0].reshape(N_EXPERTS, d, ff)
    wu = expert_w_up[0].reshape(N_EXPERTS, d, ff)
    wd = expert_w_down[0].reshape(N_EXPERTS, ff, d)
    ys = pl.pallas_call(
        _expert_kernel,
        grid_spec=pltpu.PrefetchScalarGridSpec(
            num_scalar_prefetch=2,
            grid=(n_tiles,),
            in_specs=[
                tt_spec(EXPERT_TILE, lambda u, te, nu: (jnp.minimum(u, nu[0] - 1), 0)),
                pl.BlockSpec((1, d, ff), lambda u, te, nu: (te[u], 0, 0)),
                pl.BlockSpec((1, d, ff), lambda u, te, nu: (te[u], 0, 0)),
                pl.BlockSpec((1, ff, d), lambda u, te, nu: (te[u], 0, 0)),
            ],
            out_specs=tt_spec(EXPERT_TILE, lambda u, te, nu: (u, 0)),
            scratch_shapes=[pltpu.VMEM((d, ff), _bf16), pltpu.VMEM((d, ff), _bf16),
                            pltpu.VMEM((ff, d), _bf16)],
        ),
        out_shape=jax.ShapeDtypeStruct((n_rows * SUBLANES, LANES), _f32),
        compiler_params=pltpu.CompilerParams(
            dimension_semantics=("arbitrary",), vmem_limit_bytes=VMEM_LIMIT),
        name="experts",
    )(tile_expert, n_used, xs, wg, wu, wd)

    out = pl.pallas_call(
        _combine_kernel,
        grid_spec=pltpu.PrefetchScalarGridSpec(
            num_scalar_prefetch=1,
            grid=(ntok // COMBINE_TILE,),
            in_specs=[
                tt_spec(COMBINE_TILE, lambda g, p: (g, 0)),
                pl.BlockSpec((COMBINE_TILE, LANES), lambda g, p: (g, 0)),
                any_spec,
                pl.BlockSpec((1, d), lambda g, p: (0, 0)),
                pl.BlockSpec((1, d), lambda g, p: (0, 0)),
            ],
            out_specs=pl.BlockSpec((COMBINE_TILE, d), lambda g, p: (g, 0)),
            scratch_shapes=[pltpu.VMEM((COMBINE_SLOTS, 2, COMBINE_TILE * SUBLANES, LANES), _f32),
                            pltpu.SemaphoreType.DMA((COMBINE_SLOTS,))],
        ),
        out_shape=jax.ShapeDtypeStruct((ntok, d), _f32),
        compiler_params=pltpu.CompilerParams(
            dimension_semantics=("arbitrary",), vmem_limit_bytes=VMEM_LIMIT),
        name="combine_ln",
    )(pos, s2, info, ys, row2(ln_ffn_g[0]), row2(ln_ffn_b[0]))
    return out.reshape(bsz, seq, d)
```

```python
import functools

import jax
import jax.numpy as jnp
import numpy as np
from jax import lax
from jax.experimental import pallas as pl
from jax.experimental.pallas import tpu as pltpu

D_MODEL = 1024
N_META = 16
GLA_HEADS = 4
GLA_DK = 128
GLA_DV = 256
GLA_QK = GLA_HEADS * GLA_DK
GLA_VW = GLA_HEADS * GLA_DV
GLA_GATE_RANK = 16
GLA_GATE_TAU = 16.0
GLA_CHUNK = 64
MLA_HEADS = 16
MLA_Q_RANK = 384
MLA_KV_RANK = 256
MLA_NOPE = 64
MLA_ROPE = 32
MLA_DV = 64
MLA_QDIM = MLA_NOPE + MLA_ROPE
ROPE_BASE = 10000.0
N_GROUPS = 4
EXPERTS_PER_GROUP = 8
N_EXPERTS = N_GROUPS * EXPERTS_PER_GROUP
EXPERT_FF = 256
DEPTH = 1
ALPHA = (2.0 * DEPTH) ** 0.25
LN_EPS = 1e-5
RMS_EPS = 1e-6

LANES = 128
SUBLANES = 8
MXU_DIM = 256
TILE = 256
PADL = TILE - N_META
HEAD_PAD = LANES
BF16_ROWS = 16
KV_TILE = 512
Q_TILE = 2 * KV_TILE
V_AUG = MLA_DV + BF16_ROWS
MERGE_TILE = 1024
ROUTE_BLOCK = 256
EXPERT_TILE = 256
DISPATCH_TILE = 2048
COMBINE_TILE = 256
COMBINE_SLOTS = 3
NEG = -1e30
LOG2E = 1.4426950408889634
BOUND_SLACK = 1.02
MIN_SOFTMAX_SUM = 2.0 ** -100
VMEM_LIMIT = 56 * 1024 * 1024

_C_Q, _C_K, _C_V, _C_R = 0, 512, 1024, 2048
_C_A = 3072
_C_CQ = _C_A + LANES
_C_CKV = _C_CQ + MLA_Q_RANK
_C_KR = _C_CKV + MLA_KV_RANK
_C_GA = _C_KR + LANES
_C_GB = _C_GA + D_MODEL
_W_COLS = _C_GB + D_MODEL

_f32 = jnp.float32
_bf16 = jnp.bfloat16


def _dot(a, b):
    return jnp.dot(a, b, preferred_element_type=_f32)


def _dot_nt(a, b):
    return lax.dot_general(a, b, (((1,), (1,)), ((), ())), preferred_element_type=_f32)


def _dot_tn(a, b):
    return lax.dot_general(a, b, (((0,), (0,)), ((), ())), preferred_element_type=_f32)


def _layer_norm(x, g, b):
    mu = jnp.mean(x, axis=-1, keepdims=True)
    xc = x - mu
    var = jnp.mean(xc * xc, axis=-1, keepdims=True)
    return xc * lax.rsqrt(var + LN_EPS) * g + b


def _rms_norm(x, g):
    ms = jnp.mean(x * x, axis=-1, keepdims=True)
    return x * lax.rsqrt(ms + RMS_EPS) * g


def _sigmoid(x):
    return 1.0 / (1.0 + jnp.exp(-x))


def _tt_load(ref, n, lead=()):
    return jnp.concatenate(
        [ref[lead + (pl.ds(a, n, stride=SUBLANES), slice(None))] for a in range(SUBLANES)], axis=1)


def _tt_store(ref, x):
    n = x.shape[0]
    for a in range(SUBLANES):
        ref[pl.ds(a, n, stride=SUBLANES), :] = x[:, a * LANES:(a + 1) * LANES]


def _tt_rows(tok):
    return pl.ds(pl.multiple_of(tok * SUBLANES, SUBLANES), SUBLANES)


def _rope(x, cos, sin_lo, sin_hi):
    half = MLA_ROPE // 2
    from_hi = pltpu.roll(x, LANES - half, 1)
    from_lo = pltpu.roll(x, half, 1)
    return x * cos + from_hi * sin_lo + from_lo * sin_hi


def _inproj_kernel(x_ref, head_ref, lng_ref, lnb_ref, w_ref, w2_ref, gb_ref, qg_ref, wuqt_ref, kvg_ref,
                   wuk_ref, wuvt_ref, cos_ref, sl_ref, sh_ref, cost_ref, sint_ref, tril_ref,
                   s_ref, qt_ref, kt_ref, ke_ref, gv_ref, dec_ref, sr_ref, qm_ref, km_ref,
                   vm_ref, ga_ref, gbt_ref):
    t = pl.program_id(1)
    x_in = jnp.where(t == 0, head_ref[...], x_ref[0])
    sn = _layer_norm(x_in, lng_ref[...], lnb_ref[...])
    s_ref[0] = sn
    snb = sn.astype(_bf16)
    row = t * TILE + lax.broadcasted_iota(jnp.int32, (TILE, 1), 0)
    valid = row >= PADL

    proj = lambda c0, width: _dot(snb, w_ref[:, c0:c0 + width])
    a_lr = proj(_C_A, LANES)
    cq = proj(_C_CQ, MLA_Q_RANK)
    ckv = proj(_C_CKV, MLA_KV_RANK)
    kr_raw = proj(_C_KR, LANES)
    gq = proj(_C_Q, GLA_QK)
    gk = proj(_C_K, GLA_QK)
    gv_ref[0] = jnp.where(valid, proj(_C_V, GLA_VW), 0.0).astype(_bf16)
    r = proj(_C_R, GLA_VW)
    sr_ref[0] = (r * _sigmoid(r)).astype(_bf16)
    ga_ref[0] = _sigmoid(proj(_C_GA, D_MODEL)).astype(_bf16)
    gbt_ref[0] = _sigmoid(proj(_C_GB, D_MODEL)).astype(_bf16)

    z = _dot(a_lr.astype(_bf16), w2_ref[...]) + gb_ref[...]
    cqn = _rms_norm(cq, qg_ref[...]).astype(_bf16)
    ckvn = _rms_norm(ckv, kvg_ref[...]).astype(_bf16)
    qft = _dot_nt(wuqt_ref[...], cqn)
    kf = _dot(ckvn, wuk_ref[...])
    vt = _dot_nt(wuvt_ref[...], ckvn)

    la = (jnp.minimum(z, 0.0) - jnp.log1p(jnp.exp(-jnp.abs(z)))) * (1.0 / GLA_GATE_TAU)
    la = jnp.where(valid, la, 0.0)
    hi = la.astype(_bf16)
    r1 = la - hi.astype(_f32)
    mid = r1.astype(_bf16)
    lo = (r1 - mid.astype(_f32)).astype(_bf16)
    tril = tril_ref[...]
    bc = _dot(tril, hi) + _dot(tril, mid) + _dot(tril, lo)
    n_chunks = TILE // GLA_CHUNK
    lasts = [bc[c * GLA_CHUNK + GLA_CHUNK - 1:(c + 1) * GLA_CHUNK, :] for c in range(n_chunks)]
    for c in range(n_chunks):
        dec_ref[0, c:c + 1, :] = jnp.exp(lasts[c])
    b_last = jnp.concatenate(
        [jnp.broadcast_to(l, (GLA_CHUNK, GLA_QK)) for l in lasts], axis=0)
    gk = jnp.where(valid, gk, 0.0)
    qt_ref[0] = (gq * (GLA_DK ** -0.5) * jnp.exp(bc)).astype(_bf16)
    kt_ref[0] = (gk * jnp.exp(-bc)).astype(_bf16)
    ke_ref[0] = (gk * jnp.exp(b_last - bc)).astype(_bf16)

    cos = cos_ref[...]
    sl = sl_ref[...]
    sh = sh_ref[...]
    scale = (MLA_QDIM ** -0.5) * LOG2E
    cost = cost_ref[...]
    sint = sint_ref[...]
    half = MLA_ROPE // 2
    for h in range(MLA_HEADS):
        base = h * HEAD_PAD
        x1 = qft[base + MLA_NOPE:base + MLA_NOPE + half]
        x2 = qft[base + MLA_NOPE + half:base + MLA_QDIM]
        qm_ref[0, base:base + MLA_NOPE] = (qft[base:base + MLA_NOPE] * scale).astype(_bf16)
        qm_ref[0, base + MLA_NOPE:base + MLA_NOPE + half] = ((x1 * cost - x2 * sint) * scale).astype(_bf16)
        qm_ref[0, base + MLA_NOPE + half:base + MLA_QDIM] = ((x1 * sint + x2 * cost) * scale).astype(_bf16)
        qm_ref[0, base + MLA_QDIM:base + HEAD_PAD] = jnp.zeros((HEAD_PAD - MLA_QDIM, TILE), _bf16)
    kr = _rope(kr_raw, cos, sl, sh)
    for h in range(MLA_HEADS):
        km_ref[0, :, h * HEAD_PAD:(h + 1) * HEAD_PAD] = (
            kf[:, h * HEAD_PAD:(h + 1) * HEAD_PAD] + kr).astype(_bf16)
    for h in range(MLA_HEADS):
        vm_ref[0, h * V_AUG:h * V_AUG + MLA_DV] = vt[h * MLA_DV:(h + 1) * MLA_DV].astype(_bf16)
        vm_ref[0, h * V_AUG + MLA_DV:(h + 1) * V_AUG] = jnp.ones((V_AUG - MLA_DV, TILE), _bf16)


def _gla_kernel(qt_ref, kt_ref, ke_ref, gv_ref, dec_ref, sr_ref, ng_ref, o_ref, st_ref):
    t = pl.program_id(1)

    @pl.when(t == 0)
    def _():
        st_ref[...] = jnp.zeros_like(st_ref)

    ri = lax.broadcasted_iota(jnp.int32, (TILE, TILE), 0)
    ci = lax.broadcasted_iota(jnp.int32, (TILE, TILE), 1)
    visible = (ri >= ci) & (ri // GLA_CHUNK == ci // GLA_CHUNK)
    ng = ng_ref[...]
    n_chunks = TILE // GLA_CHUNK
    chunk_rows = [slice(c * GLA_CHUNK, (c + 1) * GLA_CHUNK) for c in range(n_chunks)]
    k_cols = [slice(h * GLA_DK, (h + 1) * GLA_DK) for h in range(GLA_HEADS)]
    v_cols = [slice(h * GLA_DV, (h + 1) * GLA_DV) for h in range(GLA_HEADS)]
    att = [jnp.where(visible, _dot_nt(qt_ref[0, :, k_cols[h]], kt_ref[0, :, k_cols[h]]), 0.0)
           for h in range(GLA_HEADS)]
    upd = [[_dot_tn(gv_ref[0, rows, v_cols[h]], ke_ref[0, rows, k_cols[h]]) for rows in chunk_rows]
           for h in range(GLA_HEADS)]
    o_intra = [_dot(att[h].astype(_bf16), gv_ref[0, :, v_cols[h]]) for h in range(GLA_HEADS)]
    for h in range(GLA_HEADS):
        st = st_ref[h]
        for c, rows in enumerate(chunk_rows):
            o = o_intra[h][rows] + _dot_nt(qt_ref[0, rows, k_cols[h]], st.astype(_bf16))
            st = st * dec_ref[0, c:c + 1, k_cols[h]] + upd[h][c]
            o = _rms_norm(o, ng) * sr_ref[0, rows, v_cols[h]].astype(_f32)
            o_ref[0, rows, v_cols[h]] = o.astype(_bf16)
        st_ref[h] = st


MLA_GROUP = 4


def _mla_tile_start(j):
    return pl.multiple_of(TILE + j * KV_TILE, TILE)


def _mla_finish(o_ref, acc_ref):
    outs = []
    for h in range(MLA_GROUP):
        a = acc_ref[h]
        outs.append(a[0:MLA_DV] / a[MLA_DV:MLA_DV + 1])
    o_ref[0] = jnp.concatenate(outs, axis=0).T.astype(_bf16)


def _mla_kernel(q_ref, k_ref, v_ref, o_ref, acc_ref, knorm_ref):
    i = pl.program_id(2)
    heads = MLA_GROUP
    ones = jnp.ones((HEAD_PAD, LANES), _bf16)

    @pl.when(i == 0)
    def _():
        for h in range(heads):
            kk = k_ref[0, :, h * HEAD_PAD:(h + 1) * HEAD_PAD].astype(_f32)
            hi = (kk * kk).astype(_bf16)
            knorm_ref[h] = jnp.max(_dot(hi, ones), axis=0, keepdims=True)

    half = slice(KV_TILE, Q_TILE)
    q_t = [q_ref[0, h * HEAD_PAD:(h + 1) * HEAD_PAD, :] for h in range(heads)]
    q_hi = [q_ref[0, h * HEAD_PAD:(h + 1) * HEAD_PAD, half] for h in range(heads)]

    def score_bound(h, q):
        qq = q.astype(_f32)
        qn2 = jnp.sum(qq * qq, axis=0, keepdims=True)
        return jnp.sqrt(qn2 * knorm_ref[h][:, 0:1]) * BOUND_SLACK

    bound = [score_bound(h, q_t[h]) for h in range(heads)]
    bound_hi = [score_bound(h, q_hi[h]) for h in range(heads)]

    def keys(j, h):
        return k_ref[0, pl.ds(_mla_tile_start(j), KV_TILE), h * HEAD_PAD:(h + 1) * HEAD_PAD]

    def weighted(j, h, p):
        return _dot(v_ref[0, h * V_AUG:(h + 1) * V_AUG, pl.ds(_mla_tile_start(j), KV_TILE)], p)

    k_row = lax.broadcasted_iota(jnp.int32, (KV_TILE, Q_TILE), 0)
    q_col = lax.broadcasted_iota(jnp.int32, (KV_TILE, Q_TILE), 1)
    k_row_sq = lax.broadcasted_iota(jnp.int32, (KV_TILE, KV_TILE), 0)
    q_col_sq = lax.broadcasted_iota(jnp.int32, (KV_TILE, KV_TILE), 1)
    s_meta = [_dot(k_ref[0, PADL:TILE, h * HEAD_PAD:(h + 1) * HEAD_PAD], q_t[h]) for h in range(heads)]
    s_lo = [jnp.where(k_row <= q_col, _dot(keys(2 * i, h), q_t[h]), NEG) for h in range(heads)]
    s_hi = [jnp.where(k_row_sq <= q_col_sq, _dot(keys(2 * i + 1, h), q_hi[h]), NEG)
            for h in range(heads)]
    for h in range(heads):
        p_meta = jnp.concatenate([jnp.zeros((PADL, Q_TILE), _bf16),
                                  jnp.exp2(s_meta[h] - bound[h]).astype(_bf16)], axis=0)
        acc_ref[h] = (_dot(v_ref[0, h * V_AUG:(h + 1) * V_AUG, 0:TILE], p_meta)
                      + weighted(2 * i, h, jnp.exp2(s_lo[h] - bound[h]).astype(_bf16)))
        acc_ref[h, :, half] += weighted(2 * i + 1, h, jnp.exp2(s_hi[h] - bound_hi[h]).astype(_bf16))

    def body(jj, c):
        tiles = (2 * jj, 2 * jj + 1)
        s = [[_dot(keys(j, h), q_t[h]) for h in range(heads)] for j in tiles]
        for h in range(heads):
            acc_ref[h] += sum(weighted(j, h, jnp.exp2(s[n][h] - bound[h]).astype(_bf16))
                              for n, j in enumerate(tiles))
        return c

    lax.fori_loop(0, i, body, 0)
    _mla_finish(o_ref, acc_ref)

    l_min = functools.reduce(jnp.minimum,
                             [jnp.min(acc_ref[h, MLA_DV:MLA_DV + 1, :]) for h in range(heads)])

    @pl.when(jnp.logical_not(l_min >= MIN_SOFTMAX_SUM))
    def _():
        _mla_exact(q_ref, k_ref, v_ref, o_ref, acc_ref)


def _mla_exact(q_ref, k_ref, v_ref, o_ref, acc_ref):
    i = pl.program_id(2)
    heads = MLA_GROUP
    q_t = [q_ref[0, h * HEAD_PAD:(h + 1) * HEAD_PAD, :] for h in range(heads)]

    ms = []
    for h in range(heads):
        kb = k_ref[0, PADL:TILE, h * HEAD_PAD:(h + 1) * HEAD_PAD]
        s = _dot(kb, q_t[h])
        m0 = jnp.max(s, axis=0, keepdims=True)
        p = jnp.concatenate([jnp.zeros((PADL, Q_TILE), _bf16), jnp.exp2(s - m0).astype(_bf16)], axis=0)
        acc_ref[h] = _dot(v_ref[0, h * V_AUG:(h + 1) * V_AUG, 0:TILE], p)
        ms.append(m0)

    k_row = lax.broadcasted_iota(jnp.int32, (KV_TILE, Q_TILE), 0)
    q_col = lax.broadcasted_iota(jnp.int32, (KV_TILE, Q_TILE), 1)

    def body(j, ms):
        visible = (j - 2 * i) * KV_TILE + k_row <= q_col
        out = []
        for h in range(heads):
            s = _dot(k_ref[0, pl.ds(_mla_tile_start(j), KV_TILE), h * HEAD_PAD:(h + 1) * HEAD_PAD],
                     q_t[h])
            s = jnp.where(visible, s, NEG)
            vb = v_ref[0, h * V_AUG:(h + 1) * V_AUG, pl.ds(_mla_tile_start(j), KV_TILE)]
            m_new = jnp.maximum(ms[h], jnp.max(s, axis=0, keepdims=True))
            alpha = jnp.exp2(ms[h] - m_new)
            acc_ref[h] = alpha * acc_ref[h] + _dot(vb, jnp.exp2(s - m_new).astype(_bf16))
            out.append(m_new)
        return tuple(out)

    lax.fori_loop(0, 2 * i + 2, body, tuple(ms))
    _mla_finish(o_ref, acc_ref)


def _merge_kernel(og_ref, om_ref, ga_ref, gbt_ref, s_ref, wbg_ref, wbm_ref, wo_ref, lng_ref,
                  lnb_ref, rwh_ref, rwl_ref, rb_ref, tril_ref,
                  s2_ref, info_ref, infot_ref, cnt_ref, carry_ref):
    step = pl.program_id(0)

    @pl.when(step == 0)
    def _():
        carry_ref[...] = jnp.zeros_like(carry_ref)

    n_blk = MERGE_TILE // ROUTE_BLOCK
    blocks = [slice(i * ROUTE_BLOCK, (i + 1) * ROUTE_BLOCK) for i in range(n_blk)]
    d = wo_ref.shape[0]
    col_blocks = [slice(c, c + MXU_DIM) for c in range(0, d, MXU_DIM)]
    lane = lax.broadcasted_iota(jnp.int32, (ROUTE_BLOCK, LANES), 1)
    is_g = lane < N_GROUPS
    merged, s2, logits, infos = {}, {}, {}, {}
    carry = [carry_ref[0:1, :]]

    def branches(i):
        rows = blocks[i]
        merged[i] = jnp.concatenate(
            [(ga_ref[rows, cols].astype(_f32) * _dot(og_ref[rows, :], wbg_ref[:, cols])
              + gbt_ref[rows, cols].astype(_f32) * _dot(om_ref[rows, :], wbm_ref[:, cols])
              ).astype(_bf16) for cols in col_blocks], axis=1)

    def residual_norm(i):
        y = ALPHA * s_ref[blocks[i], :] + _dot(merged[i], wo_ref[...])
        s2[i] = _layer_norm(y, lng_ref[...], lnb_ref[...])

    def router_logits(i):
        xh = s2[i].astype(_bf16)
        xl = (s2[i] - xh.astype(_f32)).astype(_bf16)
        logits[i] = (_dot(xh, rwh_ref[...]) + _dot(xl, rwh_ref[...]) + _dot(xh, rwl_ref[...])
                     + rb_ref[...])

    def route(i):
        gl = jnp.where(is_g, logits[i], NEG)
        gmax = jnp.max(gl, axis=-1, keepdims=True)
        gidx = jnp.min(jnp.where(gl == gmax, lane, LANES), axis=-1, keepdims=True)
        p_g = 1.0 / jnp.sum(jnp.where(is_g, jnp.exp(gl - gmax), 0.0), axis=-1, keepdims=True)
        lo = N_GROUPS + EXPERTS_PER_GROUP * gidx
        el = jnp.where((lane >= lo) & (lane < lo + EXPERTS_PER_GROUP), logits[i], NEG)
        v1 = jnp.max(el, axis=-1, keepdims=True)
        i1 = jnp.min(jnp.where(el == v1, lane, LANES), axis=-1, keepdims=True)
        el2 = jnp.where(lane == i1, NEG, el)
        v2 = jnp.max(el2, axis=-1, keepdims=True)
        i2 = jnp.min(jnp.where(el2 == v2, lane, LANES), axis=-1, keepdims=True)
        tt = jnp.exp(v2 - v1)
        p1 = 1.0 / (1.0 + tt)
        p2 = tt / (1.0 + tt)
        e1 = i1 - N_GROUPS
        e2 = i2 - N_GROUPS
        hit1 = lane == e1
        hit2 = lane == e2
        onehot = jnp.where(hit1 | hit2, 1.0, 0.0)
        before = _dot(tril_ref[...], onehot.astype(_bf16)) + carry[0]
        r1 = jnp.sum(jnp.where(hit1, before, 0.0), axis=-1, keepdims=True)
        r2 = jnp.sum(jnp.where(hit2, before, 0.0), axis=-1, keepdims=True)
        carry[0] = carry[0] + jnp.sum(onehot, axis=0, keepdims=True)
        infos[i] = jnp.where(lane == 0, e1.astype(_f32),
                   jnp.where(lane == 1, e2.astype(_f32),
                   jnp.where(lane == 2, p_g * p1,
                   jnp.where(lane == 3, p_g * p2,
                   jnp.where(lane == 4, r1,
                   jnp.where(lane == 5, r2, 0.0))))))

    for stage in (branches, residual_norm, router_logits, route):
        for i in range(n_blk):
            stage(i)
    _tt_store(s2_ref, jnp.concatenate([s2[i] for i in range(n_blk)], axis=0))
    carry_ref[...] = jnp.broadcast_to(carry[0], carry_ref.shape)
    cnt_ref[...] = jnp.broadcast_to(carry[0], cnt_ref.shape)
    info = jnp.concatenate([infos[i] for i in range(n_blk)], axis=0)
    info_ref[...] = info
    infot_ref[...] = info.T[0:SUBLANES]


def _dispatch_kernel(pos_ref, zrow_ref, nused_ref, s2_ref, xs_hbm, zero_ref, sems):
    g = pl.program_id(0)
    zero_sem = sems.at[1]
    row_sem = sems.at[0]

    tile_rows = EXPERT_TILE * SUBLANES

    def zero_copy(row):
        start = pl.multiple_of(row * SUBLANES, tile_rows)
        return pltpu.make_async_copy(zero_ref, xs_hbm.at[pl.ds(start, tile_rows)], zero_sem)

    @pl.when(g == 0)
    def _():
        zero_ref[...] = jnp.zeros_like(zero_ref)

        def start(e, c):
            @pl.when(zrow_ref[e] >= 0)
            def _():
                zero_copy(zrow_ref[e]).start()
            return c

        def wait(e, c):
            @pl.when(zrow_ref[e] >= 0)
            def _():
                zero_copy(0).wait()
            return c

        def start_tail(u, c):
            zero_copy(u * EXPERT_TILE).start()
            return c

        def wait_tail(u, c):
            zero_copy(0).wait()
            return c

        n_tiles = xs_hbm.shape[0] // tile_rows
        lax.fori_loop(0, N_EXPERTS, start, 0)
        lax.fori_loop(nused_ref[0], n_tiles, start_tail, 0)
        lax.fori_loop(0, N_EXPERTS, wait, 0)
        lax.fori_loop(nused_ref[0], n_tiles, wait_tail, 0)

    n_tok = pl.num_programs(0) * DISPATCH_TILE

    def issue(r, c):
        tok = g * DISPATCH_TILE + r
        for k in range(2):
            pltpu.make_async_copy(s2_ref.at[_tt_rows(r)],
                                  xs_hbm.at[_tt_rows(pos_ref[k * n_tok + tok])], row_sem).start(priority=k)
        return c

    lax.fori_loop(0, DISPATCH_TILE, issue, 0, unroll=8)
    for k in range(2):
        pltpu.make_async_copy(s2_ref, xs_hbm.at[pl.ds(0, DISPATCH_TILE * SUBLANES)], row_sem).wait()


def _expert_kernel(te_ref, nused_ref, x_ref, wg_ref, wu_ref, wd_ref, o_ref, wgb_ref, wub_ref, wdb_ref):
    u = pl.program_id(0)
    used = u < nused_ref[0]
    first_of_expert = jnp.logical_or(u == 0, te_ref[u] != te_ref[jnp.maximum(u - 1, 0)])

    @pl.when(jnp.logical_and(used, first_of_expert))
    def _():
        wgb_ref[...] = wg_ref[0].astype(_bf16)
        wub_ref[...] = wu_ref[0].astype(_bf16)
        wdb_ref[...] = wd_ref[0].astype(_bf16)

    @pl.when(used)
    def _():
        x = _tt_load(x_ref, EXPERT_TILE).astype(_bf16)
        a = _dot(x, wgb_ref[...])
        up = _dot(x, wub_ref[...])
        hid = a * _sigmoid(a) * up
        _tt_store(o_ref, _dot(hid.astype(_bf16), wdb_ref[...]))

    @pl.when(u >= nused_ref[0])
    def _():
        o_ref[...] = jnp.zeros_like(o_ref)


def _combine_kernel(pos_ref, s2_ref, info_ref, ys_hbm, lng_ref, lnb_ref, o_ref, buf_ref, sems):
    g = pl.program_id(0)
    last = pl.num_programs(0) - 1
    n_tok = pl.num_programs(0) * COMBINE_TILE

    def start_copies(tile, slot, r):
        tok = tile * COMBINE_TILE + r
        for k in range(2):
            pltpu.make_async_copy(ys_hbm.at[_tt_rows(pos_ref[k * n_tok + tok])],
                                  buf_ref.at[slot, k, _tt_rows(r)], sems.at[slot]).start(priority=k)

    def wait_slot(slot):
        for k in range(2):
            pltpu.make_async_copy(ys_hbm.at[pl.ds(0, COMBINE_TILE * SUBLANES)], buf_ref.at[slot, k],
                                  sems.at[slot]).wait()

    @pl.when(g == 0)
    def _():
        def body(r, c):
            for tile in range(COMBINE_SLOTS - 1):
                start_copies(tile, tile, r)
            return c

        lax.fori_loop(0, COMBINE_TILE, body, 0, unroll=8)

    slot = g % COMBINE_SLOTS
    wait_slot(slot)
    info = info_ref[...]
    y = (ALPHA * _tt_load(s2_ref, COMBINE_TILE)
         + info[:, 2:3] * _tt_load(buf_ref, COMBINE_TILE, (slot, 0))
         + info[:, 3:4] * _tt_load(buf_ref, COMBINE_TILE, (slot, 1)))
    o_ref[...] = _layer_norm(y, lng_ref[...], lnb_ref[...])

    ahead = COMBINE_SLOTS - 1
    ahead_slot = (g + ahead) % COMBINE_SLOTS
    for r in range(COMBINE_TILE):
        start_copies(jnp.minimum(g + ahead, last), ahead_slot, r)

    @pl.when(g == last)
    def _():
        for n in range(1, COMBINE_SLOTS):
            wait_slot((g + n) % COMBINE_SLOTS)


def _const_spec(shape):
    nd = len(shape)
    return pl.BlockSpec(shape, lambda *_: (0,) * nd)


def _rope_tables(lp):
    pos = jnp.maximum(jnp.arange(lp, dtype=_f32) - PADL, 0.0)
    inv_freq = ROPE_BASE ** (-jnp.arange(0, MLA_ROPE, 2, dtype=_f32) / MLA_ROPE)
    ang = pos[:, None] * inv_freq[None, :]
    cos, sin = jnp.cos(ang), jnp.sin(ang)
    half = MLA_ROPE // 2
    ones = jnp.ones((lp, MLA_NOPE), _f32)
    zeros_n = jnp.zeros((lp, MLA_NOPE), _f32)
    zeros_h = jnp.zeros((lp, half), _f32)
    tail1 = jnp.ones((lp, LANES - MLA_QDIM), _f32)
    tail0 = jnp.zeros((lp, LANES - MLA_QDIM), _f32)
    cos_t = jnp.concatenate([ones, cos, cos, tail1], axis=1)
    sin_lo = jnp.concatenate([zeros_n, -sin, zeros_h, tail0], axis=1)
    sin_hi = jnp.concatenate([zeros_n, zeros_h, sin, tail0], axis=1)
    return cos_t, sin_lo, sin_hi, cos.T, sin.T


def _pad_heads(w, width):
    k = w.shape[0]
    w = w.reshape(k, MLA_HEADS, width)
    w = jnp.pad(w, ((0, 0), (0, 0), (0, HEAD_PAD - width)))
    return w.reshape(k, MLA_HEADS * HEAD_PAD)


def kernel(x, meta_tokens, ln_emb_g, ln_emb_b, w_in, gla_gate_w2, gla_gate_b, gla_norm_g, mla_q_norm_g, mla_w_uq, mla_kv_norm_g, mla_w_uk, mla_w_uv, w_branch_gla, w_branch_mla, w_out, ln_mix_g, ln_mix_b, router_group_w, router_group_b, router_expert_w, router_expert_b, expert_w_gate, expert_w_up, expert_w_down, ln_ffn_g, ln_ffn_b):
    bsz, seq, d = x.shape
    assert d == D_MODEL and seq % Q_TILE == 0 and w_in.shape[0] == DEPTH == 1
    lp = PADL + N_META + seq
    nt = lp // TILE
    ntok = bsz * seq
    row2 = lambda v: v.reshape(1, -1).astype(_f32)

    head_tile = jnp.concatenate([jnp.zeros((PADL, d), _f32), meta_tokens.astype(_f32)], axis=0)
    wi = w_in[0]
    o_a = 2 * GLA_QK + 2 * GLA_VW
    o_cq = o_a + GLA_GATE_RANK
    o_ckv = o_cq + MLA_Q_RANK
    o_kr = o_ckv + MLA_KV_RANK
    o_ga = o_kr + MLA_ROPE
    w_a = jnp.pad(wi[:, o_a:o_cq], ((0, 0), (0, LANES - GLA_GATE_RANK)))
    w_kr = jnp.pad(wi[:, o_kr:o_ga], ((0, 0), (MLA_NOPE, LANES - MLA_QDIM)))
    w_all = jnp.concatenate([wi[:, :o_a], w_a, wi[:, o_cq:o_kr], w_kr, wi[:, o_ga:]], axis=1).astype(_bf16)
    assert w_all.shape == (d, _W_COLS)
    w2p = jnp.pad(gla_gate_w2[0], ((0, LANES - GLA_GATE_RANK), (0, 0))).astype(_bf16)
    wuqt = _pad_heads(mla_w_uq[0], MLA_QDIM).T.astype(_bf16)
    wuk = _pad_heads(mla_w_uk[0], MLA_NOPE).astype(_bf16)
    wuvt = mla_w_uv[0].T.astype(_bf16)
    cos_t, sin_lo, sin_hi, cos_tr, sin_tr = _rope_tables(lp)
    blk = np.arange(TILE)
    tril_chunks = jnp.asarray(
        ((blk[:, None] >= blk[None, :]) & (blk[:, None] // GLA_CHUNK == blk[None, :] // GLA_CHUNK)),
        dtype=_bf16)

    pad_map = lambda b, t: (b, t, 0)
    real_map = lambda b, t: (b, jnp.maximum(t - 1, 0), 0)
    real_map_t = lambda b, t: (b, 0, jnp.maximum(t - 1, 0))
    out_shapes = (
        jax.ShapeDtypeStruct((bsz, seq, d), _f32),
        jax.ShapeDtypeStruct((bsz, lp, GLA_QK), _bf16),
        jax.ShapeDtypeStruct((bsz, lp, GLA_QK), _bf16),
        jax.ShapeDtypeStruct((bsz, lp, GLA_QK), _bf16),
        jax.ShapeDtypeStruct((bsz, lp, GLA_VW), _bf16),
        jax.ShapeDtypeStruct((bsz * nt, TILE // GLA_CHUNK, GLA_QK), _f32),
        jax.ShapeDtypeStruct((bsz, seq, GLA_VW), _bf16),
        jax.ShapeDtypeStruct((bsz, MLA_HEADS * HEAD_PAD, seq), _bf16),
        jax.ShapeDtypeStruct((bsz, lp, MLA_HEADS * HEAD_PAD), _bf16),
        jax.ShapeDtypeStruct((bsz, MLA_HEADS * V_AUG, lp), _bf16),
        jax.ShapeDtypeStruct((bsz, seq, d), _bf16),
        jax.ShapeDtypeStruct((bsz, seq, d), _bf16),
    )
    out_specs = (
        pl.BlockSpec((1, TILE, d), real_map),
        pl.BlockSpec((1, TILE, GLA_QK), pad_map),
        pl.BlockSpec((1, TILE, GLA_QK), pad_map),
        pl.BlockSpec((1, TILE, GLA_QK), pad_map),
        pl.BlockSpec((1, TILE, GLA_VW), pad_map),
        pl.BlockSpec((1, TILE // GLA_CHUNK, GLA_QK), lambda b, t: (b * nt + t, 0, 0)),
        pl.BlockSpec((1, TILE, GLA_VW), real_map),
        pl.BlockSpec((1, MLA_HEADS * HEAD_PAD, TILE), real_map_t),
        pl.BlockSpec((1, TILE, MLA_HEADS * HEAD_PAD), pad_map),
        pl.BlockSpec((1, MLA_HEADS * V_AUG, TILE), lambda b, t: (b, 0, t)),
        pl.BlockSpec((1, TILE, d), real_map),
        pl.BlockSpec((1, TILE, d), real_map),
    )
    tab_spec = pl.BlockSpec((TILE, LANES), lambda b, t: (t, 0))
    tabt_spec = pl.BlockSpec((MLA_ROPE // 2, TILE), lambda b, t: (0, t))
    (s_emb, qt, kt, ke, gv, dec, sr, qm, km, vm, gate_a, gate_b) = pl.pallas_call(
        _inproj_kernel,
        grid=(bsz, nt),
        in_specs=[
            pl.BlockSpec((1, TILE, d), real_map),
            _const_spec((TILE, d)),
            _const_spec((1, d)), _const_spec((1, d)),
            _const_spec((d, _W_COLS)),
            _const_spec((LANES, GLA_QK)), _const_spec((1, GLA_QK)),
            _const_spec((1, MLA_Q_RANK)), _const_spec((MLA_HEADS * HEAD_PAD, MLA_Q_RANK)),
            _const_spec((1, MLA_KV_RANK)), _const_spec((MLA_KV_RANK, MLA_HEADS * HEAD_PAD)),
            _const_spec((MLA_HEADS * MLA_DV, MLA_KV_RANK)),
            tab_spec, tab_spec, tab_spec, tabt_spec, tabt_spec,
            _const_spec((TILE, TILE)),
        ],
        out_specs=out_specs,
        out_shape=out_shapes,
        compiler_params=pltpu.CompilerParams(
            dimension_semantics=("arbitrary", "arbitrary"), vmem_limit_bytes=VMEM_LIMIT),
        name="inproj",
    )(x, head_tile, row2(ln_emb_g), row2(ln_emb_b), w_all, w2p, row2(gla_gate_b[0]),
      row2(mla_q_norm_g[0]), wuqt, row2(mla_kv_norm_g[0]), wuk, wuvt, cos_t, sin_lo, sin_hi,
      cos_tr, sin_tr, tril_chunks)

    o_gla = pl.pallas_call(
        _gla_kernel,
        grid=(bsz, nt),
        in_specs=[
            pl.BlockSpec((1, TILE, GLA_QK), pad_map),
            pl.BlockSpec((1, TILE, GLA_QK), pad_map),
            pl.BlockSpec((1, TILE, GLA_QK), pad_map),
            pl.BlockSpec((1, TILE, GLA_VW), pad_map),
            pl.BlockSpec((1, TILE // GLA_CHUNK, GLA_QK), lambda b, t: (b * nt + t, 0, 0)),
            pl.BlockSpec((1, TILE, GLA_VW), real_map),
            _const_spec((1, GLA_DV)),
        ],
        out_specs=pl.BlockSpec((1, TILE, GLA_VW), real_map),
        out_shape=jax.ShapeDtypeStruct((bsz, seq, GLA_VW), _bf16),
        scratch_shapes=[pltpu.VMEM((GLA_HEADS, GLA_DV, GLA_DK), _f32)],
        compiler_params=pltpu.CompilerParams(
            dimension_semantics=("arbitrary", "arbitrary"), vmem_limit_bytes=VMEM_LIMIT),
        name="gla",
    )(qt, kt, ke, gv, dec, sr, row2(gla_norm_g[0]))

    group_width = MLA_GROUP * HEAD_PAD
    o_mla = pl.pallas_call(
        _mla_kernel,
        grid=(bsz, MLA_HEADS // MLA_GROUP, seq // Q_TILE),
        in_specs=[
            pl.BlockSpec((1, group_width, Q_TILE), lambda b, hp, i: (b, hp, i)),
            pl.BlockSpec((1, lp, group_width), lambda b, hp, i: (b, 0, hp)),
            pl.BlockSpec((1, MLA_GROUP * V_AUG, lp), lambda b, hp, i: (b, hp, 0)),
        ],
        out_specs=pl.BlockSpec((1, Q_TILE, MLA_GROUP * MLA_DV), lambda b, hp, i: (b, i, hp)),
        out_shape=jax.ShapeDtypeStruct((bsz, seq, MLA_HEADS * MLA_DV), _bf16),
        scratch_shapes=[pltpu.VMEM((MLA_GROUP, V_AUG, Q_TILE), _f32),
                        pltpu.VMEM((MLA_GROUP, 1, LANES), _f32)],
        compiler_params=pltpu.CompilerParams(
            dimension_semantics=("arbitrary", "arbitrary", "arbitrary"),
            vmem_limit_bytes=VMEM_LIMIT),
        name="mla",
    )(qm, km, vm)

    rw = jnp.concatenate([router_group_w[0], router_expert_w[0]], axis=1)
    rw = jnp.pad(rw, ((0, 0), (0, LANES - rw.shape[1])))
    rwh = rw.astype(_bf16)
    rwl = (rw - rwh.astype(_f32)).astype(_bf16)
    rb = jnp.concatenate([router_group_b[0], router_expert_b[0]])
    rb = jnp.pad(rb, (0, LANES - rb.shape[0])).reshape(1, LANES)
    mi = np.arange(ROUTE_BLOCK)
    tril_strict = jnp.asarray(mi[:, None] > mi[None, :], dtype=_bf16)
    flat = lambda a: a.reshape(ntok, a.shape[-1])
    tok_spec = lambda w: pl.BlockSpec((MERGE_TILE, w), lambda g: (g, 0))
    tt_spec = lambda n, index_map: pl.BlockSpec((n * SUBLANES, LANES), index_map)
    s2, info, info_t, cnt = pl.pallas_call(
        _merge_kernel,
        grid=(ntok // MERGE_TILE,),
        in_specs=[tok_spec(d), tok_spec(d), tok_spec(d), tok_spec(d), tok_spec(d),
                  _const_spec((d, d)), _const_spec((d, d)), _const_spec((d, d)),
                  _const_spec((1, d)), _const_spec((1, d)),
                  _const_spec((d, LANES)), _const_spec((d, LANES)), _const_spec((1, LANES)),
                  _const_spec((ROUTE_BLOCK, ROUTE_BLOCK))],
        out_specs=(tt_spec(MERGE_TILE, lambda g: (g, 0)), tok_spec(LANES),
                   pl.BlockSpec((SUBLANES, MERGE_TILE), lambda g: (0, g)), _const_spec((8, LANES))),
        out_shape=(jax.ShapeDtypeStruct((ntok * SUBLANES, LANES), _f32),
                   jax.ShapeDtypeStruct((ntok, LANES), _f32),
                   jax.ShapeDtypeStruct((SUBLANES, ntok), _f32),
                   jax.ShapeDtypeStruct((8, LANES), _f32)),
        scratch_shapes=[pltpu.VMEM((8, LANES), _f32)],
        compiler_params=pltpu.CompilerParams(
            dimension_semantics=("arbitrary",), vmem_limit_bytes=VMEM_LIMIT),
        name="merge_router",
    )(flat(o_gla), flat(o_mla), flat(gate_a), flat(gate_b), flat(s_emb),
      w_branch_gla[0].astype(_bf16), w_branch_mla[0].astype(_bf16), w_out[0].astype(_bf16),
      row2(ln_mix_g[0]), row2(ln_mix_b[0]), rwh, rwl, rb, tril_strict)

    n_tiles = (2 * ntok + N_EXPERTS * (EXPERT_TILE - 1)) // EXPERT_TILE
    n_rows = n_tiles * EXPERT_TILE
    e_idx = info_t[0:2].astype(jnp.int32)
    rank = info_t[4:6].astype(jnp.int32)
    counts = cnt[0, :N_EXPERTS].astype(jnp.int32)
    padded = ((counts + EXPERT_TILE - 1) // EXPERT_TILE) * EXPERT_TILE
    ends = jnp.cumsum(padded)
    starts = ends - padded
    expert_ids = jnp.arange(N_EXPERTS, dtype=jnp.int32)[:, None, None]
    start_of = jnp.sum(jnp.where(e_idx[None] == expert_ids, starts[:, None, None], 0), axis=0)
    pos = (start_of + rank).reshape(-1)
    tile_start = jnp.arange(n_tiles, dtype=jnp.int32) * EXPERT_TILE
    tile_expert = jnp.minimum(
        jnp.sum((ends[None, :] <= tile_start[:, None]).astype(jnp.int32), axis=1), N_EXPERTS - 1)
    n_used = (ends[-1:] // EXPERT_TILE).astype(jnp.int32)
    zero_row = jnp.where(padded > 0, ends - EXPERT_TILE, -1).astype(jnp.int32)

    any_spec = pl.BlockSpec(memory_space=pl.ANY)
    xs = pl.pallas_call(
        _dispatch_kernel,
        grid_spec=pltpu.PrefetchScalarGridSpec(
            num_scalar_prefetch=3,
            grid=(ntok // DISPATCH_TILE,),
            in_specs=[tt_spec(DISPATCH_TILE, lambda g, p, z, nu: (g, 0))],
            out_specs=any_spec,
            scratch_shapes=[pltpu.VMEM((EXPERT_TILE * SUBLANES, LANES), _f32),
                            pltpu.SemaphoreType.DMA((2,))],
        ),
        out_shape=jax.ShapeDtypeStruct((n_rows * SUBLANES, LANES), _f32),
        compiler_params=pltpu.CompilerParams(
            dimension_semantics=("arbitrary",), vmem_limit_bytes=VMEM_LIMIT, has_side_effects=True),
        name="dispatch",
    )(pos, zero_row, n_used, s2)

    ff = EXPERT_FF
    wg = expert_w_gate[0].reshape(N_EXPERTS, d, ff)
    wu = expert_w_up[0].reshape(N_EXPERTS, d, ff)
    wd = expert_w_down[0].reshape(N_EXPERTS, ff, d)
    ys = pl.pallas_call(
        _expert_kernel,
        grid_spec=pltpu.PrefetchScalarGridSpec(
            num_scalar_prefetch=2,
            grid=(n_tiles,),
            in_specs=[
                tt_spec(EXPERT_TILE, lambda u, te, nu: (jnp.minimum(u, nu[0] - 1), 0)),
                pl.BlockSpec((1, d, ff), lambda u, te, nu: (te[u], 0, 0)),
                pl.BlockSpec((1, d, ff), lambda u, te, nu: (te[u], 0, 0)),
                pl.BlockSpec((1, ff, d), lambda u, te, nu: (te[u], 0, 0)),
            ],
            out_specs=tt_spec(EXPERT_TILE, lambda u, te, nu: (u, 0)),
            scratch_shapes=[pltpu.VMEM((d, ff), _bf16), pltpu.VMEM((d, ff), _bf16),
                            pltpu.VMEM((ff, d), _bf16)],
        ),
        out_shape=jax.ShapeDtypeStruct((n_rows * SUBLANES, LANES), _f32),
        compiler_params=pltpu.CompilerParams(
            dimension_semantics=("arbitrary",), vmem_limit_bytes=VMEM_LIMIT),
        name="experts",
    )(tile_expert, n_used, xs, wg, wu, wd)

    out = pl.pallas_call(
        _combine_kernel,
        grid_spec=pltpu.PrefetchScalarGridSpec(
            num_scalar_prefetch=1,
            grid=(ntok // COMBINE_TILE,),
            in_specs=[
                tt_spec(COMBINE_TILE, lambda g, p: (g, 0)),
                pl.BlockSpec((COMBINE_TILE, LANES), lambda g, p: (g, 0)),
                any_spec,
                pl.BlockSpec((1, d), lambda g, p: (0, 0)),
                pl.BlockSpec((1, d), lambda g, p: (0, 0)),
            ],
            out_specs=pl.BlockSpec((COMBINE_TILE, d), lambda g, p: (g, 0)),
            scratch_shapes=[pltpu.VMEM((COMBINE_SLOTS, 2, COMBINE_TILE * SUBLANES, LANES), _f32),
                            pltpu.SemaphoreType.DMA((COMBINE_SLOTS,))],
        ),
        out_shape=jax.ShapeDtypeStruct((ntok, d), _f32),
        compiler_params=pltpu.CompilerParams(
            dimension_semantics=("arbitrary",), vmem_limit_bytes=VMEM_LIMIT),
        name="combine_ln",
    )(pos, s2, info, ys, row2(ln_ffn_g[0]), row2(ln_ffn_b[0]))
    return out.reshape(bsz, seq, d)
```

```python
import functools

import jax
import jax.numpy as jnp
import numpy as np
from jax import lax
from jax.experimental import pallas as pl
from jax.experimental.pallas import tpu as pltpu

D_MODEL = 1024
N_META = 16
GLA_HEADS = 4
GLA_DK = 128
GLA_DV = 256
GLA_QK = GLA_HEADS * GLA_DK
GLA_VW = GLA_HEADS * GLA_DV
GLA_GATE_RANK = 16
GLA_GATE_TAU = 16.0
GLA_CHUNK = 64
MLA_HEADS = 16
MLA_Q_RANK = 384
MLA_KV_RANK = 256
MLA_NOPE = 64
MLA_ROPE = 32
MLA_DV = 64
MLA_QDIM = MLA_NOPE + MLA_ROPE
ROPE_BASE = 10000.0
N_GROUPS = 4
EXPERTS_PER_GROUP = 8
N_EXPERTS = N_GROUPS * EXPERTS_PER_GROUP
EXPERT_FF = 256
DEPTH = 1
ALPHA = (2.0 * DEPTH) ** 0.25
LN_EPS = 1e-5
RMS_EPS = 1e-6

LANES = 128
SUBLANES = 8
MXU_DIM = 256
TILE = 256
PADL = TILE - N_META
HEAD_PAD = LANES
BF16_ROWS = 16
KV_TILE = 512
Q_TILE = 2 * KV_TILE
V_AUG = MLA_DV + BF16_ROWS
MERGE_TILE = 1024
ROUTE_BLOCK = 256
EXPERT_TILE = 256
DISPATCH_TILE = 2048
COMBINE_TILE = 256
COMBINE_SLOTS = 3
NEG = -1e30
LOG2E = 1.4426950408889634
BOUND_SLACK = 1.02
MIN_SOFTMAX_SUM = 2.0 ** -100
VMEM_LIMIT = 56 * 1024 * 1024

_C_Q, _C_K, _C_V, _C_R = 0, 512, 1024, 2048
_C_A = 3072
_C_CQ = _C_A + LANES
_C_CKV = _C_CQ + MLA_Q_RANK
_C_KR = _C_CKV + MLA_KV_RANK
_C_GA = _C_KR + LANES
_C_GB = _C_GA + D_MODEL
_W_COLS = _C_GB + D_MODEL

_f32 = jnp.float32
_bf16 = jnp.bfloat16


def _dot(a, b):
    return jnp.dot(a, b, preferred_element_type=_f32)


def _dot_nt(a, b):
    return lax.dot_general(a, b, (((1,), (1,)), ((), ())), preferred_element_type=_f32)


def _dot_tn(a, b):
    return lax.dot_general(a, b, (((0,), (0,)), ((), ())), preferred_element_type=_f32)


def _layer_norm(x, g, b):
    mu = jnp.mean(x, axis=-1, keepdims=True)
    xc = x - mu
    var = jnp.mean(xc * xc, axis=-1, keepdims=True)
    return xc * lax.rsqrt(var + LN_EPS) * g + b


def _rms_norm(x, g):
    ms = jnp.mean(x * x, axis=-1, keepdims=True)
    return x * lax.rsqrt(ms + RMS_EPS) * g


def _sigmoid(x):
    return 1.0 / (1.0 + jnp.exp(-x))


def _tt_load(ref, n, lead=()):
    return jnp.concatenate(
        [ref[lead + (pl.ds(a, n, stride=SUBLANES), slice(None))] for a in range(SUBLANES)], axis=1)


def _tt_store(ref, x):
    n = x.shape[0]
    for a in range(SUBLANES):
        ref[pl.ds(a, n, stride=SUBLANES), :] = x[:, a * LANES:(a + 1) * LANES]


def _tt_rows(tok):
    return pl.ds(pl.multiple_of(tok * SUBLANES, SUBLANES), SUBLANES)


def _rope(x, cos, sin_lo, sin_hi):
    half = MLA_ROPE // 2
    from_hi = pltpu.roll(x, LANES - half, 1)
    from_lo = pltpu.roll(x, half, 1)
    return x * cos + from_hi * sin_lo + from_lo * sin_hi


def _inproj_kernel(x_ref, head_ref, lng_ref, lnb_ref, w_ref, w2_ref, gb_ref, qg_ref, wuqt_ref, kvg_ref,
                   wuk_ref, wuvt_ref, cos_ref, sl_ref, sh_ref, cost_ref, sint_ref, tril_ref,
                   s_ref, qt_ref, kt_ref, ke_ref, gv_ref, dec_ref, sr_ref, qm_ref, km_ref,
                   vm_ref, ga_ref, gbt_ref):
    t = pl.program_id(1)
    x_in = jnp.where(t == 0, head_ref[...], x_ref[0])
    sn = _layer_norm(x_in, lng_ref[...], lnb_ref[...])
    s_ref[0] = sn
    snb = sn.astype(_bf16)
    row = t * TILE + lax.broadcasted_iota(jnp.int32, (TILE, 1), 0)
    valid = row >= PADL

    proj = lambda c0, width: _dot(snb, w_ref[:, c0:c0 + width])
    a_lr = proj(_C_A, LANES)
    cq = proj(_C_CQ, MLA_Q_RANK)
    ckv = proj(_C_CKV, MLA_KV_RANK)
    kr_raw = proj(_C_KR, LANES)
    gq = proj(_C_Q, GLA_QK)
    gk = proj(_C_K, GLA_QK)
    gv_ref[0] = jnp.where(valid, proj(_C_V, GLA_VW), 0.0).astype(_bf16)
    r = proj(_C_R, GLA_VW)
    sr_ref[0] = (r * _sigmoid(r)).astype(_bf16)
    ga_ref[0] = _sigmoid(proj(_C_GA, D_MODEL)).astype(_bf16)
    gbt_ref[0] = _sigmoid(proj(_C_GB, D_MODEL)).astype(_bf16)

    z = _dot(a_lr.astype(_bf16), w2_ref[...]) + gb_ref[...]
    cqn = _rms_norm(cq, qg_ref[...]).astype(_bf16)
    ckvn = _rms_norm(ckv, kvg_ref[...]).astype(_bf16)
    qft = _dot_nt(wuqt_ref[...], cqn)
    kf = _dot(ckvn, wuk_ref[...])
    vt = _dot_nt(wuvt_ref[...], ckvn)

    la = (jnp.minimum(z, 0.0) - jnp.log1p(jnp.exp(-jnp.abs(z)))) * (1.0 / GLA_GATE_TAU)
    la = jnp.where(valid, la, 0.0)
    hi = la.astype(_bf16)
    r1 = la - hi.astype(_f32)
    mid = r1.astype(_bf16)
    lo = (r1 - mid.astype(_f32)).astype(_bf16)
    tril = tril_ref[...]
    bc = _dot(tril, hi) + _dot(tril, mid) + _dot(tril, lo)
    n_chunks = TILE // GLA_CHUNK
    lasts = [bc[c * GLA_CHUNK + GLA_CHUNK - 1:(c + 1) * GLA_CHUNK, :] for c in range(n_chunks)]
    for c in range(n_chunks):
        dec_ref[0, c:c + 1, :] = jnp.exp(lasts[c])
    b_last = jnp.concatenate(
        [jnp.broadcast_to(l, (GLA_CHUNK, GLA_QK)) for l in lasts], axis=0)
    gk = jnp.where(valid, gk, 0.0)
    qt_ref[0] = (gq * (GLA_DK ** -0.5) * jnp.exp(bc)).astype(_bf16)
    kt_ref[0] = (gk * jnp.exp(-bc)).astype(_bf16)
    ke_ref[0] = (gk * jnp.exp(b_last - bc)).astype(_bf16)

    cos = cos_ref[...]
    sl = sl_ref[...]
    sh = sh_ref[...]
    scale = (MLA_QDIM ** -0.5) * LOG2E
    cost = cost_ref[...]
    sint = sint_ref[...]
    half = MLA_ROPE // 2
    for h in range(MLA_HEADS):
        base = h * MLA_QDIM
        x1 = qft[base + MLA_NOPE:base + MLA_NOPE + half]
        x2 = qft[base + MLA_NOPE + half:base + MLA_QDIM]
        qm_ref[0, base:base + MLA_NOPE] = (qft[base:base + MLA_NOPE] * scale).astype(_bf16)
        qm_ref[0, base + MLA_NOPE:base + MLA_NOPE + half] = ((x1 * cost - x2 * sint) * scale).astype(_bf16)
        qm_ref[0, base + MLA_NOPE + half:base + MLA_QDIM] = ((x1 * sint + x2 * cost) * scale).astype(_bf16)
    kr = _rope(kr_raw, cos, sl, sh)
    for h in range(MLA_HEADS):
        km_ref[0, :, h * HEAD_PAD:(h + 1) * HEAD_PAD] = (
            kf[:, h * HEAD_PAD:(h + 1) * HEAD_PAD] + kr).astype(_bf16)
    for h in range(MLA_HEADS):
        vm_ref[0, h * V_AUG:h * V_AUG + MLA_DV] = vt[h * MLA_DV:(h + 1) * MLA_DV].astype(_bf16)
        vm_ref[0, h * V_AUG + MLA_DV:(h + 1) * V_AUG] = jnp.ones((V_AUG - MLA_DV, TILE), _bf16)


def _gla_kernel(qt_ref, kt_ref, ke_ref, gv_ref, dec_ref, sr_ref, ng_ref, o_ref, st_ref):
    t = pl.program_id(1)

    @pl.when(t == 0)
    def _():
        st_ref[...] = jnp.zeros_like(st_ref)

    ri = lax.broadcasted_iota(jnp.int32, (TILE, TILE), 0)
    ci = lax.broadcasted_iota(jnp.int32, (TILE, TILE), 1)
    visible = (ri >= ci) & (ri // GLA_CHUNK == ci // GLA_CHUNK)
    ng = ng_ref[...]
    n_chunks = TILE // GLA_CHUNK
    chunk_rows = [slice(c * GLA_CHUNK, (c + 1) * GLA_CHUNK) for c in range(n_chunks)]
    k_cols = [slice(h * GLA_DK, (h + 1) * GLA_DK) for h in range(GLA_HEADS)]
    v_cols = [slice(h * GLA_DV, (h + 1) * GLA_DV) for h in range(GLA_HEADS)]
    att = [jnp.where(visible, _dot_nt(qt_ref[0, :, k_cols[h]], kt_ref[0, :, k_cols[h]]), 0.0)
           for h in range(GLA_HEADS)]
    upd = [[_dot_tn(gv_ref[0, rows, v_cols[h]], ke_ref[0, rows, k_cols[h]]) for rows in chunk_rows]
           for h in range(GLA_HEADS)]
    o_intra = [_dot(att[h].astype(_bf16), gv_ref[0, :, v_cols[h]]) for h in range(GLA_HEADS)]
    for h in range(GLA_HEADS):
        st = st_ref[h]
        for c, rows in enumerate(chunk_rows):
            o = o_intra[h][rows] + _dot_nt(qt_ref[0, rows, k_cols[h]], st.astype(_bf16))
            st = st * dec_ref[0, c:c + 1, k_cols[h]] + upd[h][c]
            o = _rms_norm(o, ng) * sr_ref[0, rows, v_cols[h]].astype(_f32)
            o_ref[0, rows, v_cols[h]] = o.astype(_bf16)
        st_ref[h] = st


MLA_GROUP = 4


def _mla_tile_start(j):
    return pl.multiple_of(TILE + j * KV_TILE, TILE)


def _mla_queries(q_ref, h, cols):
    q = q_ref[0, h * MLA_QDIM:(h + 1) * MLA_QDIM, cols]
    return jnp.concatenate([q, jnp.zeros((HEAD_PAD - MLA_QDIM, q.shape[1]), q.dtype)], axis=0)


def _mla_finish(o_ref, acc_ref):
    outs = []
    for h in range(MLA_GROUP):
        a = acc_ref[h]
        outs.append(a[0:MLA_DV] / a[MLA_DV:MLA_DV + 1])
    o_ref[0] = jnp.concatenate(outs, axis=0).T.astype(_bf16)


def _mla_kernel(q_ref, k_ref, v_ref, o_ref, acc_ref, knorm_ref):
    i = pl.program_id(2)
    heads = MLA_GROUP
    ones = jnp.ones((HEAD_PAD, LANES), _bf16)

    @pl.when(i == 0)
    def _():
        for h in range(heads):
            kk = k_ref[0, :, h * HEAD_PAD:(h + 1) * HEAD_PAD].astype(_f32)
            hi = (kk * kk).astype(_bf16)
            knorm_ref[h] = jnp.max(_dot(hi, ones), axis=0, keepdims=True)

    half = slice(KV_TILE, Q_TILE)
    q_t = [_mla_queries(q_ref, h, slice(None)) for h in range(heads)]
    q_hi = [_mla_queries(q_ref, h, half) for h in range(heads)]

    def score_bound(h, q):
        qq = q.astype(_f32)
        qn2 = jnp.sum(qq * qq, axis=0, keepdims=True)
        return jnp.sqrt(qn2 * knorm_ref[h][:, 0:1]) * BOUND_SLACK

    bound = [score_bound(h, q_t[h]) for h in range(heads)]
    bound_hi = [score_bound(h, q_hi[h]) for h in range(heads)]

    def keys(j, h):
        return k_ref[0, pl.ds(_mla_tile_start(j), KV_TILE), h * HEAD_PAD:(h + 1) * HEAD_PAD]

    def weighted(j, h, p):
        return _dot(v_ref[0, h * V_AUG:(h + 1) * V_AUG, pl.ds(_mla_tile_start(j), KV_TILE)], p)

    k_row = lax.broadcasted_iota(jnp.int32, (KV_TILE, Q_TILE), 0)
    q_col = lax.broadcasted_iota(jnp.int32, (KV_TILE, Q_TILE), 1)
    k_row_sq = lax.broadcasted_iota(jnp.int32, (KV_TILE, KV_TILE), 0)
    q_col_sq = lax.broadcasted_iota(jnp.int32, (KV_TILE, KV_TILE), 1)
    s_meta = [_dot(k_ref[0, PADL:TILE, h * HEAD_PAD:(h + 1) * HEAD_PAD], q_t[h]) for h in range(heads)]
    s_lo = [jnp.where(k_row <= q_col, _dot(keys(2 * i, h), q_t[h]), NEG) for h in range(heads)]
    s_hi = [jnp.where(k_row_sq <= q_col_sq, _dot(keys(2 * i + 1, h), q_hi[h]), NEG)
            for h in range(heads)]
    for h in range(heads):
        p_meta = jnp.concatenate([jnp.zeros((PADL, Q_TILE), _bf16),
                                  jnp.exp2(s_meta[h] - bound[h]).astype(_bf16)], axis=0)
        acc_ref[h] = (_dot(v_ref[0, h * V_AUG:(h + 1) * V_AUG, 0:TILE], p_meta)
                      + weighted(2 * i, h, jnp.exp2(s_lo[h] - bound[h]).astype(_bf16)))
        acc_ref[h, :, half] += weighted(2 * i + 1, h, jnp.exp2(s_hi[h] - bound_hi[h]).astype(_bf16))

    def body(jj, c):
        tiles = (2 * jj, 2 * jj + 1)
        s = [[_dot(keys(j, h), q_t[h]) for h in range(heads)] for j in tiles]
        for h in range(heads):
            acc_ref[h] += sum(weighted(j, h, jnp.exp2(s[n][h] - bound[h]).astype(_bf16))
                              for n, j in enumerate(tiles))
        return c

    lax.fori_loop(0, i, body, 0)
    _mla_finish(o_ref, acc_ref)

    l_min = functools.reduce(jnp.minimum,
                             [jnp.min(acc_ref[h, MLA_DV:MLA_DV + 1, :]) for h in range(heads)])

    @pl.when(jnp.logical_not(l_min >= MIN_SOFTMAX_SUM))
    def _():
        _mla_exact(q_ref, k_ref, v_ref, o_ref, acc_ref)


def _mla_exact(q_ref, k_ref, v_ref, o_ref, acc_ref):
    i = pl.program_id(2)
    heads = MLA_GROUP
    q_t = [_mla_queries(q_ref, h, slice(None)) for h in range(heads)]

    ms = []
    for h in range(heads):
        kb = k_ref[0, PADL:TILE, h * HEAD_PAD:(h + 1) * HEAD_PAD]
        s = _dot(kb, q_t[h])
        m0 = jnp.max(s, axis=0, keepdims=True)
        p = jnp.concatenate([jnp.zeros((PADL, Q_TILE), _bf16), jnp.exp2(s - m0).astype(_bf16)], axis=0)
        acc_ref[h] = _dot(v_ref[0, h * V_AUG:(h + 1) * V_AUG, 0:TILE], p)
        ms.append(m0)

    k_row = lax.broadcasted_iota(jnp.int32, (KV_TILE, Q_TILE), 0)
    q_col = lax.broadcasted_iota(jnp.int32, (KV_TILE, Q_TILE), 1)

    def body(j, ms):
        visible = (j - 2 * i) * KV_TILE + k_row <= q_col
        out = []
        for h in range(heads):
            s = _dot(k_ref[0, pl.ds(_mla_tile_start(j), KV_TILE), h * HEAD_PAD:(h + 1) * HEAD_PAD],
                     q_t[h])
            s = jnp.where(visible, s, NEG)
            vb = v_ref[0, h * V_AUG:(h + 1) * V_AUG, pl.ds(_mla_tile_start(j), KV_TILE)]
            m_new = jnp.maximum(ms[h], jnp.max(s, axis=0, keepdims=True))
            alpha = jnp.exp2(ms[h] - m_new)
            acc_ref[h] = alpha * acc_ref[h] + _dot(vb, jnp.exp2(s - m_new).astype(_bf16))
            out.append(m_new)
        return tuple(out)

    lax.fori_loop(0, 2 * i + 2, body, tuple(ms))
    _mla_finish(o_ref, acc_ref)


def _merge_kernel(og_ref, om_ref, ga_ref, gbt_ref, s_ref, wbg_ref, wbm_ref, wo_ref, lng_ref,
                  lnb_ref, rwh_ref, rwl_ref, rb_ref, tril_ref,
                  s2_ref, info_ref, infot_ref, cnt_ref, carry_ref):
    step = pl.program_id(0)

    @pl.when(step == 0)
    def _():
        carry_ref[...] = jnp.zeros_like(carry_ref)

    n_blk = MERGE_TILE // ROUTE_BLOCK
    blocks = [slice(i * ROUTE_BLOCK, (i + 1) * ROUTE_BLOCK) for i in range(n_blk)]
    d = wo_ref.shape[0]
    col_blocks = [slice(c, c + MXU_DIM) for c in range(0, d, MXU_DIM)]
    lane = lax.broadcasted_iota(jnp.int32, (ROUTE_BLOCK, LANES), 1)
    is_g = lane < N_GROUPS
    merged, s2, logits, infos = {}, {}, {}, {}
    carry = [carry_ref[0:1, :]]

    def branches(i):
        rows = blocks[i]
        merged[i] = jnp.concatenate(
            [(ga_ref[rows, cols].astype(_f32) * _dot(og_ref[rows, :], wbg_ref[:, cols])
              + gbt_ref[rows, cols].astype(_f32) * _dot(om_ref[rows, :], wbm_ref[:, cols])
              ).astype(_bf16) for cols in col_blocks], axis=1)

    def residual_norm(i):
        y = ALPHA * s_ref[blocks[i], :] + _dot(merged[i], wo_ref[...])
        s2[i] = _layer_norm(y, lng_ref[...], lnb_ref[...])

    def router_logits(i):
        xh = s2[i].astype(_bf16)
        xl = (s2[i] - xh.astype(_f32)).astype(_bf16)
        logits[i] = (_dot(xh, rwh_ref[...]) + _dot(xl, rwh_ref[...]) + _dot(xh, rwl_ref[...])
                     + rb_ref[...])

    def route(i):
        gl = jnp.where(is_g, logits[i], NEG)
        gmax = jnp.max(gl, axis=-1, keepdims=True)
        gidx = jnp.min(jnp.where(gl == gmax, lane, LANES), axis=-1, keepdims=True)
        p_g = 1.0 / jnp.sum(jnp.where(is_g, jnp.exp(gl - gmax), 0.0), axis=-1, keepdims=True)
        lo = N_GROUPS + EXPERTS_PER_GROUP * gidx
        el = jnp.where((lane >= lo) & (lane < lo + EXPERTS_PER_GROUP), logits[i], NEG)
        v1 = jnp.max(el, axis=-1, keepdims=True)
        i1 = jnp.min(jnp.where(el == v1, lane, LANES), axis=-1, keepdims=True)
        el2 = jnp.where(lane == i1, NEG, el)
        v2 = jnp.max(el2, axis=-1, keepdims=True)
        i2 = jnp.min(jnp.where(el2 == v2, lane, LANES), axis=-1, keepdims=True)
        tt = jnp.exp(v2 - v1)
        p1 = 1.0 / (1.0 + tt)
        p2 = tt / (1.0 + tt)
        e1 = i1 - N_GROUPS
        e2 = i2 - N_GROUPS
        hit1 = lane == e1
        hit2 = lane == e2
        onehot = jnp.where(hit1 | hit2, 1.0, 0.0)
        before = _dot(tril_ref[...], onehot.astype(_bf16)) + carry[0]
        r1 = jnp.sum(jnp.where(hit1, before, 0.0), axis=-1, keepdims=True)
        r2 = jnp.sum(jnp.where(hit2, before, 0.0), axis=-1, keepdims=True)
        carry[0] = carry[0] + jnp.sum(onehot, axis=0, keepdims=True)
        infos[i] = jnp.where(lane == 0, e1.astype(_f32),
                   jnp.where(lane == 1, e2.astype(_f32),
                   jnp.where(lane == 2, p_g * p1,
                   jnp.where(lane == 3, p_g * p2,
                   jnp.where(lane == 4, r1,
                   jnp.where(lane == 5, r2, 0.0))))))

    for stage in (branches, residual_norm, router_logits, route):
        for i in range(n_blk):
            stage(i)
    _tt_store(s2_ref, jnp.concatenate([s2[i] for i in range(n_blk)], axis=0))
    carry_ref[...] = jnp.broadcast_to(carry[0], carry_ref.shape)
    cnt_ref[...] = jnp.broadcast_to(carry[0], cnt_ref.shape)
    info = jnp.concatenate([infos[i] for i in range(n_blk)], axis=0)
    info_ref[...] = info
    infot_ref[...] = info.T[0:SUBLANES]


def _dispatch_kernel(pos_ref, zrow_ref, nused_ref, s2_ref, xs_hbm, zero_ref, sems):
    g = pl.program_id(0)
    zero_sem = sems.at[1]
    row_sem = sems.at[0]

    tile_rows = EXPERT_TILE * SUBLANES

    def zero_copy(row):
        start = pl.multiple_of(row * SUBLANES, tile_rows)
        return pltpu.make_async_copy(zero_ref, xs_hbm.at[pl.ds(start, tile_rows)], zero_sem)

    @pl.when(g == 0)
    def _():
        zero_ref[...] = jnp.zeros_like(zero_ref)

        def start(e, c):
            @pl.when(zrow_ref[e] >= 0)
            def _():
                zero_copy(zrow_ref[e]).start()
            return c

        def wait(e, c):
            @pl.when(zrow_ref[e] >= 0)
            def _():
                zero_copy(0).wait()
            return c

        def start_tail(u, c):
            zero_copy(u * EXPERT_TILE).start()
            return c

        def wait_tail(u, c):
            zero_copy(0).wait()
            return c

        n_tiles = xs_hbm.shape[0] // tile_rows
        lax.fori_loop(0, N_EXPERTS, start, 0)
        lax.fori_loop(nused_ref[0], n_tiles, start_tail, 0)
        lax.fori_loop(0, N_EXPERTS, wait, 0)
        lax.fori_loop(nused_ref[0], n_tiles, wait_tail, 0)

    n_tok = pl.num_programs(0) * DISPATCH_TILE

    def issue(r, c):
        tok = g * DISPATCH_TILE + r
        for k in range(2):
            pltpu.make_async_copy(s2_ref.at[_tt_rows(r)],
                                  xs_hbm.at[_tt_rows(pos_ref[k * n_tok + tok])], row_sem).start(priority=k)
        return c

    lax.fori_loop(0, DISPATCH_TILE, issue, 0, unroll=8)
    for k in range(2):
        pltpu.make_async_copy(s2_ref, xs_hbm.at[pl.ds(0, DISPATCH_TILE * SUBLANES)], row_sem).wait()


def _expert_kernel(te_ref, nused_ref, x_ref, wg_ref, wu_ref, wd_ref, o_ref, wgb_ref, wub_ref, wdb_ref):
    u = pl.program_id(0)
    used = u < nused_ref[0]
    first_of_expert = jnp.logical_or(u == 0, te_ref[u] != te_ref[jnp.maximum(u - 1, 0)])

    @pl.when(jnp.logical_and(used, first_of_expert))
    def _():
        wgb_ref[...] = wg_ref[0].astype(_bf16)
        wub_ref[...] = wu_ref[0].astype(_bf16)
        wdb_ref[...] = wd_ref[0].astype(_bf16)

    @pl.when(used)
    def _():
        x = _tt_load(x_ref, EXPERT_TILE).astype(_bf16)
        a = _dot(x, wgb_ref[...])
        up = _dot(x, wub_ref[...])
        hid = a * _sigmoid(a) * up
        _tt_store(o_ref, _dot(hid.astype(_bf16), wdb_ref[...]))

    @pl.when(u >= nused_ref[0])
    def _():
        o_ref[...] = jnp.zeros_like(o_ref)


def _combine_kernel(pos_ref, s2_ref, info_ref, ys_hbm, lng_ref, lnb_ref, o_ref, buf_ref, sems):
    g = pl.program_id(0)
    last = pl.num_programs(0) - 1
    n_tok = pl.num_programs(0) * COMBINE_TILE

    def start_copies(tile, slot, r):
        tok = tile * COMBINE_TILE + r
        for k in range(2):
            pltpu.make_async_copy(ys_hbm.at[_tt_rows(pos_ref[k * n_tok + tok])],
                                  buf_ref.at[slot, k, _tt_rows(r)], sems.at[slot]).start(priority=k)

    def wait_slot(slot):
        for k in range(2):
            pltpu.make_async_copy(ys_hbm.at[pl.ds(0, COMBINE_TILE * SUBLANES)], buf_ref.at[slot, k],
                                  sems.at[slot]).wait()

    @pl.when(g == 0)
    def _():
        def body(r, c):
            for tile in range(COMBINE_SLOTS - 1):
                start_copies(tile, tile, r)
            return c

        lax.fori_loop(0, COMBINE_TILE, body, 0, unroll=8)

    slot = g % COMBINE_SLOTS
    wait_slot(slot)
    info = info_ref[...]
    y = (ALPHA * _tt_load(s2_ref, COMBINE_TILE)
         + info[:, 2:3] * _tt_load(buf_ref, COMBINE_TILE, (slot, 0))
         + info[:, 3:4] * _tt_load(buf_ref, COMBINE_TILE, (slot, 1)))
    o_ref[...] = _layer_norm(y, lng_ref[...], lnb_ref[...])

    ahead = COMBINE_SLOTS - 1
    ahead_slot = (g + ahead) % COMBINE_SLOTS
    for r in range(COMBINE_TILE):
        start_copies(jnp.minimum(g + ahead, last), ahead_slot, r)

    @pl.when(g == last)
    def _():
        for n in range(1, COMBINE_SLOTS):
            wait_slot((g + n) % COMBINE_SLOTS)


def _const_spec(shape):
    nd = len(shape)
    return pl.BlockSpec(shape, lambda *_: (0,) * nd)


def _rope_tables(lp):
    pos = jnp.maximum(jnp.arange(lp, dtype=_f32) - PADL, 0.0)
    inv_freq = ROPE_BASE ** (-jnp.arange(0, MLA_ROPE, 2, dtype=_f32) / MLA_ROPE)
    ang = pos[:, None] * inv_freq[None, :]
    cos, sin = jnp.cos(ang), jnp.sin(ang)
    half = MLA_ROPE // 2
    ones = jnp.ones((lp, MLA_NOPE), _f32)
    zeros_n = jnp.zeros((lp, MLA_NOPE), _f32)
    zeros_h = jnp.zeros((lp, half), _f32)
    tail1 = jnp.ones((lp, LANES - MLA_QDIM), _f32)
    tail0 = jnp.zeros((lp, LANES - MLA_QDIM), _f32)
    cos_t = jnp.concatenate([ones, cos, cos, tail1], axis=1)
    sin_lo = jnp.concatenate([zeros_n, -sin, zeros_h, tail0], axis=1)
    sin_hi = jnp.concatenate([zeros_n, zeros_h, sin, tail0], axis=1)
    return cos_t, sin_lo, sin_hi, cos.T, sin.T


def _pad_heads(w, width):
    k = w.shape[0]
    w = w.reshape(k, MLA_HEADS, width)
    w = jnp.pad(w, ((0, 0), (0, 0), (0, HEAD_PAD - width)))
    return w.reshape(k, MLA_HEADS * HEAD_PAD)


def kernel(x, meta_tokens, ln_emb_g, ln_emb_b, w_in, gla_gate_w2, gla_gate_b, gla_norm_g, mla_q_norm_g, mla_w_uq, mla_kv_norm_g, mla_w_uk, mla_w_uv, w_branch_gla, w_branch_mla, w_out, ln_mix_g, ln_mix_b, router_group_w, router_group_b, router_expert_w, router_expert_b, expert_w_gate, expert_w_up, expert_w_down, ln_ffn_g, ln_ffn_b):
    bsz, seq, d = x.shape
    assert d == D_MODEL and seq % Q_TILE == 0 and w_in.shape[0] == DEPTH == 1
    lp = PADL + N_META + seq
    nt = lp // TILE
    ntok = bsz * seq
    row2 = lambda v: v.reshape(1, -1).astype(_f32)

    head_tile = jnp.concatenate([jnp.zeros((PADL, d), _f32), meta_tokens.astype(_f32)], axis=0)
    wi = w_in[0]
    o_a = 2 * GLA_QK + 2 * GLA_VW
    o_cq = o_a + GLA_GATE_RANK
    o_ckv = o_cq + MLA_Q_RANK
    o_kr = o_ckv + MLA_KV_RANK
    o_ga = o_kr + MLA_ROPE
    w_a = jnp.pad(wi[:, o_a:o_cq], ((0, 0), (0, LANES - GLA_GATE_RANK)))
    w_kr = jnp.pad(wi[:, o_kr:o_ga], ((0, 0), (MLA_NOPE, LANES - MLA_QDIM)))
    w_all = jnp.concatenate([wi[:, :o_a], w_a, wi[:, o_cq:o_kr], w_kr, wi[:, o_ga:]], axis=1).astype(_bf16)
    assert w_all.shape == (d, _W_COLS)
    w2p = jnp.pad(gla_gate_w2[0], ((0, LANES - GLA_GATE_RANK), (0, 0))).astype(_bf16)
    wuqt = mla_w_uq[0].T.astype(_bf16)
    wuk = _pad_heads(mla_w_uk[0], MLA_NOPE).astype(_bf16)
    wuvt = mla_w_uv[0].T.astype(_bf16)
    cos_t, sin_lo, sin_hi, cos_tr, sin_tr = _rope_tables(lp)
    blk = np.arange(TILE)
    tril_chunks = jnp.asarray(
        ((blk[:, None] >= blk[None, :]) & (blk[:, None] // GLA_CHUNK == blk[None, :] // GLA_CHUNK)),
        dtype=_bf16)

    pad_map = lambda b, t: (b, t, 0)
    real_map = lambda b, t: (b, jnp.maximum(t - 1, 0), 0)
    real_map_t = lambda b, t: (b, 0, jnp.maximum(t - 1, 0))
    out_shapes = (
        jax.ShapeDtypeStruct((bsz, seq, d), _f32),
        jax.ShapeDtypeStruct((bsz, lp, GLA_QK), _bf16),
        jax.ShapeDtypeStruct((bsz, lp, GLA_QK), _bf16),
        jax.ShapeDtypeStruct((bsz, lp, GLA_QK), _bf16),
        jax.ShapeDtypeStruct((bsz, lp, GLA_VW), _bf16),
        jax.ShapeDtypeStruct((bsz * nt, TILE // GLA_CHUNK, GLA_QK), _f32),
        jax.ShapeDtypeStruct((bsz, seq, GLA_VW), _bf16),
        jax.ShapeDtypeStruct((bsz, MLA_HEADS * MLA_QDIM, seq), _bf16),
        jax.ShapeDtypeStruct((bsz, lp, MLA_HEADS * HEAD_PAD), _bf16),
        jax.ShapeDtypeStruct((bsz, MLA_HEADS * V_AUG, lp), _bf16),
        jax.ShapeDtypeStruct((bsz, seq, d), _bf16),
        jax.ShapeDtypeStruct((bsz, seq, d), _bf16),
    )
    out_specs = (
        pl.BlockSpec((1, TILE, d), real_map),
        pl.BlockSpec((1, TILE, GLA_QK), pad_map),
        pl.BlockSpec((1, TILE, GLA_QK), pad_map),
        pl.BlockSpec((1, TILE, GLA_QK), pad_map),
        pl.BlockSpec((1, TILE, GLA_VW), pad_map),
        pl.BlockSpec((1, TILE // GLA_CHUNK, GLA_QK), lambda b, t: (b * nt + t, 0, 0)),
        pl.BlockSpec((1, TILE, GLA_VW), real_map),
        pl.BlockSpec((1, MLA_HEADS * MLA_QDIM, TILE), real_map_t),
        pl.BlockSpec((1, TILE, MLA_HEADS * HEAD_PAD), pad_map),
        pl.BlockSpec((1, MLA_HEADS * V_AUG, TILE), lambda b, t: (b, 0, t)),
        pl.BlockSpec((1, TILE, d), real_map),
        pl.BlockSpec((1, TILE, d), real_map),
    )
    tab_spec = pl.BlockSpec((TILE, LANES), lambda b, t: (t, 0))
    tabt_spec = pl.BlockSpec((MLA_ROPE // 2, TILE), lambda b, t: (0, t))
    (s_emb, qt, kt, ke, gv, dec, sr, qm, km, vm, gate_a, gate_b) = pl.pallas_call(
        _inproj_kernel,
        grid=(bsz, nt),
        in_specs=[
            pl.BlockSpec((1, TILE, d), real_map),
            _const_spec((TILE, d)),
            _const_spec((1, d)), _const_spec((1, d)),
            _const_spec((d, _W_COLS)),
            _const_spec((LANES, GLA_QK)), _const_spec((1, GLA_QK)),
            _const_spec((1, MLA_Q_RANK)), _const_spec((MLA_HEADS * MLA_QDIM, MLA_Q_RANK)),
            _const_spec((1, MLA_KV_RANK)), _const_spec((MLA_KV_RANK, MLA_HEADS * HEAD_PAD)),
            _const_spec((MLA_HEADS * MLA_DV, MLA_KV_RANK)),
            tab_spec, tab_spec, tab_spec, tabt_spec, tabt_spec,
            _const_spec((TILE, TILE)),
        ],
        out_specs=out_specs,
        out_shape=out_shapes,
        compiler_params=pltpu.CompilerParams(
            dimension_semantics=("arbitrary", "arbitrary"), vmem_limit_bytes=VMEM_LIMIT),
        name="inproj",
    )(x, head_tile, row2(ln_emb_g), row2(ln_emb_b), w_all, w2p, row2(gla_gate_b[0]),
      row2(mla_q_norm_g[0]), wuqt, row2(mla_kv_norm_g[0]), wuk, wuvt, cos_t, sin_lo, sin_hi,
      cos_tr, sin_tr, tril_chunks)

    o_gla = pl.pallas_call(
        _gla_kernel,
        grid=(bsz, nt),
        in_specs=[
            pl.BlockSpec((1, TILE, GLA_QK), pad_map),
            pl.BlockSpec((1, TILE, GLA_QK), pad_map),
            pl.BlockSpec((1, TILE, GLA_QK), pad_map),
            pl.BlockSpec((1, TILE, GLA_VW), pad_map),
            pl.BlockSpec((1, TILE // GLA_CHUNK, GLA_QK), lambda b, t: (b * nt + t, 0, 0)),
            pl.BlockSpec((1, TILE, GLA_VW), real_map),
            _const_spec((1, GLA_DV)),
        ],
        out_specs=pl.BlockSpec((1, TILE, GLA_VW), real_map),
        out_shape=jax.ShapeDtypeStruct((bsz, seq, GLA_VW), _bf16),
        scratch_shapes=[pltpu.VMEM((GLA_HEADS, GLA_DV, GLA_DK), _f32)],
        compiler_params=pltpu.CompilerParams(
            dimension_semantics=("arbitrary", "arbitrary"), vmem_limit_bytes=VMEM_LIMIT),
        name="gla",
    )(qt, kt, ke, gv, dec, sr, row2(gla_norm_g[0]))

    group_width = MLA_GROUP * HEAD_PAD
    o_mla = pl.pallas_call(
        _mla_kernel,
        grid=(bsz, MLA_HEADS // MLA_GROUP, seq // Q_TILE),
        in_specs=[
            pl.BlockSpec((1, MLA_GROUP * MLA_QDIM, Q_TILE), lambda b, hp, i: (b, hp, i)),
            pl.BlockSpec((1, lp, group_width), lambda b, hp, i: (b, 0, hp)),
            pl.BlockSpec((1, MLA_GROUP * V_AUG, lp), lambda b, hp, i: (b, hp, 0)),
        ],
        out_specs=pl.BlockSpec((1, Q_TILE, MLA_GROUP * MLA_DV), lambda b, hp, i: (b, i, hp)),
        out_shape=jax.ShapeDtypeStruct((bsz, seq, MLA_HEADS * MLA_DV), _bf16),
        scratch_shapes=[pltpu.VMEM((MLA_GROUP, V_AUG, Q_TILE), _f32),
                        pltpu.VMEM((MLA_GROUP, 1, LANES), _f32)],
        compiler_params=pltpu.CompilerParams(
            dimension_semantics=("arbitrary", "arbitrary", "arbitrary"),
            vmem_limit_bytes=VMEM_LIMIT),
        name="mla",
    )(qm, km, vm)

    rw = jnp.concatenate([router_group_w[0], router_expert_w[0]], axis=1)
    rw = jnp.pad(rw, ((0, 0), (0, LANES - rw.shape[1])))
    rwh = rw.astype(_bf16)
    rwl = (rw - rwh.astype(_f32)).astype(_bf16)
    rb = jnp.concatenate([router_group_b[0], router_expert_b[0]])
    rb = jnp.pad(rb, (0, LANES - rb.shape[0])).reshape(1, LANES)
    mi = np.arange(ROUTE_BLOCK)
    tril_strict = jnp.asarray(mi[:, None] > mi[None, :], dtype=_bf16)
    flat = lambda a: a.reshape(ntok, a.shape[-1])
    tok_spec = lambda w: pl.BlockSpec((MERGE_TILE, w), lambda g: (g, 0))
    tt_spec = lambda n, index_map: pl.BlockSpec((n * SUBLANES, LANES), index_map)
    s2, info, info_t, cnt = pl.pallas_call(
        _merge_kernel,
        grid=(ntok // MERGE_TILE,),
        in_specs=[tok_spec(d), tok_spec(d), tok_spec(d), tok_spec(d), tok_spec(d),
                  _const_spec((d, d)), _const_spec((d, d)), _const_spec((d, d)),
                  _const_spec((1, d)), _const_spec((1, d)),
                  _const_spec((d, LANES)), _const_spec((d, LANES)), _const_spec((1, LANES)),
                  _const_spec((ROUTE_BLOCK, ROUTE_BLOCK))],
        out_specs=(tt_spec(MERGE_TILE, lambda g: (g, 0)), tok_spec(LANES),
                   pl.BlockSpec((SUBLANES, MERGE_TILE), lambda g: (0, g)), _const_spec((8, LANES))),
        out_shape=(jax.ShapeDtypeStruct((ntok * SUBLANES, LANES), _f32),
                   jax.ShapeDtypeStruct((ntok, LANES), _f32),
                   jax.ShapeDtypeStruct((SUBLANES, ntok), _f32),
                   jax.ShapeDtypeStruct((8, LANES), _f32)),
        scratch_shapes=[pltpu.VMEM((8, LANES), _f32)],
        compiler_params=pltpu.CompilerParams(
            dimension_semantics=("arbitrary",), vmem_limit_bytes=VMEM_LIMIT),
        name="merge_router",
    )(flat(o_gla), flat(o_mla), flat(gate_a), flat(gate_b), flat(s_emb),
      w_branch_gla[0].astype(_bf16), w_branch_mla[0].astype(_bf16), w_out[0].astype(_bf16),
      row2(ln_mix_g[0]), row2(ln_mix_b[0]), rwh, rwl, rb, tril_strict)

    n_tiles = (2 * ntok + N_EXPERTS * (EXPERT_TILE - 1)) // EXPERT_TILE
    n_rows = n_tiles * EXPERT_TILE
    e_idx = info_t[0:2].astype(jnp.int32)
    rank = info_t[4:6].astype(jnp.int32)
    counts = cnt[0, :N_EXPERTS].astype(jnp.int32)
    padded = ((counts + EXPERT_TILE - 1) // EXPERT_TILE) * EXPERT_TILE
    ends = jnp.cumsum(padded)
    starts = ends - padded
    expert_ids = jnp.arange(N_EXPERTS, dtype=jnp.int32)[:, None, None]
    start_of = jnp.sum(jnp.where(e_idx[None] == expert_ids, starts[:, None, None], 0), axis=0)
    pos = (start_of + rank).reshape(-1)
    tile_start = jnp.arange(n_tiles, dtype=jnp.int32) * EXPERT_TILE
    tile_expert = jnp.minimum(
        jnp.sum((ends[None, :] <= tile_start[:, None]).astype(jnp.int32), axis=1), N_EXPERTS - 1)
    n_used = (ends[-1:] // EXPERT_TILE).astype(jnp.int32)
    zero_row = jnp.where(padded > 0, ends - EXPERT_TILE, -1).astype(jnp.int32)

    any_spec = pl.BlockSpec(memory_space=pl.ANY)
    xs = pl.pallas_call(
        _dispatch_kernel,
        grid_spec=pltpu.PrefetchScalarGridSpec(
            num_scalar_prefetch=3,
            grid=(ntok // DISPATCH_TILE,),
            in_specs=[tt_spec(DISPATCH_TILE, lambda g, p, z, nu: (g, 0))],
            out_specs=any_spec,
            scratch_shapes=[pltpu.VMEM((EXPERT_TILE * SUBLANES, LANES), _f32),
                            pltpu.SemaphoreType.DMA((2,))],
        ),
        out_shape=jax.ShapeDtypeStruct((n_rows * SUBLANES, LANES), _f32),
        compiler_params=pltpu.CompilerParams(
            dimension_semantics=("arbitrary",), vmem_limit_bytes=VMEM_LIMIT, has_side_effects=True),
        name="dispatch",
    )(pos, zero_row, n_used, s2)

    ff = EXPERT_FF
    wg = expert_w_gate[0].reshape(N_EXPERTS, d, ff)
    wu = expert_w_up[0].reshape(N_EXPERTS, d, ff)
    wd = expert_w_down[0].reshape(N_EXPERTS, ff, d)
    ys = pl.pallas_call(
        _expert_kernel,
        grid_spec=pltpu.PrefetchScalarGridSpec(
            num_scalar_prefetch=2,
            grid=(n_tiles,),
            in_specs=[
                tt_spec(EXPERT_TILE, lambda u, te, nu: (jnp.minimum(u, nu[0] - 1), 0)),
                pl.BlockSpec((1, d, ff), lambda u, te, nu: (te[u], 0, 0)),
                pl.BlockSpec((1, d, ff), lambda u, te, nu: (te[u], 0, 0)),
                pl.BlockSpec((1, ff, d), lambda u, te, nu: (te[u], 0, 0)),
            ],
            out_specs=tt_spec(EXPERT_TILE, lambda u, te, nu: (u, 0)),
            scratch_shapes=[pltpu.VMEM((d, ff), _bf16), pltpu.VMEM((d, ff), _bf16),
                            pltpu.VMEM((ff, d), _bf16)],
        ),
        out_shape=jax.ShapeDtypeStruct((n_rows * SUBLANES, LANES), _f32),
        compiler_params=pltpu.CompilerParams(
            dimension_semantics=("arbitrary",), vmem_limit_bytes=VMEM_LIMIT),
        name="experts",
    )(tile_expert, n_used, xs, wg, wu, wd)

    out = pl.pallas_call(
        _combine_kernel,
        grid_spec=pltpu.PrefetchScalarGridSpec(
            num_scalar_prefetch=1,
            grid=(ntok // COMBINE_TILE,),
            in_specs=[
                tt_spec(COMBINE_TILE, lambda g, p: (g, 0)),
                pl.BlockSpec((COMBINE_TILE, LANES), lambda g, p: (g, 0)),
                any_spec,
                pl.BlockSpec((1, d), lambda g, p: (0, 0)),
                pl.BlockSpec((1, d), lambda g, p: (0, 0)),
            ],
            out_specs=pl.BlockSpec((COMBINE_TILE, d), lambda g, p: (g, 0)),
            scratch_shapes=[pltpu.VMEM((COMBINE_SLOTS, 2, COMBINE_TILE * SUBLANES, LANES), _f32),
                            pltpu.SemaphoreType.DMA((COMBINE_SLOTS,))],
        ),
        out_shape=jax.ShapeDtypeStruct((ntok, d), _f32),
        compiler_params=pltpu.CompilerParams(
            dimension_semantics=("arbitrary",), vmem_limit_bytes=VMEM_LIMIT),
        name="combine_ln",
    )(pos, s2, info, ys, row2(ln_ffn_g[0]), row2(ln_ffn_b[0]))
    return out.reshape(bsz, seq, d)
```

```python
import functools

import jax
import jax.numpy as jnp
import numpy as np
from jax import lax
from jax.experimental import pallas as pl
from jax.experimental.pallas import tpu as pltpu

D_MODEL = 1024
N_META = 16
GLA_HEADS = 4
GLA_DK = 128
GLA_DV = 256
GLA_QK = GLA_HEADS * GLA_DK
GLA_VW = GLA_HEADS * GLA_DV
GLA_GATE_RANK = 16
GLA_GATE_TAU = 16.0
GLA_CHUNK = 64
MLA_HEADS = 16
MLA_Q_RANK = 384
MLA_KV_RANK = 256
MLA_NOPE = 64
MLA_ROPE = 32
MLA_DV = 64
MLA_QDIM = MLA_NOPE + MLA_ROPE
ROPE_BASE = 10000.0
N_GROUPS = 4
EXPERTS_PER_GROUP = 8
N_EXPERTS = N_GROUPS * EXPERTS_PER_GROUP
EXPERT_FF = 256
DEPTH = 1
ALPHA = (2.0 * DEPTH) ** 0.25
LN_EPS = 1e-5
RMS_EPS = 1e-6

LANES = 128
SUBLANES = 8
MXU_DIM = 256
TILE = 256
PADL = TILE - N_META
HEAD_PAD = LANES
BF16_ROWS = 16
KV_TILE = 512
Q_TILE = 2 * KV_TILE
V_AUG = MLA_DV + BF16_ROWS
MERGE_TILE = 1024
ROUTE_BLOCK = 256
EXPERT_TILE = 256
DISPATCH_TILE = 2048
COMBINE_TILE = 256
COMBINE_SLOTS = 3
NEG = -1e30
LOG2E = 1.4426950408889634
BOUND_SLACK = 1.02
MIN_SOFTMAX_SUM = 2.0 ** -100
VMEM_LIMIT = 56 * 1024 * 1024

_C_Q, _C_K, _C_V, _C_R = 0, 512, 1024, 2048
_C_A = 3072
_C_CQ = _C_A + LANES
_C_CKV = _C_CQ + MLA_Q_RANK
_C_KR = _C_CKV + MLA_KV_RANK
_C_GA = _C_KR + LANES
_C_GB = _C_GA + D_MODEL
_W_COLS = _C_GB + D_MODEL

_f32 = jnp.float32
_bf16 = jnp.bfloat16


def _dot(a, b):
    return jnp.dot(a, b, preferred_element_type=_f32)


def _dot_nt(a, b):
    return lax.dot_general(a, b, (((1,), (1,)), ((), ())), preferred_element_type=_f32)


def _dot_tn(a, b):
    return lax.dot_general(a, b, (((0,), (0,)), ((), ())), preferred_element_type=_f32)


def _layer_norm(x, g, b):
    mu = jnp.mean(x, axis=-1, keepdims=True)
    xc = x - mu
    var = jnp.mean(xc * xc, axis=-1, keepdims=True)
    return xc * lax.rsqrt(var + LN_EPS) * g + b


def _rms_norm(x, g):
    ms = jnp.mean(x * x, axis=-1, keepdims=True)
    return x * lax.rsqrt(ms + RMS_EPS) * g


def _sigmoid(x):
    return 1.0 / (1.0 + jnp.exp(-x))


def _tt_load(ref, n, lead=()):
    return jnp.concatenate(
        [ref[lead + (pl.ds(a, n, stride=SUBLANES), slice(None))] for a in range(SUBLANES)], axis=1)


def _tt_store(ref, x):
    n = x.shape[0]
    for a in range(SUBLANES):
        ref[pl.ds(a, n, stride=SUBLANES), :] = x[:, a * LANES:(a + 1) * LANES]


def _tt_rows(tok):
    return pl.ds(pl.multiple_of(tok * SUBLANES, SUBLANES), SUBLANES)


def _rope(x, cos, sin_lo, sin_hi):
    half = MLA_ROPE // 2
    from_hi = pltpu.roll(x, LANES - half, 1)
    from_lo = pltpu.roll(x, half, 1)
    return x * cos + from_hi * sin_lo + from_lo * sin_hi


def _inproj_kernel(x_ref, head_ref, lng_ref, lnb_ref, w_ref, w2_ref, gb_ref, qg_ref, wuqt_ref, kvg_ref,
                   wuk_ref, wuvt_ref, cos_ref, sl_ref, sh_ref, cost_ref, sint_ref, tril_ref,
                   s_ref, qt_ref, kt_ref, ke_ref, gv_ref, dec_ref, sr_ref, qm_ref, km_ref,
                   vm_ref, ga_ref, gbt_ref):
    t = pl.program_id(1)
    x_in = jnp.where(t == 0, head_ref[...], x_ref[0])
    sn = _layer_norm(x_in, lng_ref[...], lnb_ref[...])
    s_ref[0] = sn
    snb = sn.astype(_bf16)
    row = t * TILE + lax.broadcasted_iota(jnp.int32, (TILE, 1), 0)
    valid = row >= PADL

    proj = lambda c0, width: _dot(snb, w_ref[:, c0:c0 + width])
    a_lr = proj(_C_A, LANES)
    cq = proj(_C_CQ, MLA_Q_RANK)
    ckv = proj(_C_CKV, MLA_KV_RANK)
    kr_raw = proj(_C_KR, LANES)
    gq = proj(_C_Q, GLA_QK)
    gk = proj(_C_K, GLA_QK)
    gv_ref[0] = jnp.where(valid, proj(_C_V, GLA_VW), 0.0).astype(_bf16)
    r = proj(_C_R, GLA_VW)
    sr_ref[0] = (r * _sigmoid(r)).astype(_bf16)
    ga_ref[0] = _sigmoid(proj(_C_GA, D_MODEL)).astype(_bf16)
    gbt_ref[0] = _sigmoid(proj(_C_GB, D_MODEL)).astype(_bf16)

    z = _dot(a_lr.astype(_bf16), w2_ref[...]) + gb_ref[...]
    cqn = _rms_norm(cq, qg_ref[...]).astype(_bf16)
    ckvn = _rms_norm(ckv, kvg_ref[...]).astype(_bf16)
    qft = _dot_nt(wuqt_ref[...], cqn)
    kf = _dot(ckvn, wuk_ref[...])
    vt = _dot_nt(wuvt_ref[...], ckvn)

    la = (jnp.minimum(z, 0.0) - jnp.log1p(jnp.exp(-jnp.abs(z)))) * (1.0 / GLA_GATE_TAU)
    la = jnp.where(valid, la, 0.0)
    hi = la.astype(_bf16)
    r1 = la - hi.astype(_f32)
    mid = r1.astype(_bf16)
    lo = (r1 - mid.astype(_f32)).astype(_bf16)
    tril = tril_ref[...]
    bc = _dot(tril, hi) + _dot(tril, mid) + _dot(tril, lo)
    n_chunks = TILE // GLA_CHUNK
    lasts = [bc[c * GLA_CHUNK + GLA_CHUNK - 1:(c + 1) * GLA_CHUNK, :] for c in range(n_chunks)]
    for c in range(n_chunks):
        dec_ref[0, c:c + 1, :] = jnp.exp(lasts[c])
    b_last = jnp.concatenate(
        [jnp.broadcast_to(l, (GLA_CHUNK, GLA_QK)) for l in lasts], axis=0)
    gk = jnp.where(valid, gk, 0.0)
    qt_ref[0] = (gq * (GLA_DK ** -0.5) * jnp.exp(bc)).astype(_bf16)
    kt_ref[0] = (gk * jnp.exp(-bc)).astype(_bf16)
    ke_ref[0] = (gk * jnp.exp(b_last - bc)).astype(_bf16)

    cos = cos_ref[...]
    sl = sl_ref[...]
    sh = sh_ref[...]
    scale = (MLA_QDIM ** -0.5) * LOG2E
    cost = cost_ref[...]
    sint = sint_ref[...]
    half = MLA_ROPE // 2
    for h in range(MLA_HEADS):
        base = h * MLA_QDIM
        x1 = qft[base + MLA_NOPE:base + MLA_NOPE + half]
        x2 = qft[base + MLA_NOPE + half:base + MLA_QDIM]
        qm_ref[0, base:base + MLA_NOPE] = (qft[base:base + MLA_NOPE] * scale).astype(_bf16)
        qm_ref[0, base + MLA_NOPE:base + MLA_NOPE + half] = ((x1 * cost - x2 * sint) * scale).astype(_bf16)
        qm_ref[0, base + MLA_NOPE + half:base + MLA_QDIM] = ((x1 * sint + x2 * cost) * scale).astype(_bf16)
    kr = _rope(kr_raw, cos, sl, sh)
    for h in range(MLA_HEADS):
        km_ref[0, :, h * HEAD_PAD:(h + 1) * HEAD_PAD] = (
            kf[:, h * HEAD_PAD:(h + 1) * HEAD_PAD] + kr).astype(_bf16)
    for h in range(MLA_HEADS):
        vm_ref[0, h * V_AUG:h * V_AUG + MLA_DV] = vt[h * MLA_DV:(h + 1) * MLA_DV].astype(_bf16)
        vm_ref[0, h * V_AUG + MLA_DV:(h + 1) * V_AUG] = jnp.ones((V_AUG - MLA_DV, TILE), _bf16)


def _gla_kernel(qt_ref, kt_ref, ke_ref, gv_ref, dec_ref, sr_ref, ng_ref, o_ref, st_ref):
    t = pl.program_id(0)
    n_batch = qt_ref.shape[0]

    @pl.when(t == 0)
    def _():
        st_ref[...] = jnp.zeros_like(st_ref)

    ri = lax.broadcasted_iota(jnp.int32, (TILE, TILE), 0)
    ci = lax.broadcasted_iota(jnp.int32, (TILE, TILE), 1)
    visible = (ri >= ci) & (ri // GLA_CHUNK == ci // GLA_CHUNK)
    ng = ng_ref[...]
    n_chunks = TILE // GLA_CHUNK
    chunk_rows = [slice(c * GLA_CHUNK, (c + 1) * GLA_CHUNK) for c in range(n_chunks)]
    k_cols = [slice(h * GLA_DK, (h + 1) * GLA_DK) for h in range(GLA_HEADS)]
    v_cols = [slice(h * GLA_DV, (h + 1) * GLA_DV) for h in range(GLA_HEADS)]
    seqs = [(b, h) for b in range(n_batch) for h in range(GLA_HEADS)]
    att = {(b, h): jnp.where(visible, _dot_nt(qt_ref[b, :, k_cols[h]], kt_ref[b, :, k_cols[h]]), 0.0)
           for b, h in seqs}
    upd = {(b, h): [_dot_tn(gv_ref[b, rows, v_cols[h]], ke_ref[b, rows, k_cols[h]])
                    for rows in chunk_rows] for b, h in seqs}
    o_intra = {(b, h): _dot(att[b, h].astype(_bf16), gv_ref[b, :, v_cols[h]]) for b, h in seqs}
    for b, h in seqs:
        st = st_ref[b, h]
        for c, rows in enumerate(chunk_rows):
            o = o_intra[b, h][rows] + _dot_nt(qt_ref[b, rows, k_cols[h]], st.astype(_bf16))
            st = st * dec_ref[b, 0, c:c + 1, k_cols[h]] + upd[b, h][c]
            o = _rms_norm(o, ng) * sr_ref[b, rows, v_cols[h]].astype(_f32)
            o_ref[b, rows, v_cols[h]] = o.astype(_bf16)
        st_ref[b, h] = st


MLA_GROUP = 4


def _mla_tile_start(j):
    return pl.multiple_of(TILE + j * KV_TILE, TILE)


def _mla_queries(q_ref, h, cols):
    q = q_ref[0, h * MLA_QDIM:(h + 1) * MLA_QDIM, cols]
    return jnp.concatenate([q, jnp.zeros((HEAD_PAD - MLA_QDIM, q.shape[1]), q.dtype)], axis=0)


def _mla_finish(o_ref, acc_ref):
    outs = []
    for h in range(MLA_GROUP):
        a = acc_ref[h]
        outs.append(a[0:MLA_DV] / a[MLA_DV:MLA_DV + 1])
    o_ref[0] = jnp.concatenate(outs, axis=0).T.astype(_bf16)


def _mla_kernel(q_ref, k_ref, v_ref, o_ref, acc_ref, knorm_ref):
    i = pl.program_id(2)
    heads = MLA_GROUP
    ones = jnp.ones((HEAD_PAD, LANES), _bf16)

    @pl.when(i == 0)
    def _():
        for h in range(heads):
            kk = k_ref[0, :, h * HEAD_PAD:(h + 1) * HEAD_PAD].astype(_f32)
            hi = (kk * kk).astype(_bf16)
            knorm_ref[h] = jnp.max(_dot(hi, ones), axis=0, keepdims=True)

    half = slice(KV_TILE, Q_TILE)
    q_t = [_mla_queries(q_ref, h, slice(None)) for h in range(heads)]
    q_hi = [_mla_queries(q_ref, h, half) for h in range(heads)]

    def score_bound(h, q):
        qq = q.astype(_f32)
        qn2 = jnp.sum(qq * qq, axis=0, keepdims=True)
        return jnp.sqrt(qn2 * knorm_ref[h][:, 0:1]) * BOUND_SLACK

    bound = [score_bound(h, q_t[h]) for h in range(heads)]
    bound_hi = [score_bound(h, q_hi[h]) for h in range(heads)]

    def keys(j, h):
        return k_ref[0, pl.ds(_mla_tile_start(j), KV_TILE), h * HEAD_PAD:(h + 1) * HEAD_PAD]

    def weighted(j, h, p):
        return _dot(v_ref[0, h * V_AUG:(h + 1) * V_AUG, pl.ds(_mla_tile_start(j), KV_TILE)], p)

    k_row = lax.broadcasted_iota(jnp.int32, (KV_TILE, Q_TILE), 0)
    q_col = lax.broadcasted_iota(jnp.int32, (KV_TILE, Q_TILE), 1)
    k_row_sq = lax.broadcasted_iota(jnp.int32, (KV_TILE, KV_TILE), 0)
    q_col_sq = lax.broadcasted_iota(jnp.int32, (KV_TILE, KV_TILE), 1)
    s_meta = [_dot(k_ref[0, PADL:TILE, h * HEAD_PAD:(h + 1) * HEAD_PAD], q_t[h]) for h in range(heads)]
    s_lo = [jnp.where(k_row <= q_col, _dot(keys(2 * i, h), q_t[h]), NEG) for h in range(heads)]
    s_hi = [jnp.where(k_row_sq <= q_col_sq, _dot(keys(2 * i + 1, h), q_hi[h]), NEG)
            for h in range(heads)]
    for h in range(heads):
        p_meta = jnp.concatenate([jnp.zeros((PADL, Q_TILE), _bf16),
                                  jnp.exp2(s_meta[h] - bound[h]).astype(_bf16)], axis=0)
        acc_ref[h] = (_dot(v_ref[0, h * V_AUG:(h + 1) * V_AUG, 0:TILE], p_meta)
                      + weighted(2 * i, h, jnp.exp2(s_lo[h] - bound[h]).astype(_bf16)))
        acc_ref[h, :, half] += weighted(2 * i + 1, h, jnp.exp2(s_hi[h] - bound_hi[h]).astype(_bf16))

    def body(jj, c):
        tiles = (2 * jj, 2 * jj + 1)
        s = [[_dot(keys(j, h), q_t[h]) for h in range(heads)] for j in tiles]
        for h in range(heads):
            acc_ref[h] += sum(weighted(j, h, jnp.exp2(s[n][h] - bound[h]).astype(_bf16))
                              for n, j in enumerate(tiles))
        return c

    lax.fori_loop(0, i, body, 0)
    _mla_finish(o_ref, acc_ref)

    l_min = functools.reduce(jnp.minimum,
                             [jnp.min(acc_ref[h, MLA_DV:MLA_DV + 1, :]) for h in range(heads)])

    @pl.when(jnp.logical_not(l_min >= MIN_SOFTMAX_SUM))
    def _():
        _mla_exact(q_ref, k_ref, v_ref, o_ref, acc_ref)


def _mla_exact(q_ref, k_ref, v_ref, o_ref, acc_ref):
    i = pl.program_id(2)
    heads = MLA_GROUP
    q_t = [_mla_queries(q_ref, h, slice(None)) for h in range(heads)]

    ms = []
    for h in range(heads):
        kb = k_ref[0, PADL:TILE, h * HEAD_PAD:(h + 1) * HEAD_PAD]
        s = _dot(kb, q_t[h])
        m0 = jnp.max(s, axis=0, keepdims=True)
        p = jnp.concatenate([jnp.zeros((PADL, Q_TILE), _bf16), jnp.exp2(s - m0).astype(_bf16)], axis=0)
        acc_ref[h] = _dot(v_ref[0, h * V_AUG:(h + 1) * V_AUG, 0:TILE], p)
        ms.append(m0)

    k_row = lax.broadcasted_iota(jnp.int32, (KV_TILE, Q_TILE), 0)
    q_col = lax.broadcasted_iota(jnp.int32, (KV_TILE, Q_TILE), 1)

    def body(j, ms):
        visible = (j - 2 * i) * KV_TILE + k_row <= q_col
        out = []
        for h in range(heads):
            s = _dot(k_ref[0, pl.ds(_mla_tile_start(j), KV_TILE), h * HEAD_PAD:(h + 1) * HEAD_PAD],
                     q_t[h])
            s = jnp.where(visible, s, NEG)
            vb = v_ref[0, h * V_AUG:(h + 1) * V_AUG, pl.ds(_mla_tile_start(j), KV_TILE)]
            m_new = jnp.maximum(ms[h], jnp.max(s, axis=0, keepdims=True))
            alpha = jnp.exp2(ms[h] - m_new)
            acc_ref[h] = alpha * acc_ref[h] + _dot(vb, jnp.exp2(s - m_new).astype(_bf16))
            out.append(m_new)
        return tuple(out)

    lax.fori_loop(0, 2 * i + 2, body, tuple(ms))
    _mla_finish(o_ref, acc_ref)


def _merge_kernel(og_ref, om_ref, ga_ref, gbt_ref, s_ref, wbg_ref, wbm_ref, wo_ref, lng_ref,
                  lnb_ref, rwh_ref, rwl_ref, rb_ref, tril_ref,
                  s2_ref, info_ref, infot_ref, cnt_ref, carry_ref):
    step = pl.program_id(0)

    @pl.when(step == 0)
    def _():
        carry_ref[...] = jnp.zeros_like(carry_ref)

    n_blk = MERGE_TILE // ROUTE_BLOCK
    blocks = [slice(i * ROUTE_BLOCK, (i + 1) * ROUTE_BLOCK) for i in range(n_blk)]
    d = wo_ref.shape[0]
    col_blocks = [slice(c, c + MXU_DIM) for c in range(0, d, MXU_DIM)]
    lane = lax.broadcasted_iota(jnp.int32, (ROUTE_BLOCK, LANES), 1)
    is_g = lane < N_GROUPS
    merged, s2, logits, infos = {}, {}, {}, {}
    carry = [carry_ref[0:1, :]]

    def branches(i):
        rows = blocks[i]
        merged[i] = jnp.concatenate(
            [(ga_ref[rows, cols].astype(_f32) * _dot(og_ref[rows, :], wbg_ref[:, cols])
              + gbt_ref[rows, cols].astype(_f32) * _dot(om_ref[rows, :], wbm_ref[:, cols])
              ).astype(_bf16) for cols in col_blocks], axis=1)

    def residual_norm(i):
        y = ALPHA * s_ref[blocks[i], :] + _dot(merged[i], wo_ref[...])
        s2[i] = _layer_norm(y, lng_ref[...], lnb_ref[...])

    def router_logits(i):
        xh = s2[i].astype(_bf16)
        xl = (s2[i] - xh.astype(_f32)).astype(_bf16)
        logits[i] = (_dot(xh, rwh_ref[...]) + _dot(xl, rwh_ref[...]) + _dot(xh, rwl_ref[...])
                     + rb_ref[...])

    def route(i):
        gl = jnp.where(is_g, logits[i], NEG)
        gmax = jnp.max(gl, axis=-1, keepdims=True)
        gidx = jnp.min(jnp.where(gl == gmax, lane, LANES), axis=-1, keepdims=True)
        p_g = 1.0 / jnp.sum(jnp.where(is_g, jnp.exp(gl - gmax), 0.0), axis=-1, keepdims=True)
        lo = N_GROUPS + EXPERTS_PER_GROUP * gidx
        el = jnp.where((lane >= lo) & (lane < lo + EXPERTS_PER_GROUP), logits[i], NEG)
        v1 = jnp.max(el, axis=-1, keepdims=True)
        i1 = jnp.min(jnp.where(el == v1, lane, LANES), axis=-1, keepdims=True)
        el2 = jnp.where(lane == i1, NEG, el)
        v2 = jnp.max(el2, axis=-1, keepdims=True)
        i2 = jnp.min(jnp.where(el2 == v2, lane, LANES), axis=-1, keepdims=True)
        tt = jnp.exp(v2 - v1)
        p1 = 1.0 / (1.0 + tt)
        p2 = tt / (1.0 + tt)
        e1 = i1 - N_GROUPS
        e2 = i2 - N_GROUPS
        hit1 = lane == e1
        hit2 = lane == e2
        onehot = jnp.where(hit1 | hit2, 1.0, 0.0)
        before = _dot(tril_ref[...], onehot.astype(_bf16)) + carry[0]
        r1 = jnp.sum(jnp.where(hit1, before, 0.0), axis=-1, keepdims=True)
        r2 = jnp.sum(jnp.where(hit2, before, 0.0), axis=-1, keepdims=True)
        carry[0] = carry[0] + jnp.sum(onehot, axis=0, keepdims=True)
        infos[i] = jnp.where(lane == 0, e1.astype(_f32),
                   jnp.where(lane == 1, e2.astype(_f32),
                   jnp.where(lane == 2, p_g * p1,
                   jnp.where(lane == 3, p_g * p2,
                   jnp.where(lane == 4, r1,
                   jnp.where(lane == 5, r2, 0.0))))))

    for stage in (branches, residual_norm, router_logits, route):
        for i in range(n_blk):
            stage(i)
    _tt_store(s2_ref, jnp.concatenate([s2[i] for i in range(n_blk)], axis=0))
    carry_ref[...] = jnp.broadcast_to(carry[0], carry_ref.shape)
    cnt_ref[...] = jnp.broadcast_to(carry[0], cnt_ref.shape)
    info = jnp.concatenate([infos[i] for i in range(n_blk)], axis=0)
    info_ref[...] = info
    infot_ref[...] = info.T[0:SUBLANES]


def _dispatch_kernel(pos_ref, zrow_ref, nused_ref, s2_ref, xs_hbm, zero_ref, sems):
    g = pl.program_id(0)
    zero_sem = sems.at[1]
    row_sem = sems.at[0]

    tile_rows = EXPERT_TILE * SUBLANES

    def zero_copy(row):
        start = pl.multiple_of(row * SUBLANES, tile_rows)
        return pltpu.make_async_copy(zero_ref, xs_hbm.at[pl.ds(start, tile_rows)], zero_sem)

    @pl.when(g == 0)
    def _():
        zero_ref[...] = jnp.zeros_like(zero_ref)

        def start(e, c):
            @pl.when(zrow_ref[e] >= 0)
            def _():
                zero_copy(zrow_ref[e]).start()
            return c

        def wait(e, c):
            @pl.when(zrow_ref[e] >= 0)
            def _():
                zero_copy(0).wait()
            return c

        def start_tail(u, c):
            zero_copy(u * EXPERT_TILE).start()
            return c

        def wait_tail(u, c):
            zero_copy(0).wait()
            return c

        n_tiles = xs_hbm.shape[0] // tile_rows
        lax.fori_loop(0, N_EXPERTS, start, 0)
        lax.fori_loop(nused_ref[0], n_tiles, start_tail, 0)
        lax.fori_loop(0, N_EXPERTS, wait, 0)
        lax.fori_loop(nused_ref[0], n_tiles, wait_tail, 0)

    n_tok = pl.num_programs(0) * DISPATCH_TILE

    def issue(r, c):
        tok = g * DISPATCH_TILE + r
        for k in range(2):
            pltpu.make_async_copy(s2_ref.at[_tt_rows(r)],
                                  xs_hbm.at[_tt_rows(pos_ref[k * n_tok + tok])], row_sem).start(priority=k)
        return c

    lax.fori_loop(0, DISPATCH_TILE, issue, 0, unroll=8)
    for k in range(2):
        pltpu.make_async_copy(s2_ref, xs_hbm.at[pl.ds(0, DISPATCH_TILE * SUBLANES)], row_sem).wait()


def _expert_kernel(te_ref, nused_ref, x_ref, wg_ref, wu_ref, wd_ref, o_ref, wgb_ref, wub_ref, wdb_ref):
    u = pl.program_id(0)
    used = u < nused_ref[0]
    first_of_expert = jnp.logical_or(u == 0, te_ref[u] != te_ref[jnp.maximum(u - 1, 0)])

    @pl.when(jnp.logical_and(used, first_of_expert))
    def _():
        wgb_ref[...] = wg_ref[0].astype(_bf16)
        wub_ref[...] = wu_ref[0].astype(_bf16)
        wdb_ref[...] = wd_ref[0].astype(_bf16)

    @pl.when(used)
    def _():
        x = _tt_load(x_ref, EXPERT_TILE).astype(_bf16)
        a = _dot(x, wgb_ref[...])
        up = _dot(x, wub_ref[...])
        hid = a * _sigmoid(a) * up
        _tt_store(o_ref, _dot(hid.astype(_bf16), wdb_ref[...]))

    @pl.when(u >= nused_ref[0])
    def _():
        o_ref[...] = jnp.zeros_like(o_ref)


def _combine_kernel(pos_ref, s2_ref, info_ref, ys_hbm, lng_ref, lnb_ref, o_ref, buf_ref, sems):
    g = pl.program_id(0)
    last = pl.num_programs(0) - 1
    n_tok = pl.num_programs(0) * COMBINE_TILE

    def start_copies(tile, slot, r):
        tok = tile * COMBINE_TILE + r
        for k in range(2):
            pltpu.make_async_copy(ys_hbm.at[_tt_rows(pos_ref[k * n_tok + tok])],
                                  buf_ref.at[slot, k, _tt_rows(r)], sems.at[slot]).start(priority=k)

    def wait_slot(slot):
        for k in range(2):
            pltpu.make_async_copy(ys_hbm.at[pl.ds(0, COMBINE_TILE * SUBLANES)], buf_ref.at[slot, k],
                                  sems.at[slot]).wait()

    @pl.when(g == 0)
    def _():
        def body(r, c):
            for tile in range(COMBINE_SLOTS - 1):
                start_copies(tile, tile, r)
            return c

        lax.fori_loop(0, COMBINE_TILE, body, 0, unroll=8)

    slot = g % COMBINE_SLOTS
    wait_slot(slot)
    info = info_ref[...]
    y = (ALPHA * _tt_load(s2_ref, COMBINE_TILE)
         + info[:, 2:3] * _tt_load(buf_ref, COMBINE_TILE, (slot, 0))
         + info[:, 3:4] * _tt_load(buf_ref, COMBINE_TILE, (slot, 1)))
    o_ref[...] = _layer_norm(y, lng_ref[...], lnb_ref[...])

    ahead = COMBINE_SLOTS - 1
    ahead_slot = (g + ahead) % COMBINE_SLOTS
    for r in range(COMBINE_TILE):
        start_copies(jnp.minimum(g + ahead, last), ahead_slot, r)

    @pl.when(g == last)
    def _():
        for n in range(1, COMBINE_SLOTS):
            wait_slot((g + n) % COMBINE_SLOTS)


def _const_spec(shape):
    nd = len(shape)
    return pl.BlockSpec(shape, lambda *_: (0,) * nd)


def _rope_tables(lp):
    pos = jnp.maximum(jnp.arange(lp, dtype=_f32) - PADL, 0.0)
    inv_freq = ROPE_BASE ** (-jnp.arange(0, MLA_ROPE, 2, dtype=_f32) / MLA_ROPE)
    ang = pos[:, None] * inv_freq[None, :]
    cos, sin = jnp.cos(ang), jnp.sin(ang)
    half = MLA_ROPE // 2
    ones = jnp.ones((lp, MLA_NOPE), _f32)
    zeros_n = jnp.zeros((lp, MLA_NOPE), _f32)
    zeros_h = jnp.zeros((lp, half), _f32)
    tail1 = jnp.ones((lp, LANES - MLA_QDIM), _f32)
    tail0 = jnp.zeros((lp, LANES - MLA_QDIM), _f32)
    cos_t = jnp.concatenate([ones, cos, cos, tail1], axis=1)
    sin_lo = jnp.concatenate([zeros_n, -sin, zeros_h, tail0], axis=1)
    sin_hi = jnp.concatenate([zeros_n, zeros_h, sin, tail0], axis=1)
    return cos_t, sin_lo, sin_hi, cos.T, sin.T


def _pad_heads(w, width):
    k = w.shape[0]
    w = w.reshape(k, MLA_HEADS, width)
    w = jnp.pad(w, ((0, 0), (0, 0), (0, HEAD_PAD - width)))
    return w.reshape(k, MLA_HEADS * HEAD_PAD)


def kernel(x, meta_tokens, ln_emb_g, ln_emb_b, w_in, gla_gate_w2, gla_gate_b, gla_norm_g, mla_q_norm_g, mla_w_uq, mla_kv_norm_g, mla_w_uk, mla_w_uv, w_branch_gla, w_branch_mla, w_out, ln_mix_g, ln_mix_b, router_group_w, router_group_b, router_expert_w, router_expert_b, expert_w_gate, expert_w_up, expert_w_down, ln_ffn_g, ln_ffn_b):
    bsz, seq, d = x.shape
    assert d == D_MODEL and seq % Q_TILE == 0 and w_in.shape[0] == DEPTH == 1
    lp = PADL + N_META + seq
    nt = lp // TILE
    ntok = bsz * seq
    row2 = lambda v: v.reshape(1, -1).astype(_f32)

    head_tile = jnp.concatenate([jnp.zeros((PADL, d), _f32), meta_tokens.astype(_f32)], axis=0)
    wi = w_in[0]
    o_a = 2 * GLA_QK + 2 * GLA_VW
    o_cq = o_a + GLA_GATE_RANK
    o_ckv = o_cq + MLA_Q_RANK
    o_kr = o_ckv + MLA_KV_RANK
    o_ga = o_kr + MLA_ROPE
    w_a = jnp.pad(wi[:, o_a:o_cq], ((0, 0), (0, LANES - GLA_GATE_RANK)))
    w_kr = jnp.pad(wi[:, o_kr:o_ga], ((0, 0), (MLA_NOPE, LANES - MLA_QDIM)))
    w_all = jnp.concatenate([wi[:, :o_a], w_a, wi[:, o_cq:o_kr], w_kr, wi[:, o_ga:]], axis=1).astype(_bf16)
    assert w_all.shape == (d, _W_COLS)
    w2p = jnp.pad(gla_gate_w2[0], ((0, LANES - GLA_GATE_RANK), (0, 0))).astype(_bf16)
    wuqt = mla_w_uq[0].T.astype(_bf16)
    wuk = _pad_heads(mla_w_uk[0], MLA_NOPE).astype(_bf16)
    wuvt = mla_w_uv[0].T.astype(_bf16)
    cos_t, sin_lo, sin_hi, cos_tr, sin_tr = _rope_tables(lp)
    blk = np.arange(TILE)
    tril_chunks = jnp.asarray(
        ((blk[:, None] >= blk[None, :]) & (blk[:, None] // GLA_CHUNK == blk[None, :] // GLA_CHUNK)),
        dtype=_bf16)

    pad_map = lambda b, t: (b, t, 0)
    real_map = lambda b, t: (b, jnp.maximum(t - 1, 0), 0)
    real_map_t = lambda b, t: (b, 0, jnp.maximum(t - 1, 0))
    out_shapes = (
        jax.ShapeDtypeStruct((bsz, seq, d), _f32),
        jax.ShapeDtypeStruct((bsz, lp, GLA_QK), _bf16),
        jax.ShapeDtypeStruct((bsz, lp, GLA_QK), _bf16),
        jax.ShapeDtypeStruct((bsz, lp, GLA_QK), _bf16),
        jax.ShapeDtypeStruct((bsz, lp, GLA_VW), _bf16),
        jax.ShapeDtypeStruct((bsz * nt, TILE // GLA_CHUNK, GLA_QK), _f32),
        jax.ShapeDtypeStruct((bsz, seq, GLA_VW), _bf16),
        jax.ShapeDtypeStruct((bsz, MLA_HEADS * MLA_QDIM, seq), _bf16),
        jax.ShapeDtypeStruct((bsz, lp, MLA_HEADS * HEAD_PAD), _bf16),
        jax.ShapeDtypeStruct((bsz, MLA_HEADS * V_AUG, lp), _bf16),
        jax.ShapeDtypeStruct((bsz, seq, d), _bf16),
        jax.ShapeDtypeStruct((bsz, seq, d), _bf16),
    )
    out_specs = (
        pl.BlockSpec((1, TILE, d), real_map),
        pl.BlockSpec((1, TILE, GLA_QK), pad_map),
        pl.BlockSpec((1, TILE, GLA_QK), pad_map),
        pl.BlockSpec((1, TILE, GLA_QK), pad_map),
        pl.BlockSpec((1, TILE, GLA_VW), pad_map),
        pl.BlockSpec((1, TILE // GLA_CHUNK, GLA_QK), lambda b, t: (b * nt + t, 0, 0)),
        pl.BlockSpec((1, TILE, GLA_VW), real_map),
        pl.BlockSpec((1, MLA_HEADS * MLA_QDIM, TILE), real_map_t),
        pl.BlockSpec((1, TILE, MLA_HEADS * HEAD_PAD), pad_map),
        pl.BlockSpec((1, MLA_HEADS * V_AUG, TILE), lambda b, t: (b, 0, t)),
        pl.BlockSpec((1, TILE, d), real_map),
        pl.BlockSpec((1, TILE, d), real_map),
    )
    tab_spec = pl.BlockSpec((TILE, LANES), lambda b, t: (t, 0))
    tabt_spec = pl.BlockSpec((MLA_ROPE // 2, TILE), lambda b, t: (0, t))
    (s_emb, qt, kt, ke, gv, dec, sr, qm, km, vm, gate_a, gate_b) = pl.pallas_call(
        _inproj_kernel,
        grid=(bsz, nt),
        in_specs=[
            pl.BlockSpec((1, TILE, d), real_map),
            _const_spec((TILE, d)),
            _const_spec((1, d)), _const_spec((1, d)),
            _const_spec((d, _W_COLS)),
            _const_spec((LANES, GLA_QK)), _const_spec((1, GLA_QK)),
            _const_spec((1, MLA_Q_RANK)), _const_spec((MLA_HEADS * MLA_QDIM, MLA_Q_RANK)),
            _const_spec((1, MLA_KV_RANK)), _const_spec((MLA_KV_RANK, MLA_HEADS * HEAD_PAD)),
            _const_spec((MLA_HEADS * MLA_DV, MLA_KV_RANK)),
            tab_spec, tab_spec, tab_spec, tabt_spec, tabt_spec,
            _const_spec((TILE, TILE)),
        ],
        out_specs=out_specs,
        out_shape=out_shapes,
        compiler_params=pltpu.CompilerParams(
            dimension_semantics=("arbitrary", "arbitrary"), vmem_limit_bytes=VMEM_LIMIT),
        name="inproj",
    )(x, head_tile, row2(ln_emb_g), row2(ln_emb_b), w_all, w2p, row2(gla_gate_b[0]),
      row2(mla_q_norm_g[0]), wuqt, row2(mla_kv_norm_g[0]), wuk, wuvt, cos_t, sin_lo, sin_hi,
      cos_tr, sin_tr, tril_chunks)

    o_gla = pl.pallas_call(
        _gla_kernel,
        grid=(nt,),
        in_specs=[
            pl.BlockSpec((bsz, TILE, GLA_QK), lambda t: (0, t, 0)),
            pl.BlockSpec((bsz, TILE, GLA_QK), lambda t: (0, t, 0)),
            pl.BlockSpec((bsz, TILE, GLA_QK), lambda t: (0, t, 0)),
            pl.BlockSpec((bsz, TILE, GLA_VW), lambda t: (0, t, 0)),
            pl.BlockSpec((bsz, 1, TILE // GLA_CHUNK, GLA_QK), lambda t: (0, t, 0, 0)),
            pl.BlockSpec((bsz, TILE, GLA_VW), lambda t: (0, jnp.maximum(t - 1, 0), 0)),
            _const_spec((1, GLA_DV)),
        ],
        out_specs=pl.BlockSpec((bsz, TILE, GLA_VW), lambda t: (0, jnp.maximum(t - 1, 0), 0)),
        out_shape=jax.ShapeDtypeStruct((bsz, seq, GLA_VW), _bf16),
        scratch_shapes=[pltpu.VMEM((bsz, GLA_HEADS, GLA_DV, GLA_DK), _f32)],
        compiler_params=pltpu.CompilerParams(
            dimension_semantics=("arbitrary",), vmem_limit_bytes=VMEM_LIMIT),
        name="gla",
    )(qt, kt, ke, gv, dec.reshape(bsz, nt, TILE // GLA_CHUNK, GLA_QK), sr, row2(gla_norm_g[0]))

    group_width = MLA_GROUP * HEAD_PAD
    o_mla = pl.pallas_call(
        _mla_kernel,
        grid=(bsz, MLA_HEADS // MLA_GROUP, seq // Q_TILE),
        in_specs=[
            pl.BlockSpec((1, MLA_GROUP * MLA_QDIM, Q_TILE), lambda b, hp, i: (b, hp, i)),
            pl.BlockSpec((1, lp, group_width), lambda b, hp, i: (b, 0, hp)),
            pl.BlockSpec((1, MLA_GROUP * V_AUG, lp), lambda b, hp, i: (b, hp, 0)),
        ],
        out_specs=pl.BlockSpec((1, Q_TILE, MLA_GROUP * MLA_DV), lambda b, hp, i: (b, i, hp)),
        out_shape=jax.ShapeDtypeStruct((bsz, seq, MLA_HEADS * MLA_DV), _bf16),
        scratch_shapes=[pltpu.VMEM((MLA_GROUP, V_AUG, Q_TILE), _f32),
                        pltpu.VMEM((MLA_GROUP, 1, LANES), _f32)],
        compiler_params=pltpu.CompilerParams(
            dimension_semantics=("arbitrary", "arbitrary", "arbitrary"),
            vmem_limit_bytes=VMEM_LIMIT),
        name="mla",
    )(qm, km, vm)

    rw = jnp.concatenate([router_group_w[0], router_expert_w[0]], axis=1)
    rw = jnp.pad(rw, ((0, 0), (0, LANES - rw.shape[1])))
    rwh = rw.astype(_bf16)
    rwl = (rw - rwh.astype(_f32)).astype(_bf16)
    rb = jnp.concatenate([router_group_b[0], router_expert_b[0]])
    rb = jnp.pad(rb, (0, LANES - rb.shape[0])).reshape(1, LANES)
    mi = np.arange(ROUTE_BLOCK)
    tril_strict = jnp.asarray(mi[:, None] > mi[None, :], dtype=_bf16)
    flat = lambda a: a.reshape(ntok, a.shape[-1])
    tok_spec = lambda w: pl.BlockSpec((MERGE_TILE, w), lambda g: (g, 0))
    tt_spec = lambda n, index_map: pl.BlockSpec((n * SUBLANES, LANES), index_map)
    s2, info, info_t, cnt = pl.pallas_call(
        _merge_kernel,
        grid=(ntok // MERGE_TILE,),
        in_specs=[tok_spec(d), tok_spec(d), tok_spec(d), tok_spec(d), tok_spec(d),
                  _const_spec((d, d)), _const_spec((d, d)), _const_spec((d, d)),
                  _const_spec((1, d)), _const_spec((1, d)),
                  _const_spec((d, LANES)), _const_spec((d, LANES)), _const_spec((1, LANES)),
                  _const_spec((ROUTE_BLOCK, ROUTE_BLOCK))],
        out_specs=(tt_spec(MERGE_TILE, lambda g: (g, 0)), tok_spec(LANES),
                   pl.BlockSpec((SUBLANES, MERGE_TILE), lambda g: (0, g)), _const_spec((8, LANES))),
        out_shape=(jax.ShapeDtypeStruct((ntok * SUBLANES, LANES), _f32),
                   jax.ShapeDtypeStruct((ntok, LANES), _f32),
                   jax.ShapeDtypeStruct((SUBLANES, ntok), _f32),
                   jax.ShapeDtypeStruct((8, LANES), _f32)),
        scratch_shapes=[pltpu.VMEM((8, LANES), _f32)],
        compiler_params=pltpu.CompilerParams(
            dimension_semantics=("arbitrary",), vmem_limit_bytes=VMEM_LIMIT),
        name="merge_router",
    )(flat(o_gla), flat(o_mla), flat(gate_a), flat(gate_b), flat(s_emb),
      w_branch_gla[0].astype(_bf16), w_branch_mla[0].astype(_bf16), w_out[0].astype(_bf16),
      row2(ln_mix_g[0]), row2(ln_mix_b[0]), rwh, rwl, rb, tril_strict)

    n_tiles = (2 * ntok + N_EXPERTS * (EXPERT_TILE - 1)) // EXPERT_TILE
    n_rows = n_tiles * EXPERT_TILE
    e_idx = info_t[0:2].astype(jnp.int32)
    rank = info_t[4:6].astype(jnp.int32)
    counts = cnt[0, :N_EXPERTS].astype(jnp.int32)
    padded = ((counts + EXPERT_TILE - 1) // EXPERT_TILE) * EXPERT_TILE
    ends = jnp.cumsum(padded)
    starts = ends - padded
    expert_ids = jnp.arange(N_EXPERTS, dtype=jnp.int32)[:, None, None]
    start_of = jnp.sum(jnp.where(e_idx[None] == expert_ids, starts[:, None, None], 0), axis=0)
    pos = (start_of + rank).reshape(-1)
    tile_start = jnp.arange(n_tiles, dtype=jnp.int32) * EXPERT_TILE
    tile_expert = jnp.minimum(
        jnp.sum((ends[None, :] <= tile_start[:, None]).astype(jnp.int32), axis=1), N_EXPERTS - 1)
    n_used = (ends[-1:] // EXPERT_TILE).astype(jnp.int32)
    zero_row = jnp.where(padded > 0, ends - EXPERT_TILE, -1).astype(jnp.int32)

    any_spec = pl.BlockSpec(memory_space=pl.ANY)
    xs = pl.pallas_call(
        _dispatch_kernel,
        grid_spec=pltpu.PrefetchScalarGridSpec(
            num_scalar_prefetch=3,
            grid=(ntok // DISPATCH_TILE,),
            in_specs=[tt_spec(DISPATCH_TILE, lambda g, p, z, nu: (g, 0))],
            out_specs=any_spec,
            scratch_shapes=[pltpu.VMEM((EXPERT_TILE * SUBLANES, LANES), _f32),
                            pltpu.SemaphoreType.DMA((2,))],
        ),
        out_shape=jax.ShapeDtypeStruct((n_rows * SUBLANES, LANES), _f32),
        compiler_params=pltpu.CompilerParams(
            dimension_semantics=("arbitrary",), vmem_limit_bytes=VMEM_LIMIT, has_side_effects=True),
        name="dispatch",
    )(pos, zero_row, n_used, s2)

    ff = EXPERT_FF
    wg = expert_w_gate[0].reshape(N_EXPERTS, d, ff)
    wu = expert_w_up[0].reshape(N_EXPERTS, d, ff)
    wd = expert_w_down[0].reshape(N_EXPERTS, ff, d)
    ys = pl.pallas_call(
        _expert_kernel,
        grid_spec=pltpu.PrefetchScalarGridSpec(
            num_scalar_prefetch=2,
            grid=(n_tiles,),
            in_specs=[
                tt_spec(EXPERT_TILE, lambda u, te, nu: (jnp.minimum(u, nu[0] - 1), 0)),
                pl.BlockSpec((1, d, ff), lambda u, te, nu: (te[u], 0, 0)),
                pl.BlockSpec((1, d, ff), lambda u, te, nu: (te[u], 0, 0)),
                pl.BlockSpec((1, ff, d), lambda u, te, nu: (te[u], 0, 0)),
            ],
            out_specs=tt_spec(EXPERT_TILE, lambda u, te, nu: (u, 0)),
            scratch_shapes=[pltpu.VMEM((d, ff), _bf16), pltpu.VMEM((d, ff), _bf16),
                            pltpu.VMEM((ff, d), _bf16)],
        ),
        out_shape=jax.ShapeDtypeStruct((n_rows * SUBLANES, LANES), _f32),
        compiler_params=pltpu.CompilerParams(
            dimension_semantics=("arbitrary",), vmem_limit_bytes=VMEM_LIMIT),
        name="experts",
    )(tile_expert, n_used, xs, wg, wu, wd)

    out = pl.pallas_call(
        _combine_kernel,
        grid_spec=pltpu.PrefetchScalarGridSpec(
            num_scalar_prefetch=1,
            grid=(ntok // COMBINE_TILE,),
            in_specs=[
                tt_spec(COMBINE_TILE, lambda g, p: (g, 0)),
                pl.BlockSpec((COMBINE_TILE, LANES), lambda g, p: (g, 0)),
                any_spec,
                pl.BlockSpec((1, d), lambda g, p: (0, 0)),
                pl.BlockSpec((1, d), lambda g, p: (0, 0)),
            ],
            out_specs=pl.BlockSpec((COMBINE_TILE, d), lambda g, p: (g, 0)),
            scratch_shapes=[pltpu.VMEM((COMBINE_SLOTS, 2, COMBINE_TILE * SUBLANES, LANES), _f32),
                            pltpu.SemaphoreType.DMA((COMBINE_SLOTS,))],
        ),
        out_shape=jax.ShapeDtypeStruct((ntok, d), _f32),
        compiler_params=pltpu.CompilerParams(
            dimension_semantics=("arbitrary",), vmem_limit_bytes=VMEM_LIMIT),
        name="combine_ln",
    )(pos, s2, info, ys, row2(ln_ffn_g[0]), row2(ln_ffn_b[0]))
    return out.reshape(bsz, seq, d)
```

```python
import functools

import jax
import jax.numpy as jnp
import numpy as np
from jax import lax
from jax.experimental import pallas as pl
from jax.experimental.pallas import tpu as pltpu

D_MODEL = 1024
N_META = 16
GLA_HEADS = 4
GLA_DK = 128
GLA_DV = 256
GLA_QK = GLA_HEADS * GLA_DK
GLA_VW = GLA_HEADS * GLA_DV
GLA_GATE_RANK = 16
GLA_GATE_TAU = 16.0
GLA_CHUNK = 64
MLA_HEADS = 16
MLA_Q_RANK = 384
MLA_KV_RANK = 256
MLA_NOPE = 64
MLA_ROPE = 32
MLA_DV = 64
MLA_QDIM = MLA_NOPE + MLA_ROPE
ROPE_BASE = 10000.0
N_GROUPS = 4
EXPERTS_PER_GROUP = 8
N_EXPERTS = N_GROUPS * EXPERTS_PER_GROUP
EXPERT_FF = 256
DEPTH = 1
ALPHA = (2.0 * DEPTH) ** 0.25
LN_EPS = 1e-5
RMS_EPS = 1e-6

LANES = 128
SUBLANES = 8
MXU_DIM = 256
TILE = 256
PADL = TILE - N_META
HEAD_PAD = LANES
BF16_ROWS = 16
KV_TILE = 512
Q_TILE = 2 * KV_TILE
V_AUG = MLA_DV + BF16_ROWS
MERGE_TILE = 1024
ROUTE_BLOCK = 256
EXPERT_TILE = 256
DISPATCH_TILE = 2048
COMBINE_TILE = 256
COMBINE_SLOTS = 3
NEG = -1e30
LOG2E = 1.4426950408889634
BOUND_SLACK = 1.02
MIN_SOFTMAX_SUM = 2.0 ** -100
VMEM_LIMIT = 56 * 1024 * 1024

_C_Q, _C_K, _C_V, _C_R = 0, 512, 1024, 2048
_C_A = 3072
_C_CQ = _C_A + LANES
_C_CKV = _C_CQ + MLA_Q_RANK
_C_KR = _C_CKV + MLA_KV_RANK
_C_GA = _C_KR + LANES
_C_GB = _C_GA + D_MODEL
_W_COLS = _C_GB + D_MODEL

_f32 = jnp.float32
_bf16 = jnp.bfloat16


def _dot(a, b):
    return jnp.dot(a, b, preferred_element_type=_f32)


def _dot_nt(a, b):
    return lax.dot_general(a, b, (((1,), (1,)), ((), ())), preferred_element_type=_f32)


def _dot_tn(a, b):
    return lax.dot_general(a, b, (((0,), (0,)), ((), ())), preferred_element_type=_f32)


def _layer_norm(x, g, b):
    mu = jnp.mean(x, axis=-1, keepdims=True)
    xc = x - mu
    var = jnp.mean(xc * xc, axis=-1, keepdims=True)
    return xc * lax.rsqrt(var + LN_EPS) * g + b


def _rms_norm(x, g):
    ms = jnp.mean(x * x, axis=-1, keepdims=True)
    return x * lax.rsqrt(ms + RMS_EPS) * g


def _sigmoid(x):
    return 1.0 / (1.0 + jnp.exp(-x))


def _tt_load(ref, n, lead=()):
    return jnp.concatenate(
        [ref[lead + (pl.ds(a, n, stride=SUBLANES), slice(None))] for a in range(SUBLANES)], axis=1)


def _tt_store(ref, x):
    n = x.shape[0]
    for a in range(SUBLANES):
        ref[pl.ds(a, n, stride=SUBLANES), :] = x[:, a * LANES:(a + 1) * LANES]


def _tt_rows(tok):
    return pl.ds(pl.multiple_of(tok * SUBLANES, SUBLANES), SUBLANES)


def _rope(x, cos, sin_lo, sin_hi):
    half = MLA_ROPE // 2
    from_hi = pltpu.roll(x, LANES - half, 1)
    from_lo = pltpu.roll(x, half, 1)
    return x * cos + from_hi * sin_lo + from_lo * sin_hi


def _inproj_kernel(x_ref, head_ref, lng_ref, lnb_ref, w_ref, w2_ref, gb_ref, qg_ref, wuqt_ref, kvg_ref,
                   wuk_ref, wuvt_ref, cos_ref, sl_ref, sh_ref, cost_ref, sint_ref, tril_ref,
                   s_ref, qt_ref, kt_ref, ke_ref, gv_ref, dec_ref, sr_ref, qm_ref, km_ref,
                   vm_ref, ga_ref, gbt_ref):
    t = pl.program_id(1)
    x_in = jnp.where(t == 0, head_ref[...], x_ref[0])
    sn = _layer_norm(x_in, lng_ref[...], lnb_ref[...])
    s_ref[0] = sn
    snb = sn.astype(_bf16)
    row = t * TILE + lax.broadcasted_iota(jnp.int32, (TILE, 1), 0)
    valid = row >= PADL

    proj = lambda c0, width: _dot(snb, w_ref[:, c0:c0 + width])
    a_lr = proj(_C_A, LANES)
    cq = proj(_C_CQ, MLA_Q_RANK)
    ckv = proj(_C_CKV, MLA_KV_RANK)
    kr_raw = proj(_C_KR, LANES)
    gq = proj(_C_Q, GLA_QK)
    gk = proj(_C_K, GLA_QK)
    gv_ref[0] = jnp.where(valid, proj(_C_V, GLA_VW), 0.0).astype(_bf16)
    r = proj(_C_R, GLA_VW)
    sr_ref[0] = (r * _sigmoid(r)).astype(_bf16)
    ga_ref[0] = _sigmoid(proj(_C_GA, D_MODEL)).astype(_bf16)
    gbt_ref[0] = _sigmoid(proj(_C_GB, D_MODEL)).astype(_bf16)

    z = _dot(a_lr.astype(_bf16), w2_ref[...]) + gb_ref[...]
    cqn = _rms_norm(cq, qg_ref[...]).astype(_bf16)
    ckvn = _rms_norm(ckv, kvg_ref[...]).astype(_bf16)
    qft = _dot_nt(wuqt_ref[...], cqn)
    kf = _dot(ckvn, wuk_ref[...])
    vt = _dot_nt(wuvt_ref[...], ckvn)

    la = (jnp.minimum(z, 0.0) - jnp.log1p(jnp.exp(-jnp.abs(z)))) * (1.0 / GLA_GATE_TAU)
    la = jnp.where(valid, la, 0.0)
    hi = la.astype(_bf16)
    r1 = la - hi.astype(_f32)
    mid = r1.astype(_bf16)
    lo = (r1 - mid.astype(_f32)).astype(_bf16)
    tril = tril_ref[...]
    bc = _dot(tril, hi) + _dot(tril, mid) + _dot(tril, lo)
    n_chunks = TILE // GLA_CHUNK
    lasts = [bc[c * GLA_CHUNK + GLA_CHUNK - 1:(c + 1) * GLA_CHUNK, :] for c in range(n_chunks)]
    for c in range(n_chunks):
        dec_ref[0, c:c + 1, :] = jnp.exp(lasts[c])
    b_last = jnp.concatenate(
        [jnp.broadcast_to(l, (GLA_CHUNK, GLA_QK)) for l in lasts], axis=0)
    gk = jnp.where(valid, gk, 0.0)
    qt_ref[0] = (gq * (GLA_DK ** -0.5) * jnp.exp(bc)).astype(_bf16)
    kt_ref[0] = (gk * jnp.exp(-bc)).astype(_bf16)
    ke_ref[0] = (gk * jnp.exp(b_last - bc)).astype(_bf16)

    cos = cos_ref[...]
    sl = sl_ref[...]
    sh = sh_ref[...]
    scale = (MLA_QDIM ** -0.5) * LOG2E
    cost = cost_ref[...]
    sint = sint_ref[...]
    half = MLA_ROPE // 2
    for h in range(MLA_HEADS):
        base = h * MLA_QDIM
        x1 = qft[base + MLA_NOPE:base + MLA_NOPE + half]
        x2 = qft[base + MLA_NOPE + half:base + MLA_QDIM]
        qm_ref[0, base:base + MLA_NOPE] = (qft[base:base + MLA_NOPE] * scale).astype(_bf16)
        qm_ref[0, base + MLA_NOPE:base + MLA_NOPE + half] = ((x1 * cost - x2 * sint) * scale).astype(_bf16)
        qm_ref[0, base + MLA_NOPE + half:base + MLA_QDIM] = ((x1 * sint + x2 * cost) * scale).astype(_bf16)
    kr = _rope(kr_raw, cos, sl, sh)
    for h in range(MLA_HEADS):
        km_ref[0, :, h * HEAD_PAD:(h + 1) * HEAD_PAD] = (
            kf[:, h * HEAD_PAD:(h + 1) * HEAD_PAD] + kr).astype(_bf16)
    for h in range(MLA_HEADS):
        vm_ref[0, h * V_AUG:h * V_AUG + MLA_DV] = vt[h * MLA_DV:(h + 1) * MLA_DV].astype(_bf16)
        vm_ref[0, h * V_AUG + MLA_DV:(h + 1) * V_AUG] = jnp.ones((V_AUG - MLA_DV, TILE), _bf16)


def _gla_kernel(qt_ref, kt_ref, ke_ref, gv_ref, dec_ref, sr_ref, ng_ref, o_ref, st_ref):
    t = pl.program_id(0)
    n_batch = qt_ref.shape[0]

    @pl.when(t == 0)
    def _():
        st_ref[...] = jnp.zeros_like(st_ref)

    ri = lax.broadcasted_iota(jnp.int32, (TILE, TILE), 0)
    ci = lax.broadcasted_iota(jnp.int32, (TILE, TILE), 1)
    visible = (ri >= ci) & (ri // GLA_CHUNK == ci // GLA_CHUNK)
    ng = ng_ref[...]
    n_chunks = TILE // GLA_CHUNK
    chunk_rows = [slice(c * GLA_CHUNK, (c + 1) * GLA_CHUNK) for c in range(n_chunks)]
    k_cols = [slice(h * GLA_DK, (h + 1) * GLA_DK) for h in range(GLA_HEADS)]
    v_cols = [slice(h * GLA_DV, (h + 1) * GLA_DV) for h in range(GLA_HEADS)]
    seqs = [(b, h) for b in range(n_batch) for h in range(GLA_HEADS)]
    att = {(b, h): jnp.where(visible, _dot_nt(qt_ref[b, :, k_cols[h]], kt_ref[b, :, k_cols[h]]), 0.0)
           for b, h in seqs}
    upd = {(b, h): [_dot_tn(gv_ref[b, rows, v_cols[h]], ke_ref[b, rows, k_cols[h]])
                    for rows in chunk_rows] for b, h in seqs}
    o_intra = {(b, h): _dot(att[b, h].astype(_bf16), gv_ref[b, :, v_cols[h]]) for b, h in seqs}
    for b, h in seqs:
        st = st_ref[b, h]
        for c, rows in enumerate(chunk_rows):
            o = o_intra[b, h][rows] + _dot_nt(qt_ref[b, rows, k_cols[h]], st.astype(_bf16))
            st = st * dec_ref[b, 0, c:c + 1, k_cols[h]] + upd[b, h][c]
            o = _rms_norm(o, ng) * sr_ref[b, rows, v_cols[h]].astype(_f32)
            o_ref[b, rows, v_cols[h]] = o.astype(_bf16)
        st_ref[b, h] = st


MLA_GROUP = 4


def _mla_tile_start(j):
    return pl.multiple_of(TILE + j * KV_TILE, TILE)


def _mla_queries(q_ref, h, cols):
    q = q_ref[0, h * MLA_QDIM:(h + 1) * MLA_QDIM, cols]
    return jnp.concatenate([q, jnp.zeros((HEAD_PAD - MLA_QDIM, q.shape[1]), q.dtype)], axis=0)


def _mla_finish(o_ref, acc_ref):
    outs = []
    for h in range(MLA_GROUP):
        a = acc_ref[h]
        outs.append(a[0:MLA_DV] / a[MLA_DV:MLA_DV + 1])
    o_ref[0] = jnp.concatenate(outs, axis=0).T.astype(_bf16)


def _mla_kernel(q_ref, k_ref, v_ref, o_ref, acc_ref, knorm_ref):
    i = pl.program_id(2)
    heads = MLA_GROUP
    ones = jnp.ones((HEAD_PAD, LANES), _bf16)

    @pl.when(i == 0)
    def _():
        for h in range(heads):
            kk = k_ref[0, :, h * HEAD_PAD:(h + 1) * HEAD_PAD].astype(_f32)
            hi = (kk * kk).astype(_bf16)
            knorm_ref[h] = jnp.max(_dot(hi, ones), axis=0, keepdims=True)

    def score_bound(h, q):
        qq = q.astype(_f32)
        qn2 = jnp.sum(qq * qq, axis=0, keepdims=True)
        return jnp.sqrt(qn2 * knorm_ref[h][:, 0:1]) * BOUND_SLACK

    def queries(cols):
        q = [_mla_queries(q_ref, h, cols) for h in range(heads)]
        return q, [score_bound(h, q[h]) for h in range(heads)]

    sub = KV_TILE // 2
    q_t, bound = queries(slice(None))
    q_lo, bound_lo = queries(slice(0, KV_TILE))
    q_hi, bound_hi = queries(slice(KV_TILE, Q_TILE))
    q_lo2, bound_lo2 = queries(slice(sub, KV_TILE))
    q_hi2, bound_hi2 = queries(slice(KV_TILE + sub, Q_TILE))

    def key_rows(start, n, h):
        return k_ref[0, pl.ds(pl.multiple_of(start, sub), n), h * HEAD_PAD:(h + 1) * HEAD_PAD]

    def value_cols(start, n, h):
        return v_ref[0, h * V_AUG:(h + 1) * V_AUG, pl.ds(pl.multiple_of(start, sub), n)]

    def keys(j, h):
        return key_rows(_mla_tile_start(j), KV_TILE, h)

    def weighted(j, h, p):
        return _dot(value_cols(_mla_tile_start(j), KV_TILE, h), p)

    def probs(s, b):
        return jnp.exp2(s - b).astype(_bf16)

    r0 = _mla_tile_start(2 * i)
    wide = (lax.broadcasted_iota(jnp.int32, (sub, KV_TILE), 0)
            <= lax.broadcasted_iota(jnp.int32, (sub, KV_TILE), 1))
    square = (lax.broadcasted_iota(jnp.int32, (sub, sub), 0)
              <= lax.broadcasted_iota(jnp.int32, (sub, sub), 1))
    s_meta = [_dot(k_ref[0, PADL:TILE, h * HEAD_PAD:(h + 1) * HEAD_PAD], q_t[h]) for h in range(heads)]
    s_full = [_dot(key_rows(r0, KV_TILE, h), q_hi[h]) for h in range(heads)]
    s_lo_a = [jnp.where(wide, _dot(key_rows(r0, sub, h), q_lo[h]), NEG) for h in range(heads)]
    s_lo_b = [jnp.where(square, _dot(key_rows(r0 + sub, sub, h), q_lo2[h]), NEG) for h in range(heads)]
    s_hi_a = [jnp.where(wide, _dot(key_rows(r0 + KV_TILE, sub, h), q_hi[h]), NEG) for h in range(heads)]
    s_hi_b = [jnp.where(square, _dot(key_rows(r0 + KV_TILE + sub, sub, h), q_hi2[h]), NEG)
              for h in range(heads)]
    for h in range(heads):
        p_meta = jnp.concatenate([jnp.zeros((PADL, Q_TILE), _bf16), probs(s_meta[h], bound[h])], axis=0)
        acc_ref[h] = _dot(v_ref[0, h * V_AUG:(h + 1) * V_AUG, 0:TILE], p_meta)
        acc_ref[h, :, 0:KV_TILE] += _dot(value_cols(r0, sub, h), probs(s_lo_a[h], bound_lo[h]))
        acc_ref[h, :, sub:KV_TILE] += _dot(value_cols(r0 + sub, sub, h), probs(s_lo_b[h], bound_lo2[h]))
        acc_ref[h, :, KV_TILE:Q_TILE] += (
            _dot(value_cols(r0, KV_TILE, h), probs(s_full[h], bound_hi[h]))
            + _dot(value_cols(r0 + KV_TILE, sub, h), probs(s_hi_a[h], bound_hi[h])))
        acc_ref[h, :, KV_TILE + sub:Q_TILE] += _dot(value_cols(r0 + KV_TILE + sub, sub, h),
                                                     probs(s_hi_b[h], bound_hi2[h]))

    def body(jj, c):
        tiles = (2 * jj, 2 * jj + 1)
        s = [[_dot(keys(j, h), q_t[h]) for h in range(heads)] for j in tiles]
        for h in range(heads):
            acc_ref[h] += sum(weighted(j, h, jnp.exp2(s[n][h] - bound[h]).astype(_bf16))
                              for n, j in enumerate(tiles))
        return c

    lax.fori_loop(0, i, body, 0)
    _mla_finish(o_ref, acc_ref)

    l_min = functools.reduce(jnp.minimum,
                             [jnp.min(acc_ref[h, MLA_DV:MLA_DV + 1, :]) for h in range(heads)])

    @pl.when(jnp.logical_not(l_min >= MIN_SOFTMAX_SUM))
    def _():
        _mla_exact(q_ref, k_ref, v_ref, o_ref, acc_ref)


def _mla_exact(q_ref, k_ref, v_ref, o_ref, acc_ref):
    i = pl.program_id(2)
    heads = MLA_GROUP
    q_t = [_mla_queries(q_ref, h, slice(None)) for h in range(heads)]

    ms = []
    for h in range(heads):
        kb = k_ref[0, PADL:TILE, h * HEAD_PAD:(h + 1) * HEAD_PAD]
        s = _dot(kb, q_t[h])
        m0 = jnp.max(s, axis=0, keepdims=True)
        p = jnp.concatenate([jnp.zeros((PADL, Q_TILE), _bf16), jnp.exp2(s - m0).astype(_bf16)], axis=0)
        acc_ref[h] = _dot(v_ref[0, h * V_AUG:(h + 1) * V_AUG, 0:TILE], p)
        ms.append(m0)

    k_row = lax.broadcasted_iota(jnp.int32, (KV_TILE, Q_TILE), 0)
    q_col = lax.broadcasted_iota(jnp.int32, (KV_TILE, Q_TILE), 1)

    def body(j, ms):
        visible = (j - 2 * i) * KV_TILE + k_row <= q_col
        out = []
        for h in range(heads):
            s = _dot(k_ref[0, pl.ds(_mla_tile_start(j), KV_TILE), h * HEAD_PAD:(h + 1) * HEAD_PAD],
                     q_t[h])
            s = jnp.where(visible, s, NEG)
            vb = v_ref[0, h * V_AUG:(h + 1) * V_AUG, pl.ds(_mla_tile_start(j), KV_TILE)]
            m_new = jnp.maximum(ms[h], jnp.max(s, axis=0, keepdims=True))
            alpha = jnp.exp2(ms[h] - m_new)
            acc_ref[h] = alpha * acc_ref[h] + _dot(vb, jnp.exp2(s - m_new).astype(_bf16))
            out.append(m_new)
        return tuple(out)

    lax.fori_loop(0, 2 * i + 2, body, tuple(ms))
    _mla_finish(o_ref, acc_ref)


def _merge_kernel(og_ref, om_ref, ga_ref, gbt_ref, s_ref, wbg_ref, wbm_ref, wo_ref, lng_ref,
                  lnb_ref, rwh_ref, rwl_ref, rb_ref, tril_ref,
                  s2_ref, info_ref, infot_ref, cnt_ref, carry_ref):
    step = pl.program_id(0)

    @pl.when(step == 0)
    def _():
        carry_ref[...] = jnp.zeros_like(carry_ref)

    n_blk = MERGE_TILE // ROUTE_BLOCK
    blocks = [slice(i * ROUTE_BLOCK, (i + 1) * ROUTE_BLOCK) for i in range(n_blk)]
    d = wo_ref.shape[0]
    col_blocks = [slice(c, c + MXU_DIM) for c in range(0, d, MXU_DIM)]
    lane = lax.broadcasted_iota(jnp.int32, (ROUTE_BLOCK, LANES), 1)
    is_g = lane < N_GROUPS
    merged, s2, logits, infos = {}, {}, {}, {}
    carry = [carry_ref[0:1, :]]

    def branches(i):
        rows = blocks[i]
        merged[i] = jnp.concatenate(
            [(ga_ref[rows, cols].astype(_f32) * _dot(og_ref[rows, :], wbg_ref[:, cols])
              + gbt_ref[rows, cols].astype(_f32) * _dot(om_ref[rows, :], wbm_ref[:, cols])
              ).astype(_bf16) for cols in col_blocks], axis=1)

    def residual_norm(i):
        y = ALPHA * s_ref[blocks[i], :] + _dot(merged[i], wo_ref[...])
        s2[i] = _layer_norm(y, lng_ref[...], lnb_ref[...])

    def router_logits(i):
        xh = s2[i].astype(_bf16)
        xl = (s2[i] - xh.astype(_f32)).astype(_bf16)
        logits[i] = (_dot(xh, rwh_ref[...]) + _dot(xl, rwh_ref[...]) + _dot(xh, rwl_ref[...])
                     + rb_ref[...])

    def route(i):
        gl = jnp.where(is_g, logits[i], NEG)
        gmax = jnp.max(gl, axis=-1, keepdims=True)
        gidx = jnp.min(jnp.where(gl == gmax, lane, LANES), axis=-1, keepdims=True)
        p_g = 1.0 / jnp.sum(jnp.where(is_g, jnp.exp(gl - gmax), 0.0), axis=-1, keepdims=True)
        lo = N_GROUPS + EXPERTS_PER_GROUP * gidx
        el = jnp.where((lane >= lo) & (lane < lo + EXPERTS_PER_GROUP), logits[i], NEG)
        v1 = jnp.max(el, axis=-1, keepdims=True)
        i1 = jnp.min(jnp.where(el == v1, lane, LANES), axis=-1, keepdims=True)
        el2 = jnp.where(lane == i1, NEG, el)
        v2 = jnp.max(el2, axis=-1, keepdims=True)
        i2 = jnp.min(jnp.where(el2 == v2, lane, LANES), axis=-1, keepdims=True)
        tt = jnp.exp(v2 - v1)
        p1 = 1.0 / (1.0 + tt)
        p2 = tt / (1.0 + tt)
        e1 = i1 - N_GROUPS
        e2 = i2 - N_GROUPS
        hit1 = lane == e1
        hit2 = lane == e2
        onehot = jnp.where(hit1 | hit2, 1.0, 0.0)
        before = _dot(tril_ref[...], onehot.astype(_bf16)) + carry[0]
        r1 = jnp.sum(jnp.where(hit1, before, 0.0), axis=-1, keepdims=True)
        r2 = jnp.sum(jnp.where(hit2, before, 0.0), axis=-1, keepdims=True)
        carry[0] = carry[0] + jnp.sum(onehot, axis=0, keepdims=True)
        infos[i] = jnp.where(lane == 0, e1.astype(_f32),
                   jnp.where(lane == 1, e2.astype(_f32),
                   jnp.where(lane == 2, p_g * p1,
                   jnp.where(lane == 3, p_g * p2,
                   jnp.where(lane == 4, r1,
                   jnp.where(lane == 5, r2, 0.0))))))

    for stage in (branches, residual_norm, router_logits, route):
        for i in range(n_blk):
            stage(i)
    _tt_store(s2_ref, jnp.concatenate([s2[i] for i in range(n_blk)], axis=0))
    carry_ref[...] = jnp.broadcast_to(carry[0], carry_ref.shape)
    cnt_ref[...] = jnp.broadcast_to(carry[0], cnt_ref.shape)
    info = jnp.concatenate([infos[i] for i in range(n_blk)], axis=0)
    info_ref[...] = info
    infot_ref[...] = info.T[0:SUBLANES]


def _dispatch_kernel(pos_ref, zrow_ref, nused_ref, s2_ref, xs_hbm, zero_ref, sems):
    g = pl.program_id(0)
    zero_sem = sems.at[1]
    row_sem = sems.at[0]

    tile_rows = EXPERT_TILE * SUBLANES

    def zero_copy(row):
        start = pl.multiple_of(row * SUBLANES, tile_rows)
        return pltpu.make_async_copy(zero_ref, xs_hbm.at[pl.ds(start, tile_rows)], zero_sem)

    @pl.when(g == 0)
    def _():
        zero_ref[...] = jnp.zeros_like(zero_ref)

        def start(e, c):
            @pl.when(zrow_ref[e] >= 0)
            def _():
                zero_copy(zrow_ref[e]).start()
            return c

        def wait(e, c):
            @pl.when(zrow_ref[e] >= 0)
            def _():
                zero_copy(0).wait()
            return c

        def start_tail(u, c):
            zero_copy(u * EXPERT_TILE).start()
            return c

        def wait_tail(u, c):
            zero_copy(0).wait()
            return c

        n_tiles = xs_hbm.shape[0] // tile_rows
        lax.fori_loop(0, N_EXPERTS, start, 0)
        lax.fori_loop(nused_ref[0], n_tiles, start_tail, 0)
        lax.fori_loop(0, N_EXPERTS, wait, 0)
        lax.fori_loop(nused_ref[0], n_tiles, wait_tail, 0)

    n_tok = pl.num_programs(0) * DISPATCH_TILE

    def issue(r, c):
        tok = g * DISPATCH_TILE + r
        for k in range(2):
            pltpu.make_async_copy(s2_ref.at[_tt_rows(r)],
                                  xs_hbm.at[_tt_rows(pos_ref[k * n_tok + tok])], row_sem).start(priority=k)
        return c

    lax.fori_loop(0, DISPATCH_TILE, issue, 0, unroll=8)
    for k in range(2):
        pltpu.make_async_copy(s2_ref, xs_hbm.at[pl.ds(0, DISPATCH_TILE * SUBLANES)], row_sem).wait()


def _expert_kernel(te_ref, nused_ref, x_ref, wg_ref, wu_ref, wd_ref, o_ref, wgb_ref, wub_ref, wdb_ref):
    u = pl.program_id(0)
    used = u < nused_ref[0]
    first_of_expert = jnp.logical_or(u == 0, te_ref[u] != te_ref[jnp.maximum(u - 1, 0)])

    @pl.when(jnp.logical_and(used, first_of_expert))
    def _():
        wgb_ref[...] = wg_ref[0].astype(_bf16)
        wub_ref[...] = wu_ref[0].astype(_bf16)
        wdb_ref[...] = wd_ref[0].astype(_bf16)

    @pl.when(used)
    def _():
        x = _tt_load(x_ref, EXPERT_TILE).astype(_bf16)
        a = _dot(x, wgb_ref[...])
        up = _dot(x, wub_ref[...])
        hid = a * _sigmoid(a) * up
        _tt_store(o_ref, _dot(hid.astype(_bf16), wdb_ref[...]))

    @pl.when(u >= nused_ref[0])
    def _():
        o_ref[...] = jnp.zeros_like(o_ref)


def _combine_kernel(pos_ref, s2_ref, info_ref, ys_hbm, lng_ref, lnb_ref, o_ref, buf_ref, sems):
    g = pl.program_id(0)
    last = pl.num_programs(0) - 1
    n_tok = pl.num_programs(0) * COMBINE_TILE

    def start_copies(tile, slot, r):
        tok = tile * COMBINE_TILE + r
        for k in range(2):
            pltpu.make_async_copy(ys_hbm.at[_tt_rows(pos_ref[k * n_tok + tok])],
                                  buf_ref.at[slot, k, _tt_rows(r)], sems.at[slot]).start(priority=k)

    def wait_slot(slot):
        for k in range(2):
            pltpu.make_async_copy(ys_hbm.at[pl.ds(0, COMBINE_TILE * SUBLANES)], buf_ref.at[slot, k],
                                  sems.at[slot]).wait()

    @pl.when(g == 0)
    def _():
        def body(r, c):
            for tile in range(COMBINE_SLOTS - 1):
                start_copies(tile, tile, r)
            return c

        lax.fori_loop(0, COMBINE_TILE, body, 0, unroll=8)

    slot = g % COMBINE_SLOTS
    wait_slot(slot)
    info = info_ref[...]
    y = (ALPHA * _tt_load(s2_ref, COMBINE_TILE)
         + info[:, 2:3] * _tt_load(buf_ref, COMBINE_TILE, (slot, 0))
         + info[:, 3:4] * _tt_load(buf_ref, COMBINE_TILE, (slot, 1)))
    o_ref[...] = _layer_norm(y, lng_ref[...], lnb_ref[...])

    ahead = COMBINE_SLOTS - 1
    ahead_slot = (g + ahead) % COMBINE_SLOTS
    for r in range(COMBINE_TILE):
        start_copies(jnp.minimum(g + ahead, last), ahead_slot, r)

    @pl.when(g == last)
    def _():
        for n in range(1, COMBINE_SLOTS):
            wait_slot((g + n) % COMBINE_SLOTS)


def _const_spec(shape):
    nd = len(shape)
    return pl.BlockSpec(shape, lambda *_: (0,) * nd)


def _rope_tables(lp):
    pos = jnp.maximum(jnp.arange(lp, dtype=_f32) - PADL, 0.0)
    inv_freq = ROPE_BASE ** (-jnp.arange(0, MLA_ROPE, 2, dtype=_f32) / MLA_ROPE)
    ang = pos[:, None] * inv_freq[None, :]
    cos, sin = jnp.cos(ang), jnp.sin(ang)
    half = MLA_ROPE // 2
    ones = jnp.ones((lp, MLA_NOPE), _f32)
    zeros_n = jnp.zeros((lp, MLA_NOPE), _f32)
    zeros_h = jnp.zeros((lp, half), _f32)
    tail1 = jnp.ones((lp, LANES - MLA_QDIM), _f32)
    tail0 = jnp.zeros((lp, LANES - MLA_QDIM), _f32)
    cos_t = jnp.concatenate([ones, cos, cos, tail1], axis=1)
    sin_lo = jnp.concatenate([zeros_n, -sin, zeros_h, tail0], axis=1)
    sin_hi = jnp.concatenate([zeros_n, zeros_h, sin, tail0], axis=1)
    return cos_t, sin_lo, sin_hi, cos.T, sin.T


def _pad_heads(w, width):
    k = w.shape[0]
    w = w.reshape(k, MLA_HEADS, width)
    w = jnp.pad(w, ((0, 0), (0, 0), (0, HEAD_PAD - width)))
    return w.reshape(k, MLA_HEADS * HEAD_PAD)


def kernel(x, meta_tokens, ln_emb_g, ln_emb_b, w_in, gla_gate_w2, gla_gate_b, gla_norm_g, mla_q_norm_g, mla_w_uq, mla_kv_norm_g, mla_w_uk, mla_w_uv, w_branch_gla, w_branch_mla, w_out, ln_mix_g, ln_mix_b, router_group_w, router_group_b, router_expert_w, router_expert_b, expert_w_gate, expert_w_up, expert_w_down, ln_ffn_g, ln_ffn_b):
    bsz, seq, d = x.shape
    assert d == D_MODEL and seq % Q_TILE == 0 and w_in.shape[0] == DEPTH == 1
    lp = PADL + N_META + seq
    nt = lp // TILE
    ntok = bsz * seq
    row2 = lambda v: v.reshape(1, -1).astype(_f32)

    head_tile = jnp.concatenate([jnp.zeros((PADL, d), _f32), meta_tokens.astype(_f32)], axis=0)
    wi = w_in[0]
    o_a = 2 * GLA_QK + 2 * GLA_VW
    o_cq = o_a + GLA_GATE_RANK
    o_ckv = o_cq + MLA_Q_RANK
    o_kr = o_ckv + MLA_KV_RANK
    o_ga = o_kr + MLA_ROPE
    w_a = jnp.pad(wi[:, o_a:o_cq], ((0, 0), (0, LANES - GLA_GATE_RANK)))
    w_kr = jnp.pad(wi[:, o_kr:o_ga], ((0, 0), (MLA_NOPE, LANES - MLA_QDIM)))
    w_all = jnp.concatenate([wi[:, :o_a], w_a, wi[:, o_cq:o_kr], w_kr, wi[:, o_ga:]], axis=1).astype(_bf16)
    assert w_all.shape == (d, _W_COLS)
    w2p = jnp.pad(gla_gate_w2[0], ((0, LANES - GLA_GATE_RANK), (0, 0))).astype(_bf16)
    wuqt = mla_w_uq[0].T.astype(_bf16)
    wuk = _pad_heads(mla_w_uk[0], MLA_NOPE).astype(_bf16)
    wuvt = mla_w_uv[0].T.astype(_bf16)
    cos_t, sin_lo, sin_hi, cos_tr, sin_tr = _rope_tables(lp)
    blk = np.arange(TILE)
    tril_chunks = jnp.asarray(
        ((blk[:, None] >= blk[None, :]) & (blk[:, None] // GLA_CHUNK == blk[None, :] // GLA_CHUNK)),
        dtype=_bf16)

    pad_map = lambda b, t: (b, t, 0)
    real_map = lambda b, t: (b, jnp.maximum(t - 1, 0), 0)
    real_map_t = lambda b, t: (b, 0, jnp.maximum(t - 1, 0))
    out_shapes = (
        jax.ShapeDtypeStruct((bsz, seq, d), _f32),
        jax.ShapeDtypeStruct((bsz, lp, GLA_QK), _bf16),
        jax.ShapeDtypeStruct((bsz, lp, GLA_QK), _bf16),
        jax.ShapeDtypeStruct((bsz, lp, GLA_QK), _bf16),
        jax.ShapeDtypeStruct((bsz, lp, GLA_VW), _bf16),
        jax.ShapeDtypeStruct((bsz * nt, TILE // GLA_CHUNK, GLA_QK), _f32),
        jax.ShapeDtypeStruct((bsz, seq, GLA_VW), _bf16),
        jax.ShapeDtypeStruct((bsz, MLA_HEADS * MLA_QDIM, seq), _bf16),
        jax.ShapeDtypeStruct((bsz, lp, MLA_HEADS * HEAD_PAD), _bf16),
        jax.ShapeDtypeStruct((bsz, MLA_HEADS * V_AUG, lp), _bf16),
        jax.ShapeDtypeStruct((bsz, seq, d), _bf16),
        jax.ShapeDtypeStruct((bsz, seq, d), _bf16),
    )
    out_specs = (
        pl.BlockSpec((1, TILE, d), real_map),
        pl.BlockSpec((1, TILE, GLA_QK), pad_map),
        pl.BlockSpec((1, TILE, GLA_QK), pad_map),
        pl.BlockSpec((1, TILE, GLA_QK), pad_map),
        pl.BlockSpec((1, TILE, GLA_VW), pad_map),
        pl.BlockSpec((1, TILE // GLA_CHUNK, GLA_QK), lambda b, t: (b * nt + t, 0, 0)),
        pl.BlockSpec((1, TILE, GLA_VW), real_map),
        pl.BlockSpec((1, MLA_HEADS * MLA_QDIM, TILE), real_map_t),
        pl.BlockSpec((1, TILE, MLA_HEADS * HEAD_PAD), pad_map),
        pl.BlockSpec((1, MLA_HEADS * V_AUG, TILE), lambda b, t: (b, 0, t)),
        pl.BlockSpec((1, TILE, d), real_map),
        pl.BlockSpec((1, TILE, d), real_map),
    )
    tab_spec = pl.BlockSpec((TILE, LANES), lambda b, t: (t, 0))
    tabt_spec = pl.BlockSpec((MLA_ROPE // 2, TILE), lambda b, t: (0, t))
    (s_emb, qt, kt, ke, gv, dec, sr, qm, km, vm, gate_a, gate_b) = pl.pallas_call(
        _inproj_kernel,
        grid=(bsz, nt),
        in_specs=[
            pl.BlockSpec((1, TILE, d), real_map),
            _const_spec((TILE, d)),
            _const_spec((1, d)), _const_spec((1, d)),
            _const_spec((d, _W_COLS)),
            _const_spec((LANES, GLA_QK)), _const_spec((1, GLA_QK)),
            _const_spec((1, MLA_Q_RANK)), _const_spec((MLA_HEADS * MLA_QDIM, MLA_Q_RANK)),
            _const_spec((1, MLA_KV_RANK)), _const_spec((MLA_KV_RANK, MLA_HEADS * HEAD_PAD)),
            _const_spec((MLA_HEADS * MLA_DV, MLA_KV_RANK)),
            tab_spec, tab_spec, tab_spec, tabt_spec, tabt_spec,
            _const_spec((TILE, TILE)),
        ],
        out_specs=out_specs,
        out_shape=out_shapes,
        compiler_params=pltpu.CompilerParams(
            dimension_semantics=("arbitrary", "arbitrary"), vmem_limit_bytes=VMEM_LIMIT),
        name="inproj",
    )(x, head_tile, row2(ln_emb_g), row2(ln_emb_b), w_all, w2p, row2(gla_gate_b[0]),
      row2(mla_q_norm_g[0]), wuqt, row2(mla_kv_norm_g[0]), wuk, wuvt, cos_t, sin_lo, sin_hi,
      cos_tr, sin_tr, tril_chunks)

    o_gla = pl.pallas_call(
        _gla_kernel,
        grid=(nt,),
        in_specs=[
            pl.BlockSpec((bsz, TILE, GLA_QK), lambda t: (0, t, 0)),
            pl.BlockSpec((bsz, TILE, GLA_QK), lambda t: (0, t, 0)),
            pl.BlockSpec((bsz, TILE, GLA_QK), lambda t: (0, t, 0)),
            pl.BlockSpec((bsz, TILE, GLA_VW), lambda t: (0, t, 0)),
            pl.BlockSpec((bsz, 1, TILE // GLA_CHUNK, GLA_QK), lambda t: (0, t, 0, 0)),
            pl.BlockSpec((bsz, TILE, GLA_VW), lambda t: (0, jnp.maximum(t - 1, 0), 0)),
            _const_spec((1, GLA_DV)),
        ],
        out_specs=pl.BlockSpec((bsz, TILE, GLA_VW), lambda t: (0, jnp.maximum(t - 1, 0), 0)),
        out_shape=jax.ShapeDtypeStruct((bsz, seq, GLA_VW), _bf16),
        scratch_shapes=[pltpu.VMEM((bsz, GLA_HEADS, GLA_DV, GLA_DK), _f32)],
        compiler_params=pltpu.CompilerParams(
            dimension_semantics=("arbitrary",), vmem_limit_bytes=VMEM_LIMIT),
        name="gla",
    )(qt, kt, ke, gv, dec.reshape(bsz, nt, TILE // GLA_CHUNK, GLA_QK), sr, row2(gla_norm_g[0]))

    group_width = MLA_GROUP * HEAD_PAD
    o_mla = pl.pallas_call(
        _mla_kernel,
        grid=(bsz, MLA_HEADS // MLA_GROUP, seq // Q_TILE),
        in_specs=[
            pl.BlockSpec((1, MLA_GROUP * MLA_QDIM, Q_TILE), lambda b, hp, i: (b, hp, i)),
            pl.BlockSpec((1, lp, group_width), lambda b, hp, i: (b, 0, hp)),
            pl.BlockSpec((1, MLA_GROUP * V_AUG, lp), lambda b, hp, i: (b, hp, 0)),
        ],
        out_specs=pl.BlockSpec((1, Q_TILE, MLA_GROUP * MLA_DV), lambda b, hp, i: (b, i, hp)),
        out_shape=jax.ShapeDtypeStruct((bsz, seq, MLA_HEADS * MLA_DV), _bf16),
        scratch_shapes=[pltpu.VMEM((MLA_GROUP, V_AUG, Q_TILE), _f32),
                        pltpu.VMEM((MLA_GROUP, 1, LANES), _f32)],
        compiler_params=pltpu.CompilerParams(
            dimension_semantics=("arbitrary", "arbitrary", "arbitrary"),
            vmem_limit_bytes=VMEM_LIMIT),
        name="mla",
    )(qm, km, vm)

    rw = jnp.concatenate([router_group_w[0], router_expert_w[0]], axis=1)
    rw = jnp.pad(rw, ((0, 0), (0, LANES - rw.shape[1])))
    rwh = rw.astype(_bf16)
    rwl = (rw - rwh.astype(_f32)).astype(_bf16)
    rb = jnp.concatenate([router_group_b[0], router_expert_b[0]])
    rb = jnp.pad(rb, (0, LANES - rb.shape[0])).reshape(1, LANES)
    mi = np.arange(ROUTE_BLOCK)
    tril_strict = jnp.asarray(mi[:, None] > mi[None, :], dtype=_bf16)
    flat = lambda a: a.reshape(ntok, a.shape[-1])
    tok_spec = lambda w: pl.BlockSpec((MERGE_TILE, w), lambda g: (g, 0))
    tt_spec = lambda n, index_map: pl.BlockSpec((n * SUBLANES, LANES), index_map)
    s2, info, info_t, cnt = pl.pallas_call(
        _merge_kernel,
        grid=(ntok // MERGE_TILE,),
        in_specs=[tok_spec(d), tok_spec(d), tok_spec(d), tok_spec(d), tok_spec(d),
                  _const_spec((d, d)), _const_spec((d, d)), _const_spec((d, d)),
                  _const_spec((1, d)), _const_spec((1, d)),
                  _const_spec((d, LANES)), _const_spec((d, LANES)), _const_spec((1, LANES)),
                  _const_spec((ROUTE_BLOCK, ROUTE_BLOCK))],
        out_specs=(tt_spec(MERGE_TILE, lambda g: (g, 0)), tok_spec(LANES),
                   pl.BlockSpec((SUBLANES, MERGE_TILE), lambda g: (0, g)), _const_spec((8, LANES))),
        out_shape=(jax.ShapeDtypeStruct((ntok * SUBLANES, LANES), _f32),
                   jax.ShapeDtypeStruct((ntok, LANES), _f32),
                   jax.ShapeDtypeStruct((SUBLANES, ntok), _f32),
                   jax.ShapeDtypeStruct((8, LANES), _f32)),
        scratch_shapes=[pltpu.VMEM((8, LANES), _f32)],
        compiler_params=pltpu.CompilerParams(
            dimension_semantics=("arbitrary",), vmem_limit_bytes=VMEM_LIMIT),
        name="merge_router",
    )(flat(o_gla), flat(o_mla), flat(gate_a), flat(gate_b), flat(s_emb),
      w_branch_gla[0].astype(_bf16), w_branch_mla[0].astype(_bf16), w_out[0].astype(_bf16),
      row2(ln_mix_g[0]), row2(ln_mix_b[0]), rwh, rwl, rb, tril_strict)

    n_tiles = (2 * ntok + N_EXPERTS * (EXPERT_TILE - 1)) // EXPERT_TILE
    n_rows = n_tiles * EXPERT_TILE
    e_idx = info_t[0:2].astype(jnp.int32)
    rank = info_t[4:6].astype(jnp.int32)
    counts = cnt[0, :N_EXPERTS].astype(jnp.int32)
    padded = ((counts + EXPERT_TILE - 1) // EXPERT_TILE) * EXPERT_TILE
    ends = jnp.cumsum(padded)
    starts = ends - padded
    expert_ids = jnp.arange(N_EXPERTS, dtype=jnp.int32)[:, None, None]
    start_of = jnp.sum(jnp.where(e_idx[None] == expert_ids, starts[:, None, None], 0), axis=0)
    pos = (start_of + rank).reshape(-1)
    tile_start = jnp.arange(n_tiles, dtype=jnp.int32) * EXPERT_TILE
    tile_expert = jnp.minimum(
        jnp.sum((ends[None, :] <= tile_start[:, None]).astype(jnp.int32), axis=1), N_EXPERTS - 1)
    n_used = (ends[-1:] // EXPERT_TILE).astype(jnp.int32)
    zero_row = jnp.where(padded > 0, ends - EXPERT_TILE, -1).astype(jnp.int32)

    any_spec = pl.BlockSpec(memory_space=pl.ANY)
    xs = pl.pallas_call(
        _dispatch_kernel,
        grid_spec=pltpu.PrefetchScalarGridSpec(
            num_scalar_prefetch=3,
            grid=(ntok // DISPATCH_TILE,),
            in_specs=[tt_spec(DISPATCH_TILE, lambda g, p, z, nu: (g, 0))],
            out_specs=any_spec,
            scratch_shapes=[pltpu.VMEM((EXPERT_TILE * SUBLANES, LANES), _f32),
                            pltpu.SemaphoreType.DMA((2,))],
        ),
        out_shape=jax.ShapeDtypeStruct((n_rows * SUBLANES, LANES), _f32),
        compiler_params=pltpu.CompilerParams(
            dimension_semantics=("arbitrary",), vmem_limit_bytes=VMEM_LIMIT, has_side_effects=True),
        name="dispatch",
    )(pos, zero_row, n_used, s2)

    ff = EXPERT_FF
    wg = expert_w_gate[0].reshape(N_EXPERTS, d, ff)
    wu = expert_w_up[0].reshape(N_EXPERTS, d, ff)
    wd = expert_w_down[0].reshape(N_EXPERTS, ff, d)
    ys = pl.pallas_call(
        _expert_kernel,
        grid_spec=pltpu.PrefetchScalarGridSpec(
            num_scalar_prefetch=2,
            grid=(n_tiles,),
            in_specs=[
                tt_spec(EXPERT_TILE, lambda u, te, nu: (jnp.minimum(u, nu[0] - 1), 0)),
                pl.BlockSpec((1, d, ff), lambda u, te, nu: (te[u], 0, 0)),
                pl.BlockSpec((1, d, ff), lambda u, te, nu: (te[u], 0, 0)),
                pl.BlockSpec((1, ff, d), lambda u, te, nu: (te[u], 0, 0)),
            ],
            out_specs=tt_spec(EXPERT_TILE, lambda u, te, nu: (u, 0)),
            scratch_shapes=[pltpu.VMEM((d, ff), _bf16), pltpu.VMEM((d, ff), _bf16),
                            pltpu.VMEM((ff, d), _bf16)],
        ),
        out_shape=jax.ShapeDtypeStruct((n_rows * SUBLANES, LANES), _f32),
        compiler_params=pltpu.CompilerParams(
            dimension_semantics=("arbitrary",), vmem_limit_bytes=VMEM_LIMIT),
        name="experts",
    )(tile_expert, n_used, xs, wg, wu, wd)

    out = pl.pallas_call(
        _combine_kernel,
        grid_spec=pltpu.PrefetchScalarGridSpec(
            num_scalar_prefetch=1,
            grid=(ntok // COMBINE_TILE,),
            in_specs=[
                tt_spec(COMBINE_TILE, lambda g, p: (g, 0)),
                pl.BlockSpec((COMBINE_TILE, LANES), lambda g, p: (g, 0)),
                any_spec,
                pl.BlockSpec((1, d), lambda g, p: (0, 0)),
                pl.BlockSpec((1, d), lambda g, p: (0, 0)),
            ],
            out_specs=pl.BlockSpec((COMBINE_TILE, d), lambda g, p: (g, 0)),
            scratch_shapes=[pltpu.VMEM((COMBINE_SLOTS, 2, COMBINE_TILE * SUBLANES, LANES), _f32),
                            pltpu.SemaphoreType.DMA((COMBINE_SLOTS,))],
        ),
        out_shape=jax.ShapeDtypeStruct((ntok, d), _f32),
        compiler_params=pltpu.CompilerParams(
            dimension_semantics=("arbitrary",), vmem_limit_bytes=VMEM_LIMIT),
        name="combine_ln",
    )(pos, s2, info, ys, row2(ln_ffn_g[0]), row2(ln_ffn_b[0]))
    return out.reshape(bsz, seq, d)
```

```python
import functools

import jax
import jax.numpy as jnp
import numpy as np
from jax import lax
from jax.experimental import pallas as pl
from jax.experimental.pallas import tpu as pltpu

D_MODEL = 1024
N_META = 16
GLA_HEADS = 4
GLA_DK = 128
GLA_DV = 256
GLA_QK = GLA_HEADS * GLA_DK
GLA_VW = GLA_HEADS * GLA_DV
GLA_GATE_RANK = 16
GLA_GATE_TAU = 16.0
GLA_CHUNK = 64
MLA_HEADS = 16
MLA_Q_RANK = 384
MLA_KV_RANK = 256
MLA_NOPE = 64
MLA_ROPE = 32
MLA_DV = 64
MLA_QDIM = MLA_NOPE + MLA_ROPE
ROPE_BASE = 10000.0
N_GROUPS = 4
EXPERTS_PER_GROUP = 8
N_EXPERTS = N_GROUPS * EXPERTS_PER_GROUP
EXPERT_FF = 256
DEPTH = 1
ALPHA = (2.0 * DEPTH) ** 0.25
LN_EPS = 1e-5
RMS_EPS = 1e-6

LANES = 128
SUBLANES = 8
MXU_DIM = 256
TILE = 256
PADL = TILE - N_META
HEAD_PAD = LANES
BF16_ROWS = 16
KV_TILE = 512
Q_TILE = 2 * KV_TILE
V_AUG = MLA_DV + BF16_ROWS
MERGE_TILE = 1024
ROUTE_BLOCK = 256
EXPERT_TILE = 256
DISPATCH_TILE = 2048
COMBINE_TILE = 256
COMBINE_SLOTS = 3
NEG = -1e30
LOG2E = 1.4426950408889634
BOUND_SLACK = 1.02
MIN_SOFTMAX_SUM = 2.0 ** -100
VMEM_LIMIT = 56 * 1024 * 1024

_C_Q, _C_K, _C_V, _C_R = 0, 512, 1024, 2048
_C_A = 3072
_C_CQ = _C_A + LANES
_C_CKV = _C_CQ + MLA_Q_RANK
_C_KR = _C_CKV + MLA_KV_RANK
_C_GA = _C_KR + LANES
_C_GB = _C_GA + D_MODEL
_W_COLS = _C_GB + D_MODEL

_f32 = jnp.float32
_bf16 = jnp.bfloat16


def _dot(a, b):
    return jnp.dot(a, b, preferred_element_type=_f32)


def _dot_nt(a, b):
    return lax.dot_general(a, b, (((1,), (1,)), ((), ())), preferred_element_type=_f32)


def _dot_tn(a, b):
    return lax.dot_general(a, b, (((0,), (0,)), ((), ())), preferred_element_type=_f32)


def _layer_norm(x, g, b):
    mu = jnp.mean(x, axis=-1, keepdims=True)
    xc = x - mu
    var = jnp.mean(xc * xc, axis=-1, keepdims=True)
    return xc * lax.rsqrt(var + LN_EPS) * g + b


def _rms_norm(x, g):
    ms = jnp.mean(x * x, axis=-1, keepdims=True)
    return x * lax.rsqrt(ms + RMS_EPS) * g


def _sigmoid(x):
    return 1.0 / (1.0 + jnp.exp(-x))


def _tt_load(ref, n, lead=()):
    return jnp.concatenate(
        [ref[lead + (pl.ds(a, n, stride=SUBLANES), slice(None))] for a in range(SUBLANES)], axis=1)


def _tt_store(ref, x):
    n = x.shape[0]
    for a in range(SUBLANES):
        ref[pl.ds(a, n, stride=SUBLANES), :] = x[:, a * LANES:(a + 1) * LANES]


def _tt_rows(tok):
    return pl.ds(pl.multiple_of(tok * SUBLANES, SUBLANES), SUBLANES)


def _rope(x, cos, sin_lo, sin_hi):
    half = MLA_ROPE // 2
    from_hi = pltpu.roll(x, LANES - half, 1)
    from_lo = pltpu.roll(x, half, 1)
    return x * cos + from_hi * sin_lo + from_lo * sin_hi


def _inproj_kernel(x_ref, head_ref, lng_ref, lnb_ref, w_ref, w2_ref, gb_ref, qg_ref, wuqt_ref, kvg_ref,
                   wuk_ref, wuvt_ref, cos_ref, sl_ref, sh_ref, cost_ref, sint_ref, tril_ref,
                   s_ref, qt_ref, kt_ref, ke_ref, gv_ref, dec_ref, sr_ref, qm_ref, km_ref,
                   vm_ref, ga_ref, gbt_ref):
    t = pl.program_id(1)
    x_in = jnp.where(t == 0, head_ref[...], x_ref[0])
    sn = _layer_norm(x_in, lng_ref[...], lnb_ref[...])
    s_ref[0] = sn
    snb = sn.astype(_bf16)
    row = t * TILE + lax.broadcasted_iota(jnp.int32, (TILE, 1), 0)
    valid = row >= PADL

    proj = lambda c0, width: _dot(snb, w_ref[:, c0:c0 + width])
    a_lr = proj(_C_A, LANES)
    cq = proj(_C_CQ, MLA_Q_RANK)
    ckv = proj(_C_CKV, MLA_KV_RANK)
    kr_raw = proj(_C_KR, LANES)
    gq = proj(_C_Q, GLA_QK)
    gk = proj(_C_K, GLA_QK)
    gv_ref[0] = jnp.where(valid, proj(_C_V, GLA_VW), 0.0).astype(_bf16)
    r = proj(_C_R, GLA_VW)
    sr_ref[0] = (r * _sigmoid(r)).astype(_bf16)
    ga_ref[0] = _sigmoid(proj(_C_GA, D_MODEL)).astype(_bf16)
    gbt_ref[0] = _sigmoid(proj(_C_GB, D_MODEL)).astype(_bf16)

    z = _dot(a_lr.astype(_bf16), w2_ref[...]) + gb_ref[...]
    cqn = _rms_norm(cq, qg_ref[...]).astype(_bf16)
    ckvn = _rms_norm(ckv, kvg_ref[...]).astype(_bf16)
    qft = _dot_nt(wuqt_ref[...], cqn)
    kf = _dot(ckvn, wuk_ref[...])
    vt = _dot_nt(wuvt_ref[...], ckvn)

    la = (jnp.minimum(z, 0.0) - jnp.log1p(jnp.exp(-jnp.abs(z)))) * (1.0 / GLA_GATE_TAU)
    la = jnp.where(valid, la, 0.0)
    hi = la.astype(_bf16)
    r1 = la - hi.astype(_f32)
    mid = r1.astype(_bf16)
    lo = (r1 - mid.astype(_f32)).astype(_bf16)
    tril = tril_ref[...]
    bc = _dot(tril, hi) + _dot(tril, mid) + _dot(tril, lo)
    n_chunks = TILE // GLA_CHUNK
    lasts = [bc[c * GLA_CHUNK + GLA_CHUNK - 1:(c + 1) * GLA_CHUNK, :] for c in range(n_chunks)]
    for c in range(n_chunks):
        dec_ref[0, c:c + 1, :] = jnp.exp(lasts[c])
    b_last = jnp.concatenate(
        [jnp.broadcast_to(l, (GLA_CHUNK, GLA_QK)) for l in lasts], axis=0)
    gk = jnp.where(valid, gk, 0.0)
    qt_ref[0] = (gq * (GLA_DK ** -0.5) * jnp.exp(bc)).astype(_bf16)
    kt_ref[0] = (gk * jnp.exp(-bc)).astype(_bf16)
    ke_ref[0] = (gk * jnp.exp(b_last - bc)).astype(_bf16)

    cos = cos_ref[...]
    sl = sl_ref[...]
    sh = sh_ref[...]
    scale = (MLA_QDIM ** -0.5) * LOG2E
    cost = cost_ref[...]
    sint = sint_ref[...]
    half = MLA_ROPE // 2
    for h in range(MLA_HEADS):
        base = h * MLA_QDIM
        x1 = qft[base + MLA_NOPE:base + MLA_NOPE + half]
        x2 = qft[base + MLA_NOPE + half:base + MLA_QDIM]
        qm_ref[0, base:base + MLA_NOPE] = (qft[base:base + MLA_NOPE] * scale).astype(_bf16)
        qm_ref[0, base + MLA_NOPE:base + MLA_NOPE + half] = ((x1 * cost - x2 * sint) * scale).astype(_bf16)
        qm_ref[0, base + MLA_NOPE + half:base + MLA_QDIM] = ((x1 * sint + x2 * cost) * scale).astype(_bf16)
    kr = _rope(kr_raw, cos, sl, sh)
    for h in range(MLA_HEADS):
        km_ref[0, :, h * HEAD_PAD:(h + 1) * HEAD_PAD] = (
            kf[:, h * HEAD_PAD:(h + 1) * HEAD_PAD] + kr).astype(_bf16)
    for h in range(MLA_HEADS):
        vm_ref[0, h * V_AUG:h * V_AUG + MLA_DV] = vt[h * MLA_DV:(h + 1) * MLA_DV].astype(_bf16)
        vm_ref[0, h * V_AUG + MLA_DV:(h + 1) * V_AUG] = jnp.ones((V_AUG - MLA_DV, TILE), _bf16)


def _gla_kernel(qt_ref, kt_ref, ke_ref, gv_ref, dec_ref, sr_ref, ng_ref, o_ref, st_ref):
    t = pl.program_id(0)
    n_batch = qt_ref.shape[0]

    @pl.when(t == 0)
    def _():
        st_ref[...] = jnp.zeros_like(st_ref)

    ri = lax.broadcasted_iota(jnp.int32, (TILE, TILE), 0)
    ci = lax.broadcasted_iota(jnp.int32, (TILE, TILE), 1)
    visible = (ri >= ci) & (ri // GLA_CHUNK == ci // GLA_CHUNK)
    ng = ng_ref[...]
    n_chunks = TILE // GLA_CHUNK
    chunk_rows = [slice(c * GLA_CHUNK, (c + 1) * GLA_CHUNK) for c in range(n_chunks)]
    k_cols = [slice(h * GLA_DK, (h + 1) * GLA_DK) for h in range(GLA_HEADS)]
    v_cols = [slice(h * GLA_DV, (h + 1) * GLA_DV) for h in range(GLA_HEADS)]
    seqs = [(b, h) for b in range(n_batch) for h in range(GLA_HEADS)]
    att = {(b, h): jnp.where(visible, _dot_nt(qt_ref[b, :, k_cols[h]], kt_ref[b, :, k_cols[h]]), 0.0)
           for b, h in seqs}
    upd = {(b, h): [_dot_tn(gv_ref[b, rows, v_cols[h]], ke_ref[b, rows, k_cols[h]])
                    for rows in chunk_rows] for b, h in seqs}
    o_intra = {(b, h): _dot(att[b, h].astype(_bf16), gv_ref[b, :, v_cols[h]]) for b, h in seqs}
    for b, h in seqs:
        st = st_ref[b, h]
        for c, rows in enumerate(chunk_rows):
            o = o_intra[b, h][rows] + _dot_nt(qt_ref[b, rows, k_cols[h]], st.astype(_bf16))
            st = st * dec_ref[b, 0, c:c + 1, k_cols[h]] + upd[b, h][c]
            o = _rms_norm(o, ng) * sr_ref[b, rows, v_cols[h]].astype(_f32)
            o_ref[b, rows, v_cols[h]] = o.astype(_bf16)
        st_ref[b, h] = st


MLA_GROUP = 4


def _mla_tile_start(j):
    return pl.multiple_of(TILE + j * KV_TILE, TILE)


def _mla_queries(q_ref, h, cols):
    q = q_ref[0, h * MLA_QDIM:(h + 1) * MLA_QDIM, cols]
    return jnp.concatenate([q, jnp.zeros((HEAD_PAD - MLA_QDIM, q.shape[1]), q.dtype)], axis=0)


def _mla_finish(o_ref, acc_ref):
    outs = []
    for h in range(MLA_GROUP):
        a = acc_ref[h]
        outs.append(a[0:MLA_DV] / a[MLA_DV:MLA_DV + 1])
    o_ref[0] = jnp.concatenate(outs, axis=0).T.astype(_bf16)


def _mla_kernel(q_ref, k_ref, v_ref, o_ref, acc_ref, knorm_ref):
    i = pl.program_id(2)
    heads = MLA_GROUP
    ones = jnp.ones((HEAD_PAD, LANES), _bf16)

    @pl.when(i == 0)
    def _():
        for h in range(heads):
            kk = k_ref[0, :, h * HEAD_PAD:(h + 1) * HEAD_PAD].astype(_f32)
            hi = (kk * kk).astype(_bf16)
            knorm_ref[h] = jnp.max(_dot(hi, ones), axis=0, keepdims=True)

    def score_bound(h, q):
        qq = q.astype(_f32)
        qn2 = jnp.sum(qq * qq, axis=0, keepdims=True)
        return jnp.sqrt(qn2 * knorm_ref[h][:, 0:1]) * BOUND_SLACK

    def queries(cols):
        q = [_mla_queries(q_ref, h, cols) for h in range(heads)]
        return q, [score_bound(h, q[h]) for h in range(heads)]

    sub = KV_TILE // 2
    q_t, bound = queries(slice(None))
    q_lo, bound_lo = queries(slice(0, KV_TILE))
    q_hi, bound_hi = queries(slice(KV_TILE, Q_TILE))
    q_lo2, bound_lo2 = queries(slice(sub, KV_TILE))
    q_hi2, bound_hi2 = queries(slice(KV_TILE + sub, Q_TILE))

    def key_rows(start, n, h):
        return k_ref[0, pl.ds(pl.multiple_of(start, sub), n), h * HEAD_PAD:(h + 1) * HEAD_PAD]

    def value_cols(start, n, h):
        return v_ref[0, h * V_AUG:(h + 1) * V_AUG, pl.ds(pl.multiple_of(start, sub), n)]

    def keys(j, h):
        return key_rows(_mla_tile_start(j), KV_TILE, h)

    def weighted(j, h, p):
        return _dot(value_cols(_mla_tile_start(j), KV_TILE, h), p)

    def probs(s, b):
        return jnp.exp2(s - b).astype(_bf16)

    r0 = _mla_tile_start(2 * i)
    wide = (lax.broadcasted_iota(jnp.int32, (sub, KV_TILE), 0)
            <= lax.broadcasted_iota(jnp.int32, (sub, KV_TILE), 1))
    square = (lax.broadcasted_iota(jnp.int32, (sub, sub), 0)
              <= lax.broadcasted_iota(jnp.int32, (sub, sub), 1))
    s_meta = [_dot(k_ref[0, PADL:TILE, h * HEAD_PAD:(h + 1) * HEAD_PAD], q_t[h]) for h in range(heads)]
    s_full = [_dot(key_rows(r0, KV_TILE, h), q_hi[h]) for h in range(heads)]
    s_lo_a = [jnp.where(wide, _dot(key_rows(r0, sub, h), q_lo[h]), NEG) for h in range(heads)]
    s_lo_b = [jnp.where(square, _dot(key_rows(r0 + sub, sub, h), q_lo2[h]), NEG) for h in range(heads)]
    s_hi_a = [jnp.where(wide, _dot(key_rows(r0 + KV_TILE, sub, h), q_hi[h]), NEG) for h in range(heads)]
    s_hi_b = [jnp.where(square, _dot(key_rows(r0 + KV_TILE + sub, sub, h), q_hi2[h]), NEG)
              for h in range(heads)]
    for h in range(heads):
        p_meta = jnp.concatenate([jnp.zeros((PADL, Q_TILE), _bf16), probs(s_meta[h], bound[h])], axis=0)
        acc_ref[h] = _dot(v_ref[0, h * V_AUG:(h + 1) * V_AUG, 0:TILE], p_meta)
        acc_ref[h, :, 0:KV_TILE] += _dot(value_cols(r0, sub, h), probs(s_lo_a[h], bound_lo[h]))
        acc_ref[h, :, sub:KV_TILE] += _dot(value_cols(r0 + sub, sub, h), probs(s_lo_b[h], bound_lo2[h]))
        acc_ref[h, :, KV_TILE:Q_TILE] += (
            _dot(value_cols(r0, KV_TILE, h), probs(s_full[h], bound_hi[h]))
            + _dot(value_cols(r0 + KV_TILE, sub, h), probs(s_hi_a[h], bound_hi[h])))
        acc_ref[h, :, KV_TILE + sub:Q_TILE] += _dot(value_cols(r0 + KV_TILE + sub, sub, h),
                                                     probs(s_hi_b[h], bound_hi2[h]))

    def body(jj, c):
        tiles = (2 * jj, 2 * jj + 1)
        s = [[_dot(keys(j, h), q_t[h]) for h in range(heads)] for j in tiles]
        for h in range(heads):
            acc_ref[h] += sum(weighted(j, h, jnp.exp2(s[n][h] - bound[h]).astype(_bf16))
                              for n, j in enumerate(tiles))
        return c

    lax.fori_loop(0, i, body, 0)
    _mla_finish(o_ref, acc_ref)

    l_min = functools.reduce(jnp.minimum,
                             [jnp.min(acc_ref[h, MLA_DV:MLA_DV + 1, :]) for h in range(heads)])

    @pl.when(jnp.logical_not(l_min >= MIN_SOFTMAX_SUM))
    def _():
        _mla_exact(q_ref, k_ref, v_ref, o_ref, acc_ref)


def _mla_exact(q_ref, k_ref, v_ref, o_ref, acc_ref):
    i = pl.program_id(2)
    heads = MLA_GROUP
    q_t = [_mla_queries(q_ref, h, slice(None)) for h in range(heads)]

    ms = []
    for h in range(heads):
        kb = k_ref[0, PADL:TILE, h * HEAD_PAD:(h + 1) * HEAD_PAD]
        s = _dot(kb, q_t[h])
        m0 = jnp.max(s, axis=0, keepdims=True)
        p = jnp.concatenate([jnp.zeros((PADL, Q_TILE), _bf16), jnp.exp2(s - m0).astype(_bf16)], axis=0)
        acc_ref[h] = _dot(v_ref[0, h * V_AUG:(h + 1) * V_AUG, 0:TILE], p)
        ms.append(m0)

    k_row = lax.broadcasted_iota(jnp.int32, (KV_TILE, Q_TILE), 0)
    q_col = lax.broadcasted_iota(jnp.int32, (KV_TILE, Q_TILE), 1)

    def body(j, ms):
        visible = (j - 2 * i) * KV_TILE + k_row <= q_col
        out = []
        for h in range(heads):
            s = _dot(k_ref[0, pl.ds(_mla_tile_start(j), KV_TILE), h * HEAD_PAD:(h + 1) * HEAD_PAD],
                     q_t[h])
            s = jnp.where(visible, s, NEG)
            vb = v_ref[0, h * V_AUG:(h + 1) * V_AUG, pl.ds(_mla_tile_start(j), KV_TILE)]
            m_new = jnp.maximum(ms[h], jnp.max(s, axis=0, keepdims=True))
            alpha = jnp.exp2(ms[h] - m_new)
            acc_ref[h] = alpha * acc_ref[h] + _dot(vb, jnp.exp2(s - m_new).astype(_bf16))
            out.append(m_new)
        return tuple(out)

    lax.fori_loop(0, 2 * i + 2, body, tuple(ms))
    _mla_finish(o_ref, acc_ref)


def _merge_kernel(og_ref, om_ref, ga_ref, gbt_ref, s_ref, wbg_ref, wbm_ref, wo_ref, lng_ref,
                  lnb_ref, rwh_ref, rwl_ref, rb_ref, tril_ref,
                  s2_ref, info_ref, infot_ref, cnt_ref, carry_ref):
    step = pl.program_id(0)

    @pl.when(step == 0)
    def _():
        carry_ref[...] = jnp.zeros_like(carry_ref)

    n_blk = MERGE_TILE // ROUTE_BLOCK
    blocks = [slice(i * ROUTE_BLOCK, (i + 1) * ROUTE_BLOCK) for i in range(n_blk)]
    d = wo_ref.shape[0]
    col_blocks = [slice(c, c + MXU_DIM) for c in range(0, d, MXU_DIM)]
    lane = lax.broadcasted_iota(jnp.int32, (ROUTE_BLOCK, LANES), 1)
    is_g = lane < N_GROUPS
    merged, s2, logits, infos = {}, {}, {}, {}
    carry = [carry_ref[0:1, :]]

    def branches(i):
        rows = blocks[i]
        merged[i] = jnp.concatenate(
            [(ga_ref[rows, cols].astype(_f32) * _dot(og_ref[rows, :], wbg_ref[:, cols])
              + gbt_ref[rows, cols].astype(_f32) * _dot(om_ref[rows, :], wbm_ref[:, cols])
              ).astype(_bf16) for cols in col_blocks], axis=1)

    def residual_norm(i):
        y = ALPHA * s_ref[blocks[i], :] + _dot(merged[i], wo_ref[...])
        s2[i] = _layer_norm(y, lng_ref[...], lnb_ref[...])

    def router_logits(i):
        xh = s2[i].astype(_bf16)
        xl = (s2[i] - xh.astype(_f32)).astype(_bf16)
        logits[i] = (_dot(xh, rwh_ref[...]) + _dot(xl, rwh_ref[...]) + _dot(xh, rwl_ref[...])
                     + rb_ref[...])

    def route(i):
        gl = jnp.where(is_g, logits[i], NEG)
        gmax = jnp.max(gl, axis=-1, keepdims=True)
        gidx = jnp.min(jnp.where(gl == gmax, lane, LANES), axis=-1, keepdims=True)
        p_g = 1.0 / jnp.sum(jnp.where(is_g, jnp.exp(gl - gmax), 0.0), axis=-1, keepdims=True)
        lo = N_GROUPS + EXPERTS_PER_GROUP * gidx
        el = jnp.where((lane >= lo) & (lane < lo + EXPERTS_PER_GROUP), logits[i], NEG)
        v1 = jnp.max(el, axis=-1, keepdims=True)
        i1 = jnp.min(jnp.where(el == v1, lane, LANES), axis=-1, keepdims=True)
        el2 = jnp.where(lane == i1, NEG, el)
        v2 = jnp.max(el2, axis=-1, keepdims=True)
        i2 = jnp.min(jnp.where(el2 == v2, lane, LANES), axis=-1, keepdims=True)
        tt = jnp.exp(v2 - v1)
        p1 = 1.0 / (1.0 + tt)
        p2 = tt / (1.0 + tt)
        e1 = i1 - N_GROUPS
        e2 = i2 - N_GROUPS
        hit1 = lane == e1
        hit2 = lane == e2
        onehot = jnp.where(hit1 | hit2, 1.0, 0.0)
        before = _dot(tril_ref[...], onehot.astype(_bf16)) + carry[0]
        r1 = jnp.sum(jnp.where(hit1, before, 0.0), axis=-1, keepdims=True)
        r2 = jnp.sum(jnp.where(hit2, before, 0.0), axis=-1, keepdims=True)
        carry[0] = carry[0] + jnp.sum(onehot, axis=0, keepdims=True)
        infos[i] = jnp.where(lane == 0, e1.astype(_f32),
                   jnp.where(lane == 1, e2.astype(_f32),
                   jnp.where(lane == 2, p_g * p1,
                   jnp.where(lane == 3, p_g * p2,
                   jnp.where(lane == 4, r1,
                   jnp.where(lane == 5, r2, 0.0))))))

    for stage in (branches, residual_norm, router_logits, route):
        for i in range(n_blk):
            stage(i)
    _tt_store(s2_ref, jnp.concatenate([s2[i] for i in range(n_blk)], axis=0))
    carry_ref[...] = jnp.broadcast_to(carry[0], carry_ref.shape)
    cnt_ref[...] = jnp.broadcast_to(carry[0], cnt_ref.shape)
    info = jnp.concatenate([infos[i] for i in range(n_blk)], axis=0)
    info_ref[...] = info
    infot_ref[...] = info.T[0:SUBLANES]


def _dispatch_kernel(pos_ref, zrow_ref, nused_ref, s2_ref, xs_hbm, zero_ref, sems):
    g = pl.program_id(0)
    zero_sem = sems.at[1]
    row_sem = sems.at[0]

    tile_rows = EXPERT_TILE * SUBLANES

    def zero_copy(row):
        start = pl.multiple_of(row * SUBLANES, tile_rows)
        return pltpu.make_async_copy(zero_ref, xs_hbm.at[pl.ds(start, tile_rows)], zero_sem)

    @pl.when(g == 0)
    def _():
        zero_ref[...] = jnp.zeros_like(zero_ref)

        def start(e, c):
            @pl.when(zrow_ref[e] >= 0)
            def _():
                zero_copy(zrow_ref[e]).start()
            return c

        def wait(e, c):
            @pl.when(zrow_ref[e] >= 0)
            def _():
                zero_copy(0).wait()
            return c

        def start_tail(u, c):
            zero_copy(u * EXPERT_TILE).start()
            return c

        def wait_tail(u, c):
            zero_copy(0).wait()
            return c

        n_tiles = xs_hbm.shape[0] // tile_rows
        lax.fori_loop(0, N_EXPERTS, start, 0)
        lax.fori_loop(nused_ref[0], n_tiles, start_tail, 0)
        lax.fori_loop(0, N_EXPERTS, wait, 0)
        lax.fori_loop(nused_ref[0], n_tiles, wait_tail, 0)

    n_tok = pl.num_programs(0) * DISPATCH_TILE

    def issue(r, c):
        tok = g * DISPATCH_TILE + r
        for k in range(2):
            pltpu.make_async_copy(s2_ref.at[_tt_rows(r)],
                                  xs_hbm.at[_tt_rows(pos_ref[k * n_tok + tok])], row_sem).start(priority=k)
        return c

    lax.fori_loop(0, DISPATCH_TILE, issue, 0, unroll=8)
    for k in range(2):
        pltpu.make_async_copy(s2_ref, xs_hbm.at[pl.ds(0, DISPATCH_TILE * SUBLANES)], row_sem).wait()


def _expert_kernel(te_ref, nused_ref, x_ref, wg_ref, wu_ref, wd_ref, o_ref, wgb_ref, wub_ref, wdb_ref):
    u = pl.program_id(0)
    used = u < nused_ref[0]
    first_of_expert = jnp.logical_or(u == 0, te_ref[u] != te_ref[jnp.maximum(u - 1, 0)])

    @pl.when(jnp.logical_and(used, first_of_expert))
    def _():
        wgb_ref[...] = wg_ref[0].astype(_bf16)
        wub_ref[...] = wu_ref[0].astype(_bf16)
        wdb_ref[...] = wd_ref[0].astype(_bf16)

    @pl.when(used)
    def _():
        x = _tt_load(x_ref, EXPERT_TILE).astype(_bf16)
        a = _dot(x, wgb_ref[...])
        up = _dot(x, wub_ref[...])
        hid = a * _sigmoid(a) * up
        _tt_store(o_ref, _dot(hid.astype(_bf16), wdb_ref[...]))

    @pl.when(u >= nused_ref[0])
    def _():
        o_ref[...] = jnp.zeros_like(o_ref)


def _combine_kernel(pos_ref, s2_ref, info_ref, ys_hbm, lng_ref, lnb_ref, o_ref, buf_ref, sems):
    g = pl.program_id(0)
    last = pl.num_programs(0) - 1
    n_tok = pl.num_programs(0) * COMBINE_TILE

    def start_copies(tile, slot, r):
        tok = tile * COMBINE_TILE + r
        for k in range(2):
            pltpu.make_async_copy(ys_hbm.at[_tt_rows(pos_ref[k * n_tok + tok])],
                                  buf_ref.at[slot, k, _tt_rows(r)], sems.at[slot]).start(priority=k)

    def wait_slot(slot):
        for k in range(2):
            pltpu.make_async_copy(ys_hbm.at[pl.ds(0, COMBINE_TILE * SUBLANES)], buf_ref.at[slot, k],
                                  sems.at[slot]).wait()

    @pl.when(g == 0)
    def _():
        def body(r, c):
            for tile in range(COMBINE_SLOTS - 1):
                start_copies(tile, tile, r)
            return c

        lax.fori_loop(0, COMBINE_TILE, body, 0, unroll=8)

    slot = g % COMBINE_SLOTS
    wait_slot(slot)
    info = info_ref[...]
    y = (ALPHA * _tt_load(s2_ref, COMBINE_TILE)
         + info[:, 2:3] * _tt_load(buf_ref, COMBINE_TILE, (slot, 0))
         + info[:, 3:4] * _tt_load(buf_ref, COMBINE_TILE, (slot, 1)))
    o_ref[...] = _layer_norm(y, lng_ref[...], lnb_ref[...])

    ahead = COMBINE_SLOTS - 1
    ahead_slot = (g + ahead) % COMBINE_SLOTS
    for r in range(COMBINE_TILE):
        start_copies(jnp.minimum(g + ahead, last), ahead_slot, r)

    @pl.when(g == last)
    def _():
        for n in range(1, COMBINE_SLOTS):
            wait_slot((g + n) % COMBINE_SLOTS)


def _const_spec(shape):
    nd = len(shape)
    return pl.BlockSpec(shape, lambda *_: (0,) * nd)


def _rope_tables(lp):
    pos = np.maximum(np.arange(lp, dtype=np.float64) - PADL, 0.0)
    inv_freq = ROPE_BASE ** (-np.arange(0, MLA_ROPE, 2, dtype=np.float64) / MLA_ROPE)
    ang = pos[:, None] * inv_freq[None, :]
    cos, sin = np.cos(ang), np.sin(ang)
    half = MLA_ROPE // 2
    ones = np.ones((lp, MLA_NOPE))
    zeros_n = np.zeros((lp, MLA_NOPE))
    zeros_h = np.zeros((lp, half))
    tail1 = np.ones((lp, LANES - MLA_QDIM))
    tail0 = np.zeros((lp, LANES - MLA_QDIM))
    cos_t = np.concatenate([ones, cos, cos, tail1], axis=1)
    sin_lo = np.concatenate([zeros_n, -sin, zeros_h, tail0], axis=1)
    sin_hi = np.concatenate([zeros_n, zeros_h, sin, tail0], axis=1)
    return tuple(jnp.asarray(np.ascontiguousarray(t), dtype=_f32)
                 for t in (cos_t, sin_lo, sin_hi, cos.T, sin.T))


def _pad_heads(w, width):
    k = w.shape[0]
    w = w.reshape(k, MLA_HEADS, width)
    w = jnp.pad(w, ((0, 0), (0, 0), (0, HEAD_PAD - width)))
    return w.reshape(k, MLA_HEADS * HEAD_PAD)


def kernel(x, meta_tokens, ln_emb_g, ln_emb_b, w_in, gla_gate_w2, gla_gate_b, gla_norm_g, mla_q_norm_g, mla_w_uq, mla_kv_norm_g, mla_w_uk, mla_w_uv, w_branch_gla, w_branch_mla, w_out, ln_mix_g, ln_mix_b, router_group_w, router_group_b, router_expert_w, router_expert_b, expert_w_gate, expert_w_up, expert_w_down, ln_ffn_g, ln_ffn_b):
    bsz, seq, d = x.shape
    assert d == D_MODEL and seq % Q_TILE == 0 and w_in.shape[0] == DEPTH == 1
    lp = PADL + N_META + seq
    nt = lp // TILE
    ntok = bsz * seq
    row2 = lambda v: v.reshape(1, -1).astype(_f32)

    head_tile = jnp.concatenate([jnp.zeros((PADL, d), _f32), meta_tokens.astype(_f32)], axis=0)
    wi = w_in[0]
    o_a = 2 * GLA_QK + 2 * GLA_VW
    o_cq = o_a + GLA_GATE_RANK
    o_ckv = o_cq + MLA_Q_RANK
    o_kr = o_ckv + MLA_KV_RANK
    o_ga = o_kr + MLA_ROPE
    w_a = jnp.pad(wi[:, o_a:o_cq], ((0, 0), (0, LANES - GLA_GATE_RANK)))
    w_kr = jnp.pad(wi[:, o_kr:o_ga], ((0, 0), (MLA_NOPE, LANES - MLA_QDIM)))
    w_all = jnp.concatenate([wi[:, :o_a], w_a, wi[:, o_cq:o_kr], w_kr, wi[:, o_ga:]], axis=1).astype(_bf16)
    assert w_all.shape == (d, _W_COLS)
    w2p = jnp.pad(gla_gate_w2[0], ((0, LANES - GLA_GATE_RANK), (0, 0))).astype(_bf16)
    wuqt = mla_w_uq[0].T.astype(_bf16)
    wuk = _pad_heads(mla_w_uk[0], MLA_NOPE).astype(_bf16)
    wuvt = mla_w_uv[0].T.astype(_bf16)
    cos_t, sin_lo, sin_hi, cos_tr, sin_tr = _rope_tables(lp)
    blk = np.arange(TILE)
    tril_chunks = jnp.asarray(
        ((blk[:, None] >= blk[None, :]) & (blk[:, None] // GLA_CHUNK == blk[None, :] // GLA_CHUNK)),
        dtype=_bf16)

    pad_map = lambda b, t: (b, t, 0)
    real_map = lambda b, t: (b, jnp.maximum(t - 1, 0), 0)
    real_map_t = lambda b, t: (b, 0, jnp.maximum(t - 1, 0))
    out_shapes = (
        jax.ShapeDtypeStruct((bsz, seq, d), _f32),
        jax.ShapeDtypeStruct((bsz, lp, GLA_QK), _bf16),
        jax.ShapeDtypeStruct((bsz, lp, GLA_QK), _bf16),
        jax.ShapeDtypeStruct((bsz, lp, GLA_QK), _bf16),
        jax.ShapeDtypeStruct((bsz, lp, GLA_VW), _bf16),
        jax.ShapeDtypeStruct((bsz * nt, TILE // GLA_CHUNK, GLA_QK), _f32),
        jax.ShapeDtypeStruct((bsz, seq, GLA_VW), _bf16),
        jax.ShapeDtypeStruct((bsz, MLA_HEADS * MLA_QDIM, seq), _bf16),
        jax.ShapeDtypeStruct((bsz, lp, MLA_HEADS * HEAD_PAD), _bf16),
        jax.ShapeDtypeStruct((bsz, MLA_HEADS * V_AUG, lp), _bf16),
        jax.ShapeDtypeStruct((bsz, seq, d), _bf16),
        jax.ShapeDtypeStruct((bsz, seq, d), _bf16),
    )
    out_specs = (
        pl.BlockSpec((1, TILE, d), real_map),
        pl.BlockSpec((1, TILE, GLA_QK), pad_map),
        pl.BlockSpec((1, TILE, GLA_QK), pad_map),
        pl.BlockSpec((1, TILE, GLA_QK), pad_map),
        pl.BlockSpec((1, TILE, GLA_VW), pad_map),
        pl.BlockSpec((1, TILE // GLA_CHUNK, GLA_QK), lambda b, t: (b * nt + t, 0, 0)),
        pl.BlockSpec((1, TILE, GLA_VW), real_map),
        pl.BlockSpec((1, MLA_HEADS * MLA_QDIM, TILE), real_map_t),
        pl.BlockSpec((1, TILE, MLA_HEADS * HEAD_PAD), pad_map),
        pl.BlockSpec((1, MLA_HEADS * V_AUG, TILE), lambda b, t: (b, 0, t)),
        pl.BlockSpec((1, TILE, d), real_map),
        pl.BlockSpec((1, TILE, d), real_map),
    )
    tab_spec = pl.BlockSpec((TILE, LANES), lambda b, t: (t, 0))
    tabt_spec = pl.BlockSpec((MLA_ROPE // 2, TILE), lambda b, t: (0, t))
    (s_emb, qt, kt, ke, gv, dec, sr, qm, km, vm, gate_a, gate_b) = pl.pallas_call(
        _inproj_kernel,
        grid=(bsz, nt),
        in_specs=[
            pl.BlockSpec((1, TILE, d), real_map),
            _const_spec((TILE, d)),
            _const_spec((1, d)), _const_spec((1, d)),
            _const_spec((d, _W_COLS)),
            _const_spec((LANES, GLA_QK)), _const_spec((1, GLA_QK)),
            _const_spec((1, MLA_Q_RANK)), _const_spec((MLA_HEADS * MLA_QDIM, MLA_Q_RANK)),
            _const_spec((1, MLA_KV_RANK)), _const_spec((MLA_KV_RANK, MLA_HEADS * HEAD_PAD)),
            _const_spec((MLA_HEADS * MLA_DV, MLA_KV_RANK)),
            tab_spec, tab_spec, tab_spec, tabt_spec, tabt_spec,
            _const_spec((TILE, TILE)),
        ],
        out_specs=out_specs,
        out_shape=out_shapes,
        compiler_params=pltpu.CompilerParams(
            dimension_semantics=("arbitrary", "arbitrary"), vmem_limit_bytes=VMEM_LIMIT),
        name="inproj",
    )(x, head_tile, row2(ln_emb_g), row2(ln_emb_b), w_all, w2p, row2(gla_gate_b[0]),
      row2(mla_q_norm_g[0]), wuqt, row2(mla_kv_norm_g[0]), wuk, wuvt, cos_t, sin_lo, sin_hi,
      cos_tr, sin_tr, tril_chunks)

    o_gla = pl.pallas_call(
        _gla_kernel,
        grid=(nt,),
        in_specs=[
            pl.BlockSpec((bsz, TILE, GLA_QK), lambda t: (0, t, 0)),
            pl.BlockSpec((bsz, TILE, GLA_QK), lambda t: (0, t, 0)),
            pl.BlockSpec((bsz, TILE, GLA_QK), lambda t: (0, t, 0)),
            pl.BlockSpec((bsz, TILE, GLA_VW), lambda t: (0, t, 0)),
            pl.BlockSpec((bsz, 1, TILE // GLA_CHUNK, GLA_QK), lambda t: (0, t, 0, 0)),
            pl.BlockSpec((bsz, TILE, GLA_VW), lambda t: (0, jnp.maximum(t - 1, 0), 0)),
            _const_spec((1, GLA_DV)),
        ],
        out_specs=pl.BlockSpec((bsz, TILE, GLA_VW), lambda t: (0, jnp.maximum(t - 1, 0), 0)),
        out_shape=jax.ShapeDtypeStruct((bsz, seq, GLA_VW), _bf16),
        scratch_shapes=[pltpu.VMEM((bsz, GLA_HEADS, GLA_DV, GLA_DK), _f32)],
        compiler_params=pltpu.CompilerParams(
            dimension_semantics=("arbitrary",), vmem_limit_bytes=VMEM_LIMIT),
        name="gla",
    )(qt, kt, ke, gv, dec.reshape(bsz, nt, TILE // GLA_CHUNK, GLA_QK), sr, row2(gla_norm_g[0]))

    group_width = MLA_GROUP * HEAD_PAD
    o_mla = pl.pallas_call(
        _mla_kernel,
        grid=(bsz, MLA_HEADS // MLA_GROUP, seq // Q_TILE),
        in_specs=[
            pl.BlockSpec((1, MLA_GROUP * MLA_QDIM, Q_TILE), lambda b, hp, i: (b, hp, i)),
            pl.BlockSpec((1, lp, group_width), lambda b, hp, i: (b, 0, hp)),
            pl.BlockSpec((1, MLA_GROUP * V_AUG, lp), lambda b, hp, i: (b, hp, 0)),
        ],
        out_specs=pl.BlockSpec((1, Q_TILE, MLA_GROUP * MLA_DV), lambda b, hp, i: (b, i, hp)),
        out_shape=jax.ShapeDtypeStruct((bsz, seq, MLA_HEADS * MLA_DV), _bf16),
        scratch_shapes=[pltpu.VMEM((MLA_GROUP, V_AUG, Q_TILE), _f32),
                        pltpu.VMEM((MLA_GROUP, 1, LANES), _f32)],
        compiler_params=pltpu.CompilerParams(
            dimension_semantics=("arbitrary", "arbitrary", "arbitrary"),
            vmem_limit_bytes=VMEM_LIMIT),
        name="mla",
    )(qm, km, vm)

    rw = jnp.concatenate([router_group_w[0], router_expert_w[0]], axis=1)
    rw = jnp.pad(rw, ((0, 0), (0, LANES - rw.shape[1])))
    rwh = rw.astype(_bf16)
    rwl = (rw - rwh.astype(_f32)).astype(_bf16)
    rb = jnp.concatenate([router_group_b[0], router_expert_b[0]])
    rb = jnp.pad(rb, (0, LANES - rb.shape[0])).reshape(1, LANES)
    mi = np.arange(ROUTE_BLOCK)
    tril_strict = jnp.asarray(mi[:, None] > mi[None, :], dtype=_bf16)
    flat = lambda a: a.reshape(ntok, a.shape[-1])
    tok_spec = lambda w: pl.BlockSpec((MERGE_TILE, w), lambda g: (g, 0))
    tt_spec = lambda n, index_map: pl.BlockSpec((n * SUBLANES, LANES), index_map)
    s2, info, info_t, cnt = pl.pallas_call(
        _merge_kernel,
        grid=(ntok // MERGE_TILE,),
        in_specs=[tok_spec(d), tok_spec(d), tok_spec(d), tok_spec(d), tok_spec(d),
                  _const_spec((d, d)), _const_spec((d, d)), _const_spec((d, d)),
                  _const_spec((1, d)), _const_spec((1, d)),
                  _const_spec((d, LANES)), _const_spec((d, LANES)), _const_spec((1, LANES)),
                  _const_spec((ROUTE_BLOCK, ROUTE_BLOCK))],
        out_specs=(tt_spec(MERGE_TILE, lambda g: (g, 0)), tok_spec(LANES),
                   pl.BlockSpec((SUBLANES, MERGE_TILE), lambda g: (0, g)), _const_spec((8, LANES))),
        out_shape=(jax.ShapeDtypeStruct((ntok * SUBLANES, LANES), _f32),
                   jax.ShapeDtypeStruct((ntok, LANES), _f32),
                   jax.ShapeDtypeStruct((SUBLANES, ntok), _f32),
                   jax.ShapeDtypeStruct((8, LANES), _f32)),
        scratch_shapes=[pltpu.VMEM((8, LANES), _f32)],
        compiler_params=pltpu.CompilerParams(
            dimension_semantics=("arbitrary",), vmem_limit_bytes=VMEM_LIMIT),
        name="merge_router",
    )(flat(o_gla), flat(o_mla), flat(gate_a), flat(gate_b), flat(s_emb),
      w_branch_gla[0].astype(_bf16), w_branch_mla[0].astype(_bf16), w_out[0].astype(_bf16),
      row2(ln_mix_g[0]), row2(ln_mix_b[0]), rwh, rwl, rb, tril_strict)

    n_tiles = (2 * ntok + N_EXPERTS * (EXPERT_TILE - 1)) // EXPERT_TILE
    n_rows = n_tiles * EXPERT_TILE
    e_idx = info_t[0:2].astype(jnp.int32)
    rank = info_t[4:6].astype(jnp.int32)
    counts = cnt[0, :N_EXPERTS].astype(jnp.int32)
    padded = ((counts + EXPERT_TILE - 1) // EXPERT_TILE) * EXPERT_TILE
    ends = jnp.cumsum(padded)
    starts = ends - padded
    expert_ids = jnp.arange(N_EXPERTS, dtype=jnp.int32)[:, None, None]
    start_of = jnp.sum(jnp.where(e_idx[None] == expert_ids, starts[:, None, None], 0), axis=0)
    pos = (start_of + rank).reshape(-1)
    tile_start = jnp.arange(n_tiles, dtype=jnp.int32) * EXPERT_TILE
    tile_expert = jnp.minimum(
        jnp.sum((ends[None, :] <= tile_start[:, None]).astype(jnp.int32), axis=1), N_EXPERTS - 1)
    n_used = (ends[-1:] // EXPERT_TILE).astype(jnp.int32)
    zero_row = jnp.where(padded > 0, ends - EXPERT_TILE, -1).astype(jnp.int32)

    any_spec = pl.BlockSpec(memory_space=pl.ANY)
    xs = pl.pallas_call(
        _dispatch_kernel,
        grid_spec=pltpu.PrefetchScalarGridSpec(
            num_scalar_prefetch=3,
            grid=(ntok // DISPATCH_TILE,),
            in_specs=[tt_spec(DISPATCH_TILE, lambda g, p, z, nu: (g, 0))],
            out_specs=any_spec,
            scratch_shapes=[pltpu.VMEM((EXPERT_TILE * SUBLANES, LANES), _f32),
                            pltpu.SemaphoreType.DMA((2,))],
        ),
        out_shape=jax.ShapeDtypeStruct((n_rows * SUBLANES, LANES), _f32),
        compiler_params=pltpu.CompilerParams(
            dimension_semantics=("arbitrary",), vmem_limit_bytes=VMEM_LIMIT, has_side_effects=True),
        name="dispatch",
    )(pos, zero_row, n_used, s2)

    ff = EXPERT_FF
    wg = expert_w_gate[0].reshape(N_EXPERTS, d, ff)
    wu = expert_w_up[0].reshape(N_EXPERTS, d, ff)
    wd = expert_w_down[0].reshape(N_EXPERTS, ff, d)
    ys = pl.pallas_call(
        _expert_kernel,
        grid_spec=pltpu.PrefetchScalarGridSpec(
            num_scalar_prefetch=2,
            grid=(n_tiles,),
            in_specs=[
                tt_spec(EXPERT_TILE, lambda u, te, nu: (jnp.minimum(u, nu[0] - 1), 0)),
                pl.BlockSpec((1, d, ff), lambda u, te, nu: (te[u], 0, 0)),
                pl.BlockSpec((1, d, ff), lambda u, te, nu: (te[u], 0, 0)),
                pl.BlockSpec((1, ff, d), lambda u, te, nu: (te[u], 0, 0)),
            ],
            out_specs=tt_spec(EXPERT_TILE, lambda u, te, nu: (u, 0)),
            scratch_shapes=[pltpu.VMEM((d, ff), _bf16), pltpu.VMEM((d, ff), _bf16),
                            pltpu.VMEM((ff, d), _bf16)],
        ),
        out_shape=jax.ShapeDtypeStruct((n_rows * SUBLANES, LANES), _f32),
        compiler_params=pltpu.CompilerParams(
            dimension_semantics=("arbitrary",), vmem_limit_bytes=VMEM_LIMIT),
        name="experts",
    )(tile_expert, n_used, xs, wg, wu, wd)

    out = pl.pallas_call(
        _combine_kernel,
        grid_spec=pltpu.PrefetchScalarGridSpec(
            num_scalar_prefetch=1,
            grid=(ntok // COMBINE_TILE,),
            in_specs=[
                tt_spec(COMBINE_TILE, lambda g, p: (g, 0)),
                pl.BlockSpec((COMBINE_TILE, LANES), lambda g, p: (g, 0)),
                any_spec,
                pl.BlockSpec((1, d), lambda g, p: (0, 0)),
                pl.BlockSpec((1, d), lambda g, p: (0, 0)),
            ],
            out_specs=pl.BlockSpec((COMBINE_TILE, d), lambda g, p: (g, 0)),
            scratch_shapes=[pltpu.VMEM((COMBINE_SLOTS, 2, COMBINE_TILE * SUBLANES, LANES), _f32),
                            pltpu.SemaphoreType.DMA((COMBINE_SLOTS,))],
        ),
        out_shape=jax.ShapeDtypeStruct((ntok, d), _f32),
        compiler_params=pltpu.CompilerParams(
            dimension_semantics=("arbitrary",), vmem_limit_bytes=VMEM_LIMIT),
        name="combine_ln",
    )(pos, s2, info, ys, row2(ln_ffn_g[0]), row2(ln_ffn_b[0]))
    return out.reshape(bsz, seq, d)
```

```python
import functools

import jax
import jax.numpy as jnp
import numpy as np
from jax import lax
from jax.experimental import pallas as pl
from jax.experimental.pallas import tpu as pltpu

D_MODEL = 1024
N_META = 16
GLA_HEADS = 4
GLA_DK = 128
GLA_DV = 256
GLA_QK = GLA_HEADS * GLA_DK
GLA_VW = GLA_HEADS * GLA_DV
GLA_GATE_RANK = 16
GLA_GATE_TAU = 16.0
GLA_CHUNK = 64
MLA_HEADS = 16
MLA_Q_RANK = 384
MLA_KV_RANK = 256
MLA_NOPE = 64
MLA_ROPE = 32
MLA_DV = 64
MLA_QDIM = MLA_NOPE + MLA_ROPE
ROPE_BASE = 10000.0
N_GROUPS = 4
EXPERTS_PER_GROUP = 8
N_EXPERTS = N_GROUPS * EXPERTS_PER_GROUP
EXPERT_FF = 256
DEPTH = 1
ALPHA = (2.0 * DEPTH) ** 0.25
LN_EPS = 1e-5
RMS_EPS = 1e-6

LANES = 128
SUBLANES = 8
MXU_DIM = 256
TILE = 256
PADL = TILE - N_META
HEAD_PAD = LANES
BF16_ROWS = 16
KV_TILE = 512
Q_TILE = 2 * KV_TILE
V_AUG = MLA_DV + BF16_ROWS
MERGE_TILE = 1024
ROUTE_BLOCK = 256
EXPERT_TILE = 512
ZERO_BLOCK = 256
DISPATCH_TILE = 2048
COMBINE_TILE = 256
COMBINE_SLOTS = 3
NEG = -1e30
LOG2E = 1.4426950408889634
BOUND_SLACK = 1.02
MIN_SOFTMAX_SUM = 2.0 ** -100
VMEM_LIMIT = 56 * 1024 * 1024

_C_Q, _C_K, _C_V, _C_R = 0, 512, 1024, 2048
_C_A = 3072
_C_CQ = _C_A + LANES
_C_CKV = _C_CQ + MLA_Q_RANK
_C_KR = _C_CKV + MLA_KV_RANK
_C_GA = _C_KR + LANES
_C_GB = _C_GA + D_MODEL
_W_COLS = _C_GB + D_MODEL

_f32 = jnp.float32
_bf16 = jnp.bfloat16


def _dot(a, b):
    return jnp.dot(a, b, preferred_element_type=_f32)


def _dot_nt(a, b):
    return lax.dot_general(a, b, (((1,), (1,)), ((), ())), preferred_element_type=_f32)


def _dot_tn(a, b):
    return lax.dot_general(a, b, (((0,), (0,)), ((), ())), preferred_element_type=_f32)


def _layer_norm(x, g, b):
    mu = jnp.mean(x, axis=-1, keepdims=True)
    xc = x - mu
    var = jnp.mean(xc * xc, axis=-1, keepdims=True)
    return xc * lax.rsqrt(var + LN_EPS) * g + b


def _rms_norm(x, g):
    ms = jnp.mean(x * x, axis=-1, keepdims=True)
    return x * lax.rsqrt(ms + RMS_EPS) * g


def _sigmoid(x):
    return 1.0 / (1.0 + jnp.exp(-x))


def _tt_load(ref, n, lead=()):
    return jnp.concatenate(
        [ref[lead + (pl.ds(a, n, stride=SUBLANES), slice(None))] for a in range(SUBLANES)], axis=1)


def _tt_store(ref, x):
    n = x.shape[0]
    for a in range(SUBLANES):
        ref[pl.ds(a, n, stride=SUBLANES), :] = x[:, a * LANES:(a + 1) * LANES]


def _tt_rows(tok):
    return pl.ds(pl.multiple_of(tok * SUBLANES, SUBLANES), SUBLANES)


def _rope(x, cos, sin_lo, sin_hi):
    half = MLA_ROPE // 2
    from_hi = pltpu.roll(x, LANES - half, 1)
    from_lo = pltpu.roll(x, half, 1)
    return x * cos + from_hi * sin_lo + from_lo * sin_hi


def _inproj_kernel(x_ref, head_ref, lng_ref, lnb_ref, w_ref, w2_ref, gb_ref, qg_ref, wuqt_ref, kvg_ref,
                   wuk_ref, wuvt_ref, cos_ref, sl_ref, sh_ref, cost_ref, sint_ref, tril_ref,
                   s_ref, qt_ref, kt_ref, ke_ref, gv_ref, dec_ref, sr_ref, qm_ref, km_ref,
                   vm_ref, ga_ref, gbt_ref):
    t = pl.program_id(1)
    x_in = jnp.where(t == 0, head_ref[...], x_ref[0])
    sn = _layer_norm(x_in, lng_ref[...], lnb_ref[...])
    s_ref[0] = sn
    snb = sn.astype(_bf16)
    row = t * TILE + lax.broadcasted_iota(jnp.int32, (TILE, 1), 0)
    valid = row >= PADL

    proj = lambda c0, width: _dot(snb, w_ref[:, c0:c0 + width])
    a_lr = proj(_C_A, LANES)
    cq = proj(_C_CQ, MLA_Q_RANK)
    ckv = proj(_C_CKV, MLA_KV_RANK)
    kr_raw = proj(_C_KR, LANES)
    gq = proj(_C_Q, GLA_QK)
    gk = proj(_C_K, GLA_QK)
    gv_ref[0] = jnp.where(valid, proj(_C_V, GLA_VW), 0.0).astype(_bf16)
    r = proj(_C_R, GLA_VW)
    sr_ref[0] = (r * _sigmoid(r)).astype(_bf16)
    ga_ref[0] = _sigmoid(proj(_C_GA, D_MODEL)).astype(_bf16)
    gbt_ref[0] = _sigmoid(proj(_C_GB, D_MODEL)).astype(_bf16)

    z = _dot(a_lr.astype(_bf16), w2_ref[...]) + gb_ref[...]
    cqn = _rms_norm(cq, qg_ref[...]).astype(_bf16)
    ckvn = _rms_norm(ckv, kvg_ref[...]).astype(_bf16)
    qft = _dot_nt(wuqt_ref[...], cqn)
    kf = _dot(ckvn, wuk_ref[...])
    vt = _dot_nt(wuvt_ref[...], ckvn)

    la = (jnp.minimum(z, 0.0) - jnp.log1p(jnp.exp(-jnp.abs(z)))) * (1.0 / GLA_GATE_TAU)
    la = jnp.where(valid, la, 0.0)
    hi = la.astype(_bf16)
    r1 = la - hi.astype(_f32)
    mid = r1.astype(_bf16)
    lo = (r1 - mid.astype(_f32)).astype(_bf16)
    tril = tril_ref[...]
    bc = _dot(tril, hi) + _dot(tril, mid) + _dot(tril, lo)
    n_chunks = TILE // GLA_CHUNK
    lasts = [bc[c * GLA_CHUNK + GLA_CHUNK - 1:(c + 1) * GLA_CHUNK, :] for c in range(n_chunks)]
    for c in range(n_chunks):
        dec_ref[0, c:c + 1, :] = jnp.exp(lasts[c])
    b_last = jnp.concatenate(
        [jnp.broadcast_to(l, (GLA_CHUNK, GLA_QK)) for l in lasts], axis=0)
    gk = jnp.where(valid, gk, 0.0)
    qt_ref[0] = (gq * (GLA_DK ** -0.5) * jnp.exp(bc)).astype(_bf16)
    kt_ref[0] = (gk * jnp.exp(-bc)).astype(_bf16)
    ke_ref[0] = (gk * jnp.exp(b_last - bc)).astype(_bf16)

    cos = cos_ref[...]
    sl = sl_ref[...]
    sh = sh_ref[...]
    scale = (MLA_QDIM ** -0.5) * LOG2E
    cost = cost_ref[...]
    sint = sint_ref[...]
    half = MLA_ROPE // 2
    for h in range(MLA_HEADS):
        base = h * MLA_QDIM
        x1 = qft[base + MLA_NOPE:base + MLA_NOPE + half]
        x2 = qft[base + MLA_NOPE + half:base + MLA_QDIM]
        qm_ref[0, base:base + MLA_NOPE] = (qft[base:base + MLA_NOPE] * scale).astype(_bf16)
        qm_ref[0, base + MLA_NOPE:base + MLA_NOPE + half] = ((x1 * cost - x2 * sint) * scale).astype(_bf16)
        qm_ref[0, base + MLA_NOPE + half:base + MLA_QDIM] = ((x1 * sint + x2 * cost) * scale).astype(_bf16)
    kr = _rope(kr_raw, cos, sl, sh)
    for h in range(MLA_HEADS):
        km_ref[0, :, h * HEAD_PAD:(h + 1) * HEAD_PAD] = (
            kf[:, h * HEAD_PAD:(h + 1) * HEAD_PAD] + kr).astype(_bf16)
    for h in range(MLA_HEADS):
        vm_ref[0, h * V_AUG:h * V_AUG + MLA_DV] = vt[h * MLA_DV:(h + 1) * MLA_DV].astype(_bf16)
        vm_ref[0, h * V_AUG + MLA_DV:(h + 1) * V_AUG] = jnp.ones((V_AUG - MLA_DV, TILE), _bf16)


def _gla_kernel(qt_ref, kt_ref, ke_ref, gv_ref, dec_ref, sr_ref, ng_ref, o_ref, st_ref):
    t = pl.program_id(0)
    n_batch = qt_ref.shape[0]

    @pl.when(t == 0)
    def _():
        st_ref[...] = jnp.zeros_like(st_ref)

    ri = lax.broadcasted_iota(jnp.int32, (TILE, TILE), 0)
    ci = lax.broadcasted_iota(jnp.int32, (TILE, TILE), 1)
    visible = (ri >= ci) & (ri // GLA_CHUNK == ci // GLA_CHUNK)
    ng = ng_ref[...]
    n_chunks = TILE // GLA_CHUNK
    chunk_rows = [slice(c * GLA_CHUNK, (c + 1) * GLA_CHUNK) for c in range(n_chunks)]
    k_cols = [slice(h * GLA_DK, (h + 1) * GLA_DK) for h in range(GLA_HEADS)]
    v_cols = [slice(h * GLA_DV, (h + 1) * GLA_DV) for h in range(GLA_HEADS)]
    seqs = [(b, h) for b in range(n_batch) for h in range(GLA_HEADS)]
    att = {(b, h): jnp.where(visible, _dot_nt(qt_ref[b, :, k_cols[h]], kt_ref[b, :, k_cols[h]]), 0.0)
           for b, h in seqs}
    upd = {(b, h): [_dot_tn(gv_ref[b, rows, v_cols[h]], ke_ref[b, rows, k_cols[h]])
                    for rows in chunk_rows] for b, h in seqs}
    o_intra = {(b, h): _dot(att[b, h].astype(_bf16), gv_ref[b, :, v_cols[h]]) for b, h in seqs}
    for b, h in seqs:
        st = st_ref[b, h]
        for c, rows in enumerate(chunk_rows):
            o = o_intra[b, h][rows] + _dot_nt(qt_ref[b, rows, k_cols[h]], st.astype(_bf16))
            st = st * dec_ref[b, 0, c:c + 1, k_cols[h]] + upd[b, h][c]
            o = _rms_norm(o, ng) * sr_ref[b, rows, v_cols[h]].astype(_f32)
            o_ref[b, rows, v_cols[h]] = o.astype(_bf16)
        st_ref[b, h] = st


MLA_GROUP = 4


def _mla_tile_start(j):
    return pl.multiple_of(TILE + j * KV_TILE, TILE)


def _mla_queries(q_ref, h, cols):
    q = q_ref[0, h * MLA_QDIM:(h + 1) * MLA_QDIM, cols]
    return jnp.concatenate([q, jnp.zeros((HEAD_PAD - MLA_QDIM, q.shape[1]), q.dtype)], axis=0)


def _mla_finish(o_ref, acc_ref):
    outs = []
    for h in range(MLA_GROUP):
        a = acc_ref[h]
        outs.append(a[0:MLA_DV] / a[MLA_DV:MLA_DV + 1])
    o_ref[0] = jnp.concatenate(outs, axis=0).T.astype(_bf16)


def _mla_kernel(q_ref, k_ref, v_ref, o_ref, acc_ref, knorm_ref):
    i = pl.program_id(2)
    heads = MLA_GROUP
    ones = jnp.ones((HEAD_PAD, LANES), _bf16)

    @pl.when(i == 0)
    def _():
        for h in range(heads):
            kk = k_ref[0, :, h * HEAD_PAD:(h + 1) * HEAD_PAD].astype(_f32)
            hi = (kk * kk).astype(_bf16)
            knorm_ref[h] = jnp.max(_dot(hi, ones), axis=0, keepdims=True)

    def score_bound(h, q):
        qq = q.astype(_f32)
        qn2 = jnp.sum(qq * qq, axis=0, keepdims=True)
        return jnp.sqrt(qn2 * knorm_ref[h][:, 0:1]) * BOUND_SLACK

    def queries(cols):
        q = [_mla_queries(q_ref, h, cols) for h in range(heads)]
        return q, [score_bound(h, q[h]) for h in range(heads)]

    sub = KV_TILE // 2
    q_t, bound = queries(slice(None))
    q_lo, bound_lo = queries(slice(0, KV_TILE))
    q_hi, bound_hi = queries(slice(KV_TILE, Q_TILE))
    q_lo2, bound_lo2 = queries(slice(sub, KV_TILE))
    q_hi2, bound_hi2 = queries(slice(KV_TILE + sub, Q_TILE))

    def key_rows(start, n, h):
        return k_ref[0, pl.ds(pl.multiple_of(start, sub), n), h * HEAD_PAD:(h + 1) * HEAD_PAD]

    def value_cols(start, n, h):
        return v_ref[0, h * V_AUG:(h + 1) * V_AUG, pl.ds(pl.multiple_of(start, sub), n)]

    def keys(j, h):
        return key_rows(_mla_tile_start(j), KV_TILE, h)

    def weighted(j, h, p):
        return _dot(value_cols(_mla_tile_start(j), KV_TILE, h), p)

    def probs(s, b):
        return jnp.exp2(s - b).astype(_bf16)

    r0 = _mla_tile_start(2 * i)
    wide = (lax.broadcasted_iota(jnp.int32, (sub, KV_TILE), 0)
            <= lax.broadcasted_iota(jnp.int32, (sub, KV_TILE), 1))
    square = (lax.broadcasted_iota(jnp.int32, (sub, sub), 0)
              <= lax.broadcasted_iota(jnp.int32, (sub, sub), 1))
    s_meta = [_dot(k_ref[0, PADL:TILE, h * HEAD_PAD:(h + 1) * HEAD_PAD], q_t[h]) for h in range(heads)]
    s_full = [_dot(key_rows(r0, KV_TILE, h), q_hi[h]) for h in range(heads)]
    s_lo_a = [jnp.where(wide, _dot(key_rows(r0, sub, h), q_lo[h]), NEG) for h in range(heads)]
    s_lo_b = [jnp.where(square, _dot(key_rows(r0 + sub, sub, h), q_lo2[h]), NEG) for h in range(heads)]
    s_hi_a = [jnp.where(wide, _dot(key_rows(r0 + KV_TILE, sub, h), q_hi[h]), NEG) for h in range(heads)]
    s_hi_b = [jnp.where(square, _dot(key_rows(r0 + KV_TILE + sub, sub, h), q_hi2[h]), NEG)
              for h in range(heads)]
    for h in range(heads):
        p_meta = jnp.concatenate([jnp.zeros((PADL, Q_TILE), _bf16), probs(s_meta[h], bound[h])], axis=0)
        acc_ref[h] = _dot(v_ref[0, h * V_AUG:(h + 1) * V_AUG, 0:TILE], p_meta)
        acc_ref[h, :, 0:KV_TILE] += _dot(value_cols(r0, sub, h), probs(s_lo_a[h], bound_lo[h]))
        acc_ref[h, :, sub:KV_TILE] += _dot(value_cols(r0 + sub, sub, h), probs(s_lo_b[h], bound_lo2[h]))
        acc_ref[h, :, KV_TILE:Q_TILE] += (
            _dot(value_cols(r0, KV_TILE, h), probs(s_full[h], bound_hi[h]))
            + _dot(value_cols(r0 + KV_TILE, sub, h), probs(s_hi_a[h], bound_hi[h])))
        acc_ref[h, :, KV_TILE + sub:Q_TILE] += _dot(value_cols(r0 + KV_TILE + sub, sub, h),
                                                     probs(s_hi_b[h], bound_hi2[h]))

    def body(jj, c):
        tiles = (2 * jj, 2 * jj + 1)
        s = [[_dot(keys(j, h), q_t[h]) for h in range(heads)] for j in tiles]
        for h in range(heads):
            acc_ref[h] += sum(weighted(j, h, jnp.exp2(s[n][h] - bound[h]).astype(_bf16))
                              for n, j in enumerate(tiles))
        return c

    lax.fori_loop(0, i, body, 0)
    _mla_finish(o_ref, acc_ref)

    l_min = functools.reduce(jnp.minimum,
                             [jnp.min(acc_ref[h, MLA_DV:MLA_DV + 1, :]) for h in range(heads)])

    @pl.when(jnp.logical_not(l_min >= MIN_SOFTMAX_SUM))
    def _():
        _mla_exact(q_ref, k_ref, v_ref, o_ref, acc_ref)


def _mla_exact(q_ref, k_ref, v_ref, o_ref, acc_ref):
    i = pl.program_id(2)
    heads = MLA_GROUP
    q_t = [_mla_queries(q_ref, h, slice(None)) for h in range(heads)]

    ms = []
    for h in range(heads):
        kb = k_ref[0, PADL:TILE, h * HEAD_PAD:(h + 1) * HEAD_PAD]
        s = _dot(kb, q_t[h])
        m0 = jnp.max(s, axis=0, keepdims=True)
        p = jnp.concatenate([jnp.zeros((PADL, Q_TILE), _bf16), jnp.exp2(s - m0).astype(_bf16)], axis=0)
        acc_ref[h] = _dot(v_ref[0, h * V_AUG:(h + 1) * V_AUG, 0:TILE], p)
        ms.append(m0)

    k_row = lax.broadcasted_iota(jnp.int32, (KV_TILE, Q_TILE), 0)
    q_col = lax.broadcasted_iota(jnp.int32, (KV_TILE, Q_TILE), 1)

    def body(j, ms):
        visible = (j - 2 * i) * KV_TILE + k_row <= q_col
        out = []
        for h in range(heads):
            s = _dot(k_ref[0, pl.ds(_mla_tile_start(j), KV_TILE), h * HEAD_PAD:(h + 1) * HEAD_PAD],
                     q_t[h])
            s = jnp.where(visible, s, NEG)
            vb = v_ref[0, h * V_AUG:(h + 1) * V_AUG, pl.ds(_mla_tile_start(j), KV_TILE)]
            m_new = jnp.maximum(ms[h], jnp.max(s, axis=0, keepdims=True))
            alpha = jnp.exp2(ms[h] - m_new)
            acc_ref[h] = alpha * acc_ref[h] + _dot(vb, jnp.exp2(s - m_new).astype(_bf16))
            out.append(m_new)
        return tuple(out)

    lax.fori_loop(0, 2 * i + 2, body, tuple(ms))
    _mla_finish(o_ref, acc_ref)


def _merge_kernel(og_ref, om_ref, ga_ref, gbt_ref, s_ref, wbg_ref, wbm_ref, wo_ref, lng_ref,
                  lnb_ref, rwh_ref, rwl_ref, rb_ref, tril_ref,
                  s2_ref, info_ref, infot_ref, cnt_ref, carry_ref):
    step = pl.program_id(0)

    @pl.when(step == 0)
    def _():
        carry_ref[...] = jnp.zeros_like(carry_ref)

    n_blk = MERGE_TILE // ROUTE_BLOCK
    blocks = [slice(i * ROUTE_BLOCK, (i + 1) * ROUTE_BLOCK) for i in range(n_blk)]
    d = wo_ref.shape[0]
    col_blocks = [slice(c, c + MXU_DIM) for c in range(0, d, MXU_DIM)]
    lane = lax.broadcasted_iota(jnp.int32, (ROUTE_BLOCK, LANES), 1)
    is_g = lane < N_GROUPS
    merged, s2, logits, infos = {}, {}, {}, {}
    carry = [carry_ref[0:1, :]]

    def branches(i):
        rows = blocks[i]
        merged[i] = jnp.concatenate(
            [(ga_ref[rows, cols].astype(_f32) * _dot(og_ref[rows, :], wbg_ref[:, cols])
              + gbt_ref[rows, cols].astype(_f32) * _dot(om_ref[rows, :], wbm_ref[:, cols])
              ).astype(_bf16) for cols in col_blocks], axis=1)

    def residual_norm(i):
        y = ALPHA * s_ref[blocks[i], :] + _dot(merged[i], wo_ref[...])
        s2[i] = _layer_norm(y, lng_ref[...], lnb_ref[...])

    def router_logits(i):
        xh = s2[i].astype(_bf16)
        xl = (s2[i] - xh.astype(_f32)).astype(_bf16)
        logits[i] = (_dot(xh, rwh_ref[...]) + _dot(xl, rwh_ref[...]) + _dot(xh, rwl_ref[...])
                     + rb_ref[...])

    def route(i):
        gl = jnp.where(is_g, logits[i], NEG)
        gmax = jnp.max(gl, axis=-1, keepdims=True)
        gidx = jnp.min(jnp.where(gl == gmax, lane, LANES), axis=-1, keepdims=True)
        p_g = 1.0 / jnp.sum(jnp.where(is_g, jnp.exp(gl - gmax), 0.0), axis=-1, keepdims=True)
        lo = N_GROUPS + EXPERTS_PER_GROUP * gidx
        el = jnp.where((lane >= lo) & (lane < lo + EXPERTS_PER_GROUP), logits[i], NEG)
        v1 = jnp.max(el, axis=-1, keepdims=True)
        i1 = jnp.min(jnp.where(el == v1, lane, LANES), axis=-1, keepdims=True)
        el2 = jnp.where(lane == i1, NEG, el)
        v2 = jnp.max(el2, axis=-1, keepdims=True)
        i2 = jnp.min(jnp.where(el2 == v2, lane, LANES), axis=-1, keepdims=True)
        tt = jnp.exp(v2 - v1)
        p1 = 1.0 / (1.0 + tt)
        p2 = tt / (1.0 + tt)
        e1 = i1 - N_GROUPS
        e2 = i2 - N_GROUPS
        hit1 = lane == e1
        hit2 = lane == e2
        onehot = jnp.where(hit1 | hit2, 1.0, 0.0)
        before = _dot(tril_ref[...], onehot.astype(_bf16)) + carry[0]
        r1 = jnp.sum(jnp.where(hit1, before, 0.0), axis=-1, keepdims=True)
        r2 = jnp.sum(jnp.where(hit2, before, 0.0), axis=-1, keepdims=True)
        carry[0] = carry[0] + jnp.sum(onehot, axis=0, keepdims=True)
        infos[i] = jnp.where(lane == 0, e1.astype(_f32),
                   jnp.where(lane == 1, e2.astype(_f32),
                   jnp.where(lane == 2, p_g * p1,
                   jnp.where(lane == 3, p_g * p2,
                   jnp.where(lane == 4, r1,
                   jnp.where(lane == 5, r2, 0.0))))))

    for stage in (branches, residual_norm, router_logits, route):
        for i in range(n_blk):
            stage(i)
    _tt_store(s2_ref, jnp.concatenate([s2[i] for i in range(n_blk)], axis=0))
    carry_ref[...] = jnp.broadcast_to(carry[0], carry_ref.shape)
    cnt_ref[...] = jnp.broadcast_to(carry[0], cnt_ref.shape)
    info = jnp.concatenate([infos[i] for i in range(n_blk)], axis=0)
    info_ref[...] = info
    infot_ref[...] = info.T[0:SUBLANES]


def _dispatch_kernel(pos_ref, zrow_ref, nused_ref, s2_ref, xs_hbm, zero_ref, sems):
    g = pl.program_id(0)
    zero_sem = sems.at[1]
    row_sem = sems.at[0]

    block_rows = ZERO_BLOCK * SUBLANES

    def zero_copy(row):
        start = pl.multiple_of(row * SUBLANES, block_rows)
        return pltpu.make_async_copy(zero_ref, xs_hbm.at[pl.ds(start, block_rows)], zero_sem)

    @pl.when(g == 0)
    def _():
        zero_ref[...] = jnp.zeros_like(zero_ref)

        def start(e, c):
            @pl.when(zrow_ref[e] >= 0)
            def _():
                zero_copy(zrow_ref[e]).start()
            return c

        def wait(e, c):
            @pl.when(zrow_ref[e] >= 0)
            def _():
                zero_copy(0).wait()
            return c

        def start_tail(u, c):
            zero_copy(u * ZERO_BLOCK).start()
            return c

        def wait_tail(u, c):
            zero_copy(0).wait()
            return c

        n_blocks = xs_hbm.shape[0] // block_rows
        lax.fori_loop(0, zrow_ref.shape[0], start, 0)
        lax.fori_loop(nused_ref[0], n_blocks, start_tail, 0)
        lax.fori_loop(0, zrow_ref.shape[0], wait, 0)
        lax.fori_loop(nused_ref[0], n_blocks, wait_tail, 0)

    n_tok = pl.num_programs(0) * DISPATCH_TILE

    def issue(r, c):
        tok = g * DISPATCH_TILE + r
        for k in range(2):
            pltpu.make_async_copy(s2_ref.at[_tt_rows(r)],
                                  xs_hbm.at[_tt_rows(pos_ref[k * n_tok + tok])], row_sem).start(priority=k)
        return c

    lax.fori_loop(0, DISPATCH_TILE, issue, 0, unroll=8)
    for k in range(2):
        pltpu.make_async_copy(s2_ref, xs_hbm.at[pl.ds(0, DISPATCH_TILE * SUBLANES)], row_sem).wait()


def _expert_kernel(te_ref, nused_ref, x_ref, wg_ref, wu_ref, wd_ref, o_ref, wgb_ref, wub_ref, wdb_ref):
    u = pl.program_id(0)
    used = u < nused_ref[0]
    first_of_expert = jnp.logical_or(u == 0, te_ref[u] != te_ref[jnp.maximum(u - 1, 0)])

    @pl.when(jnp.logical_and(used, first_of_expert))
    def _():
        wgb_ref[...] = wg_ref[0].astype(_bf16)
        wub_ref[...] = wu_ref[0].astype(_bf16)
        wdb_ref[...] = wd_ref[0].astype(_bf16)

    @pl.when(used)
    def _():
        x = _tt_load(x_ref, EXPERT_TILE).astype(_bf16)
        a = _dot(x, wgb_ref[...])
        up = _dot(x, wub_ref[...])
        hid = a * _sigmoid(a) * up
        _tt_store(o_ref, _dot(hid.astype(_bf16), wdb_ref[...]))

    @pl.when(u >= nused_ref[0])
    def _():
        o_ref[...] = jnp.zeros_like(o_ref)


def _combine_kernel(pos_ref, s2_ref, info_ref, ys_hbm, lng_ref, lnb_ref, o_ref, buf_ref, sems):
    g = pl.program_id(0)
    last = pl.num_programs(0) - 1
    n_tok = pl.num_programs(0) * COMBINE_TILE

    def start_copies(tile, slot, r):
        tok = tile * COMBINE_TILE + r
        for k in range(2):
            pltpu.make_async_copy(ys_hbm.at[_tt_rows(pos_ref[k * n_tok + tok])],
                                  buf_ref.at[slot, k, _tt_rows(r)], sems.at[slot]).start(priority=k)

    def wait_slot(slot):
        for k in range(2):
            pltpu.make_async_copy(ys_hbm.at[pl.ds(0, COMBINE_TILE * SUBLANES)], buf_ref.at[slot, k],
                                  sems.at[slot]).wait()

    @pl.when(g == 0)
    def _():
        def body(r, c):
            for tile in range(COMBINE_SLOTS - 1):
                start_copies(tile, tile, r)
            return c

        lax.fori_loop(0, COMBINE_TILE, body, 0, unroll=8)

    slot = g % COMBINE_SLOTS
    wait_slot(slot)
    info = info_ref[...]
    y = (ALPHA * _tt_load(s2_ref, COMBINE_TILE)
         + info[:, 2:3] * _tt_load(buf_ref, COMBINE_TILE, (slot, 0))
         + info[:, 3:4] * _tt_load(buf_ref, COMBINE_TILE, (slot, 1)))
    o_ref[...] = _layer_norm(y, lng_ref[...], lnb_ref[...])

    ahead = COMBINE_SLOTS - 1
    ahead_slot = (g + ahead) % COMBINE_SLOTS
    for r in range(COMBINE_TILE):
        start_copies(jnp.minimum(g + ahead, last), ahead_slot, r)

    @pl.when(g == last)
    def _():
        for n in range(1, COMBINE_SLOTS):
            wait_slot((g + n) % COMBINE_SLOTS)


def _const_spec(shape):
    nd = len(shape)
    return pl.BlockSpec(shape, lambda *_: (0,) * nd)


def _rope_tables(lp):
    pos = np.maximum(np.arange(lp, dtype=np.float64) - PADL, 0.0)
    inv_freq = ROPE_BASE ** (-np.arange(0, MLA_ROPE, 2, dtype=np.float64) / MLA_ROPE)
    ang = pos[:, None] * inv_freq[None, :]
    cos, sin = np.cos(ang), np.sin(ang)
    half = MLA_ROPE // 2
    ones = np.ones((lp, MLA_NOPE))
    zeros_n = np.zeros((lp, MLA_NOPE))
    zeros_h = np.zeros((lp, half))
    tail1 = np.ones((lp, LANES - MLA_QDIM))
    tail0 = np.zeros((lp, LANES - MLA_QDIM))
    cos_t = np.concatenate([ones, cos, cos, tail1], axis=1)
    sin_lo = np.concatenate([zeros_n, -sin, zeros_h, tail0], axis=1)
    sin_hi = np.concatenate([zeros_n, zeros_h, sin, tail0], axis=1)
    return tuple(jnp.asarray(np.ascontiguousarray(t), dtype=_f32)
                 for t in (cos_t, sin_lo, sin_hi, cos.T, sin.T))


def _pad_heads(w, width):
    k = w.shape[0]
    w = w.reshape(k, MLA_HEADS, width)
    w = jnp.pad(w, ((0, 0), (0, 0), (0, HEAD_PAD - width)))
    return w.reshape(k, MLA_HEADS * HEAD_PAD)


def kernel(x, meta_tokens, ln_emb_g, ln_emb_b, w_in, gla_gate_w2, gla_gate_b, gla_norm_g, mla_q_norm_g, mla_w_uq, mla_kv_norm_g, mla_w_uk, mla_w_uv, w_branch_gla, w_branch_mla, w_out, ln_mix_g, ln_mix_b, router_group_w, router_group_b, router_expert_w, router_expert_b, expert_w_gate, expert_w_up, expert_w_down, ln_ffn_g, ln_ffn_b):
    bsz, seq, d = x.shape
    assert d == D_MODEL and seq % Q_TILE == 0 and w_in.shape[0] == DEPTH == 1
    lp = PADL + N_META + seq
    nt = lp // TILE
    ntok = bsz * seq
    row2 = lambda v: v.reshape(1, -1).astype(_f32)

    head_tile = jnp.concatenate([jnp.zeros((PADL, d), _f32), meta_tokens.astype(_f32)], axis=0)
    wi = w_in[0]
    o_a = 2 * GLA_QK + 2 * GLA_VW
    o_cq = o_a + GLA_GATE_RANK
    o_ckv = o_cq + MLA_Q_RANK
    o_kr = o_ckv + MLA_KV_RANK
    o_ga = o_kr + MLA_ROPE
    w_a = jnp.pad(wi[:, o_a:o_cq], ((0, 0), (0, LANES - GLA_GATE_RANK)))
    w_kr = jnp.pad(wi[:, o_kr:o_ga], ((0, 0), (MLA_NOPE, LANES - MLA_QDIM)))
    w_all = jnp.concatenate([wi[:, :o_a], w_a, wi[:, o_cq:o_kr], w_kr, wi[:, o_ga:]], axis=1).astype(_bf16)
    assert w_all.shape == (d, _W_COLS)
    w2p = jnp.pad(gla_gate_w2[0], ((0, LANES - GLA_GATE_RANK), (0, 0))).astype(_bf16)
    wuqt = mla_w_uq[0].T.astype(_bf16)
    wuk = _pad_heads(mla_w_uk[0], MLA_NOPE).astype(_bf16)
    wuvt = mla_w_uv[0].T.astype(_bf16)
    cos_t, sin_lo, sin_hi, cos_tr, sin_tr = _rope_tables(lp)
    blk = np.arange(TILE)
    tril_chunks = jnp.asarray(
        ((blk[:, None] >= blk[None, :]) & (blk[:, None] // GLA_CHUNK == blk[None, :] // GLA_CHUNK)),
        dtype=_bf16)

    pad_map = lambda b, t: (b, t, 0)
    real_map = lambda b, t: (b, jnp.maximum(t - 1, 0), 0)
    real_map_t = lambda b, t: (b, 0, jnp.maximum(t - 1, 0))
    out_shapes = (
        jax.ShapeDtypeStruct((bsz, seq, d), _f32),
        jax.ShapeDtypeStruct((bsz, lp, GLA_QK), _bf16),
        jax.ShapeDtypeStruct((bsz, lp, GLA_QK), _bf16),
        jax.ShapeDtypeStruct((bsz, lp, GLA_QK), _bf16),
        jax.ShapeDtypeStruct((bsz, lp, GLA_VW), _bf16),
        jax.ShapeDtypeStruct((bsz * nt, TILE // GLA_CHUNK, GLA_QK), _f32),
        jax.ShapeDtypeStruct((bsz, seq, GLA_VW), _bf16),
        jax.ShapeDtypeStruct((bsz, MLA_HEADS * MLA_QDIM, seq), _bf16),
        jax.ShapeDtypeStruct((bsz, lp, MLA_HEADS * HEAD_PAD), _bf16),
        jax.ShapeDtypeStruct((bsz, MLA_HEADS * V_AUG, lp), _bf16),
        jax.ShapeDtypeStruct((bsz, seq, d), _bf16),
        jax.ShapeDtypeStruct((bsz, seq, d), _bf16),
    )
    out_specs = (
        pl.BlockSpec((1, TILE, d), real_map),
        pl.BlockSpec((1, TILE, GLA_QK), pad_map),
        pl.BlockSpec((1, TILE, GLA_QK), pad_map),
        pl.BlockSpec((1, TILE, GLA_QK), pad_map),
        pl.BlockSpec((1, TILE, GLA_VW), pad_map),
        pl.BlockSpec((1, TILE // GLA_CHUNK, GLA_QK), lambda b, t: (b * nt + t, 0, 0)),
        pl.BlockSpec((1, TILE, GLA_VW), real_map),
        pl.BlockSpec((1, MLA_HEADS * MLA_QDIM, TILE), real_map_t),
        pl.BlockSpec((1, TILE, MLA_HEADS * HEAD_PAD), pad_map),
        pl.BlockSpec((1, MLA_HEADS * V_AUG, TILE), lambda b, t: (b, 0, t)),
        pl.BlockSpec((1, TILE, d), real_map),
        pl.BlockSpec((1, TILE, d), real_map),
    )
    tab_spec = pl.BlockSpec((TILE, LANES), lambda b, t: (t, 0))
    tabt_spec = pl.BlockSpec((MLA_ROPE // 2, TILE), lambda b, t: (0, t))
    (s_emb, qt, kt, ke, gv, dec, sr, qm, km, vm, gate_a, gate_b) = pl.pallas_call(
        _inproj_kernel,
        grid=(bsz, nt),
        in_specs=[
            pl.BlockSpec((1, TILE, d), real_map),
            _const_spec((TILE, d)),
            _const_spec((1, d)), _const_spec((1, d)),
            _const_spec((d, _W_COLS)),
            _const_spec((LANES, GLA_QK)), _const_spec((1, GLA_QK)),
            _const_spec((1, MLA_Q_RANK)), _const_spec((MLA_HEADS * MLA_QDIM, MLA_Q_RANK)),
            _const_spec((1, MLA_KV_RANK)), _const_spec((MLA_KV_RANK, MLA_HEADS * HEAD_PAD)),
            _const_spec((MLA_HEADS * MLA_DV, MLA_KV_RANK)),
            tab_spec, tab_spec, tab_spec, tabt_spec, tabt_spec,
            _const_spec((TILE, TILE)),
        ],
        out_specs=out_specs,
        out_shape=out_shapes,
        compiler_params=pltpu.CompilerParams(
            dimension_semantics=("arbitrary", "arbitrary"), vmem_limit_bytes=VMEM_LIMIT),
        name="inproj",
    )(x, head_tile, row2(ln_emb_g), row2(ln_emb_b), w_all, w2p, row2(gla_gate_b[0]),
      row2(mla_q_norm_g[0]), wuqt, row2(mla_kv_norm_g[0]), wuk, wuvt, cos_t, sin_lo, sin_hi,
      cos_tr, sin_tr, tril_chunks)

    o_gla = pl.pallas_call(
        _gla_kernel,
        grid=(nt,),
        in_specs=[
            pl.BlockSpec((bsz, TILE, GLA_QK), lambda t: (0, t, 0)),
            pl.BlockSpec((bsz, TILE, GLA_QK), lambda t: (0, t, 0)),
            pl.BlockSpec((bsz, TILE, GLA_QK), lambda t: (0, t, 0)),
            pl.BlockSpec((bsz, TILE, GLA_VW), lambda t: (0, t, 0)),
            pl.BlockSpec((bsz, 1, TILE // GLA_CHUNK, GLA_QK), lambda t: (0, t, 0, 0)),
            pl.BlockSpec((bsz, TILE, GLA_VW), lambda t: (0, jnp.maximum(t - 1, 0), 0)),
            _const_spec((1, GLA_DV)),
        ],
        out_specs=pl.BlockSpec((bsz, TILE, GLA_VW), lambda t: (0, jnp.maximum(t - 1, 0), 0)),
        out_shape=jax.ShapeDtypeStruct((bsz, seq, GLA_VW), _bf16),
        scratch_shapes=[pltpu.VMEM((bsz, GLA_HEADS, GLA_DV, GLA_DK), _f32)],
        compiler_params=pltpu.CompilerParams(
            dimension_semantics=("arbitrary",), vmem_limit_bytes=VMEM_LIMIT),
        name="gla",
    )(qt, kt, ke, gv, dec.reshape(bsz, nt, TILE // GLA_CHUNK, GLA_QK), sr, row2(gla_norm_g[0]))

    group_width = MLA_GROUP * HEAD_PAD
    o_mla = pl.pallas_call(
        _mla_kernel,
        grid=(bsz, MLA_HEADS // MLA_GROUP, seq // Q_TILE),
        in_specs=[
            pl.BlockSpec((1, MLA_GROUP * MLA_QDIM, Q_TILE), lambda b, hp, i: (b, hp, i)),
            pl.BlockSpec((1, lp, group_width), lambda b, hp, i: (b, 0, hp)),
            pl.BlockSpec((1, MLA_GROUP * V_AUG, lp), lambda b, hp, i: (b, hp, 0)),
        ],
        out_specs=pl.BlockSpec((1, Q_TILE, MLA_GROUP * MLA_DV), lambda b, hp, i: (b, i, hp)),
        out_shape=jax.ShapeDtypeStruct((bsz, seq, MLA_HEADS * MLA_DV), _bf16),
        scratch_shapes=[pltpu.VMEM((MLA_GROUP, V_AUG, Q_TILE), _f32),
                        pltpu.VMEM((MLA_GROUP, 1, LANES), _f32)],
        compiler_params=pltpu.CompilerParams(
            dimension_semantics=("arbitrary", "arbitrary", "arbitrary"),
            vmem_limit_bytes=VMEM_LIMIT),
        name="mla",
    )(qm, km, vm)

    rw = jnp.concatenate([router_group_w[0], router_expert_w[0]], axis=1)
    rw = jnp.pad(rw, ((0, 0), (0, LANES - rw.shape[1])))
    rwh = rw.astype(_bf16)
    rwl = (rw - rwh.astype(_f32)).astype(_bf16)
    rb = jnp.concatenate([router_group_b[0], router_expert_b[0]])
    rb = jnp.pad(rb, (0, LANES - rb.shape[0])).reshape(1, LANES)
    mi = np.arange(ROUTE_BLOCK)
    tril_strict = jnp.asarray(mi[:, None] > mi[None, :], dtype=_bf16)
    flat = lambda a: a.reshape(ntok, a.shape[-1])
    tok_spec = lambda w: pl.BlockSpec((MERGE_TILE, w), lambda g: (g, 0))
    tt_spec = lambda n, index_map: pl.BlockSpec((n * SUBLANES, LANES), index_map)
    s2, info, info_t, cnt = pl.pallas_call(
        _merge_kernel,
        grid=(ntok // MERGE_TILE,),
        in_specs=[tok_spec(d), tok_spec(d), tok_spec(d), tok_spec(d), tok_spec(d),
                  _const_spec((d, d)), _const_spec((d, d)), _const_spec((d, d)),
                  _const_spec((1, d)), _const_spec((1, d)),
                  _const_spec((d, LANES)), _const_spec((d, LANES)), _const_spec((1, LANES)),
                  _const_spec((ROUTE_BLOCK, ROUTE_BLOCK))],
        out_specs=(tt_spec(MERGE_TILE, lambda g: (g, 0)), tok_spec(LANES),
                   pl.BlockSpec((SUBLANES, MERGE_TILE), lambda g: (0, g)), _const_spec((8, LANES))),
        out_shape=(jax.ShapeDtypeStruct((ntok * SUBLANES, LANES), _f32),
                   jax.ShapeDtypeStruct((ntok, LANES), _f32),
                   jax.ShapeDtypeStruct((SUBLANES, ntok), _f32),
                   jax.ShapeDtypeStruct((8, LANES), _f32)),
        scratch_shapes=[pltpu.VMEM((8, LANES), _f32)],
        compiler_params=pltpu.CompilerParams(
            dimension_semantics=("arbitrary",), vmem_limit_bytes=VMEM_LIMIT),
        name="merge_router",
    )(flat(o_gla), flat(o_mla), flat(gate_a), flat(gate_b), flat(s_emb),
      w_branch_gla[0].astype(_bf16), w_branch_mla[0].astype(_bf16), w_out[0].astype(_bf16),
      row2(ln_mix_g[0]), row2(ln_mix_b[0]), rwh, rwl, rb, tril_strict)

    n_tiles = (2 * ntok + N_EXPERTS * (EXPERT_TILE - 1)) // EXPERT_TILE
    n_rows = n_tiles * EXPERT_TILE
    e_idx = info_t[0:2].astype(jnp.int32)
    rank = info_t[4:6].astype(jnp.int32)
    counts = cnt[0, :N_EXPERTS].astype(jnp.int32)
    padded = ((counts + EXPERT_TILE - 1) // EXPERT_TILE) * EXPERT_TILE
    ends = jnp.cumsum(padded)
    starts = ends - padded
    expert_ids = jnp.arange(N_EXPERTS, dtype=jnp.int32)[:, None, None]
    start_of = jnp.sum(jnp.where(e_idx[None] == expert_ids, starts[:, None, None], 0), axis=0)
    pos = (start_of + rank).reshape(-1)
    tile_start = jnp.arange(n_tiles, dtype=jnp.int32) * EXPERT_TILE
    tile_expert = jnp.minimum(
        jnp.sum((ends[None, :] <= tile_start[:, None]).astype(jnp.int32), axis=1), N_EXPERTS - 1)
    n_used = (ends[-1:] // EXPERT_TILE).astype(jnp.int32)
    used_blocks = (ends[-1:] // ZERO_BLOCK).astype(jnp.int32)
    pad_rows = padded - counts
    zero_row = jnp.concatenate([
        jnp.where(pad_rows > j * ZERO_BLOCK, ends - (j + 1) * ZERO_BLOCK, -1)
        for j in range(EXPERT_TILE // ZERO_BLOCK)]).astype(jnp.int32)

    any_spec = pl.BlockSpec(memory_space=pl.ANY)
    xs = pl.pallas_call(
        _dispatch_kernel,
        grid_spec=pltpu.PrefetchScalarGridSpec(
            num_scalar_prefetch=3,
            grid=(ntok // DISPATCH_TILE,),
            in_specs=[tt_spec(DISPATCH_TILE, lambda g, p, z, nu: (g, 0))],
            out_specs=any_spec,
            scratch_shapes=[pltpu.VMEM((ZERO_BLOCK * SUBLANES, LANES), _f32),
                            pltpu.SemaphoreType.DMA((2,))],
        ),
        out_shape=jax.ShapeDtypeStruct((n_rows * SUBLANES, LANES), _f32),
        compiler_params=pltpu.CompilerParams(
            dimension_semantics=("arbitrary",), vmem_limit_bytes=VMEM_LIMIT, has_side_effects=True),
        name="dispatch",
    )(pos, zero_row, used_blocks, s2)

    ff = EXPERT_FF
    wg = expert_w_gate[0].reshape(N_EXPERTS, d, ff)
    wu = expert_w_up[0].reshape(N_EXPERTS, d, ff)
    wd = expert_w_down[0].reshape(N_EXPERTS, ff, d)
    ys = pl.pallas_call(
        _expert_kernel,
        grid_spec=pltpu.PrefetchScalarGridSpec(
            num_scalar_prefetch=2,
            grid=(n_tiles,),
            in_specs=[
                tt_spec(EXPERT_TILE, lambda u, te, nu: (jnp.minimum(u, nu[0] - 1), 0)),
                pl.BlockSpec((1, d, ff), lambda u, te, nu: (te[u], 0, 0)),
                pl.BlockSpec((1, d, ff), lambda u, te, nu: (te[u], 0, 0)),
                pl.BlockSpec((1, ff, d), lambda u, te, nu: (te[u], 0, 0)),
            ],
            out_specs=tt_spec(EXPERT_TILE, lambda u, te, nu: (u, 0)),
            scratch_shapes=[pltpu.VMEM((d, ff), _bf16), pltpu.VMEM((d, ff), _bf16),
                            pltpu.VMEM((ff, d), _bf16)],
        ),
        out_shape=jax.ShapeDtypeStruct((n_rows * SUBLANES, LANES), _f32),
        compiler_params=pltpu.CompilerParams(
            dimension_semantics=("arbitrary",), vmem_limit_bytes=VMEM_LIMIT),
        name="experts",
    )(tile_expert, n_used, xs, wg, wu, wd)

    out = pl.pallas_call(
        _combine_kernel,
        grid_spec=pltpu.PrefetchScalarGridSpec(
            num_scalar_prefetch=1,
            grid=(ntok // COMBINE_TILE,),
            in_specs=[
                tt_spec(COMBINE_TILE, lambda g, p: (g, 0)),
                pl.BlockSpec((COMBINE_TILE, LANES), lambda g, p: (g, 0)),
                any_spec,
                pl.BlockSpec((1, d), lambda g, p: (0, 0)),
                pl.BlockSpec((1, d), lambda g, p: (0, 0)),
            ],
            out_specs=pl.BlockSpec((COMBINE_TILE, d), lambda g, p: (g, 0)),
            scratch_shapes=[pltpu.VMEM((COMBINE_SLOTS, 2, COMBINE_TILE * SUBLANES, LANES), _f32),
                            pltpu.SemaphoreType.DMA((COMBINE_SLOTS,))],
        ),
        out_shape=jax.ShapeDtypeStruct((ntok, d), _f32),
        compiler_params=pltpu.CompilerParams(
            dimension_semantics=("arbitrary",), vmem_limit_bytes=VMEM_LIMIT),
        name="combine_ln",
    )(pos, s2, info, ys, row2(ln_ffn_g[0]), row2(ln_ffn_b[0]))
    return out.reshape(bsz, seq, d)
```

```python
import functools

import jax
import jax.numpy as jnp
import numpy as np
from jax import lax
from jax.experimental import pallas as pl
from jax.experimental.pallas import tpu as pltpu

D_MODEL = 1024
N_META = 16
GLA_HEADS = 4
GLA_DK = 128
GLA_DV = 256
GLA_QK = GLA_HEADS * GLA_DK
GLA_VW = GLA_HEADS * GLA_DV
GLA_GATE_RANK = 16
GLA_GATE_TAU = 16.0
GLA_CHUNK = 64
MLA_HEADS = 16
MLA_Q_RANK = 384
MLA_KV_RANK = 256
MLA_NOPE = 64
MLA_ROPE = 32
MLA_DV = 64
MLA_QDIM = MLA_NOPE + MLA_ROPE
ROPE_BASE = 10000.0
N_GROUPS = 4
EXPERTS_PER_GROUP = 8
N_EXPERTS = N_GROUPS * EXPERTS_PER_GROUP
EXPERT_FF = 256
DEPTH = 1
ALPHA = (2.0 * DEPTH) ** 0.25
LN_EPS = 1e-5
RMS_EPS = 1e-6

LANES = 128
SUBLANES = 8
MXU_DIM = 256
TILE = 256
PADL = TILE - N_META
HEAD_PAD = LANES
BF16_ROWS = 16
KV_TILE = 512
Q_TILE = 2 * KV_TILE
V_AUG = MLA_DV + BF16_ROWS
MERGE_TILE = 1024
ROUTE_BLOCK = 256
EXPERT_TILE = 512
EXPERT_SLOTS = 3
ZERO_BLOCK = 256
DISPATCH_TILE = 2048
COMBINE_TILE = 256
COMBINE_SLOTS = 3
NEG = -1e30
LOG2E = 1.4426950408889634
BOUND_SLACK = 1.02
MIN_SOFTMAX_SUM = 2.0 ** -100
VMEM_LIMIT = 56 * 1024 * 1024

_C_Q, _C_K, _C_V, _C_R = 0, 512, 1024, 2048
_C_A = 3072
_C_CQ = _C_A + LANES
_C_CKV = _C_CQ + MLA_Q_RANK
_C_KR = _C_CKV + MLA_KV_RANK
_C_GA = _C_KR + LANES
_C_GB = _C_GA + D_MODEL
_W_COLS = _C_GB + D_MODEL

_f32 = jnp.float32
_bf16 = jnp.bfloat16


def _dot(a, b):
    return jnp.dot(a, b, preferred_element_type=_f32)


def _dot_nt(a, b):
    return lax.dot_general(a, b, (((1,), (1,)), ((), ())), preferred_element_type=_f32)


def _dot_tn(a, b):
    return lax.dot_general(a, b, (((0,), (0,)), ((), ())), preferred_element_type=_f32)


def _layer_norm(x, g, b):
    mu = jnp.mean(x, axis=-1, keepdims=True)
    xc = x - mu
    var = jnp.mean(xc * xc, axis=-1, keepdims=True)
    return xc * lax.rsqrt(var + LN_EPS) * g + b


def _rms_norm(x, g):
    ms = jnp.mean(x * x, axis=-1, keepdims=True)
    return x * lax.rsqrt(ms + RMS_EPS) * g


def _sigmoid(x):
    return 1.0 / (1.0 + jnp.exp(-x))


def _tt_load(ref, n, lead=()):
    return jnp.concatenate(
        [ref[lead + (pl.ds(a, n, stride=SUBLANES), slice(None))] for a in range(SUBLANES)], axis=1)


def _tt_store(ref, x):
    n = x.shape[0]
    for a in range(SUBLANES):
        ref[pl.ds(a, n, stride=SUBLANES), :] = x[:, a * LANES:(a + 1) * LANES]


def _tt_rows(tok):
    return pl.ds(pl.multiple_of(tok * SUBLANES, SUBLANES), SUBLANES)


def _rope(x, cos, sin_lo, sin_hi):
    half = MLA_ROPE // 2
    from_hi = pltpu.roll(x, LANES - half, 1)
    from_lo = pltpu.roll(x, half, 1)
    return x * cos + from_hi * sin_lo + from_lo * sin_hi


def _inproj_kernel(x_ref, head_ref, lng_ref, lnb_ref, w_ref, w2_ref, gb_ref, qg_ref, wuqt_ref, kvg_ref,
                   wuk_ref, wuvt_ref, cos_ref, sl_ref, sh_ref, cost_ref, sint_ref, tril_ref,
                   s_ref, qt_ref, kt_ref, ke_ref, gv_ref, dec_ref, sr_ref, qm_ref, km_ref,
                   vm_ref, ga_ref, gbt_ref):
    t = pl.program_id(1)
    x_in = jnp.where(t == 0, head_ref[...], x_ref[0])
    sn = _layer_norm(x_in, lng_ref[...], lnb_ref[...])
    s_ref[0] = sn
    snb = sn.astype(_bf16)
    row = t * TILE + lax.broadcasted_iota(jnp.int32, (TILE, 1), 0)
    valid = row >= PADL

    proj = lambda c0, width: _dot(snb, w_ref[:, c0:c0 + width])
    a_lr = proj(_C_A, LANES)
    cq = proj(_C_CQ, MLA_Q_RANK)
    ckv = proj(_C_CKV, MLA_KV_RANK)
    kr_raw = proj(_C_KR, LANES)
    gq = proj(_C_Q, GLA_QK)
    gk = proj(_C_K, GLA_QK)
    gv_ref[0] = jnp.where(valid, proj(_C_V, GLA_VW), 0.0).astype(_bf16)
    r = proj(_C_R, GLA_VW)
    sr_ref[0] = (r * _sigmoid(r)).astype(_bf16)
    ga_ref[0] = _sigmoid(proj(_C_GA, D_MODEL)).astype(_bf16)
    gbt_ref[0] = _sigmoid(proj(_C_GB, D_MODEL)).astype(_bf16)

    z = _dot(a_lr.astype(_bf16), w2_ref[...]) + gb_ref[...]
    cqn = _rms_norm(cq, qg_ref[...]).astype(_bf16)
    ckvn = _rms_norm(ckv, kvg_ref[...]).astype(_bf16)
    qft = _dot_nt(wuqt_ref[...], cqn)
    kf = _dot(ckvn, wuk_ref[...])
    vt = _dot_nt(wuvt_ref[...], ckvn)

    la = (jnp.minimum(z, 0.0) - jnp.log1p(jnp.exp(-jnp.abs(z)))) * (1.0 / GLA_GATE_TAU)
    la = jnp.where(valid, la, 0.0)
    hi = la.astype(_bf16)
    r1 = la - hi.astype(_f32)
    mid = r1.astype(_bf16)
    lo = (r1 - mid.astype(_f32)).astype(_bf16)
    tril = tril_ref[...]
    bc = _dot(tril, hi) + _dot(tril, mid) + _dot(tril, lo)
    n_chunks = TILE // GLA_CHUNK
    lasts = [bc[c * GLA_CHUNK + GLA_CHUNK - 1:(c + 1) * GLA_CHUNK, :] for c in range(n_chunks)]
    for c in range(n_chunks):
        dec_ref[0, c:c + 1, :] = jnp.exp(lasts[c])
    b_last = jnp.concatenate(
        [jnp.broadcast_to(l, (GLA_CHUNK, GLA_QK)) for l in lasts], axis=0)
    gk = jnp.where(valid, gk, 0.0)
    qt_ref[0] = (gq * (GLA_DK ** -0.5) * jnp.exp(bc)).astype(_bf16)
    kt_ref[0] = (gk * jnp.exp(-bc)).astype(_bf16)
    ke_ref[0] = (gk * jnp.exp(b_last - bc)).astype(_bf16)

    cos = cos_ref[...]
    sl = sl_ref[...]
    sh = sh_ref[...]
    scale = (MLA_QDIM ** -0.5) * LOG2E
    cost = cost_ref[...]
    sint = sint_ref[...]
    half = MLA_ROPE // 2
    for h in range(MLA_HEADS):
        base = h * MLA_QDIM
        x1 = qft[base + MLA_NOPE:base + MLA_NOPE + half]
        x2 = qft[base + MLA_NOPE + half:base + MLA_QDIM]
        qm_ref[0, base:base + MLA_NOPE] = (qft[base:base + MLA_NOPE] * scale).astype(_bf16)
        qm_ref[0, base + MLA_NOPE:base + MLA_NOPE + half] = ((x1 * cost - x2 * sint) * scale).astype(_bf16)
        qm_ref[0, base + MLA_NOPE + half:base + MLA_QDIM] = ((x1 * sint + x2 * cost) * scale).astype(_bf16)
    kr = _rope(kr_raw, cos, sl, sh)
    for h in range(MLA_HEADS):
        km_ref[0, :, h * HEAD_PAD:(h + 1) * HEAD_PAD] = (
            kf[:, h * HEAD_PAD:(h + 1) * HEAD_PAD] + kr).astype(_bf16)
    for h in range(MLA_HEADS):
        vm_ref[0, h * V_AUG:h * V_AUG + MLA_DV] = vt[h * MLA_DV:(h + 1) * MLA_DV].astype(_bf16)
        vm_ref[0, h * V_AUG + MLA_DV:(h + 1) * V_AUG] = jnp.ones((V_AUG - MLA_DV, TILE), _bf16)


def _gla_kernel(qt_ref, kt_ref, ke_ref, gv_ref, dec_ref, sr_ref, ng_ref, o_ref, st_ref):
    t = pl.program_id(0)
    n_batch = qt_ref.shape[0]

    @pl.when(t == 0)
    def _():
        st_ref[...] = jnp.zeros_like(st_ref)

    ri = lax.broadcasted_iota(jnp.int32, (TILE, TILE), 0)
    ci = lax.broadcasted_iota(jnp.int32, (TILE, TILE), 1)
    visible = (ri >= ci) & (ri // GLA_CHUNK == ci // GLA_CHUNK)
    ng = ng_ref[...]
    n_chunks = TILE // GLA_CHUNK
    chunk_rows = [slice(c * GLA_CHUNK, (c + 1) * GLA_CHUNK) for c in range(n_chunks)]
    k_cols = [slice(h * GLA_DK, (h + 1) * GLA_DK) for h in range(GLA_HEADS)]
    v_cols = [slice(h * GLA_DV, (h + 1) * GLA_DV) for h in range(GLA_HEADS)]
    seqs = [(b, h) for b in range(n_batch) for h in range(GLA_HEADS)]
    att = {(b, h): jnp.where(visible, _dot_nt(qt_ref[b, :, k_cols[h]], kt_ref[b, :, k_cols[h]]), 0.0)
           for b, h in seqs}
    upd = {(b, h): [_dot_tn(gv_ref[b, rows, v_cols[h]], ke_ref[b, rows, k_cols[h]])
                    for rows in chunk_rows] for b, h in seqs}
    o_intra = {(b, h): _dot(att[b, h].astype(_bf16), gv_ref[b, :, v_cols[h]]) for b, h in seqs}
    for b, h in seqs:
        st = st_ref[b, h]
        for c, rows in enumerate(chunk_rows):
            o = o_intra[b, h][rows] + _dot_nt(qt_ref[b, rows, k_cols[h]], st.astype(_bf16))
            st = st * dec_ref[b, 0, c:c + 1, k_cols[h]] + upd[b, h][c]
            o = _rms_norm(o, ng) * sr_ref[b, rows, v_cols[h]].astype(_f32)
            o_ref[b, rows, v_cols[h]] = o.astype(_bf16)
        st_ref[b, h] = st


MLA_GROUP = 4


def _mla_tile_start(j):
    return pl.multiple_of(TILE + j * KV_TILE, TILE)


def _mla_queries(q_ref, h, cols):
    q = q_ref[0, h * MLA_QDIM:(h + 1) * MLA_QDIM, cols]
    return jnp.concatenate([q, jnp.zeros((HEAD_PAD - MLA_QDIM, q.shape[1]), q.dtype)], axis=0)


def _mla_finish(o_ref, acc_ref):
    outs = []
    for h in range(MLA_GROUP):
        a = acc_ref[h]
        outs.append(a[0:MLA_DV] / a[MLA_DV:MLA_DV + 1])
    o_ref[0] = jnp.concatenate(outs, axis=0).T.astype(_bf16)


def _mla_kernel(q_ref, k_ref, v_ref, o_ref, acc_ref, knorm_ref):
    i = pl.program_id(2)
    heads = MLA_GROUP
    ones = jnp.ones((HEAD_PAD, LANES), _bf16)

    @pl.when(i == 0)
    def _():
        for h in range(heads):
            kk = k_ref[0, :, h * HEAD_PAD:(h + 1) * HEAD_PAD].astype(_f32)
            hi = (kk * kk).astype(_bf16)
            knorm_ref[h] = jnp.max(_dot(hi, ones), axis=0, keepdims=True)

    def score_bound(h, q):
        qq = q.astype(_f32)
        qn2 = jnp.sum(qq * qq, axis=0, keepdims=True)
        return jnp.sqrt(qn2 * knorm_ref[h][:, 0:1]) * BOUND_SLACK

    def queries(cols):
        q = [_mla_queries(q_ref, h, cols) for h in range(heads)]
        return q, [score_bound(h, q[h]) for h in range(heads)]

    sub = KV_TILE // 2
    q_t, bound = queries(slice(None))
    q_lo, bound_lo = queries(slice(0, KV_TILE))
    q_hi, bound_hi = queries(slice(KV_TILE, Q_TILE))
    q_lo2, bound_lo2 = queries(slice(sub, KV_TILE))
    q_hi2, bound_hi2 = queries(slice(KV_TILE + sub, Q_TILE))

    def key_rows(start, n, h):
        return k_ref[0, pl.ds(pl.multiple_of(start, sub), n), h * HEAD_PAD:(h + 1) * HEAD_PAD]

    def value_cols(start, n, h):
        return v_ref[0, h * V_AUG:(h + 1) * V_AUG, pl.ds(pl.multiple_of(start, sub), n)]

    def keys(j, h):
        return key_rows(_mla_tile_start(j), KV_TILE, h)

    def weighted(j, h, p):
        return _dot(value_cols(_mla_tile_start(j), KV_TILE, h), p)

    def probs(s, b):
        return jnp.exp2(s - b).astype(_bf16)

    r0 = _mla_tile_start(2 * i)
    wide = (lax.broadcasted_iota(jnp.int32, (sub, KV_TILE), 0)
            <= lax.broadcasted_iota(jnp.int32, (sub, KV_TILE), 1))
    square = (lax.broadcasted_iota(jnp.int32, (sub, sub), 0)
              <= lax.broadcasted_iota(jnp.int32, (sub, sub), 1))
    s_meta = [_dot(k_ref[0, PADL:TILE, h * HEAD_PAD:(h + 1) * HEAD_PAD], q_t[h]) for h in range(heads)]
    s_full = [_dot(key_rows(r0, KV_TILE, h), q_hi[h]) for h in range(heads)]
    s_lo_a = [jnp.where(wide, _dot(key_rows(r0, sub, h), q_lo[h]), NEG) for h in range(heads)]
    s_lo_b = [jnp.where(square, _dot(key_rows(r0 + sub, sub, h), q_lo2[h]), NEG) for h in range(heads)]
    s_hi_a = [jnp.where(wide, _dot(key_rows(r0 + KV_TILE, sub, h), q_hi[h]), NEG) for h in range(heads)]
    s_hi_b = [jnp.where(square, _dot(key_rows(r0 + KV_TILE + sub, sub, h), q_hi2[h]), NEG)
              for h in range(heads)]
    for h in range(heads):
        p_meta = jnp.concatenate([jnp.zeros((PADL, Q_TILE), _bf16), probs(s_meta[h], bound[h])], axis=0)
        acc_ref[h] = _dot(v_ref[0, h * V_AUG:(h + 1) * V_AUG, 0:TILE], p_meta)
        acc_ref[h, :, 0:KV_TILE] += _dot(value_cols(r0, sub, h), probs(s_lo_a[h], bound_lo[h]))
        acc_ref[h, :, sub:KV_TILE] += _dot(value_cols(r0 + sub, sub, h), probs(s_lo_b[h], bound_lo2[h]))
        acc_ref[h, :, KV_TILE:Q_TILE] += (
            _dot(value_cols(r0, KV_TILE, h), probs(s_full[h], bound_hi[h]))
            + _dot(value_cols(r0 + KV_TILE, sub, h), probs(s_hi_a[h], bound_hi[h])))
        acc_ref[h, :, KV_TILE + sub:Q_TILE] += _dot(value_cols(r0 + KV_TILE + sub, sub, h),
                                                     probs(s_hi_b[h], bound_hi2[h]))

    def body(jj, c):
        tiles = (2 * jj, 2 * jj + 1)
        s = [[_dot(keys(j, h), q_t[h]) for h in range(heads)] for j in tiles]
        for h in range(heads):
            acc_ref[h] += sum(weighted(j, h, jnp.exp2(s[n][h] - bound[h]).astype(_bf16))
                              for n, j in enumerate(tiles))
        return c

    lax.fori_loop(0, i, body, 0)
    _mla_finish(o_ref, acc_ref)

    l_min = functools.reduce(jnp.minimum,
                             [jnp.min(acc_ref[h, MLA_DV:MLA_DV + 1, :]) for h in range(heads)])

    @pl.when(jnp.logical_not(l_min >= MIN_SOFTMAX_SUM))
    def _():
        _mla_exact(q_ref, k_ref, v_ref, o_ref, acc_ref)


def _mla_exact(q_ref, k_ref, v_ref, o_ref, acc_ref):
    i = pl.program_id(2)
    heads = MLA_GROUP
    q_t = [_mla_queries(q_ref, h, slice(None)) for h in range(heads)]

    ms = []
    for h in range(heads):
        kb = k_ref[0, PADL:TILE, h * HEAD_PAD:(h + 1) * HEAD_PAD]
        s = _dot(kb, q_t[h])
        m0 = jnp.max(s, axis=0, keepdims=True)
        p = jnp.concatenate([jnp.zeros((PADL, Q_TILE), _bf16), jnp.exp2(s - m0).astype(_bf16)], axis=0)
        acc_ref[h] = _dot(v_ref[0, h * V_AUG:(h + 1) * V_AUG, 0:TILE], p)
        ms.append(m0)

    k_row = lax.broadcasted_iota(jnp.int32, (KV_TILE, Q_TILE), 0)
    q_col = lax.broadcasted_iota(jnp.int32, (KV_TILE, Q_TILE), 1)

    def body(j, ms):
        visible = (j - 2 * i) * KV_TILE + k_row <= q_col
        out = []
        for h in range(heads):
            s = _dot(k_ref[0, pl.ds(_mla_tile_start(j), KV_TILE), h * HEAD_PAD:(h + 1) * HEAD_PAD],
                     q_t[h])
            s = jnp.where(visible, s, NEG)
            vb = v_ref[0, h * V_AUG:(h + 1) * V_AUG, pl.ds(_mla_tile_start(j), KV_TILE)]
            m_new = jnp.maximum(ms[h], jnp.max(s, axis=0, keepdims=True))
            alpha = jnp.exp2(ms[h] - m_new)
            acc_ref[h] = alpha * acc_ref[h] + _dot(vb, jnp.exp2(s - m_new).astype(_bf16))
            out.append(m_new)
        return tuple(out)

    lax.fori_loop(0, 2 * i + 2, body, tuple(ms))
    _mla_finish(o_ref, acc_ref)


def _merge_kernel(og_ref, om_ref, ga_ref, gbt_ref, s_ref, wbg_ref, wbm_ref, wo_ref, lng_ref,
                  lnb_ref, rwh_ref, rwl_ref, rb_ref, tril_ref,
                  s2_ref, info_ref, infot_ref, cnt_ref, carry_ref):
    step = pl.program_id(0)

    @pl.when(step == 0)
    def _():
        carry_ref[...] = jnp.zeros_like(carry_ref)

    n_blk = MERGE_TILE // ROUTE_BLOCK
    blocks = [slice(i * ROUTE_BLOCK, (i + 1) * ROUTE_BLOCK) for i in range(n_blk)]
    d = wo_ref.shape[0]
    col_blocks = [slice(c, c + MXU_DIM) for c in range(0, d, MXU_DIM)]
    lane = lax.broadcasted_iota(jnp.int32, (ROUTE_BLOCK, LANES), 1)
    is_g = lane < N_GROUPS
    merged, s2, logits, infos = {}, {}, {}, {}
    carry = [carry_ref[0:1, :]]

    def branches(i):
        rows = blocks[i]
        merged[i] = jnp.concatenate(
            [(ga_ref[rows, cols].astype(_f32) * _dot(og_ref[rows, :], wbg_ref[:, cols])
              + gbt_ref[rows, cols].astype(_f32) * _dot(om_ref[rows, :], wbm_ref[:, cols])
              ).astype(_bf16) for cols in col_blocks], axis=1)

    def residual_norm(i):
        y = ALPHA * s_ref[blocks[i], :] + _dot(merged[i], wo_ref[...])
        s2[i] = _layer_norm(y, lng_ref[...], lnb_ref[...])

    def router_logits(i):
        xh = s2[i].astype(_bf16)
        xl = (s2[i] - xh.astype(_f32)).astype(_bf16)
        logits[i] = (_dot(xh, rwh_ref[...]) + _dot(xl, rwh_ref[...]) + _dot(xh, rwl_ref[...])
                     + rb_ref[...])

    def route(i):
        gl = jnp.where(is_g, logits[i], NEG)
        gmax = jnp.max(gl, axis=-1, keepdims=True)
        gidx = jnp.min(jnp.where(gl == gmax, lane, LANES), axis=-1, keepdims=True)
        p_g = 1.0 / jnp.sum(jnp.where(is_g, jnp.exp(gl - gmax), 0.0), axis=-1, keepdims=True)
        lo = N_GROUPS + EXPERTS_PER_GROUP * gidx
        el = jnp.where((lane >= lo) & (lane < lo + EXPERTS_PER_GROUP), logits[i], NEG)
        v1 = jnp.max(el, axis=-1, keepdims=True)
        i1 = jnp.min(jnp.where(el == v1, lane, LANES), axis=-1, keepdims=True)
        el2 = jnp.where(lane == i1, NEG, el)
        v2 = jnp.max(el2, axis=-1, keepdims=True)
        i2 = jnp.min(jnp.where(el2 == v2, lane, LANES), axis=-1, keepdims=True)
        tt = jnp.exp(v2 - v1)
        p1 = 1.0 / (1.0 + tt)
        p2 = tt / (1.0 + tt)
        e1 = i1 - N_GROUPS
        e2 = i2 - N_GROUPS
        hit1 = lane == e1
        hit2 = lane == e2
        onehot = jnp.where(hit1 | hit2, 1.0, 0.0)
        before = _dot(tril_ref[...], onehot.astype(_bf16)) + carry[0]
        r1 = jnp.sum(jnp.where(hit1, before, 0.0), axis=-1, keepdims=True)
        r2 = jnp.sum(jnp.where(hit2, before, 0.0), axis=-1, keepdims=True)
        carry[0] = carry[0] + jnp.sum(onehot, axis=0, keepdims=True)
        infos[i] = jnp.where(lane == 0, e1.astype(_f32),
                   jnp.where(lane == 1, e2.astype(_f32),
                   jnp.where(lane == 2, p_g * p1,
                   jnp.where(lane == 3, p_g * p2,
                   jnp.where(lane == 4, r1,
                   jnp.where(lane == 5, r2, 0.0))))))

    for stage in (branches, residual_norm, router_logits, route):
        for i in range(n_blk):
            stage(i)
    _tt_store(s2_ref, jnp.concatenate([s2[i] for i in range(n_blk)], axis=0))
    carry_ref[...] = jnp.broadcast_to(carry[0], carry_ref.shape)
    cnt_ref[...] = jnp.broadcast_to(carry[0], cnt_ref.shape)
    info = jnp.concatenate([infos[i] for i in range(n_blk)], axis=0)
    info_ref[...] = info
    infot_ref[...] = info.T[0:SUBLANES]


def _dispatch_kernel(pos_ref, zrow_ref, nused_ref, s2_ref, xs_hbm, zero_ref, sems):
    g = pl.program_id(0)
    zero_sem = sems.at[1]
    row_sem = sems.at[0]

    block_rows = ZERO_BLOCK * SUBLANES

    def zero_copy(row):
        start = pl.multiple_of(row * SUBLANES, block_rows)
        return pltpu.make_async_copy(zero_ref, xs_hbm.at[pl.ds(start, block_rows)], zero_sem)

    @pl.when(g == 0)
    def _():
        zero_ref[...] = jnp.zeros_like(zero_ref)

        def start(e, c):
            @pl.when(zrow_ref[e] >= 0)
            def _():
                zero_copy(zrow_ref[e]).start()
            return c

        def wait(e, c):
            @pl.when(zrow_ref[e] >= 0)
            def _():
                zero_copy(0).wait()
            return c

        def start_tail(u, c):
            zero_copy(u * ZERO_BLOCK).start()
            return c

        def wait_tail(u, c):
            zero_copy(0).wait()
            return c

        n_blocks = xs_hbm.shape[0] // block_rows
        lax.fori_loop(0, zrow_ref.shape[0], start, 0)
        lax.fori_loop(nused_ref[0], n_blocks, start_tail, 0)
        lax.fori_loop(0, zrow_ref.shape[0], wait, 0)
        lax.fori_loop(nused_ref[0], n_blocks, wait_tail, 0)

    n_tok = pl.num_programs(0) * DISPATCH_TILE

    def issue(r, c):
        tok = g * DISPATCH_TILE + r
        for k in range(2):
            pltpu.make_async_copy(s2_ref.at[_tt_rows(r)],
                                  xs_hbm.at[_tt_rows(pos_ref[k * n_tok + tok])], row_sem).start(priority=k)
        return c

    lax.fori_loop(0, DISPATCH_TILE, issue, 0, unroll=8)
    for k in range(2):
        pltpu.make_async_copy(s2_ref, xs_hbm.at[pl.ds(0, DISPATCH_TILE * SUBLANES)], row_sem).wait()


def _expert_kernel(te_ref, nused_ref, xs_hbm, wg_ref, wu_ref, wd_ref, o_ref, wgb_ref, wub_ref, wdb_ref,
                   x_ref, sems):
    u = pl.program_id(0)
    last = pl.num_programs(0) - 1
    used = u < nused_ref[0]
    first_of_expert = jnp.logical_or(u == 0, te_ref[u] != te_ref[jnp.maximum(u - 1, 0)])
    tile_rows = EXPERT_TILE * SUBLANES
    ahead = EXPERT_SLOTS - 1

    def tile_copy(tile, slot):
        start = pl.multiple_of(jnp.minimum(tile, nused_ref[0] - 1) * tile_rows, tile_rows)
        return pltpu.make_async_copy(xs_hbm.at[pl.ds(start, tile_rows)], x_ref.at[slot], sems.at[slot])

    @pl.when(u == 0)
    def _():
        for tile in range(ahead):
            tile_copy(tile, tile).start(priority=1)

    tile_copy(jnp.minimum(u + ahead, last), (u + ahead) % EXPERT_SLOTS).start(priority=1)
    slot = u % EXPERT_SLOTS
    tile_copy(u, slot).wait()

    @pl.when(jnp.logical_and(used, first_of_expert))
    def _():
        wgb_ref[...] = wg_ref[0].astype(_bf16)
        wub_ref[...] = wu_ref[0].astype(_bf16)
        wdb_ref[...] = wd_ref[0].astype(_bf16)

    @pl.when(used)
    def _():
        x = _tt_load(x_ref, EXPERT_TILE, (slot,)).astype(_bf16)
        a = _dot(x, wgb_ref[...])
        up = _dot(x, wub_ref[...])
        hid = a * _sigmoid(a) * up
        _tt_store(o_ref, _dot(hid.astype(_bf16), wdb_ref[...]))

    @pl.when(u >= nused_ref[0])
    def _():
        o_ref[...] = jnp.zeros_like(o_ref)

    @pl.when(u == last)
    def _():
        for n in range(1, EXPERT_SLOTS):
            tile_copy(last, (u + n) % EXPERT_SLOTS).wait()


def _combine_kernel(pos_ref, s2_ref, info_ref, ys_hbm, lng_ref, lnb_ref, o_ref, buf_ref, sems):
    g = pl.program_id(0)
    last = pl.num_programs(0) - 1
    n_tok = pl.num_programs(0) * COMBINE_TILE

    def start_copies(tile, slot, r):
        tok = tile * COMBINE_TILE + r
        for k in range(2):
            pltpu.make_async_copy(ys_hbm.at[_tt_rows(pos_ref[k * n_tok + tok])],
                                  buf_ref.at[slot, k, _tt_rows(r)], sems.at[slot]).start(priority=k)

    def wait_slot(slot):
        for k in range(2):
            pltpu.make_async_copy(ys_hbm.at[pl.ds(0, COMBINE_TILE * SUBLANES)], buf_ref.at[slot, k],
                                  sems.at[slot]).wait()

    @pl.when(g == 0)
    def _():
        def body(r, c):
            for tile in range(COMBINE_SLOTS - 1):
                start_copies(tile, tile, r)
            return c

        lax.fori_loop(0, COMBINE_TILE, body, 0, unroll=8)

    slot = g % COMBINE_SLOTS
    wait_slot(slot)
    info = info_ref[...]
    y = (ALPHA * _tt_load(s2_ref, COMBINE_TILE)
         + info[:, 2:3] * _tt_load(buf_ref, COMBINE_TILE, (slot, 0))
         + info[:, 3:4] * _tt_load(buf_ref, COMBINE_TILE, (slot, 1)))
    o_ref[...] = _layer_norm(y, lng_ref[...], lnb_ref[...])

    ahead = COMBINE_SLOTS - 1
    ahead_slot = (g + ahead) % COMBINE_SLOTS
    for r in range(COMBINE_TILE):
        start_copies(jnp.minimum(g + ahead, last), ahead_slot, r)

    @pl.when(g == last)
    def _():
        for n in range(1, COMBINE_SLOTS):
            wait_slot((g + n) % COMBINE_SLOTS)


def _const_spec(shape):
    nd = len(shape)
    return pl.BlockSpec(shape, lambda *_: (0,) * nd)


def _rope_tables(lp):
    pos = np.maximum(np.arange(lp, dtype=np.float64) - PADL, 0.0)
    inv_freq = ROPE_BASE ** (-np.arange(0, MLA_ROPE, 2, dtype=np.float64) / MLA_ROPE)
    ang = pos[:, None] * inv_freq[None, :]
    cos, sin = np.cos(ang), np.sin(ang)
    half = MLA_ROPE // 2
    ones = np.ones((lp, MLA_NOPE))
    zeros_n = np.zeros((lp, MLA_NOPE))
    zeros_h = np.zeros((lp, half))
    tail1 = np.ones((lp, LANES - MLA_QDIM))
    tail0 = np.zeros((lp, LANES - MLA_QDIM))
    cos_t = np.concatenate([ones, cos, cos, tail1], axis=1)
    sin_lo = np.concatenate([zeros_n, -sin, zeros_h, tail0], axis=1)
    sin_hi = np.concatenate([zeros_n, zeros_h, sin, tail0], axis=1)
    return tuple(jnp.asarray(np.ascontiguousarray(t), dtype=_f32)
                 for t in (cos_t, sin_lo, sin_hi, cos.T, sin.T))


def _pad_heads(w, width):
    k = w.shape[0]
    w = w.reshape(k, MLA_HEADS, width)
    w = jnp.pad(w, ((0, 0), (0, 0), (0, HEAD_PAD - width)))
    return w.reshape(k, MLA_HEADS * HEAD_PAD)


def kernel(x, meta_tokens, ln_emb_g, ln_emb_b, w_in, gla_gate_w2, gla_gate_b, gla_norm_g, mla_q_norm_g, mla_w_uq, mla_kv_norm_g, mla_w_uk, mla_w_uv, w_branch_gla, w_branch_mla, w_out, ln_mix_g, ln_mix_b, router_group_w, router_group_b, router_expert_w, router_expert_b, expert_w_gate, expert_w_up, expert_w_down, ln_ffn_g, ln_ffn_b):
    bsz, seq, d = x.shape
    assert d == D_MODEL and seq % Q_TILE == 0 and w_in.shape[0] == DEPTH == 1
    lp = PADL + N_META + seq
    nt = lp // TILE
    ntok = bsz * seq
    row2 = lambda v: v.reshape(1, -1).astype(_f32)

    head_tile = jnp.concatenate([jnp.zeros((PADL, d), _f32), meta_tokens.astype(_f32)], axis=0)
    wi = w_in[0]
    o_a = 2 * GLA_QK + 2 * GLA_VW
    o_cq = o_a + GLA_GATE_RANK
    o_ckv = o_cq + MLA_Q_RANK
    o_kr = o_ckv + MLA_KV_RANK
    o_ga = o_kr + MLA_ROPE
    w_a = jnp.pad(wi[:, o_a:o_cq], ((0, 0), (0, LANES - GLA_GATE_RANK)))
    w_kr = jnp.pad(wi[:, o_kr:o_ga], ((0, 0), (MLA_NOPE, LANES - MLA_QDIM)))
    w_all = jnp.concatenate([wi[:, :o_a], w_a, wi[:, o_cq:o_kr], w_kr, wi[:, o_ga:]], axis=1).astype(_bf16)
    assert w_all.shape == (d, _W_COLS)
    w2p = jnp.pad(gla_gate_w2[0], ((0, LANES - GLA_GATE_RANK), (0, 0))).astype(_bf16)
    wuqt = mla_w_uq[0].T.astype(_bf16)
    wuk = _pad_heads(mla_w_uk[0], MLA_NOPE).astype(_bf16)
    wuvt = mla_w_uv[0].T.astype(_bf16)
    cos_t, sin_lo, sin_hi, cos_tr, sin_tr = _rope_tables(lp)
    blk = np.arange(TILE)
    tril_chunks = jnp.asarray(
        ((blk[:, None] >= blk[None, :]) & (blk[:, None] // GLA_CHUNK == blk[None, :] // GLA_CHUNK)),
        dtype=_bf16)

    pad_map = lambda b, t: (b, t, 0)
    real_map = lambda b, t: (b, jnp.maximum(t - 1, 0), 0)
    real_map_t = lambda b, t: (b, 0, jnp.maximum(t - 1, 0))
    out_shapes = (
        jax.ShapeDtypeStruct((bsz, seq, d), _f32),
        jax.ShapeDtypeStruct((bsz, lp, GLA_QK), _bf16),
        jax.ShapeDtypeStruct((bsz, lp, GLA_QK), _bf16),
        jax.ShapeDtypeStruct((bsz, lp, GLA_QK), _bf16),
        jax.ShapeDtypeStruct((bsz, lp, GLA_VW), _bf16),
        jax.ShapeDtypeStruct((bsz * nt, TILE // GLA_CHUNK, GLA_QK), _f32),
        jax.ShapeDtypeStruct((bsz, seq, GLA_VW), _bf16),
        jax.ShapeDtypeStruct((bsz, MLA_HEADS * MLA_QDIM, seq), _bf16),
        jax.ShapeDtypeStruct((bsz, lp, MLA_HEADS * HEAD_PAD), _bf16),
        jax.ShapeDtypeStruct((bsz, MLA_HEADS * V_AUG, lp), _bf16),
        jax.ShapeDtypeStruct((bsz, seq, d), _bf16),
        jax.ShapeDtypeStruct((bsz, seq, d), _bf16),
    )
    out_specs = (
        pl.BlockSpec((1, TILE, d), real_map),
        pl.BlockSpec((1, TILE, GLA_QK), pad_map),
        pl.BlockSpec((1, TILE, GLA_QK), pad_map),
        pl.BlockSpec((1, TILE, GLA_QK), pad_map),
        pl.BlockSpec((1, TILE, GLA_VW), pad_map),
        pl.BlockSpec((1, TILE // GLA_CHUNK, GLA_QK), lambda b, t: (b * nt + t, 0, 0)),
        pl.BlockSpec((1, TILE, GLA_VW), real_map),
        pl.BlockSpec((1, MLA_HEADS * MLA_QDIM, TILE), real_map_t),
        pl.BlockSpec((1, TILE, MLA_HEADS * HEAD_PAD), pad_map),
        pl.BlockSpec((1, MLA_HEADS * V_AUG, TILE), lambda b, t: (b, 0, t)),
        pl.BlockSpec((1, TILE, d), real_map),
        pl.BlockSpec((1, TILE, d), real_map),
    )
    tab_spec = pl.BlockSpec((TILE, LANES), lambda b, t: (t, 0))
    tabt_spec = pl.BlockSpec((MLA_ROPE // 2, TILE), lambda b, t: (0, t))
    (s_emb, qt, kt, ke, gv, dec, sr, qm, km, vm, gate_a, gate_b) = pl.pallas_call(
        _inproj_kernel,
        grid=(bsz, nt),
        in_specs=[
            pl.BlockSpec((1, TILE, d), real_map),
            _const_spec((TILE, d)),
            _const_spec((1, d)), _const_spec((1, d)),
            _const_spec((d, _W_COLS)),
            _const_spec((LANES, GLA_QK)), _const_spec((1, GLA_QK)),
            _const_spec((1, MLA_Q_RANK)), _const_spec((MLA_HEADS * MLA_QDIM, MLA_Q_RANK)),
            _const_spec((1, MLA_KV_RANK)), _const_spec((MLA_KV_RANK, MLA_HEADS * HEAD_PAD)),
            _const_spec((MLA_HEADS * MLA_DV, MLA_KV_RANK)),
            tab_spec, tab_spec, tab_spec, tabt_spec, tabt_spec,
            _const_spec((TILE, TILE)),
        ],
        out_specs=out_specs,
        out_shape=out_shapes,
        compiler_params=pltpu.CompilerParams(
            dimension_semantics=("arbitrary", "arbitrary"), vmem_limit_bytes=VMEM_LIMIT),
        name="inproj",
    )(x, head_tile, row2(ln_emb_g), row2(ln_emb_b), w_all, w2p, row2(gla_gate_b[0]),
      row2(mla_q_norm_g[0]), wuqt, row2(mla_kv_norm_g[0]), wuk, wuvt, cos_t, sin_lo, sin_hi,
      cos_tr, sin_tr, tril_chunks)

    o_gla = pl.pallas_call(
        _gla_kernel,
        grid=(nt,),
        in_specs=[
            pl.BlockSpec((bsz, TILE, GLA_QK), lambda t: (0, t, 0)),
            pl.BlockSpec((bsz, TILE, GLA_QK), lambda t: (0, t, 0)),
            pl.BlockSpec((bsz, TILE, GLA_QK), lambda t: (0, t, 0)),
            pl.BlockSpec((bsz, TILE, GLA_VW), lambda t: (0, t, 0)),
            pl.BlockSpec((bsz, 1, TILE // GLA_CHUNK, GLA_QK), lambda t: (0, t, 0, 0)),
            pl.BlockSpec((bsz, TILE, GLA_VW), lambda t: (0, jnp.maximum(t - 1, 0), 0)),
            _const_spec((1, GLA_DV)),
        ],
        out_specs=pl.BlockSpec((bsz, TILE, GLA_VW), lambda t: (0, jnp.maximum(t - 1, 0), 0)),
        out_shape=jax.ShapeDtypeStruct((bsz, seq, GLA_VW), _bf16),
        scratch_shapes=[pltpu.VMEM((bsz, GLA_HEADS, GLA_DV, GLA_DK), _f32)],
        compiler_params=pltpu.CompilerParams(
            dimension_semantics=("arbitrary",), vmem_limit_bytes=VMEM_LIMIT),
        name="gla",
    )(qt, kt, ke, gv, dec.reshape(bsz, nt, TILE // GLA_CHUNK, GLA_QK), sr, row2(gla_norm_g[0]))

    group_width = MLA_GROUP * HEAD_PAD
    o_mla = pl.pallas_call(
        _mla_kernel,
        grid=(bsz, MLA_HEADS // MLA_GROUP, seq // Q_TILE),
        in_specs=[
            pl.BlockSpec((1, MLA_GROUP * MLA_QDIM, Q_TILE), lambda b, hp, i: (b, hp, i)),
            pl.BlockSpec((1, lp, group_width), lambda b, hp, i: (b, 0, hp)),
            pl.BlockSpec((1, MLA_GROUP * V_AUG, lp), lambda b, hp, i: (b, hp, 0)),
        ],
        out_specs=pl.BlockSpec((1, Q_TILE, MLA_GROUP * MLA_DV), lambda b, hp, i: (b, i, hp)),
        out_shape=jax.ShapeDtypeStruct((bsz, seq, MLA_HEADS * MLA_DV), _bf16),
        scratch_shapes=[pltpu.VMEM((MLA_GROUP, V_AUG, Q_TILE), _f32),
                        pltpu.VMEM((MLA_GROUP, 1, LANES), _f32)],
        compiler_params=pltpu.CompilerParams(
            dimension_semantics=("arbitrary", "arbitrary", "arbitrary"),
            vmem_limit_bytes=VMEM_LIMIT),
        name="mla",
    )(qm, km, vm)

    rw = jnp.concatenate([router_group_w[0], router_expert_w[0]], axis=1)
    rw = jnp.pad(rw, ((0, 0), (0, LANES - rw.shape[1])))
    rwh = rw.astype(_bf16)
    rwl = (rw - rwh.astype(_f32)).astype(_bf16)
    rb = jnp.concatenate([router_group_b[0], router_expert_b[0]])
    rb = jnp.pad(rb, (0, LANES - rb.shape[0])).reshape(1, LANES)
    mi = np.arange(ROUTE_BLOCK)
    tril_strict = jnp.asarray(mi[:, None] > mi[None, :], dtype=_bf16)
    flat = lambda a: a.reshape(ntok, a.shape[-1])
    tok_spec = lambda w: pl.BlockSpec((MERGE_TILE, w), lambda g: (g, 0))
    tt_spec = lambda n, index_map: pl.BlockSpec((n * SUBLANES, LANES), index_map)
    s2, info, info_t, cnt = pl.pallas_call(
        _merge_kernel,
        grid=(ntok // MERGE_TILE,),
        in_specs=[tok_spec(d), tok_spec(d), tok_spec(d), tok_spec(d), tok_spec(d),
                  _const_spec((d, d)), _const_spec((d, d)), _const_spec((d, d)),
                  _const_spec((1, d)), _const_spec((1, d)),
                  _const_spec((d, LANES)), _const_spec((d, LANES)), _const_spec((1, LANES)),
                  _const_spec((ROUTE_BLOCK, ROUTE_BLOCK))],
        out_specs=(tt_spec(MERGE_TILE, lambda g: (g, 0)), tok_spec(LANES),
                   pl.BlockSpec((SUBLANES, MERGE_TILE), lambda g: (0, g)), _const_spec((8, LANES))),
        out_shape=(jax.ShapeDtypeStruct((ntok * SUBLANES, LANES), _f32),
                   jax.ShapeDtypeStruct((ntok, LANES), _f32),
                   jax.ShapeDtypeStruct((SUBLANES, ntok), _f32),
                   jax.ShapeDtypeStruct((8, LANES), _f32)),
        scratch_shapes=[pltpu.VMEM((8, LANES), _f32)],
        compiler_params=pltpu.CompilerParams(
            dimension_semantics=("arbitrary",), vmem_limit_bytes=VMEM_LIMIT),
        name="merge_router",
    )(flat(o_gla), flat(o_mla), flat(gate_a), flat(gate_b), flat(s_emb),
      w_branch_gla[0].astype(_bf16), w_branch_mla[0].astype(_bf16), w_out[0].astype(_bf16),
      row2(ln_mix_g[0]), row2(ln_mix_b[0]), rwh, rwl, rb, tril_strict)

    n_tiles = (2 * ntok + N_EXPERTS * (EXPERT_TILE - 1)) // EXPERT_TILE
    n_rows = n_tiles * EXPERT_TILE
    e_idx = info_t[0:2].astype(jnp.int32)
    rank = info_t[4:6].astype(jnp.int32)
    counts = cnt[0, :N_EXPERTS].astype(jnp.int32)
    padded = ((counts + EXPERT_TILE - 1) // EXPERT_TILE) * EXPERT_TILE
    ends = jnp.cumsum(padded)
    starts = ends - padded
    expert_ids = jnp.arange(N_EXPERTS, dtype=jnp.int32)[:, None, None]
    start_of = jnp.sum(jnp.where(e_idx[None] == expert_ids, starts[:, None, None], 0), axis=0)
    pos = (start_of + rank).reshape(-1)
    tile_start = jnp.arange(n_tiles, dtype=jnp.int32) * EXPERT_TILE
    tile_expert = jnp.minimum(
        jnp.sum((ends[None, :] <= tile_start[:, None]).astype(jnp.int32), axis=1), N_EXPERTS - 1)
    n_used = (ends[-1:] // EXPERT_TILE).astype(jnp.int32)
    used_blocks = (ends[-1:] // ZERO_BLOCK).astype(jnp.int32)
    pad_rows = padded - counts
    zero_row = jnp.concatenate([
        jnp.where(pad_rows > j * ZERO_BLOCK, ends - (j + 1) * ZERO_BLOCK, -1)
        for j in range(EXPERT_TILE // ZERO_BLOCK)]).astype(jnp.int32)

    any_spec = pl.BlockSpec(memory_space=pl.ANY)
    xs = pl.pallas_call(
        _dispatch_kernel,
        grid_spec=pltpu.PrefetchScalarGridSpec(
            num_scalar_prefetch=3,
            grid=(ntok // DISPATCH_TILE,),
            in_specs=[tt_spec(DISPATCH_TILE, lambda g, p, z, nu: (g, 0))],
            out_specs=any_spec,
            scratch_shapes=[pltpu.VMEM((ZERO_BLOCK * SUBLANES, LANES), _f32),
                            pltpu.SemaphoreType.DMA((2,))],
        ),
        out_shape=jax.ShapeDtypeStruct((n_rows * SUBLANES, LANES), _f32),
        compiler_params=pltpu.CompilerParams(
            dimension_semantics=("arbitrary",), vmem_limit_bytes=VMEM_LIMIT, has_side_effects=True),
        name="dispatch",
    )(pos, zero_row, used_blocks, s2)

    ff = EXPERT_FF
    wg = expert_w_gate[0].reshape(N_EXPERTS, d, ff)
    wu = expert_w_up[0].reshape(N_EXPERTS, d, ff)
    wd = expert_w_down[0].reshape(N_EXPERTS, ff, d)
    ys = pl.pallas_call(
        _expert_kernel,
        grid_spec=pltpu.PrefetchScalarGridSpec(
            num_scalar_prefetch=2,
            grid=(n_tiles,),
            in_specs=[
                any_spec,
                pl.BlockSpec((1, d, ff), lambda u, te, nu: (te[u], 0, 0)),
                pl.BlockSpec((1, d, ff), lambda u, te, nu: (te[u], 0, 0)),
                pl.BlockSpec((1, ff, d), lambda u, te, nu: (te[u], 0, 0)),
            ],
            out_specs=tt_spec(EXPERT_TILE, lambda u, te, nu: (u, 0)),
            scratch_shapes=[pltpu.VMEM((d, ff), _bf16), pltpu.VMEM((d, ff), _bf16),
                            pltpu.VMEM((ff, d), _bf16),
                            pltpu.VMEM((EXPERT_SLOTS, EXPERT_TILE * SUBLANES, LANES), _f32),
                            pltpu.SemaphoreType.DMA((EXPERT_SLOTS,))],
        ),
        out_shape=jax.ShapeDtypeStruct((n_rows * SUBLANES, LANES), _f32),
        compiler_params=pltpu.CompilerParams(
            dimension_semantics=("arbitrary",), vmem_limit_bytes=VMEM_LIMIT),
        name="experts",
    )(tile_expert, n_used, xs, wg, wu, wd)

    out = pl.pallas_call(
        _combine_kernel,
        grid_spec=pltpu.PrefetchScalarGridSpec(
            num_scalar_prefetch=1,
            grid=(ntok // COMBINE_TILE,),
            in_specs=[
                tt_spec(COMBINE_TILE, lambda g, p: (g, 0)),
                pl.BlockSpec((COMBINE_TILE, LANES), lambda g, p: (g, 0)),
                any_spec,
                pl.BlockSpec((1, d), lambda g, p: (0, 0)),
                pl.BlockSpec((1, d), lambda g, p: (0, 0)),
            ],
            out_specs=pl.BlockSpec((COMBINE_TILE, d), lambda g, p: (g, 0)),
            scratch_shapes=[pltpu.VMEM((COMBINE_SLOTS, 2, COMBINE_TILE * SUBLANES, LANES), _f32),
                            pltpu.SemaphoreType.DMA((COMBINE_SLOTS,))],
        ),
        out_shape=jax.ShapeDtypeStruct((ntok, d), _f32),
        compiler_params=pltpu.CompilerParams(
            dimension_semantics=("arbitrary",), vmem_limit_bytes=VMEM_LIMIT),
        name="combine_ln",
    )(pos, s2, info, ys, row2(ln_ffn_g[0]), row2(ln_ffn_b[0]))
    return out.reshape(bsz, seq, d)
```

```python
import functools

import jax
import jax.numpy as jnp
import numpy as np
from jax import lax
from jax.experimental import pallas as pl
from jax.experimental.pallas import tpu as pltpu

D_MODEL = 1024
N_META = 16
GLA_HEADS = 4
GLA_DK = 128
GLA_DV = 256
GLA_QK = GLA_HEADS * GLA_DK
GLA_VW = GLA_HEADS * GLA_DV
GLA_GATE_RANK = 16
GLA_GATE_TAU = 16.0
GLA_CHUNK = 64
MLA_HEADS = 16
MLA_Q_RANK = 384
MLA_KV_RANK = 256
MLA_NOPE = 64
MLA_ROPE = 32
MLA_DV = 64
MLA_QDIM = MLA_NOPE + MLA_ROPE
ROPE_BASE = 10000.0
N_GROUPS = 4
EXPERTS_PER_GROUP = 8
N_EXPERTS = N_GROUPS * EXPERTS_PER_GROUP
EXPERT_FF = 256
DEPTH = 1
ALPHA = (2.0 * DEPTH) ** 0.25
LN_EPS = 1e-5
RMS_EPS = 1e-6

LANES = 128
SUBLANES = 8
MXU_DIM = 256
TILE = 256
PADL = TILE - N_META
HEAD_PAD = LANES
BF16_ROWS = 16
KV_TILE = 512
Q_TILE = 2 * KV_TILE
V_AUG = MLA_DV + BF16_ROWS
MERGE_TILE = 1024
ROUTE_BLOCK = 256
EXPERT_TILE = 512
EXPERT_SLOTS = 3
ZERO_BLOCK = 256
DISPATCH_TILE = 2048
DISPATCH_SLOTS = 3
COMBINE_TILE = 256
COMBINE_SLOTS = 3
NEG = -1e30
LOG2E = 1.4426950408889634
BOUND_SLACK = 1.02
MIN_SOFTMAX_SUM = 2.0 ** -100
VMEM_LIMIT = 56 * 1024 * 1024

_C_Q, _C_K, _C_V, _C_R = 0, 512, 1024, 2048
_C_A = 3072
_C_CQ = _C_A + LANES
_C_CKV = _C_CQ + MLA_Q_RANK
_C_KR = _C_CKV + MLA_KV_RANK
_C_GA = _C_KR + LANES
_C_GB = _C_GA + D_MODEL
_W_COLS = _C_GB + D_MODEL

_f32 = jnp.float32
_bf16 = jnp.bfloat16


def _dot(a, b):
    return jnp.dot(a, b, preferred_element_type=_f32)


def _dot_nt(a, b):
    return lax.dot_general(a, b, (((1,), (1,)), ((), ())), preferred_element_type=_f32)


def _dot_tn(a, b):
    return lax.dot_general(a, b, (((0,), (0,)), ((), ())), preferred_element_type=_f32)


def _layer_norm(x, g, b):
    mu = jnp.mean(x, axis=-1, keepdims=True)
    xc = x - mu
    var = jnp.mean(xc * xc, axis=-1, keepdims=True)
    return xc * lax.rsqrt(var + LN_EPS) * g + b


def _rms_norm(x, g):
    ms = jnp.mean(x * x, axis=-1, keepdims=True)
    return x * lax.rsqrt(ms + RMS_EPS) * g


def _sigmoid(x):
    return 1.0 / (1.0 + jnp.exp(-x))


def _tt_load(ref, n, lead=()):
    return jnp.concatenate(
        [ref[lead + (pl.ds(a, n, stride=SUBLANES), slice(None))] for a in range(SUBLANES)], axis=1)


def _tt_store(ref, x):
    n = x.shape[0]
    for a in range(SUBLANES):
        ref[pl.ds(a, n, stride=SUBLANES), :] = x[:, a * LANES:(a + 1) * LANES]


def _tt_rows(tok):
    return pl.ds(pl.multiple_of(tok * SUBLANES, SUBLANES), SUBLANES)


def _rope(x, cos, sin_lo, sin_hi):
    half = MLA_ROPE // 2
    from_hi = pltpu.roll(x, LANES - half, 1)
    from_lo = pltpu.roll(x, half, 1)
    return x * cos + from_hi * sin_lo + from_lo * sin_hi


def _inproj_kernel(x_ref, head_ref, lng_ref, lnb_ref, w_ref, w2_ref, gb_ref, qg_ref, wuqt_ref, kvg_ref,
                   wuk_ref, wuvt_ref, cos_ref, sl_ref, sh_ref, cost_ref, sint_ref, tril_ref,
                   s_ref, qt_ref, kt_ref, ke_ref, gv_ref, dec_ref, sr_ref, qm_ref, km_ref,
                   vm_ref, ga_ref, gbt_ref):
    t = pl.program_id(1)
    x_in = jnp.where(t == 0, head_ref[...], x_ref[0])
    sn = _layer_norm(x_in, lng_ref[...], lnb_ref[...])
    s_ref[0] = sn
    snb = sn.astype(_bf16)
    row = t * TILE + lax.broadcasted_iota(jnp.int32, (TILE, 1), 0)
    valid = row >= PADL

    proj = lambda c0, width: _dot(snb, w_ref[:, c0:c0 + width])
    a_lr = proj(_C_A, LANES)
    cq = proj(_C_CQ, MLA_Q_RANK)
    ckv = proj(_C_CKV, MLA_KV_RANK)
    kr_raw = proj(_C_KR, LANES)
    gq = proj(_C_Q, GLA_QK)
    gk = proj(_C_K, GLA_QK)
    gv_ref[0] = jnp.where(valid, proj(_C_V, GLA_VW), 0.0).astype(_bf16)
    r = proj(_C_R, GLA_VW)
    sr_ref[0] = (r * _sigmoid(r)).astype(_bf16)
    ga_ref[0] = _sigmoid(proj(_C_GA, D_MODEL)).astype(_bf16)
    gbt_ref[0] = _sigmoid(proj(_C_GB, D_MODEL)).astype(_bf16)

    z = _dot(a_lr.astype(_bf16), w2_ref[...]) + gb_ref[...]
    cqn = _rms_norm(cq, qg_ref[...]).astype(_bf16)
    ckvn = _rms_norm(ckv, kvg_ref[...]).astype(_bf16)
    qft = _dot_nt(wuqt_ref[...], cqn)
    kf = _dot(ckvn, wuk_ref[...])
    vt = _dot_nt(wuvt_ref[...], ckvn)

    la = (jnp.minimum(z, 0.0) - jnp.log1p(jnp.exp(-jnp.abs(z)))) * (1.0 / GLA_GATE_TAU)
    la = jnp.where(valid, la, 0.0)
    hi = la.astype(_bf16)
    r1 = la - hi.astype(_f32)
    mid = r1.astype(_bf16)
    lo = (r1 - mid.astype(_f32)).astype(_bf16)
    tril = tril_ref[...]
    bc = _dot(tril, hi) + _dot(tril, mid) + _dot(tril, lo)
    n_chunks = TILE // GLA_CHUNK
    lasts = [bc[c * GLA_CHUNK + GLA_CHUNK - 1:(c + 1) * GLA_CHUNK, :] for c in range(n_chunks)]
    for c in range(n_chunks):
        dec_ref[0, c:c + 1, :] = jnp.exp(lasts[c])
    b_last = jnp.concatenate(
        [jnp.broadcast_to(l, (GLA_CHUNK, GLA_QK)) for l in lasts], axis=0)
    gk = jnp.where(valid, gk, 0.0)
    qt_ref[0] = (gq * (GLA_DK ** -0.5) * jnp.exp(bc)).astype(_bf16)
    kt_ref[0] = (gk * jnp.exp(-bc)).astype(_bf16)
    ke_ref[0] = (gk * jnp.exp(b_last - bc)).astype(_bf16)

    cos = cos_ref[...]
    sl = sl_ref[...]
    sh = sh_ref[...]
    scale = (MLA_QDIM ** -0.5) * LOG2E
    cost = cost_ref[...]
    sint = sint_ref[...]
    half = MLA_ROPE // 2
    for h in range(MLA_HEADS):
        base = h * MLA_QDIM
        x1 = qft[base + MLA_NOPE:base + MLA_NOPE + half]
        x2 = qft[base + MLA_NOPE + half:base + MLA_QDIM]
        qm_ref[0, base:base + MLA_NOPE] = (qft[base:base + MLA_NOPE] * scale).astype(_bf16)
        qm_ref[0, base + MLA_NOPE:base + MLA_NOPE + half] = ((x1 * cost - x2 * sint) * scale).astype(_bf16)
        qm_ref[0, base + MLA_NOPE + half:base + MLA_QDIM] = ((x1 * sint + x2 * cost) * scale).astype(_bf16)
    kr = _rope(kr_raw, cos, sl, sh)
    for h in range(MLA_HEADS):
        km_ref[0, :, h * HEAD_PAD:(h + 1) * HEAD_PAD] = (
            kf[:, h * HEAD_PAD:(h + 1) * HEAD_PAD] + kr).astype(_bf16)
    for h in range(MLA_HEADS):
        vm_ref[0, h * V_AUG:h * V_AUG + MLA_DV] = vt[h * MLA_DV:(h + 1) * MLA_DV].astype(_bf16)
        vm_ref[0, h * V_AUG + MLA_DV:(h + 1) * V_AUG] = jnp.ones((V_AUG - MLA_DV, TILE), _bf16)


def _gla_kernel(qt_ref, kt_ref, ke_ref, gv_ref, dec_ref, sr_ref, ng_ref, o_ref, st_ref):
    t = pl.program_id(0)
    n_batch = qt_ref.shape[0]

    @pl.when(t == 0)
    def _():
        st_ref[...] = jnp.zeros_like(st_ref)

    ri = lax.broadcasted_iota(jnp.int32, (TILE, TILE), 0)
    ci = lax.broadcasted_iota(jnp.int32, (TILE, TILE), 1)
    visible = (ri >= ci) & (ri // GLA_CHUNK == ci // GLA_CHUNK)
    ng = ng_ref[...]
    n_chunks = TILE // GLA_CHUNK
    chunk_rows = [slice(c * GLA_CHUNK, (c + 1) * GLA_CHUNK) for c in range(n_chunks)]
    k_cols = [slice(h * GLA_DK, (h + 1) * GLA_DK) for h in range(GLA_HEADS)]
    v_cols = [slice(h * GLA_DV, (h + 1) * GLA_DV) for h in range(GLA_HEADS)]
    seqs = [(b, h) for b in range(n_batch) for h in range(GLA_HEADS)]
    att = {(b, h): jnp.where(visible, _dot_nt(qt_ref[b, :, k_cols[h]], kt_ref[b, :, k_cols[h]]), 0.0)
           for b, h in seqs}
    upd = {(b, h): [_dot_tn(gv_ref[b, rows, v_cols[h]], ke_ref[b, rows, k_cols[h]])
                    for rows in chunk_rows] for b, h in seqs}
    o_intra = {(b, h): _dot(att[b, h].astype(_bf16), gv_ref[b, :, v_cols[h]]) for b, h in seqs}
    for b, h in seqs:
        st = st_ref[b, h]
        for c, rows in enumerate(chunk_rows):
            o = o_intra[b, h][rows] + _dot_nt(qt_ref[b, rows, k_cols[h]], st.astype(_bf16))
            st = st * dec_ref[b, 0, c:c + 1, k_cols[h]] + upd[b, h][c]
            o = _rms_norm(o, ng) * sr_ref[b, rows, v_cols[h]].astype(_f32)
            o_ref[b, rows, v_cols[h]] = o.astype(_bf16)
        st_ref[b, h] = st


MLA_GROUP = 4


def _mla_tile_start(j):
    return pl.multiple_of(TILE + j * KV_TILE, TILE)


def _mla_queries(q_ref, h, cols):
    q = q_ref[0, h * MLA_QDIM:(h + 1) * MLA_QDIM, cols]
    return jnp.concatenate([q, jnp.zeros((HEAD_PAD - MLA_QDIM, q.shape[1]), q.dtype)], axis=0)


def _mla_finish(o_ref, acc_ref):
    outs = []
    for h in range(MLA_GROUP):
        a = acc_ref[h]
        outs.append(a[0:MLA_DV] / a[MLA_DV:MLA_DV + 1])
    o_ref[0] = jnp.concatenate(outs, axis=0).T.astype(_bf16)


def _mla_kernel(q_ref, k_ref, v_ref, o_ref, acc_ref, knorm_ref):
    i = pl.program_id(2)
    heads = MLA_GROUP
    ones = jnp.ones((HEAD_PAD, LANES), _bf16)

    @pl.when(i == 0)
    def _():
        for h in range(heads):
            kk = k_ref[0, :, h * HEAD_PAD:(h + 1) * HEAD_PAD].astype(_f32)
            hi = (kk * kk).astype(_bf16)
            knorm_ref[h] = jnp.max(_dot(hi, ones), axis=0, keepdims=True)

    def score_bound(h, q):
        qq = q.astype(_f32)
        qn2 = jnp.sum(qq * qq, axis=0, keepdims=True)
        return jnp.sqrt(qn2 * knorm_ref[h][:, 0:1]) * BOUND_SLACK

    def queries(cols):
        q = [_mla_queries(q_ref, h, cols) for h in range(heads)]
        return q, [score_bound(h, q[h]) for h in range(heads)]

    sub = KV_TILE // 2
    q_t, bound = queries(slice(None))
    q_lo, bound_lo = queries(slice(0, KV_TILE))
    q_hi, bound_hi = queries(slice(KV_TILE, Q_TILE))
    q_lo2, bound_lo2 = queries(slice(sub, KV_TILE))
    q_hi2, bound_hi2 = queries(slice(KV_TILE + sub, Q_TILE))

    def key_rows(start, n, h):
        return k_ref[0, pl.ds(pl.multiple_of(start, sub), n), h * HEAD_PAD:(h + 1) * HEAD_PAD]

    def value_cols(start, n, h):
        return v_ref[0, h * V_AUG:(h + 1) * V_AUG, pl.ds(pl.multiple_of(start, sub), n)]

    def keys(j, h):
        return key_rows(_mla_tile_start(j), KV_TILE, h)

    def weighted(j, h, p):
        return _dot(value_cols(_mla_tile_start(j), KV_TILE, h), p)

    def probs(s, b):
        return jnp.exp2(s - b).astype(_bf16)

    r0 = _mla_tile_start(2 * i)
    wide = (lax.broadcasted_iota(jnp.int32, (sub, KV_TILE), 0)
            <= lax.broadcasted_iota(jnp.int32, (sub, KV_TILE), 1))
    square = (lax.broadcasted_iota(jnp.int32, (sub, sub), 0)
              <= lax.broadcasted_iota(jnp.int32, (sub, sub), 1))
    s_meta = [_dot(k_ref[0, PADL:TILE, h * HEAD_PAD:(h + 1) * HEAD_PAD], q_t[h]) for h in range(heads)]
    s_full = [_dot(key_rows(r0, KV_TILE, h), q_hi[h]) for h in range(heads)]
    s_lo_a = [jnp.where(wide, _dot(key_rows(r0, sub, h), q_lo[h]), NEG) for h in range(heads)]
    s_lo_b = [jnp.where(square, _dot(key_rows(r0 + sub, sub, h), q_lo2[h]), NEG) for h in range(heads)]
    s_hi_a = [jnp.where(wide, _dot(key_rows(r0 + KV_TILE, sub, h), q_hi[h]), NEG) for h in range(heads)]
    s_hi_b = [jnp.where(square, _dot(key_rows(r0 + KV_TILE + sub, sub, h), q_hi2[h]), NEG)
              for h in range(heads)]
    for h in range(heads):
        p_meta = jnp.concatenate([jnp.zeros((PADL, Q_TILE), _bf16), probs(s_meta[h], bound[h])], axis=0)
        acc_ref[h] = _dot(v_ref[0, h * V_AUG:(h + 1) * V_AUG, 0:TILE], p_meta)
        acc_ref[h, :, 0:KV_TILE] += _dot(value_cols(r0, sub, h), probs(s_lo_a[h], bound_lo[h]))
        acc_ref[h, :, sub:KV_TILE] += _dot(value_cols(r0 + sub, sub, h), probs(s_lo_b[h], bound_lo2[h]))
        acc_ref[h, :, KV_TILE:Q_TILE] += (
            _dot(value_cols(r0, KV_TILE, h), probs(s_full[h], bound_hi[h]))
            + _dot(value_cols(r0 + KV_TILE, sub, h), probs(s_hi_a[h], bound_hi[h])))
        acc_ref[h, :, KV_TILE + sub:Q_TILE] += _dot(value_cols(r0 + KV_TILE + sub, sub, h),
                                                     probs(s_hi_b[h], bound_hi2[h]))

    def body(jj, c):
        tiles = (2 * jj, 2 * jj + 1)
        s = [[_dot(keys(j, h), q_t[h]) for h in range(heads)] for j in tiles]
        for h in range(heads):
            acc_ref[h] += sum(weighted(j, h, jnp.exp2(s[n][h] - bound[h]).astype(_bf16))
                              for n, j in enumerate(tiles))
        return c

    lax.fori_loop(0, i, body, 0)
    _mla_finish(o_ref, acc_ref)

    l_min = functools.reduce(jnp.minimum,
                             [jnp.min(acc_ref[h, MLA_DV:MLA_DV + 1, :]) for h in range(heads)])

    @pl.when(jnp.logical_not(l_min >= MIN_SOFTMAX_SUM))
    def _():
        _mla_exact(q_ref, k_ref, v_ref, o_ref, acc_ref)


def _mla_exact(q_ref, k_ref, v_ref, o_ref, acc_ref):
    i = pl.program_id(2)
    heads = MLA_GROUP
    q_t = [_mla_queries(q_ref, h, slice(None)) for h in range(heads)]

    ms = []
    for h in range(heads):
        kb = k_ref[0, PADL:TILE, h * HEAD_PAD:(h + 1) * HEAD_PAD]
        s = _dot(kb, q_t[h])
        m0 = jnp.max(s, axis=0, keepdims=True)
        p = jnp.concatenate([jnp.zeros((PADL, Q_TILE), _bf16), jnp.exp2(s - m0).astype(_bf16)], axis=0)
        acc_ref[h] = _dot(v_ref[0, h * V_AUG:(h + 1) * V_AUG, 0:TILE], p)
        ms.append(m0)

    k_row = lax.broadcasted_iota(jnp.int32, (KV_TILE, Q_TILE), 0)
    q_col = lax.broadcasted_iota(jnp.int32, (KV_TILE, Q_TILE), 1)

    def body(j, ms):
        visible = (j - 2 * i) * KV_TILE + k_row <= q_col
        out = []
        for h in range(heads):
            s = _dot(k_ref[0, pl.ds(_mla_tile_start(j), KV_TILE), h * HEAD_PAD:(h + 1) * HEAD_PAD],
                     q_t[h])
            s = jnp.where(visible, s, NEG)
            vb = v_ref[0, h * V_AUG:(h + 1) * V_AUG, pl.ds(_mla_tile_start(j), KV_TILE)]
            m_new = jnp.maximum(ms[h], jnp.max(s, axis=0, keepdims=True))
            alpha = jnp.exp2(ms[h] - m_new)
            acc_ref[h] = alpha * acc_ref[h] + _dot(vb, jnp.exp2(s - m_new).astype(_bf16))
            out.append(m_new)
        return tuple(out)

    lax.fori_loop(0, 2 * i + 2, body, tuple(ms))
    _mla_finish(o_ref, acc_ref)


def _merge_kernel(og_ref, om_ref, ga_ref, gbt_ref, s_ref, wbg_ref, wbm_ref, wo_ref, lng_ref,
                  lnb_ref, rwh_ref, rwl_ref, rb_ref, tril_ref,
                  s2_ref, info_ref, infot_ref, cnt_ref, carry_ref):
    step = pl.program_id(0)

    @pl.when(step == 0)
    def _():
        carry_ref[...] = jnp.zeros_like(carry_ref)

    n_blk = MERGE_TILE // ROUTE_BLOCK
    blocks = [slice(i * ROUTE_BLOCK, (i + 1) * ROUTE_BLOCK) for i in range(n_blk)]
    d = wo_ref.shape[0]
    col_blocks = [slice(c, c + MXU_DIM) for c in range(0, d, MXU_DIM)]
    lane = lax.broadcasted_iota(jnp.int32, (ROUTE_BLOCK, LANES), 1)
    is_g = lane < N_GROUPS
    merged, s2, logits, infos = {}, {}, {}, {}
    carry = [carry_ref[0:1, :]]

    def branches(i):
        rows = blocks[i]
        merged[i] = jnp.concatenate(
            [(ga_ref[rows, cols].astype(_f32) * _dot(og_ref[rows, :], wbg_ref[:, cols])
              + gbt_ref[rows, cols].astype(_f32) * _dot(om_ref[rows, :], wbm_ref[:, cols])
              ).astype(_bf16) for cols in col_blocks], axis=1)

    def residual_norm(i):
        y = ALPHA * s_ref[blocks[i], :] + _dot(merged[i], wo_ref[...])
        s2[i] = _layer_norm(y, lng_ref[...], lnb_ref[...])

    def router_logits(i):
        xh = s2[i].astype(_bf16)
        xl = (s2[i] - xh.astype(_f32)).astype(_bf16)
        logits[i] = (_dot(xh, rwh_ref[...]) + _dot(xl, rwh_ref[...]) + _dot(xh, rwl_ref[...])
                     + rb_ref[...])

    def route(i):
        gl = jnp.where(is_g, logits[i], NEG)
        gmax = jnp.max(gl, axis=-1, keepdims=True)
        gidx = jnp.min(jnp.where(gl == gmax, lane, LANES), axis=-1, keepdims=True)
        p_g = 1.0 / jnp.sum(jnp.where(is_g, jnp.exp(gl - gmax), 0.0), axis=-1, keepdims=True)
        lo = N_GROUPS + EXPERTS_PER_GROUP * gidx
        el = jnp.where((lane >= lo) & (lane < lo + EXPERTS_PER_GROUP), logits[i], NEG)
        v1 = jnp.max(el, axis=-1, keepdims=True)
        i1 = jnp.min(jnp.where(el == v1, lane, LANES), axis=-1, keepdims=True)
        el2 = jnp.where(lane == i1, NEG, el)
        v2 = jnp.max(el2, axis=-1, keepdims=True)
        i2 = jnp.min(jnp.where(el2 == v2, lane, LANES), axis=-1, keepdims=True)
        tt = jnp.exp(v2 - v1)
        p1 = 1.0 / (1.0 + tt)
        p2 = tt / (1.0 + tt)
        e1 = i1 - N_GROUPS
        e2 = i2 - N_GROUPS
        hit1 = lane == e1
        hit2 = lane == e2
        onehot = jnp.where(hit1 | hit2, 1.0, 0.0)
        before = _dot(tril_ref[...], onehot.astype(_bf16)) + carry[0]
        r1 = jnp.sum(jnp.where(hit1, before, 0.0), axis=-1, keepdims=True)
        r2 = jnp.sum(jnp.where(hit2, before, 0.0), axis=-1, keepdims=True)
        carry[0] = carry[0] + jnp.sum(onehot, axis=0, keepdims=True)
        infos[i] = jnp.where(lane == 0, e1.astype(_f32),
                   jnp.where(lane == 1, e2.astype(_f32),
                   jnp.where(lane == 2, p_g * p1,
                   jnp.where(lane == 3, p_g * p2,
                   jnp.where(lane == 4, r1,
                   jnp.where(lane == 5, r2, 0.0))))))

    for stage in (branches, residual_norm, router_logits, route):
        for i in range(n_blk):
            stage(i)
    _tt_store(s2_ref, jnp.concatenate([s2[i] for i in range(n_blk)], axis=0))
    carry_ref[...] = jnp.broadcast_to(carry[0], carry_ref.shape)
    cnt_ref[...] = jnp.broadcast_to(carry[0], cnt_ref.shape)
    info = jnp.concatenate([infos[i] for i in range(n_blk)], axis=0)
    info_ref[...] = info
    infot_ref[...] = info.T[0:SUBLANES]


def _dispatch_kernel(pos_ref, zrow_ref, nused_ref, s2_hbm, xs_hbm, zero_ref, in_ref, sems):
    g = pl.program_id(0)
    last = pl.num_programs(0) - 1
    zero_sem = sems.at[0]
    tile_rows = DISPATCH_TILE * SUBLANES

    def in_copy(tile, slot):
        start = pl.multiple_of(tile * tile_rows, tile_rows)
        return pltpu.make_async_copy(s2_hbm.at[pl.ds(start, tile_rows)], in_ref.at[slot],
                                     sems.at[1 + slot])

    def wait_rows(parity):
        for k in range(2):
            pltpu.make_async_copy(in_ref.at[0], xs_hbm.at[pl.ds(0, tile_rows)],
                                  sems.at[1 + DISPATCH_SLOTS + parity]).wait()

    @pl.when(g == 0)
    def _():
        for tile in range(DISPATCH_SLOTS - 1):
            in_copy(tile, tile).start()

    block_rows = ZERO_BLOCK * SUBLANES

    def zero_copy(row):
        start = pl.multiple_of(row * SUBLANES, block_rows)
        return pltpu.make_async_copy(zero_ref, xs_hbm.at[pl.ds(start, block_rows)], zero_sem)

    @pl.when(g == 0)
    def _():
        zero_ref[...] = jnp.zeros_like(zero_ref)

        def start(e, c):
            @pl.when(zrow_ref[e] >= 0)
            def _():
                zero_copy(zrow_ref[e]).start()
            return c

        def wait(e, c):
            @pl.when(zrow_ref[e] >= 0)
            def _():
                zero_copy(0).wait()
            return c

        def start_tail(u, c):
            zero_copy(u * ZERO_BLOCK).start()
            return c

        def wait_tail(u, c):
            zero_copy(0).wait()
            return c

        n_blocks = xs_hbm.shape[0] // block_rows
        lax.fori_loop(0, zrow_ref.shape[0], start, 0)
        lax.fori_loop(nused_ref[0], n_blocks, start_tail, 0)
        lax.fori_loop(0, zrow_ref.shape[0], wait, 0)
        lax.fori_loop(nused_ref[0], n_blocks, wait_tail, 0)

    n_tok = pl.num_programs(0) * DISPATCH_TILE

    slot = g % DISPATCH_SLOTS
    in_copy(g, slot).wait()
    row_sem = sems.at[1 + DISPATCH_SLOTS + g % 2]

    def issue(r, c):
        tok = g * DISPATCH_TILE + r
        for k in range(2):
            pltpu.make_async_copy(in_ref.at[slot, _tt_rows(r)],
                                  xs_hbm.at[_tt_rows(pos_ref[k * n_tok + tok])], row_sem).start(priority=k)
        return c

    lax.fori_loop(0, DISPATCH_TILE, issue, 0, unroll=8)

    @pl.when(g > 0)
    def _():
        wait_rows((g - 1) % 2)

    ahead = DISPATCH_SLOTS - 1

    @pl.when(g + ahead <= last)
    def _():
        in_copy(g + ahead, (g + ahead) % DISPATCH_SLOTS).start()

    @pl.when(g == last)
    def _():
        wait_rows(g % 2)


def _expert_kernel(te_ref, nused_ref, xs_hbm, wg_ref, wu_ref, wd_ref, o_ref, wgb_ref, wub_ref, wdb_ref,
                   x_ref, sems):
    u = pl.program_id(0)
    last = pl.num_programs(0) - 1
    used = u < nused_ref[0]
    first_of_expert = jnp.logical_or(u == 0, te_ref[u] != te_ref[jnp.maximum(u - 1, 0)])
    tile_rows = EXPERT_TILE * SUBLANES
    ahead = EXPERT_SLOTS - 1

    def tile_copy(tile, slot):
        start = pl.multiple_of(jnp.minimum(tile, nused_ref[0] - 1) * tile_rows, tile_rows)
        return pltpu.make_async_copy(xs_hbm.at[pl.ds(start, tile_rows)], x_ref.at[slot], sems.at[slot])

    @pl.when(u == 0)
    def _():
        for tile in range(ahead):
            tile_copy(tile, tile).start(priority=1)

    tile_copy(jnp.minimum(u + ahead, last), (u + ahead) % EXPERT_SLOTS).start(priority=1)
    slot = u % EXPERT_SLOTS
    tile_copy(u, slot).wait()

    @pl.when(jnp.logical_and(used, first_of_expert))
    def _():
        wgb_ref[...] = wg_ref[0].astype(_bf16)
        wub_ref[...] = wu_ref[0].astype(_bf16)
        wdb_ref[...] = wd_ref[0].astype(_bf16)

    @pl.when(used)
    def _():
        x = _tt_load(x_ref, EXPERT_TILE, (slot,)).astype(_bf16)
        a = _dot(x, wgb_ref[...])
        up = _dot(x, wub_ref[...])
        hid = a * _sigmoid(a) * up
        _tt_store(o_ref, _dot(hid.astype(_bf16), wdb_ref[...]))

    @pl.when(u >= nused_ref[0])
    def _():
        o_ref[...] = jnp.zeros_like(o_ref)

    @pl.when(u == last)
    def _():
        for n in range(1, EXPERT_SLOTS):
            tile_copy(last, (u + n) % EXPERT_SLOTS).wait()


def _combine_kernel(pos_ref, s2_ref, info_ref, ys_hbm, lng_ref, lnb_ref, o_ref, buf_ref, sems):
    g = pl.program_id(0)
    last = pl.num_programs(0) - 1
    n_tok = pl.num_programs(0) * COMBINE_TILE

    def start_copies(tile, slot, r):
        tok = tile * COMBINE_TILE + r
        for k in range(2):
            pltpu.make_async_copy(ys_hbm.at[_tt_rows(pos_ref[k * n_tok + tok])],
                                  buf_ref.at[slot, k, _tt_rows(r)], sems.at[slot]).start(priority=k)

    def wait_slot(slot):
        for k in range(2):
            pltpu.make_async_copy(ys_hbm.at[pl.ds(0, COMBINE_TILE * SUBLANES)], buf_ref.at[slot, k],
                                  sems.at[slot]).wait()

    @pl.when(g == 0)
    def _():
        def body(r, c):
            for tile in range(COMBINE_SLOTS - 1):
                start_copies(tile, tile, r)
            return c

        lax.fori_loop(0, COMBINE_TILE, body, 0, unroll=8)

    slot = g % COMBINE_SLOTS
    wait_slot(slot)
    info = info_ref[...]
    y = (ALPHA * _tt_load(s2_ref, COMBINE_TILE)
         + info[:, 2:3] * _tt_load(buf_ref, COMBINE_TILE, (slot, 0))
         + info[:, 3:4] * _tt_load(buf_ref, COMBINE_TILE, (slot, 1)))
    o_ref[...] = _layer_norm(y, lng_ref[...], lnb_ref[...])

    ahead = COMBINE_SLOTS - 1
    ahead_slot = (g + ahead) % COMBINE_SLOTS
    for r in range(COMBINE_TILE):
        start_copies(jnp.minimum(g + ahead, last), ahead_slot, r)

    @pl.when(g == last)
    def _():
        for n in range(1, COMBINE_SLOTS):
            wait_slot((g + n) % COMBINE_SLOTS)


def _const_spec(shape):
    nd = len(shape)
    return pl.BlockSpec(shape, lambda *_: (0,) * nd)


def _rope_tables(lp):
    pos = np.maximum(np.arange(lp, dtype=np.float64) - PADL, 0.0)
    inv_freq = ROPE_BASE ** (-np.arange(0, MLA_ROPE, 2, dtype=np.float64) / MLA_ROPE)
    ang = pos[:, None] * inv_freq[None, :]
    cos, sin = np.cos(ang), np.sin(ang)
    half = MLA_ROPE // 2
    ones = np.ones((lp, MLA_NOPE))
    zeros_n = np.zeros((lp, MLA_NOPE))
    zeros_h = np.zeros((lp, half))
    tail1 = np.ones((lp, LANES - MLA_QDIM))
    tail0 = np.zeros((lp, LANES - MLA_QDIM))
    cos_t = np.concatenate([ones, cos, cos, tail1], axis=1)
    sin_lo = np.concatenate([zeros_n, -sin, zeros_h, tail0], axis=1)
    sin_hi = np.concatenate([zeros_n, zeros_h, sin, tail0], axis=1)
    return tuple(jnp.asarray(np.ascontiguousarray(t), dtype=_f32)
                 for t in (cos_t, sin_lo, sin_hi, cos.T, sin.T))


def _pad_heads(w, width):
    k = w.shape[0]
    w = w.reshape(k, MLA_HEADS, width)
    w = jnp.pad(w, ((0, 0), (0, 0), (0, HEAD_PAD - width)))
    return w.reshape(k, MLA_HEADS * HEAD_PAD)


def kernel(x, meta_tokens, ln_emb_g, ln_emb_b, w_in, gla_gate_w2, gla_gate_b, gla_norm_g, mla_q_norm_g, mla_w_uq, mla_kv_norm_g, mla_w_uk, mla_w_uv, w_branch_gla, w_branch_mla, w_out, ln_mix_g, ln_mix_b, router_group_w, router_group_b, router_expert_w, router_expert_b, expert_w_gate, expert_w_up, expert_w_down, ln_ffn_g, ln_ffn_b):
    bsz, seq, d = x.shape
    assert d == D_MODEL and seq % Q_TILE == 0 and w_in.shape[0] == DEPTH == 1
    lp = PADL + N_META + seq
    nt = lp // TILE
    ntok = bsz * seq
    row2 = lambda v: v.reshape(1, -1).astype(_f32)

    head_tile = jnp.concatenate([jnp.zeros((PADL, d), _f32), meta_tokens.astype(_f32)], axis=0)
    wi = w_in[0]
    o_a = 2 * GLA_QK + 2 * GLA_VW
    o_cq = o_a + GLA_GATE_RANK
    o_ckv = o_cq + MLA_Q_RANK
    o_kr = o_ckv + MLA_KV_RANK
    o_ga = o_kr + MLA_ROPE
    w_a = jnp.pad(wi[:, o_a:o_cq], ((0, 0), (0, LANES - GLA_GATE_RANK)))
    w_kr = jnp.pad(wi[:, o_kr:o_ga], ((0, 0), (MLA_NOPE, LANES - MLA_QDIM)))
    w_all = jnp.concatenate([wi[:, :o_a], w_a, wi[:, o_cq:o_kr], w_kr, wi[:, o_ga:]], axis=1).astype(_bf16)
    assert w_all.shape == (d, _W_COLS)
    w2p = jnp.pad(gla_gate_w2[0], ((0, LANES - GLA_GATE_RANK), (0, 0))).astype(_bf16)
    wuqt = mla_w_uq[0].T.astype(_bf16)
    wuk = _pad_heads(mla_w_uk[0], MLA_NOPE).astype(_bf16)
    wuvt = mla_w_uv[0].T.astype(_bf16)
    cos_t, sin_lo, sin_hi, cos_tr, sin_tr = _rope_tables(lp)
    blk = np.arange(TILE)
    tril_chunks = jnp.asarray(
        ((blk[:, None] >= blk[None, :]) & (blk[:, None] // GLA_CHUNK == blk[None, :] // GLA_CHUNK)),
        dtype=_bf16)

    pad_map = lambda b, t: (b, t, 0)
    real_map = lambda b, t: (b, jnp.maximum(t - 1, 0), 0)
    real_map_t = lambda b, t: (b, 0, jnp.maximum(t - 1, 0))
    out_shapes = (
        jax.ShapeDtypeStruct((bsz, seq, d), _f32),
        jax.ShapeDtypeStruct((bsz, lp, GLA_QK), _bf16),
        jax.ShapeDtypeStruct((bsz, lp, GLA_QK), _bf16),
        jax.ShapeDtypeStruct((bsz, lp, GLA_QK), _bf16),
        jax.ShapeDtypeStruct((bsz, lp, GLA_VW), _bf16),
        jax.ShapeDtypeStruct((bsz * nt, TILE // GLA_CHUNK, GLA_QK), _f32),
        jax.ShapeDtypeStruct((bsz, seq, GLA_VW), _bf16),
        jax.ShapeDtypeStruct((bsz, MLA_HEADS * MLA_QDIM, seq), _bf16),
        jax.ShapeDtypeStruct((bsz, lp, MLA_HEADS * HEAD_PAD), _bf16),
        jax.ShapeDtypeStruct((bsz, MLA_HEADS * V_AUG, lp), _bf16),
        jax.ShapeDtypeStruct((bsz, seq, d), _bf16),
        jax.ShapeDtypeStruct((bsz, seq, d), _bf16),
    )
    out_specs = (
        pl.BlockSpec((1, TILE, d), real_map),
        pl.BlockSpec((1, TILE, GLA_QK), pad_map),
        pl.BlockSpec((1, TILE, GLA_QK), pad_map),
        pl.BlockSpec((1, TILE, GLA_QK), pad_map),
        pl.BlockSpec((1, TILE, GLA_VW), pad_map),
        pl.BlockSpec((1, TILE // GLA_CHUNK, GLA_QK), lambda b, t: (b * nt + t, 0, 0)),
        pl.BlockSpec((1, TILE, GLA_VW), real_map),
        pl.BlockSpec((1, MLA_HEADS * MLA_QDIM, TILE), real_map_t),
        pl.BlockSpec((1, TILE, MLA_HEADS * HEAD_PAD), pad_map),
        pl.BlockSpec((1, MLA_HEADS * V_AUG, TILE), lambda b, t: (b, 0, t)),
        pl.BlockSpec((1, TILE, d), real_map),
        pl.BlockSpec((1, TILE, d), real_map),
    )
    tab_spec = pl.BlockSpec((TILE, LANES), lambda b, t: (t, 0))
    tabt_spec = pl.BlockSpec((MLA_ROPE // 2, TILE), lambda b, t: (0, t))
    (s_emb, qt, kt, ke, gv, dec, sr, qm, km, vm, gate_a, gate_b) = pl.pallas_call(
        _inproj_kernel,
        grid=(bsz, nt),
        in_specs=[
            pl.BlockSpec((1, TILE, d), real_map),
            _const_spec((TILE, d)),
            _const_spec((1, d)), _const_spec((1, d)),
            _const_spec((d, _W_COLS)),
            _const_spec((LANES, GLA_QK)), _const_spec((1, GLA_QK)),
            _const_spec((1, MLA_Q_RANK)), _const_spec((MLA_HEADS * MLA_QDIM, MLA_Q_RANK)),
            _const_spec((1, MLA_KV_RANK)), _const_spec((MLA_KV_RANK, MLA_HEADS * HEAD_PAD)),
            _const_spec((MLA_HEADS * MLA_DV, MLA_KV_RANK)),
            tab_spec, tab_spec, tab_spec, tabt_spec, tabt_spec,
            _const_spec((TILE, TILE)),
        ],
        out_specs=out_specs,
        out_shape=out_shapes,
        compiler_params=pltpu.CompilerParams(
            dimension_semantics=("arbitrary", "arbitrary"), vmem_limit_bytes=VMEM_LIMIT),
        name="inproj",
    )(x, head_tile, row2(ln_emb_g), row2(ln_emb_b), w_all, w2p, row2(gla_gate_b[0]),
      row2(mla_q_norm_g[0]), wuqt, row2(mla_kv_norm_g[0]), wuk, wuvt, cos_t, sin_lo, sin_hi,
      cos_tr, sin_tr, tril_chunks)

    o_gla = pl.pallas_call(
        _gla_kernel,
        grid=(nt,),
        in_specs=[
            pl.BlockSpec((bsz, TILE, GLA_QK), lambda t: (0, t, 0)),
            pl.BlockSpec((bsz, TILE, GLA_QK), lambda t: (0, t, 0)),
            pl.BlockSpec((bsz, TILE, GLA_QK), lambda t: (0, t, 0)),
            pl.BlockSpec((bsz, TILE, GLA_VW), lambda t: (0, t, 0)),
            pl.BlockSpec((bsz, 1, TILE // GLA_CHUNK, GLA_QK), lambda t: (0, t, 0, 0)),
            pl.BlockSpec((bsz, TILE, GLA_VW), lambda t: (0, jnp.maximum(t - 1, 0), 0)),
            _const_spec((1, GLA_DV)),
        ],
        out_specs=pl.BlockSpec((bsz, TILE, GLA_VW), lambda t: (0, jnp.maximum(t - 1, 0), 0)),
        out_shape=jax.ShapeDtypeStruct((bsz, seq, GLA_VW), _bf16),
        scratch_shapes=[pltpu.VMEM((bsz, GLA_HEADS, GLA_DV, GLA_DK), _f32)],
        compiler_params=pltpu.CompilerParams(
            dimension_semantics=("arbitrary",), vmem_limit_bytes=VMEM_LIMIT),
        name="gla",
    )(qt, kt, ke, gv, dec.reshape(bsz, nt, TILE // GLA_CHUNK, GLA_QK), sr, row2(gla_norm_g[0]))

    group_width = MLA_GROUP * HEAD_PAD
    o_mla = pl.pallas_call(
        _mla_kernel,
        grid=(bsz, MLA_HEADS // MLA_GROUP, seq // Q_TILE),
        in_specs=[
            pl.BlockSpec((1, MLA_GROUP * MLA_QDIM, Q_TILE), lambda b, hp, i: (b, hp, i)),
            pl.BlockSpec((1, lp, group_width), lambda b, hp, i: (b, 0, hp)),
            pl.BlockSpec((1, MLA_GROUP * V_AUG, lp), lambda b, hp, i: (b, hp, 0)),
        ],
        out_specs=pl.BlockSpec((1, Q_TILE, MLA_GROUP * MLA_DV), lambda b, hp, i: (b, i, hp)),
        out_shape=jax.ShapeDtypeStruct((bsz, seq, MLA_HEADS * MLA_DV), _bf16),
        scratch_shapes=[pltpu.VMEM((MLA_GROUP, V_AUG, Q_TILE), _f32),
                        pltpu.VMEM((MLA_GROUP, 1, LANES), _f32)],
        compiler_params=pltpu.CompilerParams(
            dimension_semantics=("arbitrary", "arbitrary", "arbitrary"),
            vmem_limit_bytes=VMEM_LIMIT),
        name="mla",
    )(qm, km, vm)

    rw = jnp.concatenate([router_group_w[0], router_expert_w[0]], axis=1)
    rw = jnp.pad(rw, ((0, 0), (0, LANES - rw.shape[1])))
    rwh = rw.astype(_bf16)
    rwl = (rw - rwh.astype(_f32)).astype(_bf16)
    rb = jnp.concatenate([router_group_b[0], router_expert_b[0]])
    rb = jnp.pad(rb, (0, LANES - rb.shape[0])).reshape(1, LANES)
    mi = np.arange(ROUTE_BLOCK)
    tril_strict = jnp.asarray(mi[:, None] > mi[None, :], dtype=_bf16)
    flat = lambda a: a.reshape(ntok, a.shape[-1])
    tok_spec = lambda w: pl.BlockSpec((MERGE_TILE, w), lambda g: (g, 0))
    tt_spec = lambda n, index_map: pl.BlockSpec((n * SUBLANES, LANES), index_map)
    s2, info, info_t, cnt = pl.pallas_call(
        _merge_kernel,
        grid=(ntok // MERGE_TILE,),
        in_specs=[tok_spec(d), tok_spec(d), tok_spec(d), tok_spec(d), tok_spec(d),
                  _const_spec((d, d)), _const_spec((d, d)), _const_spec((d, d)),
                  _const_spec((1, d)), _const_spec((1, d)),
                  _const_spec((d, LANES)), _const_spec((d, LANES)), _const_spec((1, LANES)),
                  _const_spec((ROUTE_BLOCK, ROUTE_BLOCK))],
        out_specs=(tt_spec(MERGE_TILE, lambda g: (g, 0)), tok_spec(LANES),
                   pl.BlockSpec((SUBLANES, MERGE_TILE), lambda g: (0, g)), _const_spec((8, LANES))),
        out_shape=(jax.ShapeDtypeStruct((ntok * SUBLANES, LANES), _f32),
                   jax.ShapeDtypeStruct((ntok, LANES), _f32),
                   jax.ShapeDtypeStruct((SUBLANES, ntok), _f32),
                   jax.ShapeDtypeStruct((8, LANES), _f32)),
        scratch_shapes=[pltpu.VMEM((8, LANES), _f32)],
        compiler_params=pltpu.CompilerParams(
            dimension_semantics=("arbitrary",), vmem_limit_bytes=VMEM_LIMIT),
        name="merge_router",
    )(flat(o_gla), flat(o_mla), flat(gate_a), flat(gate_b), flat(s_emb),
      w_branch_gla[0].astype(_bf16), w_branch_mla[0].astype(_bf16), w_out[0].astype(_bf16),
      row2(ln_mix_g[0]), row2(ln_mix_b[0]), rwh, rwl, rb, tril_strict)

    n_tiles = (2 * ntok + N_EXPERTS * (EXPERT_TILE - 1)) // EXPERT_TILE
    n_rows = n_tiles * EXPERT_TILE
    e_idx = info_t[0:2].astype(jnp.int32)
    rank = info_t[4:6].astype(jnp.int32)
    counts = cnt[0, :N_EXPERTS].astype(jnp.int32)
    padded = ((counts + EXPERT_TILE - 1) // EXPERT_TILE) * EXPERT_TILE
    ends = jnp.cumsum(padded)
    starts = ends - padded
    expert_ids = jnp.arange(N_EXPERTS, dtype=jnp.int32)[:, None, None]
    start_of = jnp.sum(jnp.where(e_idx[None] == expert_ids, starts[:, None, None], 0), axis=0)
    pos = (start_of + rank).reshape(-1)
    tile_start = jnp.arange(n_tiles, dtype=jnp.int32) * EXPERT_TILE
    tile_expert = jnp.minimum(
        jnp.sum((ends[None, :] <= tile_start[:, None]).astype(jnp.int32), axis=1), N_EXPERTS - 1)
    n_used = (ends[-1:] // EXPERT_TILE).astype(jnp.int32)
    used_blocks = (ends[-1:] // ZERO_BLOCK).astype(jnp.int32)
    pad_rows = padded - counts
    zero_row = jnp.concatenate([
        jnp.where(pad_rows > j * ZERO_BLOCK, ends - (j + 1) * ZERO_BLOCK, -1)
        for j in range(EXPERT_TILE // ZERO_BLOCK)]).astype(jnp.int32)

    any_spec = pl.BlockSpec(memory_space=pl.ANY)
    xs = pl.pallas_call(
        _dispatch_kernel,
        grid_spec=pltpu.PrefetchScalarGridSpec(
            num_scalar_prefetch=3,
            grid=(ntok // DISPATCH_TILE,),
            in_specs=[any_spec],
            out_specs=any_spec,
            scratch_shapes=[pltpu.VMEM((ZERO_BLOCK * SUBLANES, LANES), _f32),
                            pltpu.VMEM((DISPATCH_SLOTS, DISPATCH_TILE * SUBLANES, LANES), _f32),
                            pltpu.SemaphoreType.DMA((1 + DISPATCH_SLOTS + 2,))],
        ),
        out_shape=jax.ShapeDtypeStruct((n_rows * SUBLANES, LANES), _f32),
        compiler_params=pltpu.CompilerParams(
            dimension_semantics=("arbitrary",), vmem_limit_bytes=VMEM_LIMIT, has_side_effects=True),
        name="dispatch",
    )(pos, zero_row, used_blocks, s2)

    ff = EXPERT_FF
    wg = expert_w_gate[0].reshape(N_EXPERTS, d, ff)
    wu = expert_w_up[0].reshape(N_EXPERTS, d, ff)
    wd = expert_w_down[0].reshape(N_EXPERTS, ff, d)
    ys = pl.pallas_call(
        _expert_kernel,
        grid_spec=pltpu.PrefetchScalarGridSpec(
            num_scalar_prefetch=2,
            grid=(n_tiles,),
            in_specs=[
                any_spec,
                pl.BlockSpec((1, d, ff), lambda u, te, nu: (te[u], 0, 0)),
                pl.BlockSpec((1, d, ff), lambda u, te, nu: (te[u], 0, 0)),
                pl.BlockSpec((1, ff, d), lambda u, te, nu: (te[u], 0, 0)),
            ],
            out_specs=tt_spec(EXPERT_TILE, lambda u, te, nu: (u, 0)),
            scratch_shapes=[pltpu.VMEM((d, ff), _bf16), pltpu.VMEM((d, ff), _bf16),
                            pltpu.VMEM((ff, d), _bf16),
                            pltpu.VMEM((EXPERT_SLOTS, EXPERT_TILE * SUBLANES, LANES), _f32),
                            pltpu.SemaphoreType.DMA((EXPERT_SLOTS,))],
        ),
        out_shape=jax.ShapeDtypeStruct((n_rows * SUBLANES, LANES), _f32),
        compiler_params=pltpu.CompilerParams(
            dimension_semantics=("arbitrary",), vmem_limit_bytes=VMEM_LIMIT),
        name="experts",
    )(tile_expert, n_used, xs, wg, wu, wd)

    out = pl.pallas_call(
        _combine_kernel,
        grid_spec=pltpu.PrefetchScalarGridSpec(
            num_scalar_prefetch=1,
            grid=(ntok // COMBINE_TILE,),
            in_specs=[
                tt_spec(COMBINE_TILE, lambda g, p: (g, 0)),
                pl.BlockSpec((COMBINE_TILE, LANES), lambda g, p: (g, 0)),
                any_spec,
                pl.BlockSpec((1, d), lambda g, p: (0, 0)),
                pl.BlockSpec((1, d), lambda g, p: (0, 0)),
            ],
            out_specs=pl.BlockSpec((COMBINE_TILE, d), lambda g, p: (g, 0)),
            scratch_shapes=[pltpu.VMEM((COMBINE_SLOTS, 2, COMBINE_TILE * SUBLANES, LANES), _f32),
                            pltpu.SemaphoreType.DMA((COMBINE_SLOTS,))],
        ),
        out_shape=jax.ShapeDtypeStruct((ntok, d), _f32),
        compiler_params=pltpu.CompilerParams(
            dimension_semantics=("arbitrary",), vmem_limit_bytes=VMEM_LIMIT),
        name="combine_ln",
    )(pos, s2, info, ys, row2(ln_ffn_g[0]), row2(ln_ffn_b[0]))
    return out.reshape(bsz, seq, d)
```
